```python
import math
import jax
import jax.numpy as jnp
from jax import lax
import numpy as np

D_MODEL = 1024
BATCH = 16
SEQ = 256
DEPTH = 2
DEC_BATCH = 2
DEC_SEQ = 2048
PAST_LEN = 256

GRID_W = 64
HEAD_DIM = 64
D_MIX = D_MODEL
D_RG = 3 * D_MODEL // 4
RG_BLOCK = 64
RG_HEADS = D_RG // RG_BLOCK
RG_C = 8.0
CONV_W = 4
D_FNET = D_MIX - D_RG
FNET_GROUP_DIM = 64
FNET_GROUPS = D_FNET // FNET_GROUP_DIM
SWA_HEADS = (D_MIX // 2) // HEAD_DIM
SWA_KV_HEADS = SWA_HEADS // 4
GQA_GROUP = SWA_HEADS // SWA_KV_HEADS
WINDOW = 128
QBLOCK = 128
DIFF_HEADS = (D_MIX // 2) // (2 * HEAD_DIM)
DIFF_V_DIM = 2 * HEAD_DIM
N_EXPERTS = 16
EC_FACTOR = 2
D_EXPERT = 2 * D_MODEL
ROPE_BASE = 10000.0
LN_EPS = 1e-5
NEG_INF = -1e30
ATTN_SCALE = HEAD_DIM ** -0.5
N_EVEN = (DEPTH + 1) // 2
N_ODD = DEPTH // 2
ALPHA = (2 * DEPTH) ** 0.25
BETA = (8 * DEPTH) ** -0.25
EVEN_IN = 2 * D_RG + D_FNET
ODD_SIZES = (SWA_HEADS * HEAD_DIM, SWA_KV_HEADS * HEAD_DIM, SWA_KV_HEADS * HEAD_DIM,
             DIFF_HEADS * 2 * HEAD_DIM, DIFF_HEADS * 2 * HEAD_DIM, DIFF_HEADS * DIFF_V_DIM)
ODD_IN = sum(ODD_SIZES)
F32 = jnp.float32

kernel_name = 'hybrid_diffusion_rglru_fnet_swa_diffattn_ec_step'


def _layernorm(x, g, b):
    xf = x.astype(F32)
    mu = jnp.mean(xf, -1, keepdims=True)
    var = jnp.mean(jnp.square(xf - mu), -1, keepdims=True)
    return ((xf - mu) * lax.rsqrt(var + LN_EPS) * g + b).astype(x.dtype)


def _modulation(cvec, w, b):
    m = jax.nn.silu(cvec) @ w + b
    return [t[:, None, :] for t in jnp.split(m, 6, axis=-1)]


def _axial_rope(x):
    S = x.shape[1]
    rows = S // GRID_W
    row = jnp.broadcast_to(jnp.arange(rows)[:, None], (rows, GRID_W)).reshape(-1).astype(F32)
    col = jnp.broadcast_to(jnp.arange(GRID_W)[None, :], (rows, GRID_W)).reshape(-1).astype(F32)
    half = HEAD_DIM // 2
    nf = half // 2
    inv = jnp.power(ROPE_BASE, -jnp.arange(nf, dtype=F32) / nf)
    shape = (1, S) + (1,) * (x.ndim - 3) + (nf,)
    xf = x.astype(F32)
    parts = []
    for pos, seg in ((row, xf[..., :half]), (col, xf[..., half:])):
        ang = (pos[:, None] * inv).reshape(shape)
        cos, sin = jnp.cos(ang), jnp.sin(ang)
        x1, x2 = seg[..., :nf], seg[..., nf:]
        parts += [x1 * cos - x2 * sin, x2 * cos + x1 * sin]
    return jnp.concatenate(parts, axis=-1).astype(x.dtype)


def _dwconv_centred(x, w, b):
    K = w.shape[0]
    left = (K - 1) // 2
    S = x.shape[1]
    xp = jnp.pad(x, ((0, 0), (left, K - 1 - left), (0, 0)))
    y = b
    for k in range(K):
        y = y + xp[:, k:k + S] * w[k]
    return y


def _linear_scan(a, b, h0, reverse):
    def combine(e1, e2):
        a1, b1 = e1
        a2, b2 = e2
        return a1 * a2, a2 * b1 + b2
    a_cum, b_cum = lax.associative_scan(combine, (a, b), reverse=reverse, axis=1)
    return a_cum * h0[:, None, :] + b_cum


def _rglru_bidir(xc, h0, w_r, b_r, w_i, b_i, lam):
    B, S, _ = xc.shape
    xf = xc.astype(F32)
    xh = xf.reshape(B, S, RG_HEADS, RG_BLOCK)

    def gate(w, bias):
        z = jnp.einsum('bshc,zhcd->zbshd', xh, w.astype(F32)).reshape(2, B, S, D_RG)
        return jax.nn.sigmoid(z + bias.astype(F32)[:, None, None, :])

    r = gate(w_r, b_r)
    i = gate(w_i, b_i)
    log_a = -RG_C * jax.nn.softplus(-lam.astype(F32))[:, None, None, :] * r
    a = jnp.exp(log_a)
    bt = jnp.sqrt(-jnp.expm1(2.0 * log_a)) * i * xf[None]
    h0f = h0.astype(F32)
    hf = _linear_scan(a[0], bt[0], h0f[:, 0], False)
    hb = _linear_scan(a[1], bt[1], h0f[:, 1], True)
    return hf, hb


def _fourier_mix(xf, w_f):
    B, S, _ = xf.shape
    z = xf.astype(F32).reshape(B, S, FNET_GROUPS, FNET_GROUP_DIM)
    f = jnp.fft.fft2(z, axes=(1, 3), norm='ortho').real
    y = jnp.einsum('bsgc,gcd->bsgd', f, w_f.astype(F32))
    return y.reshape(B, S, D_FNET).astype(xf.dtype)


def _even_mixer(u, h0, w_in, conv_w, conv_b, w_r, b_r, w_i, b_i, lam, w_f, w_out):
    proj = u @ w_in
    xa, ga, xfn = jnp.split(proj, [D_RG, 2 * D_RG], axis=-1)
    xc = _dwconv_centred(xa, conv_w, conv_b)
    hf, hb = _rglru_bidir(xc, h0, w_r, b_r, w_i, b_i, lam)
    y_rg = (hf + hb).astype(u.dtype) * jax.nn.gelu(ga)
    y_fn = _fourier_mix(xfn, w_f)
    out = jnp.concatenate([y_rg, y_fn], axis=-1) @ w_out
    final_state = jnp.stack([hf[:, -1], hb[:, 0]], axis=1)
    return out, final_state


def _odd_project(u, w_in):
    B, S, _ = u.shape
    cuts = np.cumsum(ODD_SIZES)[:-1].tolist()
    qs, ks, vs, qd, kd, vd = jnp.split(u @ w_in, cuts, axis=-1)
    return (qs.reshape(B, S, SWA_KV_HEADS, GQA_GROUP, HEAD_DIM),
            ks.reshape(B, S, SWA_KV_HEADS, HEAD_DIM),
            vs.reshape(B, S, SWA_KV_HEADS, HEAD_DIM),
            qd.reshape(B, S, DIFF_HEADS, 2, HEAD_DIM),
            kd.reshape(B, S, DIFF_HEADS, 2, HEAD_DIM),
            vd.reshape(B, S, DIFF_HEADS, DIFF_V_DIM))


def _map_qblocks(fn, q):
    B, S = q.shape[:2]
    nb = S // QBLOCK
    qb = jnp.moveaxis(q.reshape((B, nb, QBLOCK) + q.shape[2:]), 1, 0)
    out = lax.map(lambda args: fn(*args), (qb, jnp.arange(nb)))
    out = jnp.moveaxis(out, 0, 1)
    return out.reshape((B, S) + out.shape[3:])


def _sink_attend(qb, segs, sink):
    B, Q = qb.shape[:2]
    logits = []
    for k, _, mask in segs:
        s = jnp.einsum('bqhgd,bkhd->bhgqk', qb, k).astype(F32) * ATTN_SCALE
        if mask is not None:
            s = jnp.where(mask, s, NEG_INF)
        logits.append(s)
    sink_l = jnp.broadcast_to(sink.astype(F32)[None, :, :, None, None], (B, SWA_KV_HEADS, GQA_GROUP, Q, 1))
    w = jax.nn.softmax(jnp.concatenate([sink_l] + logits, axis=-1), axis=-1)
    out = 0.0
    off = 1
    for k, v, _ in segs:
        n = k.shape[1]
        out = out + jnp.einsum('bhgqk,bkhd->bqhgd', w[..., off:off + n].astype(v.dtype), v)
        off += n
    return out


def _swa_context(q, k, v, sink):
    return _map_qblocks(lambda qb, blk: _sink_attend(qb, [(k, v, None)], sink), q)


def _swa_latent(q, k, v, ck, cv, sink):
    S = q.shape[1]
    pad = ((0, 0), (WINDOW, WINDOW), (0, 0), (0, 0))
    kp, vp = jnp.pad(k, pad), jnp.pad(v, pad)
    span = QBLOCK + 2 * WINDOW

    def fn(qb, blk):
        start = blk * QBLOCK
        kw = lax.dynamic_slice_in_dim(kp, start, span, axis=1)
        vw = lax.dynamic_slice_in_dim(vp, start, span, axis=1)
        qpos = start + jnp.arange(QBLOCK)
        kpos = start - WINDOW + jnp.arange(span)
        mask = ((jnp.abs(qpos[:, None] - kpos[None, :]) <= WINDOW)
                & (kpos >= 0)[None, :] & (kpos < S)[None, :])
        return _sink_attend(qb, [(ck, cv, None), (kw, vw, mask)], sink)

    return _map_qblocks(fn, q)


def _diff_attend(q, k, v, lam):
    def fn(qb, blk):
        s = jnp.einsum('bqhmd,bkhmd->bhmqk', qb, k).astype(F32) * ATTN_SCALE
        a = jax.nn.softmax(s, axis=-1)
        w = a[:, :, 0] - lam[None, :, None, None] * a[:, :, 1]
        return jnp.einsum('bhqk,bkhe->bqhe', w.astype(v.dtype), v)
    return _map_qblocks(fn, q)


def _lambda_init(layer):
    return 0.8 - 0.6 * math.exp(-0.3 * layer)


def _diff_lambda(lam_p, lam_init):
    lp = lam_p.astype(F32)
    return jnp.exp(jnp.sum(lp[0] * lp[1], -1)) - jnp.exp(jnp.sum(lp[2] * lp[3], -1)) + lam_init


def _odd_merge(u, ys, yd, subln_g, lam_init, w_out):
    B, S = u.shape[:2]
    ydf = yd.astype(F32)
    ydn = (ydf * lax.rsqrt(jnp.mean(jnp.square(ydf), -1, keepdims=True) + LN_EPS)
           * subln_g.astype(F32) * (1.0 - lam_init))
    mixed = jnp.concatenate([ys.reshape(B, S, -1), ydn.astype(u.dtype).reshape(B, S, -1)], axis=-1)
    return mixed @ w_out


def _odd_mixer_context(u, layer, w_in, sink, lam_p, subln_g, w_out):
    qs, ks, vs, qd, kd, vd = _odd_project(u, w_in)
    lam_init = _lambda_init(layer)
    ys = _swa_context(qs, ks, vs, sink.reshape(SWA_KV_HEADS, GQA_GROUP))
    yd = _diff_attend(qd, kd, vd, _diff_lambda(lam_p, lam_init))
    return _odd_merge(u, ys, yd, subln_g, lam_init, w_out), (ks, vs, kd, vd)


def _odd_mixer_latent(u, ctx_swa_k, ctx_swa_v, ctx_diff_k, ctx_diff_v, layer, w_in, sink, lam_p, subln_g, w_out):
    qs, ks, vs, qd, kd, vd = _odd_project(u, w_in)
    qs, ks, qd, kd = [_axial_rope(t) for t in (qs, ks, qd, kd)]
    lam_init = _lambda_init(layer)
    ys = _swa_latent(qs, ks, vs, ctx_swa_k, ctx_swa_v, sink.reshape(SWA_KV_HEADS, GQA_GROUP))
    k_all = jnp.concatenate([kd, ctx_diff_k], axis=1)
    v_all = jnp.concatenate([vd, ctx_diff_v], axis=1)
    yd = _diff_attend(qd, k_all, v_all, _diff_lambda(lam_p, lam_init))
    return _odd_merge(u, ys, yd, subln_g, lam_init, w_out)


def _expert_choice_ffn(u, w_router, w_gate, w_up, w_down):
    B, S, D = u.shape
    n = B * S
    cap = EC_FACTOR * n // N_EXPERTS
    tok = u.reshape(n, D)
    aff = jax.nn.softmax((tok @ w_router).astype(F32), axis=-1)
    g, idx = lax.top_k(aff.T, cap)
    xe = tok[idx]
    h = jax.nn.silu(jnp.einsum('ecd,edf->ecf', xe, w_gate)) * jnp.einsum('ecd,edf->ecf', xe, w_up)
    ye = jnp.einsum('ecf,efd->ecd', h, w_down) * g[..., None].astype(u.dtype)
    out = jnp.zeros_like(tok).at[idx.reshape(-1)].add(ye.reshape(-1, D).astype(tok.dtype))
    return out.reshape(B, S, D)


def setup_inputs(seed: int = 0) -> dict:
    key = jax.random.key(seed)
    keys = iter(jax.random.split(key, 40))

    def nrm(shape, scale):
        return jax.random.normal(next(keys), shape, F32) * scale

    a0 = jax.random.uniform(next(keys), (N_EVEN, 2, D_RG), F32, minval=0.9, maxval=0.999)
    r0 = a0 ** (1.0 / RG_C)
    e_lambda = jnp.log(r0) - jnp.log1p(-r0)
    return {
        'x_prompt': nrm((BATCH, SEQ, D_MODEL), 1.0),
        'x_sample': nrm((DEC_BATCH, DEC_SEQ, D_MODEL), 1.0),
        'state_rglru': nrm((DEC_BATCH, N_EVEN, 2, D_RG), 0.5),
        'cache_swa_k': nrm((DEC_BATCH, N_ODD, PAST_LEN, SWA_KV_HEADS, HEAD_DIM), 1.0),
        'cache_swa_v': nrm((DEC_BATCH, N_ODD, PAST_LEN, SWA_KV_HEADS, HEAD_DIM), 1.0),
        'cache_diff_k': nrm((DEC_BATCH, N_ODD, PAST_LEN, DIFF_HEADS, 2, HEAD_DIM), 1.0),
        'cache_diff_v': nrm((DEC_BATCH, N_ODD, PAST_LEN, DIFF_HEADS, DIFF_V_DIM), 1.0),
        'c': nrm((DEC_BATCH, D_MODEL), 1.0),
        'c_ctx': nrm((D_MODEL,), 1.0),
        'w_mod': nrm((DEPTH, D_MODEL, 6 * D_MODEL), 0.5 * D_MODEL ** -0.5),
        'b_mod': nrm((DEPTH, 6 * D_MODEL), 0.02),
        'ln_g': 1.0 + nrm((DEPTH, 2, D_MODEL), 0.02),
        'ln_b': nrm((DEPTH, 2, D_MODEL), 0.02),
        'e_w_in': nrm((N_EVEN, D_MODEL, EVEN_IN), D_MODEL ** -0.5),
        'e_conv_w': nrm((N_EVEN, CONV_W, D_RG), CONV_W ** -0.5),
        'e_conv_b': nrm((N_EVEN, D_RG), 0.02),
        'e_w_rgate': nrm((N_EVEN, 2, RG_HEADS, RG_BLOCK, RG_BLOCK), RG_BLOCK ** -0.5),
        'e_b_rgate': nrm((N_EVEN, 2, D_RG), 0.1),
        'e_w_igate': nrm((N_EVEN, 2, RG_HEADS, RG_BLOCK, RG_BLOCK), RG_BLOCK ** -0.5),
        'e_b_igate': nrm((N_EVEN, 2, D_RG), 0.1),
        'e_lambda': e_lambda,
        'e_w_fnet': nrm((N_EVEN, FNET_GROUPS, FNET_GROUP_DIM, FNET_GROUP_DIM), FNET_GROUP_DIM ** -0.5),
        'e_w_out': nrm((N_EVEN, D_MIX, D_MODEL), BETA * D_MIX ** -0.5),
        'o_w_in': nrm((N_ODD, D_MODEL, ODD_IN), D_MODEL ** -0.5),
        'o_sink': nrm((N_ODD, SWA_HEADS), 0.5),
        'o_lambda': nrm((N_ODD, 4, DIFF_HEADS, HEAD_DIM), 0.1),
        'o_subln_g': 1.0 + nrm((N_ODD, DIFF_V_DIM), 0.02),
        'o_w_out': nrm((N_ODD, D_MIX, D_MODEL), BETA * D_MIX ** -0.5),
        'w_router': nrm((DEPTH, D_MODEL, N_EXPERTS), D_MODEL ** -0.5),
        'w_gate': nrm((DEPTH, N_EXPERTS, D_MODEL, D_EXPERT), D_MODEL ** -0.5),
        'w_up': nrm((DEPTH, N_EXPERTS, D_MODEL, D_EXPERT), D_MODEL ** -0.5),
        'w_down': nrm((DEPTH, N_EXPERTS, D_EXPERT, D_MODEL), BETA * D_EXPERT ** -0.5),
    }


def reference(x_prompt, x_sample, state_rglru, cache_swa_k, cache_swa_v, cache_diff_k, cache_diff_v,
              c, c_ctx, w_mod, b_mod, ln_g, ln_b,
              e_w_in, e_conv_w, e_conv_b, e_w_rgate, e_b_rgate, e_w_igate, e_b_igate, e_lambda,
              e_w_fnet, e_w_out, o_w_in, o_sink, o_lambda, o_subln_g, o_w_out,
              w_router, w_gate, w_up, w_down):

    def run_layer(l, x, cvec, mix):
        sa, ca, ga, sf, cf, gf = _modulation(cvec, w_mod[l], b_mod[l])
        out, extra = mix(x * (1.0 + ca) + sa)
        x = _layernorm(ALPHA * x + ga * out, ln_g[l, 0], ln_b[l, 0])
        ffn = _expert_choice_ffn(x * (1.0 + cf) + sf, w_router[l], w_gate[l], w_up[l], w_down[l])
        x = _layernorm(ALPHA * x + gf * ffn, ln_g[l, 1], ln_b[l, 1])
        return x, extra

    def even_mix(j, u, h0):
        return _even_mixer(u, h0, e_w_in[j], e_conv_w[j], e_conv_b[j], e_w_rgate[j], e_b_rgate[j],
                           e_w_igate[j], e_b_igate[j], e_lambda[j], e_w_fnet[j], e_w_out[j])

    x = x_prompt
    cvec_ctx = c_ctx[None, :]
    rg_states, swa_k, swa_v, diff_k, diff_v = [], [], [], [], []
    for l in range(DEPTH):
        j = l // 2
        if l % 2 == 0:
            h0 = jnp.zeros((x.shape[0], 2, D_RG), F32)
            x, st = run_layer(l, x, cvec_ctx, lambda u, j=j, h0=h0: even_mix(j, u, h0))
            rg_states.append(st)
        else:
            x, (ks, vs, kd, vd) = run_layer(
                l, x, cvec_ctx,
                lambda u, j=j, l=l: _odd_mixer_context(u, l, o_w_in[j], o_sink[j], o_lambda[j],
                                                       o_subln_g[j], o_w_out[j]))
            swa_k.append(ks)
            swa_v.append(vs)
            diff_k.append(kd)
            diff_v.append(vd)
    y_prompt = x
    new_state_rglru = jnp.stack(rg_states, axis=1)
    new_cache_swa_k = jnp.stack(swa_k, axis=1)
    new_cache_swa_v = jnp.stack(swa_v, axis=1)
    new_cache_diff_k = jnp.stack(diff_k, axis=1)
    new_cache_diff_v = jnp.stack(diff_v, axis=1)

    x = x_sample
    for l in range(DEPTH):
        j = l // 2
        if l % 2 == 0:
            x, _ = run_layer(l, x, c, lambda u, j=j: even_mix(j, u, state_rglru[:, j]))
        else:
            x, _ = run_layer(
                l, x, c,
                lambda u, j=j, l=l: (_odd_mixer_latent(u, cache_swa_k[:, j], cache_swa_v[:, j],
                                                       cache_diff_k[:, j], cache_diff_v[:, j], l,
                                                       o_w_in[j], o_sink[j], o_lambda[j],
                                                       o_subln_g[j], o_w_out[j]), None))
    y_sample = x

    return (y_prompt, y_sample, new_state_rglru, new_cache_swa_k, new_cache_swa_v, new_cache_diff_k, new_cache_diff_v)
```

```python
import functools
import math

import numpy as np
import jax
import jax.numpy as jnp
from jax import lax
from jax.experimental import pallas as pl
from jax.experimental.pallas import tpu as pltpu

F32 = jnp.float32
BF16 = jnp.bfloat16
I32 = jnp.int32

D_MODEL = 1024
BATCH, SEQ = 16, 256
DEC_BATCH, DEC_SEQ = 2, 2048
PAST_LEN = 256
DEPTH = 2
GRID_W = 64
HEAD_DIM = 64
D_RG = 768
RG_BLOCK = 64
RG_C = 8.0
CONV_W = 4
D_FNET = 256
FNET_GROUP_DIM = 64
FNET_GROUPS = 4
SWA_HEADS = 8
SWA_KV_HEADS = 2
GQA_GROUP = 4
WINDOW = 128
DIFF_HEADS = 4
DIFF_V_DIM = 128
N_EXPERTS = 16
EC_FACTOR = 2
D_EXPERT = 2048
ROPE_BASE = 10000.0
LN_EPS = 1e-5
NEG_INF = -1e30
ATTN_SCALE = HEAD_DIM ** -0.5
ALPHA = (2 * DEPTH) ** 0.25
EVEN_IN = 2 * D_RG + D_FNET
ODD_IN = 2304

N_CTX = BATCH * SEQ
N_LAT = DEC_BATCH * DEC_SEQ
N_TOK = N_CTX + N_LAT
N_SEG = 1 + DEC_BATCH
CAP = EC_FACTOR * N_CTX // N_EXPERTS
assert N_CTX == N_LAT

LANE = 128
SUBLANE = 8
MXU_DIM = 256
VMEM_LIMIT = 56 * 1024 * 1024
TM = 512
GC = TM
N_CHUNKS = N_CTX // GC
WIN = MXU_DIM
TOPK_MIN_EXP = -150.0
TOPK_GEO_STEPS = 24
TOPK_LIN_STEPS = 24
N_TILES = N_TOK // TM
CTX_TILES = N_CTX // TM
LAT_TILES_PER_SEQ = DEC_SEQ // TM


def _cparams(sem):
    return pltpu.CompilerParams(dimension_semantics=sem, vmem_limit_bytes=VMEM_LIMIT)


def _dot(a, b):
    return jnp.dot(a, b, preferred_element_type=F32)


def _dot_nt(a, b):
    return lax.dot_general(a, b, (((1,), (1,)), ((), ())), preferred_element_type=F32)


def _split(a):
    hi = a.astype(BF16)
    lo = (a - hi.astype(F32)).astype(BF16)
    return hi, lo


def _dot3(a, b):
    ah, al = _split(a)
    bh, bl = _split(b)
    return _dot(ah, bh) + (_dot(ah, bl) + _dot(al, bh))


def _layernorm(y, g, b):
    mu = jnp.mean(y, axis=-1, keepdims=True)
    d = y - mu
    var = jnp.mean(d * d, axis=-1, keepdims=True)
    return d * lax.rsqrt(var + LN_EPS) * g + b


def _pair_specs(width, lat_block0=0):
    return [
        pl.BlockSpec((TM, width), lambda i, *_: (jnp.minimum(i, CTX_TILES - 1), 0)),
        pl.BlockSpec((TM, width), lambda i, *_: (jnp.maximum(i - CTX_TILES, 0) + lat_block0, 0)),
    ]


def _pick(i, ctx_ref, lat_ref):
    return jnp.where(i < CTX_TILES, ctx_ref[...], lat_ref[...])


def _seg_of_tile(i):
    return jnp.where(i < CTX_TILES, 0, (i - CTX_TILES) // LAT_TILES_PER_SEQ + 1)


MOD_TN = 1536


def _mod_kernel(c_ref, w_ref, b_ref, o_ref):
    c = c_ref[...]
    s = c * jax.nn.sigmoid(c)
    o_ref[0] = _dot3(s, w_ref[0]) + b_ref[0]


def _modulation(cvec8, w_mod, b_mod):
    n = 6 * D_MODEL
    return pl.pallas_call(
        _mod_kernel,
        grid=(DEPTH, n // MOD_TN),
        in_specs=[
            pl.BlockSpec((SUBLANE, D_MODEL), lambda l, j: (0, 0)),
            pl.BlockSpec((1, D_MODEL, MOD_TN), lambda l, j: (l, 0, j)),
            pl.BlockSpec((1, 1, MOD_TN), lambda l, j: (l, 0, j)),
        ],
        out_specs=pl.BlockSpec((1, SUBLANE, MOD_TN), lambda l, j: (l, 0, j)),
        out_shape=jax.ShapeDtypeStruct((DEPTH, SUBLANE, n), F32),
        compiler_params=_cparams(("arbitrary", "arbitrary")),
        name="modulation",
    )(cvec8, w_mod, b_mod.reshape(DEPTH, 1, n))


def _rope_tile(p, cos, sin, rope_groups):
    lane = lax.broadcasted_iota(I32, (1, LANE), 1)
    first_half = (lane % 32) < 16
    pieces = []
    for k in range(p.shape[1] // LANE):
        xg = p[:, k * LANE:(k + 1) * LANE]
        if k in rope_groups:
            partner = jnp.where(first_half, pltpu.roll(xg, LANE - 16, 1), pltpu.roll(xg, 16, 1))
            xg = xg * cos + partner * sin
        pieces.append(xg)
    return jnp.concatenate(pieces, axis=1)


def _inproj_kernel(*refs, rope_groups):
    if rope_groups:
        xc_ref, xl_ref, mod_ref, w_ref, cos_ref, sin_ref, o_ref, wbf_ref = refs
    else:
        xc_ref, xl_ref, mod_ref, w_ref, o_ref, wbf_ref = refs
    i = pl.program_id(0)

    @pl.when(i == 0)
    def _():
        wbf_ref[...] = w_ref[0].astype(BF16)

    m = mod_ref[0, 0]
    u = _pick(i, xc_ref, xl_ref) * (1.0 + m[1:2, :]) + m[0:1, :]
    p = _dot(u.astype(BF16), wbf_ref[...])
    if rope_groups:
        @pl.when(i < CTX_TILES)
        def _():
            o_ref[...] = p

        @pl.when(i >= CTX_TILES)
        def _():
            o_ref[...] = _rope_tile(p, cos_ref[...], sin_ref[...], rope_groups)
    else:
        o_ref[...] = p


def _inproj(x_pair, mod, layer, w_in, j, rope=None):
    n = w_in.shape[-1]
    rope_groups = ()
    combined = x_pair[0] is x_pair[1]
    in_specs = _pair_specs(D_MODEL, CTX_TILES if combined else 0) + [
        pl.BlockSpec((1, 1, 6, D_MODEL), lambda i: (layer, _seg_of_tile(i), 0, 0)),
        pl.BlockSpec((1, D_MODEL, n), lambda i: (j, 0, 0)),
    ]
    args = [x_pair[0], x_pair[1], mod, w_in]
    if rope is not None:
        cos, sin, rope_groups = rope
        tab = pl.BlockSpec((TM, LANE), lambda i: (jnp.maximum(i - CTX_TILES, 0) % LAT_TILES_PER_SEQ, 0))
        in_specs += [tab, tab]
        args += [cos, sin]
    return pl.pallas_call(
        functools.partial(_inproj_kernel, rope_groups=rope_groups),
        grid=(N_TILES,),
        in_specs=in_specs,
        out_specs=pl.BlockSpec((TM, n), lambda i: (i, 0)),
        out_shape=jax.ShapeDtypeStruct((N_TOK, n), F32),
        scratch_shapes=[pltpu.VMEM((D_MODEL, n), BF16)],
        compiler_params=_cparams(("arbitrary",)),
        name="inproj_rope" if rope is not None else "inproj",
    )(*args)


def _rope_tables():
    s = np.arange(DEC_SEQ)
    row, col = (s // GRID_W).astype(np.float64), (s % GRID_W).astype(np.float64)
    nf = HEAD_DIM // 4
    inv = np.power(ROPE_BASE, -np.arange(nf, dtype=np.float64) / nf)
    ang = np.concatenate([row[:, None] * inv, row[:, None] * inv, col[:, None] * inv, col[:, None] * inv], axis=1)
    sign = np.concatenate([-np.ones(nf), np.ones(nf), -np.ones(nf), np.ones(nf)])
    cos = np.tile(np.cos(ang), (1, LANE // HEAD_DIM))
    sin = np.tile(np.sin(ang) * sign, (1, LANE // HEAD_DIM))
    return jnp.asarray(cos, F32), jnp.asarray(sin, F32)


RG_T = 256
RG_TILES = D_RG // MXU_DIM


def _neg_expm1(y):
    return -jnp.tanh(0.5 * y) * (jnp.exp(y) + 1.0)


def _softplus(x):
    u = jnp.exp(-jnp.abs(x))
    w = 1.0 + u
    l1p = jnp.where(w == 1.0, u, jnp.log(w) * (u / jnp.where(w == 1.0, 1.0, w - 1.0)))
    return jnp.maximum(x, 0.0) + l1p


def _rglru_kernel(xa_ref, ga_ref, h0_ref, cw_ref, cb_ref, wg_ref, br_ref, bi_ref, lam_ref,
                  y_ref, st_ref, hf_s, a_s, b_s, hs_s, *, seq_len):
    nchunk = seq_len // RG_T
    cw = cw_ref[0]
    cb = cb_ref[0]

    def conv_chunk(c):
        base = pl.multiple_of(c * RG_T, RG_T)
        cur = xa_ref[pl.ds(base, RG_T), :]
        prev = xa_ref[pl.ds(pl.multiple_of(jnp.maximum(base - SUBLANE, 0), SUBLANE), SUBLANE), :]
        prev = jnp.where(c > 0, prev, 0.0)
        nxt = xa_ref[pl.ds(pl.multiple_of(jnp.minimum(base + RG_T, seq_len - SUBLANE), SUBLANE), SUBLANE), :]
        nxt = jnp.where(c < nchunk - 1, nxt, 0.0)
        win = jnp.concatenate([prev, cur, nxt], axis=0)
        xc = cb
        for k in range(CONV_W):
            off = SUBLANE - 1 + k
            xc = xc + win[off:off + RG_T] * cw[k:k + 1, :]
        return xc

    def gates(xc, z):
        xb = xc.astype(BF16)
        rs, gs = [], []
        for t in range(RG_TILES):
            zz = _dot(xb[:, t * MXU_DIM:(t + 1) * MXU_DIM], wg_ref[0, z, t].astype(BF16))
            rs.append(zz[:, :MXU_DIM])
            gs.append(zz[:, MXU_DIM:])
        r = jax.nn.sigmoid(jnp.concatenate(rs, axis=1) + br_ref[0, z:z + 1, :])
        g = jax.nn.sigmoid(jnp.concatenate(gs, axis=1) + bi_ref[0, z:z + 1, :])
        log_a = (-RG_C * _softplus(-lam_ref[0, z:z + 1, :])) * r
        a = jnp.exp(log_a)
        bt = jnp.sqrt(_neg_expm1(2.0 * log_a)) * g * xc
        return a, bt

    def scan_chunk(h, reverse):
        def step(t, h):
            tt = RG_T - 1 - t if reverse else t
            h = a_s[pl.ds(tt, 1), :] * h + b_s[pl.ds(tt, 1), :]
            hs_s[pl.ds(tt, 1), :] = h
            return h
        return lax.fori_loop(0, RG_T, step, h, unroll=8)

    def fwd_chunk(c, h):
        a, bt = gates(conv_chunk(c), 0)
        a_s[...] = a
        b_s[...] = bt
        h = scan_chunk(h, False)
        hf_s[pl.ds(pl.multiple_of(c * RG_T, RG_T), RG_T), :] = hs_s[...]
        return h

    def bwd_chunk(k, h):
        c = nchunk - 1 - k
        a, bt = gates(conv_chunk(c), 1)
        a_s[...] = a
        b_s[...] = bt
        h = scan_chunk(h, True)
        rows = pl.ds(pl.multiple_of(c * RG_T, RG_T), RG_T)
        y_ref[rows, :] = ((hf_s[rows, :] + hs_s[...]) * jax.nn.gelu(ga_ref[rows, :])).astype(BF16)
        return h

    hf = lax.fori_loop(0, nchunk, fwd_chunk, h0_ref[0, 0:1, :])
    hb = lax.fori_loop(0, nchunk, bwd_chunk, h0_ref[0, 1:2, :])
    st_ref[0, 0:1, :] = hf
    st_ref[0, 1:2, :] = hb


def _rglru(proj, h0, j, cw, cb, wg, br, bi, lam, *, seq_len, nseq, row_block0):
    wspec3 = lambda shape: pl.BlockSpec((1,) + shape, lambda b: (j,) + (0,) * len(shape))
    return pl.pallas_call(
        functools.partial(_rglru_kernel, seq_len=seq_len),
        grid=(nseq,),
        in_specs=[
            pl.BlockSpec((seq_len, D_RG), lambda b: (row_block0 + b, 0)),
            pl.BlockSpec((seq_len, D_RG), lambda b: (row_block0 + b, 1)),
            pl.BlockSpec((1, 2, D_RG), lambda b: (b, 0, 0)),
            wspec3((CONV_W, D_RG)),
            wspec3((1, D_RG)),
            wspec3((2, RG_TILES, MXU_DIM, 2 * MXU_DIM)),
            wspec3((2, D_RG)),
            wspec3((2, D_RG)),
            wspec3((2, D_RG)),
        ],
        out_specs=[
            pl.BlockSpec((seq_len, D_RG), lambda b: (b, 0)),
            pl.BlockSpec((1, 2, D_RG), lambda b: (b, 0, 0)),
        ],
        out_shape=[
            jax.ShapeDtypeStruct((nseq * seq_len, D_RG), BF16),
            jax.ShapeDtypeStruct((nseq, 2, D_RG), F32),
        ],
        scratch_shapes=[
            pltpu.VMEM((seq_len, D_RG), F32),
            pltpu.VMEM((RG_T, D_RG), F32),
            pltpu.VMEM((RG_T, D_RG), F32),
            pltpu.VMEM((RG_T, D_RG), F32),
        ],
        compiler_params=_cparams(("arbitrary",)),
        name=f"rglru_s{seq_len}",
    )(proj, proj, h0, cw, cb, wg, br, bi, lam)


def _gate_tiles(w_r, w_i):
    per_tile = MXU_DIM // RG_BLOCK
    eye = jnp.eye(per_tile, dtype=F32)

    def tiles(w):
        w = w.reshape(w.shape[0], 2, RG_TILES, per_tile, RG_BLOCK, RG_BLOCK)
        return jnp.einsum('nztbcd,be->nztbced', w, eye).reshape(w.shape[0], 2, RG_TILES, MXU_DIM, MXU_DIM)

    return jnp.concatenate([tiles(w_r), tiles(w_i)], axis=-1)


def _fnet_kernel(z_ref, cc_ref, sc_ref, w_ref, cs_ref, ns_ref, y_ref, p_s, q_s):
    @pl.when(pl.program_id(1) == 0)
    def _():
        w = w_ref[0]
        a = _dot3(cc_ref[...], w).astype(BF16)
        b = _dot3(sc_ref[...], w).astype(BF16)
        zb = z_ref[...].astype(BF16)
        p_s[...] = _dot(zb, a).astype(BF16)
        q_s[...] = _dot(zb, b).astype(BF16)

    y_ref[...] = (_dot(cs_ref[...].astype(BF16), p_s[...])
                  + _dot(ns_ref[...].astype(BF16), q_s[...])).astype(BF16)


def _fnet(proj, j, wbd, *, seq_len, nseq, row_block0):
    tr = min(seq_len, 512)
    nrt = seq_len // tr
    cc, sc = _dft_tables(FNET_GROUP_DIM)
    ccbd = jnp.asarray(np.kron(np.eye(FNET_GROUPS), cc), F32)
    scbd = jnp.asarray(np.kron(np.eye(FNET_GROUPS), sc), F32)
    cs, ss = _dft_tables(seq_len)
    cs_b, ns_b = jnp.asarray(cs, F32), jnp.asarray(-ss, F32)
    zcol = (2 * D_RG) // D_FNET
    const = lambda shape: pl.BlockSpec(shape, lambda b, r: (0,) * len(shape))
    return pl.pallas_call(
        _fnet_kernel,
        grid=(nseq, nrt),
        in_specs=[
            pl.BlockSpec((seq_len, D_FNET), lambda b, r: (row_block0 + b, zcol)),
            const((D_FNET, D_FNET)),
            const((D_FNET, D_FNET)),
            pl.BlockSpec((1, D_FNET, D_FNET), lambda b, r: (j, 0, 0)),
            pl.BlockSpec((tr, seq_len), lambda b, r: (r, 0)),
            pl.BlockSpec((tr, seq_len), lambda b, r: (r, 0)),
        ],
        out_specs=pl.BlockSpec((tr, D_FNET), lambda b, r: (b * nrt + r, 0)),
        out_shape=jax.ShapeDtypeStruct((nseq * seq_len, D_FNET), BF16),
        scratch_shapes=[pltpu.VMEM((seq_len, D_FNET), BF16), pltpu.VMEM((seq_len, D_FNET), BF16)],
        compiler_params=_cparams(("arbitrary", "arbitrary")),
        name=f"fnet_s{seq_len}",
    )(proj, ccbd, scbd, wbd, cs_b, ns_b)


def _dft_tables(n):
    k = np.arange(n)
    ang = 2.0 * np.pi * ((k[:, None] * k[None, :]) % n) / n
    return np.cos(ang) / np.sqrt(n), np.sin(ang) / np.sqrt(n)


def _fnet_blockdiag(w_f):
    eye = jnp.eye(FNET_GROUPS, dtype=F32)
    return jnp.einsum('ngcd,gh->ngchd', w_f, eye).reshape(w_f.shape[0], D_FNET, D_FNET)


def _outproj_kernel(ac_ref, al_ref, bc_ref, bl_ref, w_ref, xc_ref, xl_ref, mod_ref, g_ref, be_ref, wr_ref,
                    x1_ref, u2_ref, aff_ref, wbf_ref, *, ka):
    i = pl.program_id(0)

    @pl.when(i == 0)
    def _():
        wbf_ref[...] = w_ref[0].astype(BF16)

    out = _dot(_pick(i, ac_ref, al_ref), wbf_ref[0:ka, :]) + _dot(_pick(i, bc_ref, bl_ref), wbf_ref[ka:, :])
    m = mod_ref[0, 0]
    x1 = _layernorm(ALPHA * _pick(i, xc_ref, xl_ref) + m[2:3, :] * out, g_ref[0, 0:1, :], be_ref[0, 0:1, :])
    x1_ref[...] = x1
    u2 = x1 * (1.0 + m[4:5, :]) + m[3:4, :]
    u2_ref[...] = u2.astype(BF16)
    logits = _dot3(u2, wr_ref[0])
    mx = jnp.max(logits, axis=-1, keepdims=True)
    ex = jnp.exp(logits - mx)
    aff_ref[...] = ex / jnp.sum(ex, axis=-1, keepdims=True)


def _outproj(a_pair, b_pair, w_out, j, x_pair, mod, layer, ln_g, ln_b, w_router):
    ka, kb = a_pair[0].shape[1], b_pair[0].shape[1]
    combined = x_pair[0] is x_pair[1]
    return pl.pallas_call(
        functools.partial(_outproj_kernel, ka=ka),
        grid=(N_TILES,),
        in_specs=_pair_specs(ka) + _pair_specs(kb) + [
            pl.BlockSpec((1, D_MODEL, D_MODEL), lambda i: (j, 0, 0)),
        ] + _pair_specs(D_MODEL, CTX_TILES if combined else 0) + [
            pl.BlockSpec((1, 1, 6, D_MODEL), lambda i: (layer, _seg_of_tile(i), 0, 0)),
            pl.BlockSpec((1, 2, D_MODEL), lambda i: (layer, 0, 0)),
            pl.BlockSpec((1, 2, D_MODEL), lambda i: (layer, 0, 0)),
            pl.BlockSpec((1, D_MODEL, N_EXPERTS), lambda i: (layer, 0, 0)),
        ],
        out_specs=[
            pl.BlockSpec((TM, D_MODEL), lambda i: (i, 0)),
            pl.BlockSpec((TM, D_MODEL), lambda i: (i, 0)),
            pl.BlockSpec((TM, N_EXPERTS), lambda i: (i, 0)),
        ],
        out_shape=[
            jax.ShapeDtypeStruct((N_TOK, D_MODEL), F32),
            jax.ShapeDtypeStruct((N_TOK, D_MODEL), BF16),
            jax.ShapeDtypeStruct((N_TOK, N_EXPERTS), F32),
        ],
        scratch_shapes=[pltpu.VMEM((D_MODEL, D_MODEL), BF16)],
        compiler_params=_cparams(("arbitrary",)),
        name="outproj_ln_router",
    )(*a_pair, *b_pair, w_out, *x_pair, mod, ln_g, ln_b, w_router)


def _topk_kernel(aff_ref, pos_ref, cnt_ref):
    aff = aff_ref[...]
    cap = jnp.float32(CAP)

    def count_ge(t):
        return jnp.sum(jnp.where(aff >= t, 1.0, 0.0), axis=1, keepdims=True)

    lo_v = jnp.zeros((N_EXPERTS, 1), F32)
    hi_v = jnp.full((N_EXPERTS, 1), 2.0, F32)
    lo_t = jnp.full((N_EXPERTS, 1), TOPK_MIN_EXP, F32)
    hi_t = jnp.full((N_EXPERTS, 1), 1.0, F32)
    for _ in range(TOPK_GEO_STEPS):
        mid_t = 0.5 * (lo_t + hi_t)
        cand = jnp.exp2(mid_t)
        ok = count_ge(cand) >= cap
        lo_t = jnp.where(ok, mid_t, lo_t)
        hi_t = jnp.where(ok, hi_t, mid_t)
        lo_v = jnp.where(ok, jnp.maximum(lo_v, cand), lo_v)
        hi_v = jnp.where(ok, hi_v, jnp.minimum(hi_v, cand))
    for _ in range(TOPK_LIN_STEPS):
        cand = lo_v + 0.5 * (hi_v - lo_v)
        ok = count_ge(cand) >= cap
        lo_v = jnp.where(ok, cand, lo_v)
        hi_v = jnp.where(ok, hi_v, cand)
    gt = aff >= hi_v
    eq = (aff >= lo_v) & (aff < hi_v)
    need = cap - jnp.sum(jnp.where(gt, 1.0, 0.0), axis=1, keepdims=True)

    tri = (lax.broadcasted_iota(I32, (LANE, LANE), 0) <= lax.broadcasted_iota(I32, (LANE, LANE), 1))
    tri = jnp.where(tri, 1.0, 0.0).astype(BF16)

    def cumsum_blocks(mask_f32):
        carry = jnp.zeros((N_EXPERTS, 1), F32)
        outs = []
        for k in range(mask_f32.shape[1] // LANE):
            blk = mask_f32[:, k * LANE:(k + 1) * LANE]
            inc = _dot(blk.astype(BF16), tri) + carry
            outs.append(inc)
            carry = carry + jnp.sum(blk, axis=1, keepdims=True)
        return outs

    eq_f = jnp.where(eq, 1.0, 0.0)
    gt_f = jnp.where(gt, 1.0, 0.0)
    eq_rank = cumsum_blocks(eq_f)
    sel_blocks = []
    for k, rk in enumerate(eq_rank):
        lanes = slice(k * LANE, (k + 1) * LANE)
        sel_blocks.append(jnp.where((eq_f[:, lanes] > 0.0) & (rk <= need), 1.0, gt_f[:, lanes]))
    sel = jnp.concatenate(sel_blocks, axis=1)
    sel_rank = cumsum_blocks(sel)
    for k, rk in enumerate(sel_rank):
        lanes = slice(k * LANE, (k + 1) * LANE)
        pos_ref[:, lanes] = jnp.where(sel[:, lanes] > 0.0, rk.astype(I32) - 1, -1)
    lane = lax.broadcasted_iota(I32, (N_EXPERTS, LANE), 1)
    cnt = jnp.zeros((N_EXPERTS, LANE), F32)
    for c in range(1, N_CHUNKS + 1):
        cnt = jnp.where(lane == c, sel_rank[c * (GC // LANE) - 1][:, LANE - 1:LANE], cnt)
    cnt_ref[0] = cnt.astype(I32)


def _topk(aff_t):
    return pl.pallas_call(
        _topk_kernel,
        grid=(2,),
        in_specs=[pl.BlockSpec((N_EXPERTS, N_CTX), lambda g: (0, g))],
        out_specs=[
            pl.BlockSpec((N_EXPERTS, N_CTX), lambda g: (0, g)),
            pl.BlockSpec((1, N_EXPERTS, LANE), lambda g: (g, 0, 0)),
        ],
        out_shape=[
            jax.ShapeDtypeStruct((N_EXPERTS, N_TOK), I32),
            jax.ShapeDtypeStruct((2, N_EXPERTS, LANE), I32),
        ],
        compiler_params=_cparams(("arbitrary",)),
        name="topk_select",
    )(aff_t)


def _chunk_windows(cnt_ref, g, e, c):
    c0 = cnt_ref[g, e, c]
    c1 = cnt_ref[g, e, c + 1]
    base = (c0 // LANE) * LANE
    k = jnp.where(c1 > c0, (c1 - base + WIN - 1) // WIN, 0)
    return base, k


def _gather_kernel(cnt_ref, pos_ref, aff_ref, u_ref, xe_ref, gs_ref, acc_s, gacc_s):
    g = pl.program_id(0)
    e = pl.program_id(1)
    acc_s[...] = jnp.zeros_like(acc_s)
    gacc_s[...] = jnp.zeros_like(gacc_s)
    prow = pos_ref[pl.ds(e, 1), :]
    arow = aff_ref[pl.ds(e, 1), :]
    slot0 = lax.broadcasted_iota(I32, (WIN, GC), 0)

    def add_window(c, lo):
        cols = slice(c * GC, (c + 1) * GC)
        pc = prow[:, cols]
        start = pl.multiple_of(jnp.minimum(lo, CAP - WIN), LANE)
        hit = (slot0 + start) == jnp.where(pc >= lo, pc, -1)
        rows = pl.ds(start, WIN)
        acc_s[rows, :] += _dot(jnp.where(hit, 1.0, 0.0).astype(BF16), u_ref[cols, :])
        gacc_s[rows, :] += jnp.sum(jnp.where(hit, arow[:, cols], 0.0), axis=1, keepdims=True)

    windows = [_chunk_windows(cnt_ref, g, e, c) for c in range(N_CHUNKS)]
    for c, (base, _) in enumerate(windows):
        add_window(c, base)
    for c, (base, k) in enumerate(windows):
        def extra(w, carry, c=c, base=base):
            add_window(c, base + w * WIN)
            return carry
        lax.fori_loop(1, k, extra, 0)
    xe_ref[0, 0] = acc_s[...].astype(BF16)
    gs_ref[0, 0] = gacc_s[...]


def _gather(cnt, pos_t, aff_t, u2):
    return pl.pallas_call(
        _gather_kernel,
        grid_spec=pltpu.PrefetchScalarGridSpec(
            num_scalar_prefetch=1,
            grid=(2, N_EXPERTS),
            in_specs=[
                pl.BlockSpec((N_EXPERTS, N_CTX), lambda g, e, cnt: (0, g)),
                pl.BlockSpec((N_EXPERTS, N_CTX), lambda g, e, cnt: (0, g)),
                pl.BlockSpec((N_CTX, D_MODEL), lambda g, e, cnt: (g, 0)),
            ],
            out_specs=[
                pl.BlockSpec((1, 1, CAP, D_MODEL), lambda g, e, cnt: (e, g, 0, 0)),
                pl.BlockSpec((1, 1, CAP, 1), lambda g, e, cnt: (e, g, 0, 0)),
            ],
            scratch_shapes=[pltpu.VMEM((CAP, D_MODEL), F32), pltpu.VMEM((CAP, 1), F32)],
        ),
        out_shape=[
            jax.ShapeDtypeStruct((N_EXPERTS, 2, CAP, D_MODEL), BF16),
            jax.ShapeDtypeStruct((N_EXPERTS, 2, CAP, 1), F32),
        ],
        compiler_params=_cparams(("arbitrary", "arbitrary")),
        name="moe_gather",
    )(cnt, pos_t, aff_t, u2)


TF = 512


def _ffn_kernel(xe_ref, gs_ref, wg_ref, wu_ref, wd_ref, ye_ref, acc_ref):
    f = pl.program_id(1)
    x = xe_ref[0]
    hg = _dot(x, wg_ref[0, 0].astype(BF16))
    hu = _dot(x, wu_ref[0, 0].astype(BF16))
    h = (hg * jax.nn.sigmoid(hg) * hu).astype(BF16)
    d = _dot(h, wd_ref[0, 0].astype(BF16))

    @pl.when(f == 0)
    def _():
        acc_ref[...] = d

    @pl.when(f > 0)
    def _():
        acc_ref[...] += d

    @pl.when(f == pl.num_programs(1) - 1)
    def _():
        ye_ref[0] = (acc_ref[...] * gs_ref[0]).astype(BF16)


def _ffn(xe, gs, layer, w_gate, w_up, w_down):
    rows = 2 * CAP
    return pl.pallas_call(
        _ffn_kernel,
        grid=(N_EXPERTS, D_EXPERT // TF),
        in_specs=[
            pl.BlockSpec((1, rows, D_MODEL), lambda e, f: (e, 0, 0)),
            pl.BlockSpec((1, rows, 1), lambda e, f: (e, 0, 0)),
            pl.BlockSpec((1, 1, D_MODEL, TF), lambda e, f: (layer, e, 0, f)),
            pl.BlockSpec((1, 1, D_MODEL, TF), lambda e, f: (layer, e, 0, f)),
            pl.BlockSpec((1, 1, TF, D_MODEL), lambda e, f: (layer, e, f, 0)),
        ],
        out_specs=pl.BlockSpec((1, rows, D_MODEL), lambda e, f: (e, 0, 0)),
        out_shape=jax.ShapeDtypeStruct((N_EXPERTS, rows, D_MODEL), BF16),
        scratch_shapes=[pltpu.VMEM((rows, D_MODEL), F32)],
        compiler_params=_cparams(("arbitrary", "arbitrary")),
        name="moe_ffn",
    )(xe, gs, w_gate, w_up, w_down)


def _combine_kernel(*refs, split_out):
    if split_out:
        cnt_ref, posn_ref, ye_ref, x1_ref, mod_ref, g_ref, be_ref, oc_ref, ol_ref, acc_s = refs
    else:
        cnt_ref, posn_ref, ye_ref, x1_ref, mod_ref, g_ref, be_ref, o_ref, acc_s = refs
    i = pl.program_id(0)
    g = i // N_CHUNKS
    c = i % N_CHUNKS
    posn = posn_ref[...]
    slot0 = lax.broadcasted_iota(I32, (TM, WIN), 1)

    def window(e, lo):
        col = posn[:, e:e + 1]
        start = pl.multiple_of(jnp.minimum(lo, CAP - WIN), LANE)
        hit = (slot0 + start) == jnp.where(col >= lo, col, -1)
        return _dot(jnp.where(hit, 1.0, 0.0).astype(BF16), ye_ref[e, 0, pl.ds(start, WIN), :])

    windows = [_chunk_windows(cnt_ref, g, e, c) for e in range(N_EXPERTS)]
    acc = window(0, windows[0][0])
    for e in range(1, N_EXPERTS):
        acc = acc + window(e, windows[e][0])
    acc_s[...] = acc
    for e, (base, k) in enumerate(windows):
        def extra(w, carry, e=e, base=base):
            acc_s[...] += window(e, base + w * WIN)
            return carry
        lax.fori_loop(1, k, extra, 0)
    m = mod_ref[0, 0]
    y = ALPHA * x1_ref[...] + m[5:6, :] * acc_s[...]
    res = _layernorm(y, g_ref[0, 1:2, :], be_ref[0, 1:2, :])
    if split_out:
        @pl.when(i < CTX_TILES)
        def _():
            oc_ref[...] = res

        @pl.when(i >= CTX_TILES)
        def _():
            ol_ref[...] = res
    else:
        o_ref[...] = res


def _combine(cnt, pos_n, ye, x1, mod, layer, ln_g, ln_b, split_out):
    if split_out:
        out_specs = _pair_specs(D_MODEL)
        out_shape = [jax.ShapeDtypeStruct((N_CTX, D_MODEL), F32), jax.ShapeDtypeStruct((N_LAT, D_MODEL), F32)]
    else:
        out_specs = pl.BlockSpec((TM, D_MODEL), lambda i, cnt: (i, 0))
        out_shape = jax.ShapeDtypeStruct((N_TOK, D_MODEL), F32)
    return pl.pallas_call(
        functools.partial(_combine_kernel, split_out=split_out),
        grid_spec=pltpu.PrefetchScalarGridSpec(
            num_scalar_prefetch=1,
            grid=(N_TILES,),
            in_specs=[
                pl.BlockSpec((TM, N_EXPERTS), lambda i, cnt: (i, 0)),
                pl.BlockSpec((N_EXPERTS, 1, CAP, D_MODEL), lambda i, cnt: (0, i // N_CHUNKS, 0, 0)),
                pl.BlockSpec((TM, D_MODEL), lambda i, cnt: (i, 0)),
                pl.BlockSpec((1, 1, 6, D_MODEL), lambda i, cnt: (layer, _seg_of_tile(i), 0, 0)),
                pl.BlockSpec((1, 2, D_MODEL), lambda i, cnt: (layer, 0, 0)),
                pl.BlockSpec((1, 2, D_MODEL), lambda i, cnt: (layer, 0, 0)),
            ],
            out_specs=out_specs,
            scratch_shapes=[pltpu.VMEM((TM, D_MODEL), F32)],
        ),
        out_shape=out_shape,
        compiler_params=_cparams(("arbitrary",)),
        name="moe_combine_ln",
    )(cnt, pos_n, ye, x1, mod, ln_g, ln_b)


def _moe(x1, u2, aff, mod, layer, ln_g, ln_b, w_gate, w_up, w_down, split_out=False):
    aff_t = aff.T
    pos_t, cnt = _topk(aff_t)
    xe, gs = _gather(cnt, pos_t, aff_t, u2)
    ye = _ffn(xe.reshape(N_EXPERTS, 2 * CAP, D_MODEL), gs.reshape(N_EXPERTS, 2 * CAP, 1),
              layer, w_gate, w_up, w_down)
    return _combine(cnt, pos_t.T, ye.reshape(N_EXPERTS, 2, CAP, D_MODEL), x1, mod, layer, ln_g, ln_b, split_out)


def _sink_softmax_pv(s, sink_col, v):
    m = jnp.maximum(jnp.max(s, axis=-1, keepdims=True), sink_col)
    p = jnp.exp(s - m)
    den = jnp.sum(p, axis=-1, keepdims=True) + jnp.exp(sink_col - m)
    return _dot(p.astype(BF16), v) / den


def _sink_column(sink_ref, h, rows_per_group):
    rid = lax.broadcasted_iota(I32, (GQA_GROUP * rows_per_group, 1), 0) // rows_per_group
    col = jnp.zeros((GQA_GROUP * rows_per_group, 1), F32)
    for g in range(GQA_GROUP):
        col = jnp.where(rid == g, sink_ref[h * GQA_GROUP + g], col)
    return col


def _stack_groups(q, h):
    return jnp.concatenate(
        [q[:, (h * GQA_GROUP + g) * HEAD_DIM:(h * GQA_GROUP + g + 1) * HEAD_DIM] for g in range(GQA_GROUP)],
        axis=0).astype(BF16)


def _unstack_groups(outs, rows):
    return jnp.concatenate([o[g * rows:(g + 1) * rows] for o in outs for g in range(GQA_GROUP)], axis=1)


def _swa_ctx_kernel(sink_ref, q_ref, k_ref, v_ref, o_ref):
    q, k, v = q_ref[...], k_ref[...], v_ref[...]
    outs = []
    for h in range(SWA_KV_HEADS):
        cols = slice(h * HEAD_DIM, (h + 1) * HEAD_DIM)
        s = _dot_nt(_stack_groups(q, h), k[:, cols].astype(BF16)) * ATTN_SCALE
        outs.append(_sink_softmax_pv(s, _sink_column(sink_ref, h, SEQ), v[:, cols].astype(BF16)))
    o_ref[...] = _unstack_groups(outs, SEQ).astype(BF16)


def _swa_ctx(proj, sink):
    return pl.pallas_call(
        _swa_ctx_kernel,
        grid=(BATCH,),
        in_specs=[
            pl.BlockSpec(memory_space=pltpu.SMEM),
            pl.BlockSpec((SEQ, 512), lambda b: (b, 0)),
            pl.BlockSpec((SEQ, LANE), lambda b: (b, 4)),
            pl.BlockSpec((SEQ, LANE), lambda b: (b, 5)),
        ],
        out_specs=pl.BlockSpec((SEQ, 512), lambda b: (b, 0)),
        out_shape=jax.ShapeDtypeStruct((N_CTX, 512), BF16),
        compiler_params=_cparams(("arbitrary",)),
        name="swa_context",
    )(sink, proj, proj, proj)


QB = 128


def _swa_lat_kernel(sink_ref, q_ref, kp_ref, kc_ref, kn_ref, vp_ref, vc_ref, vn_ref, ck_ref, cv_ref, o_ref):
    qb = pl.program_id(1)
    q = q_ref[...]
    kall = jnp.concatenate([ck_ref[0], kp_ref[...], kc_ref[...], kn_ref[...]], axis=0).astype(BF16)
    vall = jnp.concatenate([cv_ref[0], vp_ref[...], vc_ref[...], vn_ref[...]], axis=0).astype(BF16)
    nk = PAST_LEN + 3 * QB
    rows = GQA_GROUP * QB
    col = lax.broadcasted_iota(I32, (rows, nk), 1)
    qpos = qb * QB + lax.broadcasted_iota(I32, (rows, nk), 0) % QB
    kpos = (qb - 1) * QB + col - PAST_LEN
    ok = (col < PAST_LEN) | ((jnp.abs(qpos - kpos) <= WINDOW) & (kpos >= 0) & (kpos < DEC_SEQ))
    outs = []
    for h in range(SWA_KV_HEADS):
        cols = slice(h * HEAD_DIM, (h + 1) * HEAD_DIM)
        s = _dot_nt(_stack_groups(q, h), kall[:, cols]) * ATTN_SCALE
        s = jnp.where(ok, s, NEG_INF)
        outs.append(_sink_softmax_pv(s, _sink_column(sink_ref, h, QB), vall[:, cols]))
    o_ref[...] = _unstack_groups(outs, QB).astype(BF16)


def _swa_lat(proj, cache_k, cache_v, sink):
    nqb = DEC_SEQ // QB
    row0 = N_CTX // QB

    def blk(col, shift):
        return pl.BlockSpec((QB, LANE), lambda b, i: (row0 + b * nqb + jnp.clip(i + shift, 0, nqb - 1), col))

    return pl.pallas_call(
        _swa_lat_kernel,
        grid=(DEC_BATCH, nqb),
        in_specs=[
            pl.BlockSpec(memory_space=pltpu.SMEM),
            pl.BlockSpec((QB, 512), lambda b, i: (row0 + b * nqb + i, 0)),
            blk(4, -1), blk(4, 0), blk(4, 1),
            blk(5, -1), blk(5, 0), blk(5, 1),
            pl.BlockSpec((1, PAST_LEN, LANE), lambda b, i: (b, 0, 0)),
            pl.BlockSpec((1, PAST_LEN, LANE), lambda b, i: (b, 0, 0)),
        ],
        out_specs=pl.BlockSpec((QB, 512), lambda b, i: (b * nqb + i, 0)),
        out_shape=jax.ShapeDtypeStruct((N_LAT, 512), BF16),
        compiler_params=_cparams(("arbitrary", "arbitrary")),
        name="swa_latent",
    )(sink, proj, proj, proj, proj, proj, proj, proj, cache_k, cache_v)


def _diff_lambda_col(lam_ref, lam_init):
    l1 = jnp.sum(lam_ref[0, 0] * lam_ref[0, 1], axis=-1, keepdims=True)
    l2 = jnp.sum(lam_ref[0, 2] * lam_ref[0, 3], axis=-1, keepdims=True)
    return jnp.exp(l1) - jnp.exp(l2) + lam_init


def _softmax_rows(s):
    p = jnp.exp(s - jnp.max(s, axis=-1, keepdims=True))
    return p / jnp.sum(p, axis=-1, keepdims=True)


def _diff_head(q, k, v, lam, subln, lam_init):
    a = [_softmax_rows(_dot_nt(q[:, m * HEAD_DIM:(m + 1) * HEAD_DIM], k[:, m * HEAD_DIM:(m + 1) * HEAD_DIM])
                       * ATTN_SCALE) for m in range(2)]
    w = a[0] - lam * a[1]
    o = _dot(w.astype(BF16), v)
    return o * lax.rsqrt(jnp.mean(o * o, axis=-1, keepdims=True) + LN_EPS) * subln * (1.0 - lam_init)


def _diff_kernel(*refs, lam_init, has_cache):
    if has_cache:
        lam_ref, sg_ref, q_ref, k_ref, v_ref, ck_ref, cv_ref, o_ref = refs
    else:
        lam_ref, sg_ref, q_ref, k_ref, v_ref, o_ref = refs
    h = pl.program_id(1)
    lam_all = _diff_lambda_col(lam_ref, lam_init)
    hid = lax.broadcasted_iota(I32, (DIFF_HEADS, 1), 0)
    lam = jnp.sum(jnp.where(hid == h, lam_all, 0.0), axis=0, keepdims=True)
    k, v = k_ref[...], v_ref[...]
    if has_cache:
        k = jnp.concatenate([k, ck_ref[0]], axis=0)
        v = jnp.concatenate([v, cv_ref[0]], axis=0)
    o_ref[...] = _diff_head(q_ref[...].astype(BF16), k.astype(BF16), v.astype(BF16), lam, sg_ref[...],
                            lam_init).astype(BF16)


def _diff_ctx(proj, lam_p, subln, j, lam_init):
    return pl.pallas_call(
        functools.partial(_diff_kernel, lam_init=lam_init, has_cache=False),
        grid=(BATCH, DIFF_HEADS),
        in_specs=[
            pl.BlockSpec((1, 4, DIFF_HEADS, HEAD_DIM), lambda b, h: (j, 0, 0, 0)),
            pl.BlockSpec((1, DIFF_V_DIM), lambda b, h: (j, 0)),
            pl.BlockSpec((SEQ, LANE), lambda b, h: (b, 6 + h)),
            pl.BlockSpec((SEQ, LANE), lambda b, h: (b, 10 + h)),
            pl.BlockSpec((SEQ, LANE), lambda b, h: (b, 14 + h)),
        ],
        out_specs=pl.BlockSpec((SEQ, LANE), lambda b, h: (b, h)),
        out_shape=jax.ShapeDtypeStruct((N_CTX, 512), BF16),
        compiler_params=_cparams(("arbitrary", "arbitrary")),
        name="diff_context",
    )(lam_p, subln, proj, proj, proj)


DQ = 256


def _diff_lat(proj, cache_k, cache_v, lam_p, subln, j, lam_init):
    nq = DEC_SEQ // DQ
    row0 = N_CTX // DQ
    seq0 = N_CTX // DEC_SEQ
    return pl.pallas_call(
        functools.partial(_diff_kernel, lam_init=lam_init, has_cache=True),
        grid=(DEC_BATCH, DIFF_HEADS, nq),
        in_specs=[
            pl.BlockSpec((1, 4, DIFF_HEADS, HEAD_DIM), lambda b, h, i: (j, 0, 0, 0)),
            pl.BlockSpec((1, DIFF_V_DIM), lambda b, h, i: (j, 0)),
            pl.BlockSpec((DQ, LANE), lambda b, h, i: (row0 + b * nq + i, 6 + h)),
            pl.BlockSpec((DEC_SEQ, LANE), lambda b, h, i: (seq0 + b, 10 + h)),
            pl.BlockSpec((DEC_SEQ, LANE), lambda b, h, i: (seq0 + b, 14 + h)),
            pl.BlockSpec((1, PAST_LEN, LANE), lambda b, h, i: (b, 0, h)),
            pl.BlockSpec((1, PAST_LEN, LANE), lambda b, h, i: (b, 0, h)),
        ],
        out_specs=pl.BlockSpec((DQ, LANE), lambda b, h, i: (b * nq + i, h)),
        out_shape=jax.ShapeDtypeStruct((N_LAT, 512), BF16),
        compiler_params=_cparams(("arbitrary", "arbitrary", "arbitrary")),
        name="diff_latent",
    )(lam_p, subln, proj, proj, proj, cache_k, cache_v)


def kernel(x_prompt, x_sample, state_rglru, cache_swa_k, cache_swa_v, cache_diff_k, cache_diff_v, c, c_ctx,
           w_mod, b_mod, ln_g, ln_b, e_w_in, e_conv_w, e_conv_b, e_w_rgate, e_b_rgate, e_w_igate, e_b_igate,
           e_lambda, e_w_fnet, e_w_out, o_w_in, o_sink, o_lambda, o_subln_g, o_w_out,
           w_router, w_gate, w_up, w_down):
    x_pair = (x_prompt.reshape(N_CTX, D_MODEL), x_sample.reshape(N_LAT, D_MODEL))
    cvec8 = jnp.concatenate([c_ctx[None, :], c, jnp.zeros((SUBLANE - N_SEG, D_MODEL), F32)], axis=0)
    mod = _modulation(cvec8, w_mod, b_mod).reshape(DEPTH, SUBLANE, 6, D_MODEL)

    j = 0
    proj = _inproj(x_pair, mod, 0, e_w_in, j)
    wg = _gate_tiles(e_w_rgate, e_w_igate)
    cb = e_conv_b.reshape(-1, 1, D_RG)
    rg_args = (j, e_conv_w, cb, wg, e_b_rgate, e_b_igate, e_lambda)
    y_rg_c, st_c = _rglru(proj, jnp.zeros((BATCH, 2, D_RG), F32), *rg_args,
                          seq_len=SEQ, nseq=BATCH, row_block0=0)
    y_rg_l, _ = _rglru(proj, state_rglru[:, j], *rg_args,
                       seq_len=DEC_SEQ, nseq=DEC_BATCH, row_block0=N_CTX // DEC_SEQ)
    wbd = _fnet_blockdiag(e_w_fnet)
    y_fn_c = _fnet(proj, j, wbd, seq_len=SEQ, nseq=BATCH, row_block0=0)
    y_fn_l = _fnet(proj, j, wbd, seq_len=DEC_SEQ, nseq=DEC_BATCH, row_block0=N_CTX // DEC_SEQ)
    x1, u2, aff = _outproj((y_rg_c, y_rg_l), (y_fn_c, y_fn_l), e_w_out, j, x_pair, mod, 0, ln_g, ln_b, w_router)
    x = _moe(x1, u2, aff, mod, 0, ln_g, ln_b, w_gate, w_up, w_down)
    new_state_rglru = st_c[:, None]

    layer = 1
    lam_init = 0.8 - 0.6 * math.exp(-0.3 * layer)
    cos, sin = _rope_tables()
    rope_groups = (0, 1, 2, 3, 4, 6, 7, 8, 9, 10, 11, 12, 13)
    proj = _inproj((x, x), mod, 1, o_w_in, j, rope=(cos, sin, rope_groups))
    sink = o_sink[j]
    ys_c = _swa_ctx(proj, sink)
    yd_c = _diff_ctx(proj, o_lambda, o_subln_g, j, lam_init)
    ck = cache_swa_k[:, j].reshape(DEC_BATCH, PAST_LEN, SWA_KV_HEADS * HEAD_DIM)
    cv = cache_swa_v[:, j].reshape(DEC_BATCH, PAST_LEN, SWA_KV_HEADS * HEAD_DIM)
    ys_l = _swa_lat(proj, ck, cv, sink)
    cdk = cache_diff_k[:, j].reshape(DEC_BATCH, PAST_LEN, DIFF_HEADS * 2 * HEAD_DIM)
    cdv = cache_diff_v[:, j].reshape(DEC_BATCH, PAST_LEN, DIFF_HEADS * DIFF_V_DIM)
    yd_l = _diff_lat(proj, cdk, cdv, o_lambda, o_subln_g, j, lam_init)
    x1, u2, aff = _outproj((ys_c, ys_l), (yd_c, yd_l), o_w_out, j, (x, x), mod, 1, ln_g, ln_b, w_router)
    y_c, y_l = _moe(x1, u2, aff, mod, 1, ln_g, ln_b, w_gate, w_up, w_down, split_out=True)

    pc = proj[:N_CTX]
    new_cache_swa_k = pc[:, 512:640].reshape(BATCH, 1, SEQ, SWA_KV_HEADS, HEAD_DIM)
    new_cache_swa_v = pc[:, 640:768].reshape(BATCH, 1, SEQ, SWA_KV_HEADS, HEAD_DIM)
    new_cache_diff_k = pc[:, 1280:1792].reshape(BATCH, 1, SEQ, DIFF_HEADS, 2, HEAD_DIM)
    new_cache_diff_v = pc[:, 1792:2304].reshape(BATCH, 1, SEQ, DIFF_HEADS, DIFF_V_DIM)
    y_prompt = y_c.reshape(BATCH, SEQ, D_MODEL)
    y_sample = y_l.reshape(DEC_BATCH, DEC_SEQ, D_MODEL)
    return (y_prompt, y_sample, new_state_rglru, new_cache_swa_k, new_cache_swa_v,
            new_cache_diff_k, new_cache_diff_v)
```

```python
import functools
import math

import numpy as np
import jax
import jax.numpy as jnp
from jax import lax
from jax.experimental import pallas as pl
from jax.experimental.pallas import tpu as pltpu

F32 = jnp.float32
BF16 = jnp.bfloat16
I32 = jnp.int32

D_MODEL = 1024
BATCH, SEQ = 16, 256
DEC_BATCH, DEC_SEQ = 2, 2048
PAST_LEN = 256
DEPTH = 2
GRID_W = 64
HEAD_DIM = 64
D_RG = 768
RG_BLOCK = 64
RG_C = 8.0
CONV_W = 4
D_FNET = 256
FNET_GROUP_DIM = 64
FNET_GROUPS = 4
SWA_HEADS = 8
SWA_KV_HEADS = 2
GQA_GROUP = 4
WINDOW = 128
DIFF_HEADS = 4
DIFF_V_DIM = 128
N_EXPERTS = 16
EC_FACTOR = 2
D_EXPERT = 2048
ROPE_BASE = 10000.0
LN_EPS = 1e-5
NEG_INF = -1e30
ATTN_SCALE = HEAD_DIM ** -0.5
ALPHA = (2 * DEPTH) ** 0.25
EVEN_IN = 2 * D_RG + D_FNET
ODD_IN = 2304

N_CTX = BATCH * SEQ
N_LAT = DEC_BATCH * DEC_SEQ
N_TOK = N_CTX + N_LAT
N_SEG = 1 + DEC_BATCH
CAP = EC_FACTOR * N_CTX // N_EXPERTS
assert N_CTX == N_LAT

LANE = 128
SUBLANE = 8
MXU_DIM = 256
VMEM_LIMIT = 56 * 1024 * 1024
TM = 512
GC = TM
N_CHUNKS = N_CTX // GC
WIN = MXU_DIM
TOPK_MIN_EXP = -150.0
TOPK_GEO_STEPS = 24
TOPK_LIN_STEPS = 24
N_TILES = N_TOK // TM
CTX_TILES = N_CTX // TM
LAT_TILES_PER_SEQ = DEC_SEQ // TM


def _cparams(sem):
    return pltpu.CompilerParams(dimension_semantics=sem, vmem_limit_bytes=VMEM_LIMIT)


def _dot(a, b):
    return jnp.dot(a, b, preferred_element_type=F32)


def _dot_nt(a, b):
    return lax.dot_general(a, b, (((1,), (1,)), ((), ())), preferred_element_type=F32)


def _split(a):
    hi = a.astype(BF16)
    lo = (a - hi.astype(F32)).astype(BF16)
    return hi, lo


def _dot3(a, b):
    ah, al = _split(a)
    bh, bl = _split(b)
    return _dot(ah, bh) + (_dot(ah, bl) + _dot(al, bh))


def _layernorm(y, g, b):
    mu = jnp.mean(y, axis=-1, keepdims=True)
    d = y - mu
    var = jnp.mean(d * d, axis=-1, keepdims=True)
    return d * lax.rsqrt(var + LN_EPS) * g + b


def _pair_specs(width, lat_block0=0):
    return [
        pl.BlockSpec((TM, width), lambda i, *_: (jnp.minimum(i, CTX_TILES - 1), 0)),
        pl.BlockSpec((TM, width), lambda i, *_: (jnp.maximum(i - CTX_TILES, 0) + lat_block0, 0)),
    ]


def _pick(i, ctx_ref, lat_ref):
    return jnp.where(i < CTX_TILES, ctx_ref[...], lat_ref[...])


def _seg_of_tile(i):
    return jnp.where(i < CTX_TILES, 0, (i - CTX_TILES) // LAT_TILES_PER_SEQ + 1)


MOD_TN = 1536


def _mod_kernel(c_ref, w_ref, b_ref, o_ref):
    c = c_ref[...]
    s = c * jax.nn.sigmoid(c)
    o_ref[0] = _dot3(s, w_ref[0]) + b_ref[0]


def _modulation(cvec8, w_mod, b_mod):
    n = 6 * D_MODEL
    return pl.pallas_call(
        _mod_kernel,
        grid=(DEPTH, n // MOD_TN),
        in_specs=[
            pl.BlockSpec((SUBLANE, D_MODEL), lambda l, j: (0, 0)),
            pl.BlockSpec((1, D_MODEL, MOD_TN), lambda l, j: (l, 0, j)),
            pl.BlockSpec((1, 1, MOD_TN), lambda l, j: (l, 0, j)),
        ],
        out_specs=pl.BlockSpec((1, SUBLANE, MOD_TN), lambda l, j: (l, 0, j)),
        out_shape=jax.ShapeDtypeStruct((DEPTH, SUBLANE, n), F32),
        compiler_params=_cparams(("arbitrary", "arbitrary")),
        name="modulation",
    )(cvec8, w_mod, b_mod.reshape(DEPTH, 1, n))


def _rope_tile(p, cos, sin, rope_groups):
    lane = lax.broadcasted_iota(I32, (1, LANE), 1)
    first_half = (lane % 32) < 16
    pieces = []
    for k in range(p.shape[1] // LANE):
        xg = p[:, k * LANE:(k + 1) * LANE]
        if k in rope_groups:
            partner = jnp.where(first_half, pltpu.roll(xg, LANE - 16, 1), pltpu.roll(xg, 16, 1))
            xg = xg * cos + partner * sin
        pieces.append(xg)
    return jnp.concatenate(pieces, axis=1)


def _inproj_kernel(*refs, rope_groups):
    if rope_groups:
        xc_ref, xl_ref, mod_ref, w_ref, cos_ref, sin_ref, o_ref, wbf_ref = refs
    else:
        xc_ref, xl_ref, mod_ref, w_ref, o_ref, wbf_ref = refs
    i = pl.program_id(0)

    @pl.when(i == 0)
    def _():
        wbf_ref[...] = w_ref[0].astype(BF16)

    m = mod_ref[0, 0]
    u = _pick(i, xc_ref, xl_ref) * (1.0 + m[1:2, :]) + m[0:1, :]
    p = _dot(u.astype(BF16), wbf_ref[...])
    if rope_groups:
        @pl.when(i < CTX_TILES)
        def _():
            o_ref[...] = p

        @pl.when(i >= CTX_TILES)
        def _():
            o_ref[...] = _rope_tile(p, cos_ref[...], sin_ref[...], rope_groups)
    else:
        o_ref[...] = p


def _inproj(x_pair, mod, layer, w_in, j, rope=None):
    n = w_in.shape[-1]
    rope_groups = ()
    combined = x_pair[0] is x_pair[1]
    in_specs = _pair_specs(D_MODEL, CTX_TILES if combined else 0) + [
        pl.BlockSpec((1, 1, 6, D_MODEL), lambda i: (layer, _seg_of_tile(i), 0, 0)),
        pl.BlockSpec((1, D_MODEL, n), lambda i: (j, 0, 0)),
    ]
    args = [x_pair[0], x_pair[1], mod, w_in]
    if rope is not None:
        cos, sin, rope_groups = rope
        tab = pl.BlockSpec((TM, LANE), lambda i: (jnp.maximum(i - CTX_TILES, 0) % LAT_TILES_PER_SEQ, 0))
        in_specs += [tab, tab]
        args += [cos, sin]
    return pl.pallas_call(
        functools.partial(_inproj_kernel, rope_groups=rope_groups),
        grid=(N_TILES,),
        in_specs=in_specs,
        out_specs=pl.BlockSpec((TM, n), lambda i: (i, 0)),
        out_shape=jax.ShapeDtypeStruct((N_TOK, n), F32),
        scratch_shapes=[pltpu.VMEM((D_MODEL, n), BF16)],
        compiler_params=_cparams(("arbitrary",)),
        name="inproj_rope" if rope is not None else "inproj",
    )(*args)


def _rope_tables():
    s = np.arange(DEC_SEQ)
    row, col = (s // GRID_W).astype(np.float64), (s % GRID_W).astype(np.float64)
    nf = HEAD_DIM // 4
    inv = np.power(ROPE_BASE, -np.arange(nf, dtype=np.float64) / nf)
    ang = np.concatenate([row[:, None] * inv, row[:, None] * inv, col[:, None] * inv, col[:, None] * inv], axis=1)
    sign = np.concatenate([-np.ones(nf), np.ones(nf), -np.ones(nf), np.ones(nf)])
    cos = np.tile(np.cos(ang), (1, LANE // HEAD_DIM))
    sin = np.tile(np.sin(ang) * sign, (1, LANE // HEAD_DIM))
    return jnp.asarray(cos, F32), jnp.asarray(sin, F32)


RG_T = 256
RG_TILES = D_RG // MXU_DIM


def _sigmoid(x):
    return 0.5 * jnp.tanh(0.5 * x) + 0.5


def _softplus(x):
    u = jnp.exp(-jnp.abs(x))
    w = 1.0 + u
    l1p = jnp.where(w == 1.0, u, jnp.log(w) * (u / jnp.where(w == 1.0, 1.0, w - 1.0)))
    return jnp.maximum(x, 0.0) + l1p


def _rglru_kernel(xa_ref, ga_ref, h0_ref, cw_ref, cb_ref, wg_ref, br_ref, bi_ref, lam_ref,
                  y_ref, st_ref, hf_s, a_s, b_s, hs_s, *, seq_len):
    nchunk = seq_len // RG_T
    cw = cw_ref[0]
    cb = cb_ref[0]

    def conv_chunk(c):
        base = pl.multiple_of(c * RG_T, RG_T)
        cur = xa_ref[pl.ds(base, RG_T), :]
        prev = xa_ref[pl.ds(pl.multiple_of(jnp.maximum(base - SUBLANE, 0), SUBLANE), SUBLANE), :]
        prev = jnp.where(c > 0, prev, 0.0)
        nxt = xa_ref[pl.ds(pl.multiple_of(jnp.minimum(base + RG_T, seq_len - SUBLANE), SUBLANE), SUBLANE), :]
        nxt = jnp.where(c < nchunk - 1, nxt, 0.0)
        win = jnp.concatenate([prev, cur, nxt], axis=0)
        xc = cb
        for k in range(CONV_W):
            off = SUBLANE - 1 + k
            xc = xc + win[off:off + RG_T] * cw[k:k + 1, :]
        return xc

    def gates(xc, z):
        xb = xc.astype(BF16)
        rs, gs = [], []
        for t in range(RG_TILES):
            zz = _dot(xb[:, t * MXU_DIM:(t + 1) * MXU_DIM], wg_ref[0, z, t].astype(BF16))
            rs.append(zz[:, :MXU_DIM])
            gs.append(zz[:, MXU_DIM:])
        r = _sigmoid(jnp.concatenate(rs, axis=1) + br_ref[0, z:z + 1, :])
        g = _sigmoid(jnp.concatenate(gs, axis=1) + bi_ref[0, z:z + 1, :])
        log_a = (-RG_C * _softplus(-lam_ref[0, z:z + 1, :])) * r
        a = jnp.exp(log_a)
        bt = jnp.sqrt(-jnp.tanh(log_a) * (a * a + 1.0)) * g * xc
        return a, bt

    def scan_chunk(h, reverse):
        def step(t, h):
            tt = RG_T - 1 - t if reverse else t
            h = a_s[pl.ds(tt, 1), :] * h + b_s[pl.ds(tt, 1), :]
            hs_s[pl.ds(tt, 1), :] = h
            return h
        return lax.fori_loop(0, RG_T, step, h, unroll=8)

    def fwd_chunk(c, h):
        a, bt = gates(conv_chunk(c), 0)
        a_s[...] = a
        b_s[...] = bt
        h = scan_chunk(h, False)
        hf_s[pl.ds(pl.multiple_of(c * RG_T, RG_T), RG_T), :] = hs_s[...]
        return h

    def bwd_chunk(k, h):
        c = nchunk - 1 - k
        a, bt = gates(conv_chunk(c), 1)
        a_s[...] = a
        b_s[...] = bt
        h = scan_chunk(h, True)
        rows = pl.ds(pl.multiple_of(c * RG_T, RG_T), RG_T)
        y_ref[rows, :] = ((hf_s[rows, :] + hs_s[...]) * jax.nn.gelu(ga_ref[rows, :])).astype(BF16)
        return h

    hf = lax.fori_loop(0, nchunk, fwd_chunk, h0_ref[0, 0:1, :])
    hb = lax.fori_loop(0, nchunk, bwd_chunk, h0_ref[0, 1:2, :])
    st_ref[0, 0:1, :] = hf
    st_ref[0, 1:2, :] = hb


def _rglru(proj, h0, j, cw, cb, wg, br, bi, lam, *, seq_len, nseq, row_block0):
    wspec3 = lambda shape: pl.BlockSpec((1,) + shape, lambda b: (j,) + (0,) * len(shape))
    return pl.pallas_call(
        functools.partial(_rglru_kernel, seq_len=seq_len),
        grid=(nseq,),
        in_specs=[
            pl.BlockSpec((seq_len, D_RG), lambda b: (row_block0 + b, 0)),
            pl.BlockSpec((seq_len, D_RG), lambda b: (row_block0 + b, 1)),
            pl.BlockSpec((1, 2, D_RG), lambda b: (b, 0, 0)),
            wspec3((CONV_W, D_RG)),
            wspec3((1, D_RG)),
            wspec3((2, RG_TILES, MXU_DIM, 2 * MXU_DIM)),
            wspec3((2, D_RG)),
            wspec3((2, D_RG)),
            wspec3((2, D_RG)),
        ],
        out_specs=[
            pl.BlockSpec((seq_len, D_RG), lambda b: (b, 0)),
            pl.BlockSpec((1, 2, D_RG), lambda b: (b, 0, 0)),
        ],
        out_shape=[
            jax.ShapeDtypeStruct((nseq * seq_len, D_RG), BF16),
            jax.ShapeDtypeStruct((nseq, 2, D_RG), F32),
        ],
        scratch_shapes=[
            pltpu.VMEM((seq_len, D_RG), F32),
            pltpu.VMEM((RG_T, D_RG), F32),
            pltpu.VMEM((RG_T, D_RG), F32),
            pltpu.VMEM((RG_T, D_RG), F32),
        ],
        compiler_params=_cparams(("arbitrary",)),
        name=f"rglru_s{seq_len}",
    )(proj, proj, h0, cw, cb, wg, br, bi, lam)


def _gate_tiles(w_r, w_i):
    per_tile = MXU_DIM // RG_BLOCK
    eye = jnp.eye(per_tile, dtype=F32)

    def tiles(w):
        w = w.reshape(w.shape[0], 2, RG_TILES, per_tile, RG_BLOCK, RG_BLOCK)
        return jnp.einsum('nztbcd,be->nztbced', w, eye).reshape(w.shape[0], 2, RG_TILES, MXU_DIM, MXU_DIM)

    return jnp.concatenate([tiles(w_r), tiles(w_i)], axis=-1)


def _fnet_kernel(z_ref, cc_ref, sc_ref, w_ref, cs_ref, ns_ref, y_ref, p_s, q_s):
    @pl.when(pl.program_id(1) == 0)
    def _():
        w = w_ref[0]
        a = _dot3(cc_ref[...], w).astype(BF16)
        b = _dot3(sc_ref[...], w).astype(BF16)
        zb = z_ref[...].astype(BF16)
        p_s[...] = _dot(zb, a).astype(BF16)
        q_s[...] = _dot(zb, b).astype(BF16)

    y_ref[...] = (_dot(cs_ref[...].astype(BF16), p_s[...])
                  + _dot(ns_ref[...].astype(BF16), q_s[...])).astype(BF16)


def _fnet(proj, j, wbd, *, seq_len, nseq, row_block0):
    tr = min(seq_len, 512)
    nrt = seq_len // tr
    cc, sc = _dft_tables(FNET_GROUP_DIM)
    ccbd = jnp.asarray(np.kron(np.eye(FNET_GROUPS), cc), F32)
    scbd = jnp.asarray(np.kron(np.eye(FNET_GROUPS), sc), F32)
    cs, ss = _dft_tables(seq_len)
    cs_b, ns_b = jnp.asarray(cs, F32), jnp.asarray(-ss, F32)
    zcol = (2 * D_RG) // D_FNET
    const = lambda shape: pl.BlockSpec(shape, lambda b, r: (0,) * len(shape))
    return pl.pallas_call(
        _fnet_kernel,
        grid=(nseq, nrt),
        in_specs=[
            pl.BlockSpec((seq_len, D_FNET), lambda b, r: (row_block0 + b, zcol)),
            const((D_FNET, D_FNET)),
            const((D_FNET, D_FNET)),
            pl.BlockSpec((1, D_FNET, D_FNET), lambda b, r: (j, 0, 0)),
            pl.BlockSpec((tr, seq_len), lambda b, r: (r, 0)),
            pl.BlockSpec((tr, seq_len), lambda b, r: (r, 0)),
        ],
        out_specs=pl.BlockSpec((tr, D_FNET), lambda b, r: (b * nrt + r, 0)),
        out_shape=jax.ShapeDtypeStruct((nseq * seq_len, D_FNET), BF16),
        scratch_shapes=[pltpu.VMEM((seq_len, D_FNET), BF16), pltpu.VMEM((seq_len, D_FNET), BF16)],
        compiler_params=_cparams(("arbitrary", "arbitrary")),
        name=f"fnet_s{seq_len}",
    )(proj, ccbd, scbd, wbd, cs_b, ns_b)


def _dft_tables(n):
    k = np.arange(n)
    ang = 2.0 * np.pi * ((k[:, None] * k[None, :]) % n) / n
    return np.cos(ang) / np.sqrt(n), np.sin(ang) / np.sqrt(n)


def _fnet_blockdiag(w_f):
    eye = jnp.eye(FNET_GROUPS, dtype=F32)
    return jnp.einsum('ngcd,gh->ngchd', w_f, eye).reshape(w_f.shape[0], D_FNET, D_FNET)


def _outproj_kernel(ac_ref, al_ref, bc_ref, bl_ref, w_ref, xc_ref, xl_ref, mod_ref, g_ref, be_ref, wr_ref,
                    x1_ref, u2_ref, aff_ref, wbf_ref, *, ka):
    i = pl.program_id(0)

    @pl.when(i == 0)
    def _():
        wbf_ref[...] = w_ref[0].astype(BF16)

    out = _dot(_pick(i, ac_ref, al_ref), wbf_ref[0:ka, :]) + _dot(_pick(i, bc_ref, bl_ref), wbf_ref[ka:, :])
    m = mod_ref[0, 0]
    x1 = _layernorm(ALPHA * _pick(i, xc_ref, xl_ref) + m[2:3, :] * out, g_ref[0, 0:1, :], be_ref[0, 0:1, :])
    x1_ref[...] = x1
    u2 = x1 * (1.0 + m[4:5, :]) + m[3:4, :]
    u2_ref[...] = u2.astype(BF16)
    logits = _dot3(u2, wr_ref[0])
    mx = jnp.max(logits, axis=-1, keepdims=True)
    ex = jnp.exp(logits - mx)
    aff_ref[...] = ex / jnp.sum(ex, axis=-1, keepdims=True)


def _outproj(a_pair, b_pair, w_out, j, x_pair, mod, layer, ln_g, ln_b, w_router):
    ka, kb = a_pair[0].shape[1], b_pair[0].shape[1]
    combined = x_pair[0] is x_pair[1]
    return pl.pallas_call(
        functools.partial(_outproj_kernel, ka=ka),
        grid=(N_TILES,),
        in_specs=_pair_specs(ka) + _pair_specs(kb) + [
            pl.BlockSpec((1, D_MODEL, D_MODEL), lambda i: (j, 0, 0)),
        ] + _pair_specs(D_MODEL, CTX_TILES if combined else 0) + [
            pl.BlockSpec((1, 1, 6, D_MODEL), lambda i: (layer, _seg_of_tile(i), 0, 0)),
            pl.BlockSpec((1, 2, D_MODEL), lambda i: (layer, 0, 0)),
            pl.BlockSpec((1, 2, D_MODEL), lambda i: (layer, 0, 0)),
            pl.BlockSpec((1, D_MODEL, N_EXPERTS), lambda i: (layer, 0, 0)),
        ],
        out_specs=[
            pl.BlockSpec((TM, D_MODEL), lambda i: (i, 0)),
            pl.BlockSpec((TM, D_MODEL), lambda i: (i, 0)),
            pl.BlockSpec((TM, N_EXPERTS), lambda i: (i, 0)),
        ],
        out_shape=[
            jax.ShapeDtypeStruct((N_TOK, D_MODEL), F32),
            jax.ShapeDtypeStruct((N_TOK, D_MODEL), BF16),
            jax.ShapeDtypeStruct((N_TOK, N_EXPERTS), F32),
        ],
        scratch_shapes=[pltpu.VMEM((D_MODEL, D_MODEL), BF16)],
        compiler_params=_cparams(("arbitrary",)),
        name="outproj_ln_router",
    )(*a_pair, *b_pair, w_out, *x_pair, mod, ln_g, ln_b, w_router)


def _topk_kernel(aff_ref, pos_ref, cnt_ref):
    aff = aff_ref[...]
    cap = jnp.float32(CAP)

    def count_ge(t):
        return jnp.sum(jnp.where(aff >= t, 1.0, 0.0), axis=1, keepdims=True)

    lo_v = jnp.zeros((N_EXPERTS, 1), F32)
    hi_v = jnp.full((N_EXPERTS, 1), 2.0, F32)
    lo_t = jnp.full((N_EXPERTS, 1), TOPK_MIN_EXP, F32)
    hi_t = jnp.full((N_EXPERTS, 1), 1.0, F32)
    for _ in range(TOPK_GEO_STEPS):
        mid_t = 0.5 * (lo_t + hi_t)
        cand = jnp.exp2(mid_t)
        ok = count_ge(cand) >= cap
        lo_t = jnp.where(ok, mid_t, lo_t)
        hi_t = jnp.where(ok, hi_t, mid_t)
        lo_v = jnp.where(ok, jnp.maximum(lo_v, cand), lo_v)
        hi_v = jnp.where(ok, hi_v, jnp.minimum(hi_v, cand))
    for _ in range(TOPK_LIN_STEPS):
        cand = lo_v + 0.5 * (hi_v - lo_v)
        ok = count_ge(cand) >= cap
        lo_v = jnp.where(ok, cand, lo_v)
        hi_v = jnp.where(ok, hi_v, cand)
    gt = aff >= hi_v
    eq = (aff >= lo_v) & (aff < hi_v)
    need = cap - jnp.sum(jnp.where(gt, 1.0, 0.0), axis=1, keepdims=True)

    tri = (lax.broadcasted_iota(I32, (LANE, LANE), 0) <= lax.broadcasted_iota(I32, (LANE, LANE), 1))
    tri = jnp.where(tri, 1.0, 0.0).astype(BF16)

    def cumsum_blocks(mask_f32):
        carry = jnp.zeros((N_EXPERTS, 1), F32)
        outs = []
        for k in range(mask_f32.shape[1] // LANE):
            blk = mask_f32[:, k * LANE:(k + 1) * LANE]
            inc = _dot(blk.astype(BF16), tri) + carry
            outs.append(inc)
            carry = carry + jnp.sum(blk, axis=1, keepdims=True)
        return outs

    eq_f = jnp.where(eq, 1.0, 0.0)
    gt_f = jnp.where(gt, 1.0, 0.0)
    eq_rank = cumsum_blocks(eq_f)
    sel_blocks = []
    for k, rk in enumerate(eq_rank):
        lanes = slice(k * LANE, (k + 1) * LANE)
        sel_blocks.append(jnp.where((eq_f[:, lanes] > 0.0) & (rk <= need), 1.0, gt_f[:, lanes]))
    sel = jnp.concatenate(sel_blocks, axis=1)
    sel_rank = cumsum_blocks(sel)
    for k, rk in enumerate(sel_rank):
        lanes = slice(k * LANE, (k + 1) * LANE)
        pos_ref[:, lanes] = jnp.where(sel[:, lanes] > 0.0, rk.astype(I32) - 1, -1)
    lane = lax.broadcasted_iota(I32, (N_EXPERTS, LANE), 1)
    cnt = jnp.zeros((N_EXPERTS, LANE), F32)
    for c in range(1, N_CHUNKS + 1):
        cnt = jnp.where(lane == c, sel_rank[c * (GC // LANE) - 1][:, LANE - 1:LANE], cnt)
    cnt_ref[0] = cnt.astype(I32)


def _topk(aff_t):
    return pl.pallas_call(
        _topk_kernel,
        grid=(2,),
        in_specs=[pl.BlockSpec((N_EXPERTS, N_CTX), lambda g: (0, g))],
        out_specs=[
            pl.BlockSpec((N_EXPERTS, N_CTX), lambda g: (0, g)),
            pl.BlockSpec((1, N_EXPERTS, LANE), lambda g: (g, 0, 0)),
        ],
        out_shape=[
            jax.ShapeDtypeStruct((N_EXPERTS, N_TOK), I32),
            jax.ShapeDtypeStruct((2, N_EXPERTS, LANE), I32),
        ],
        compiler_params=_cparams(("arbitrary",)),
        name="topk_select",
    )(aff_t)


def _chunk_windows(cnt_ref, g, e, c):
    c0 = cnt_ref[g, e, c]
    c1 = cnt_ref[g, e, c + 1]
    base = (c0 // LANE) * LANE
    k = jnp.where(c1 > c0, (c1 - base + WIN - 1) // WIN, 0)
    return base, k


def _gather_kernel(cnt_ref, pos_ref, aff_ref, u_ref, xe_ref, gs_ref, acc_s, gacc_s):
    g = pl.program_id(0)
    e = pl.program_id(1)
    acc_s[...] = jnp.zeros_like(acc_s)
    gacc_s[...] = jnp.zeros_like(gacc_s)
    prow = pos_ref[pl.ds(e, 1), :]
    arow = aff_ref[pl.ds(e, 1), :]
    slot0 = lax.broadcasted_iota(I32, (WIN, GC), 0)

    def add_window(c, lo):
        cols = slice(c * GC, (c + 1) * GC)
        pc = prow[:, cols]
        start = pl.multiple_of(jnp.minimum(lo, CAP - WIN), LANE)
        hit = (slot0 + start) == jnp.where(pc >= lo, pc, -1)
        rows = pl.ds(start, WIN)
        acc_s[rows, :] += _dot(jnp.where(hit, 1.0, 0.0).astype(BF16), u_ref[cols, :])
        gacc_s[rows, :] += jnp.sum(jnp.where(hit, arow[:, cols], 0.0), axis=1, keepdims=True)

    windows = [_chunk_windows(cnt_ref, g, e, c) for c in range(N_CHUNKS)]
    for c, (base, _) in enumerate(windows):
        add_window(c, base)
    for c, (base, k) in enumerate(windows):
        def extra(w, carry, c=c, base=base):
            add_window(c, base + w * WIN)
            return carry
        lax.fori_loop(1, k, extra, 0)
    xe_ref[0, 0] = acc_s[...].astype(BF16)
    gs_ref[0, 0] = gacc_s[...]


def _gather(cnt, pos_t, aff_t, u2):
    return pl.pallas_call(
        _gather_kernel,
        grid_spec=pltpu.PrefetchScalarGridSpec(
            num_scalar_prefetch=1,
            grid=(2, N_EXPERTS),
            in_specs=[
                pl.BlockSpec((N_EXPERTS, N_CTX), lambda g, e, cnt: (0, g)),
                pl.BlockSpec((N_EXPERTS, N_CTX), lambda g, e, cnt: (0, g)),
                pl.BlockSpec((N_CTX, D_MODEL), lambda g, e, cnt: (g, 0)),
            ],
            out_specs=[
                pl.BlockSpec((1, 1, CAP, D_MODEL), lambda g, e, cnt: (e, g, 0, 0)),
                pl.BlockSpec((1, 1, CAP, 1), lambda g, e, cnt: (e, g, 0, 0)),
            ],
            scratch_shapes=[pltpu.VMEM((CAP, D_MODEL), F32), pltpu.VMEM((CAP, 1), F32)],
        ),
        out_shape=[
            jax.ShapeDtypeStruct((N_EXPERTS, 2, CAP, D_MODEL), BF16),
            jax.ShapeDtypeStruct((N_EXPERTS, 2, CAP, 1), F32),
        ],
        compiler_params=_cparams(("arbitrary", "arbitrary")),
        name="moe_gather",
    )(cnt, pos_t, aff_t, u2)


TF = 1024


def _ffn_kernel(xe_ref, gs_ref, wg_ref, wu_ref, wd_ref, ye_ref, acc_ref):
    f = pl.program_id(1)
    x = xe_ref[0]
    hg = _dot(x, wg_ref[0, 0].astype(BF16))
    hu = _dot(x, wu_ref[0, 0].astype(BF16))
    h = (hg * _sigmoid(hg) * hu).astype(BF16)

    @pl.when(f == 0)
    def _():
        acc_ref[...] = jnp.zeros_like(acc_ref)

    acc_ref[...] += _dot(h, wd_ref[0, 0].astype(BF16))

    @pl.when(f == pl.num_programs(1) - 1)
    def _():
        ye_ref[0] = (acc_ref[...] * gs_ref[0]).astype(BF16)


def _ffn(xe, gs, layer, w_gate, w_up, w_down):
    rows = 2 * CAP
    return pl.pallas_call(
        _ffn_kernel,
        grid=(N_EXPERTS, D_EXPERT // TF),
        in_specs=[
            pl.BlockSpec((1, rows, D_MODEL), lambda e, f: (e, 0, 0)),
            pl.BlockSpec((1, rows, 1), lambda e, f: (e, 0, 0)),
            pl.BlockSpec((1, 1, D_MODEL, TF), lambda e, f: (layer, e, 0, f)),
            pl.BlockSpec((1, 1, D_MODEL, TF), lambda e, f: (layer, e, 0, f)),
            pl.BlockSpec((1, 1, TF, D_MODEL), lambda e, f: (layer, e, f, 0)),
        ],
        out_specs=pl.BlockSpec((1, rows, D_MODEL), lambda e, f: (e, 0, 0)),
        out_shape=jax.ShapeDtypeStruct((N_EXPERTS, rows, D_MODEL), BF16),
        scratch_shapes=[pltpu.VMEM((rows, D_MODEL), F32)],
        compiler_params=_cparams(("arbitrary", "arbitrary")),
        name="moe_ffn",
    )(xe, gs, w_gate, w_up, w_down)


def _combine_kernel(*refs, split_out):
    if split_out:
        cnt_ref, posn_ref, ye_ref, x1_ref, mod_ref, g_ref, be_ref, oc_ref, ol_ref, acc_s = refs
    else:
        cnt_ref, posn_ref, ye_ref, x1_ref, mod_ref, g_ref, be_ref, o_ref, acc_s = refs
    i = pl.program_id(0)
    g = i // N_CHUNKS
    c = i % N_CHUNKS
    posn = posn_ref[...]
    slot0 = lax.broadcasted_iota(I32, (TM, WIN), 1)

    def window(e, lo):
        col = posn[:, e:e + 1]
        start = pl.multiple_of(jnp.minimum(lo, CAP - WIN), LANE)
        hit = (slot0 + start) == jnp.where(col >= lo, col, -1)
        return _dot(jnp.where(hit, 1.0, 0.0).astype(BF16), ye_ref[e, 0, pl.ds(start, WIN), :])

    windows = [_chunk_windows(cnt_ref, g, e, c) for e in range(N_EXPERTS)]
    acc = window(0, windows[0][0])
    for e in range(1, N_EXPERTS):
        acc = acc + window(e, windows[e][0])
    acc_s[...] = acc
    for e, (base, k) in enumerate(windows):
        def extra(w, carry, e=e, base=base):
            acc_s[...] += window(e, base + w * WIN)
            return carry
        lax.fori_loop(1, k, extra, 0)
    m = mod_ref[0, 0]
    y = ALPHA * x1_ref[...] + m[5:6, :] * acc_s[...]
    res = _layernorm(y, g_ref[0, 1:2, :], be_ref[0, 1:2, :])
    if split_out:
        @pl.when(i < CTX_TILES)
        def _():
            oc_ref[...] = res

        @pl.when(i >= CTX_TILES)
        def _():
            ol_ref[...] = res
    else:
        o_ref[...] = res


def _combine(cnt, pos_n, ye, x1, mod, layer, ln_g, ln_b, split_out):
    if split_out:
        out_specs = _pair_specs(D_MODEL)
        out_shape = [jax.ShapeDtypeStruct((N_CTX, D_MODEL), F32), jax.ShapeDtypeStruct((N_LAT, D_MODEL), F32)]
    else:
        out_specs = pl.BlockSpec((TM, D_MODEL), lambda i, cnt: (i, 0))
        out_shape = jax.ShapeDtypeStruct((N_TOK, D_MODEL), F32)
    return pl.pallas_call(
        functools.partial(_combine_kernel, split_out=split_out),
        grid_spec=pltpu.PrefetchScalarGridSpec(
            num_scalar_prefetch=1,
            grid=(N_TILES,),
            in_specs=[
                pl.BlockSpec((TM, N_EXPERTS), lambda i, cnt: (i, 0)),
                pl.BlockSpec((N_EXPERTS, 1, CAP, D_MODEL), lambda i, cnt: (0, i // N_CHUNKS, 0, 0)),
                pl.BlockSpec((TM, D_MODEL), lambda i, cnt: (i, 0)),
                pl.BlockSpec((1, 1, 6, D_MODEL), lambda i, cnt: (layer, _seg_of_tile(i), 0, 0)),
                pl.BlockSpec((1, 2, D_MODEL), lambda i, cnt: (layer, 0, 0)),
                pl.BlockSpec((1, 2, D_MODEL), lambda i, cnt: (layer, 0, 0)),
            ],
            out_specs=out_specs,
            scratch_shapes=[pltpu.VMEM((TM, D_MODEL), F32)],
        ),
        out_shape=out_shape,
        compiler_params=_cparams(("arbitrary",)),
        name="moe_combine_ln",
    )(cnt, pos_n, ye, x1, mod, ln_g, ln_b)


def _moe(x1, u2, aff, mod, layer, ln_g, ln_b, w_gate, w_up, w_down, split_out=False):
    aff_t = aff.T
    pos_t, cnt = _topk(aff_t)
    xe, gs = _gather(cnt, pos_t, aff_t, u2)
    ye = _ffn(xe.reshape(N_EXPERTS, 2 * CAP, D_MODEL), gs.reshape(N_EXPERTS, 2 * CAP, 1),
              layer, w_gate, w_up, w_down)
    return _combine(cnt, pos_t.T, ye.reshape(N_EXPERTS, 2, CAP, D_MODEL), x1, mod, layer, ln_g, ln_b, split_out)


def _sink_softmax_pv(s, sink_col, v):
    m = jnp.maximum(jnp.max(s, axis=-1, keepdims=True), sink_col)
    p = jnp.exp(s - m)
    den = jnp.sum(p, axis=-1, keepdims=True) + jnp.exp(sink_col - m)
    return _dot(p.astype(BF16), v) / den


def _sink_column(sink_ref, h, rows_per_group):
    rid = lax.broadcasted_iota(I32, (GQA_GROUP * rows_per_group, 1), 0) // rows_per_group
    col = jnp.zeros((GQA_GROUP * rows_per_group, 1), F32)
    for g in range(GQA_GROUP):
        col = jnp.where(rid == g, sink_ref[h * GQA_GROUP + g], col)
    return col


def _stack_groups(q, h):
    return jnp.concatenate(
        [q[:, (h * GQA_GROUP + g) * HEAD_DIM:(h * GQA_GROUP + g + 1) * HEAD_DIM] * ATTN_SCALE
         for g in range(GQA_GROUP)], axis=0).astype(BF16)


def _unstack_groups(outs, rows):
    return jnp.concatenate([o[g * rows:(g + 1) * rows] for o in outs for g in range(GQA_GROUP)], axis=1)


def _swa_ctx_kernel(sink_ref, q_ref, k_ref, v_ref, o_ref):
    q, k, v = q_ref[...], k_ref[...], v_ref[...]
    outs = []
    for h in range(SWA_KV_HEADS):
        cols = slice(h * HEAD_DIM, (h + 1) * HEAD_DIM)
        s = _dot_nt(_stack_groups(q, h), k[:, cols].astype(BF16))
        outs.append(_sink_softmax_pv(s, _sink_column(sink_ref, h, SEQ), v[:, cols].astype(BF16)))
    o_ref[...] = _unstack_groups(outs, SEQ).astype(BF16)


def _swa_ctx(proj, sink):
    return pl.pallas_call(
        _swa_ctx_kernel,
        grid=(BATCH,),
        in_specs=[
            pl.BlockSpec(memory_space=pltpu.SMEM),
            pl.BlockSpec((SEQ, 512), lambda b: (b, 0)),
            pl.BlockSpec((SEQ, LANE), lambda b: (b, 4)),
            pl.BlockSpec((SEQ, LANE), lambda b: (b, 5)),
        ],
        out_specs=pl.BlockSpec((SEQ, 512), lambda b: (b, 0)),
        out_shape=jax.ShapeDtypeStruct((N_CTX, 512), BF16),
        compiler_params=_cparams(("arbitrary",)),
        name="swa_context",
    )(sink, proj, proj, proj)


QB = 128


def _swa_lat_kernel(sink_ref, q_ref, kp_ref, kc_ref, kn_ref, vp_ref, vc_ref, vn_ref, ck_ref, cv_ref, o_ref):
    qb = pl.program_id(1)
    q = q_ref[...]
    kall = jnp.concatenate([ck_ref[0], kp_ref[...], kc_ref[...], kn_ref[...]], axis=0).astype(BF16)
    vall = jnp.concatenate([cv_ref[0], vp_ref[...], vc_ref[...], vn_ref[...]], axis=0).astype(BF16)
    nk = PAST_LEN + 3 * QB
    rows = GQA_GROUP * QB
    col = lax.broadcasted_iota(I32, (rows, nk), 1)
    qpos = qb * QB + lax.broadcasted_iota(I32, (rows, nk), 0) % QB
    kpos = (qb - 1) * QB + col - PAST_LEN
    ok = (col < PAST_LEN) | ((jnp.abs(qpos - kpos) <= WINDOW) & (kpos >= 0) & (kpos < DEC_SEQ))
    outs = []
    for h in range(SWA_KV_HEADS):
        cols = slice(h * HEAD_DIM, (h + 1) * HEAD_DIM)
        s = _dot_nt(_stack_groups(q, h), kall[:, cols])
        s = jnp.where(ok, s, NEG_INF)
        outs.append(_sink_softmax_pv(s, _sink_column(sink_ref, h, QB), vall[:, cols]))
    o_ref[...] = _unstack_groups(outs, QB).astype(BF16)


def _swa_lat(proj, cache_k, cache_v, sink):
    nqb = DEC_SEQ // QB
    row0 = N_CTX // QB

    def blk(col, shift):
        return pl.BlockSpec((QB, LANE), lambda b, i: (row0 + b * nqb + jnp.clip(i + shift, 0, nqb - 1), col))

    return pl.pallas_call(
        _swa_lat_kernel,
        grid=(DEC_BATCH, nqb),
        in_specs=[
            pl.BlockSpec(memory_space=pltpu.SMEM),
            pl.BlockSpec((QB, 512), lambda b, i: (row0 + b * nqb + i, 0)),
            blk(4, -1), blk(4, 0), blk(4, 1),
            blk(5, -1), blk(5, 0), blk(5, 1),
            pl.BlockSpec((1, PAST_LEN, LANE), lambda b, i: (b, 0, 0)),
            pl.BlockSpec((1, PAST_LEN, LANE), lambda b, i: (b, 0, 0)),
        ],
        out_specs=pl.BlockSpec((QB, 512), lambda b, i: (b * nqb + i, 0)),
        out_shape=jax.ShapeDtypeStruct((N_LAT, 512), BF16),
        compiler_params=_cparams(("arbitrary", "arbitrary")),
        name="swa_latent",
    )(sink, proj, proj, proj, proj, proj, proj, proj, cache_k, cache_v)


def _diff_lambda_col(lam_ref, lam_init):
    l1 = jnp.sum(lam_ref[0, 0] * lam_ref[0, 1], axis=-1, keepdims=True)
    l2 = jnp.sum(lam_ref[0, 2] * lam_ref[0, 3], axis=-1, keepdims=True)
    return jnp.exp(l1) - jnp.exp(l2) + lam_init


def _softmax_pv(q, k, v):
    s = _dot_nt(q, k)
    p = jnp.exp(s - jnp.max(s, axis=-1, keepdims=True))
    return _dot(p.astype(BF16), v) / jnp.sum(p, axis=-1, keepdims=True)


def _diff_head(q, k, v, lam, subln, lam_init):
    o = [_softmax_pv((q[:, m * HEAD_DIM:(m + 1) * HEAD_DIM] * ATTN_SCALE).astype(BF16),
                     k[:, m * HEAD_DIM:(m + 1) * HEAD_DIM], v) for m in range(2)]
    o = o[0] - lam * o[1]
    return o * lax.rsqrt(jnp.mean(o * o, axis=-1, keepdims=True) + LN_EPS) * subln * (1.0 - lam_init)


def _diff_kernel(*refs, lam_init, has_cache):
    if has_cache:
        lam_ref, sg_ref, q_ref, k_ref, v_ref, ck_ref, cv_ref, o_ref = refs
    else:
        lam_ref, sg_ref, q_ref, k_ref, v_ref, o_ref = refs
    h = pl.program_id(1)
    lam_all = _diff_lambda_col(lam_ref, lam_init)
    hid = lax.broadcasted_iota(I32, (DIFF_HEADS, 1), 0)
    lam = jnp.sum(jnp.where(hid == h, lam_all, 0.0), axis=0, keepdims=True)
    k, v = k_ref[...], v_ref[...]
    if has_cache:
        k = jnp.concatenate([k, ck_ref[0]], axis=0)
        v = jnp.concatenate([v, cv_ref[0]], axis=0)
    o_ref[...] = _diff_head(q_ref[...], k.astype(BF16), v.astype(BF16), lam, sg_ref[...],
                            lam_init).astype(BF16)


def _diff_ctx(proj, lam_p, subln, j, lam_init):
    return pl.pallas_call(
        functools.partial(_diff_kernel, lam_init=lam_init, has_cache=False),
        grid=(BATCH, DIFF_HEADS),
        in_specs=[
            pl.BlockSpec((1, 4, DIFF_HEADS, HEAD_DIM), lambda b, h: (j, 0, 0, 0)),
            pl.BlockSpec((1, DIFF_V_DIM), lambda b, h: (j, 0)),
            pl.BlockSpec((SEQ, LANE), lambda b, h: (b, 6 + h)),
            pl.BlockSpec((SEQ, LANE), lambda b, h: (b, 10 + h)),
            pl.BlockSpec((SEQ, LANE), lambda b, h: (b, 14 + h)),
        ],
        out_specs=pl.BlockSpec((SEQ, LANE), lambda b, h: (b, h)),
        out_shape=jax.ShapeDtypeStruct((N_CTX, 512), BF16),
        compiler_params=_cparams(("arbitrary", "arbitrary")),
        name="diff_context",
    )(lam_p, subln, proj, proj, proj)


DQ = 256


def _diff_lat(proj, cache_k, cache_v, lam_p, subln, j, lam_init):
    nq = DEC_SEQ // DQ
    row0 = N_CTX // DQ
    seq0 = N_CTX // DEC_SEQ
    return pl.pallas_call(
        functools.partial(_diff_kernel, lam_init=lam_init, has_cache=True),
        grid=(DEC_BATCH, DIFF_HEADS, nq),
        in_specs=[
            pl.BlockSpec((1, 4, DIFF_HEADS, HEAD_DIM), lambda b, h, i: (j, 0, 0, 0)),
            pl.BlockSpec((1, DIFF_V_DIM), lambda b, h, i: (j, 0)),
            pl.BlockSpec((DQ, LANE), lambda b, h, i: (row0 + b * nq + i, 6 + h)),
            pl.BlockSpec((DEC_SEQ, LANE), lambda b, h, i: (seq0 + b, 10 + h)),
            pl.BlockSpec((DEC_SEQ, LANE), lambda b, h, i: (seq0 + b, 14 + h)),
            pl.BlockSpec((1, PAST_LEN, LANE), lambda b, h, i: (b, 0, h)),
            pl.BlockSpec((1, PAST_LEN, LANE), lambda b, h, i: (b, 0, h)),
        ],
        out_specs=pl.BlockSpec((DQ, LANE), lambda b, h, i: (b * nq + i, h)),
        out_shape=jax.ShapeDtypeStruct((N_LAT, 512), BF16),
        compiler_params=_cparams(("arbitrary", "arbitrary", "arbitrary")),
        name="diff_latent",
    )(lam_p, subln, proj, proj, proj, cache_k, cache_v)


def kernel(x_prompt, x_sample, state_rglru, cache_swa_k, cache_swa_v, cache_diff_k, cache_diff_v, c, c_ctx,
           w_mod, b_mod, ln_g, ln_b, e_w_in, e_conv_w, e_conv_b, e_w_rgate, e_b_rgate, e_w_igate, e_b_igate,
           e_lambda, e_w_fnet, e_w_out, o_w_in, o_sink, o_lambda, o_subln_g, o_w_out,
           w_router, w_gate, w_up, w_down):
    x_pair = (x_prompt.reshape(N_CTX, D_MODEL), x_sample.reshape(N_LAT, D_MODEL))
    cvec8 = jnp.concatenate([c_ctx[None, :], c, jnp.zeros((SUBLANE - N_SEG, D_MODEL), F32)], axis=0)
    mod = _modulation(cvec8, w_mod, b_mod).reshape(DEPTH, SUBLANE, 6, D_MODEL)

    j = 0
    proj = _inproj(x_pair, mod, 0, e_w_in, j)
    wg = _gate_tiles(e_w_rgate, e_w_igate)
    cb = e_conv_b.reshape(-1, 1, D_RG)
    rg_args = (j, e_conv_w, cb, wg, e_b_rgate, e_b_igate, e_lambda)
    y_rg_c, st_c = _rglru(proj, jnp.zeros((BATCH, 2, D_RG), F32), *rg_args,
                          seq_len=SEQ, nseq=BATCH, row_block0=0)
    y_rg_l, _ = _rglru(proj, state_rglru[:, j], *rg_args,
                       seq_len=DEC_SEQ, nseq=DEC_BATCH, row_block0=N_CTX // DEC_SEQ)
    wbd = _fnet_blockdiag(e_w_fnet)
    y_fn_c = _fnet(proj, j, wbd, seq_len=SEQ, nseq=BATCH, row_block0=0)
    y_fn_l = _fnet(proj, j, wbd, seq_len=DEC_SEQ, nseq=DEC_BATCH, row_block0=N_CTX // DEC_SEQ)
    x1, u2, aff = _outproj((y_rg_c, y_rg_l), (y_fn_c, y_fn_l), e_w_out, j, x_pair, mod, 0, ln_g, ln_b, w_router)
    x = _moe(x1, u2, aff, mod, 0, ln_g, ln_b, w_gate, w_up, w_down)
    new_state_rglru = st_c[:, None]

    layer = 1
    lam_init = 0.8 - 0.6 * math.exp(-0.3 * layer)
    cos, sin = _rope_tables()
    rope_groups = (0, 1, 2, 3, 4, 6, 7, 8, 9, 10, 11, 12, 13)
    proj = _inproj((x, x), mod, 1, o_w_in, j, rope=(cos, sin, rope_groups))
    sink = o_sink[j]
    ys_c = _swa_ctx(proj, sink)
    yd_c = _diff_ctx(proj, o_lambda, o_subln_g, j, lam_init)
    ck = cache_swa_k[:, j].reshape(DEC_BATCH, PAST_LEN, SWA_KV_HEADS * HEAD_DIM)
    cv = cache_swa_v[:, j].reshape(DEC_BATCH, PAST_LEN, SWA_KV_HEADS * HEAD_DIM)
    ys_l = _swa_lat(proj, ck, cv, sink)
    cdk = cache_diff_k[:, j].reshape(DEC_BATCH, PAST_LEN, DIFF_HEADS * 2 * HEAD_DIM)
    cdv = cache_diff_v[:, j].reshape(DEC_BATCH, PAST_LEN, DIFF_HEADS * DIFF_V_DIM)
    yd_l = _diff_lat(proj, cdk, cdv, o_lambda, o_subln_g, j, lam_init)
    x1, u2, aff = _outproj((ys_c, ys_l), (yd_c, yd_l), o_w_out, j, (x, x), mod, 1, ln_g, ln_b, w_router)
    y_c, y_l = _moe(x1, u2, aff, mod, 1, ln_g, ln_b, w_gate, w_up, w_down, split_out=True)

    pc = proj[:N_CTX]
    new_cache_swa_k = pc[:, 512:640].reshape(BATCH, 1, SEQ, SWA_KV_HEADS, HEAD_DIM)
    new_cache_swa_v = pc[:, 640:768].reshape(BATCH, 1, SEQ, SWA_KV_HEADS, HEAD_DIM)
    new_cache_diff_k = pc[:, 1280:1792].reshape(BATCH, 1, SEQ, DIFF_HEADS, 2, HEAD_DIM)
    new_cache_diff_v = pc[:, 1792:2304].reshape(BATCH, 1, SEQ, DIFF_HEADS, DIFF_V_DIM)
    y_prompt = y_c.reshape(BATCH, SEQ, D_MODEL)
    y_sample = y_l.reshape(DEC_BATCH, DEC_SEQ, D_MODEL)
    return (y_prompt, y_sample, new_state_rglru, new_cache_swa_k, new_cache_swa_v,
            new_cache_diff_k, new_cache_diff_v)
```

```python
import functools
import math

import numpy as np
import jax
import jax.numpy as jnp
from jax import lax
from jax.experimental import pallas as pl
from jax.experimental.pallas import tpu as pltpu

F32 = jnp.float32
BF16 = jnp.bfloat16
I32 = jnp.int32

D_MODEL = 1024
BATCH, SEQ = 16, 256
DEC_BATCH, DEC_SEQ = 2, 2048
PAST_LEN = 256
DEPTH = 2
GRID_W = 64
HEAD_DIM = 64
D_RG = 768
RG_BLOCK = 64
RG_C = 8.0
CONV_W = 4
D_FNET = 256
FNET_GROUP_DIM = 64
FNET_GROUPS = 4
SWA_HEADS = 8
SWA_KV_HEADS = 2
GQA_GROUP = 4
WINDOW = 128
DIFF_HEADS = 4
DIFF_V_DIM = 128
N_EXPERTS = 16
EC_FACTOR = 2
D_EXPERT = 2048
ROPE_BASE = 10000.0
LN_EPS = 1e-5
NEG_INF = -1e30
ATTN_SCALE = HEAD_DIM ** -0.5
ALPHA = (2 * DEPTH) ** 0.25
EVEN_IN = 2 * D_RG + D_FNET
ODD_IN = 2304

N_CTX = BATCH * SEQ
N_LAT = DEC_BATCH * DEC_SEQ
N_TOK = N_CTX + N_LAT
N_SEG = 1 + DEC_BATCH
CAP = EC_FACTOR * N_CTX // N_EXPERTS
assert N_CTX == N_LAT

LANE = 128
SUBLANE = 8
MXU_DIM = 256
VMEM_LIMIT = 56 * 1024 * 1024
TM = 512
GC = TM
N_CHUNKS = N_CTX // GC
WIN = 128
WIN_ALIGN = 16
TOPK_MIN_EXP = -150.0
TOPK_GEO_STEPS = 24
TOPK_LIN_STEPS = 24
N_TILES = N_TOK // TM
CTX_TILES = N_CTX // TM
LAT_TILES_PER_SEQ = DEC_SEQ // TM


def _cparams(sem):
    return pltpu.CompilerParams(dimension_semantics=sem, vmem_limit_bytes=VMEM_LIMIT)


def _dot(a, b):
    return jnp.dot(a, b, preferred_element_type=F32)


def _dot_nt(a, b):
    return lax.dot_general(a, b, (((1,), (1,)), ((), ())), preferred_element_type=F32)


def _split(a):
    hi = a.astype(BF16)
    lo = (a - hi.astype(F32)).astype(BF16)
    return hi, lo


def _dot3(a, b):
    ah, al = _split(a)
    bh, bl = _split(b)
    return _dot(ah, bh) + (_dot(ah, bl) + _dot(al, bh))


def _layernorm(y, g, b):
    mu = jnp.mean(y, axis=-1, keepdims=True)
    d = y - mu
    var = jnp.mean(d * d, axis=-1, keepdims=True)
    return d * lax.rsqrt(var + LN_EPS) * g + b


def _pair_specs(width, lat_block0=0):
    return [
        pl.BlockSpec((TM, width), lambda i, *_: (jnp.minimum(i, CTX_TILES - 1), 0)),
        pl.BlockSpec((TM, width), lambda i, *_: (jnp.maximum(i - CTX_TILES, 0) + lat_block0, 0)),
    ]


def _pick(i, ctx_ref, lat_ref):
    return jnp.where(i < CTX_TILES, ctx_ref[...], lat_ref[...])


def _seg_of_tile(i):
    return jnp.where(i < CTX_TILES, 0, (i - CTX_TILES) // LAT_TILES_PER_SEQ + 1)


MOD_TN = 1536


def _mod_kernel(c_ref, w_ref, b_ref, o_ref):
    c = c_ref[...]
    s = c * jax.nn.sigmoid(c)
    o_ref[0] = _dot3(s, w_ref[0]) + b_ref[0]


def _modulation(cvec8, w_mod, b_mod):
    n = 6 * D_MODEL
    return pl.pallas_call(
        _mod_kernel,
        grid=(DEPTH, n // MOD_TN),
        in_specs=[
            pl.BlockSpec((SUBLANE, D_MODEL), lambda l, j: (0, 0)),
            pl.BlockSpec((1, D_MODEL, MOD_TN), lambda l, j: (l, 0, j)),
            pl.BlockSpec((1, 1, MOD_TN), lambda l, j: (l, 0, j)),
        ],
        out_specs=pl.BlockSpec((1, SUBLANE, MOD_TN), lambda l, j: (l, 0, j)),
        out_shape=jax.ShapeDtypeStruct((DEPTH, SUBLANE, n), F32),
        compiler_params=_cparams(("arbitrary", "arbitrary")),
        name="modulation",
    )(cvec8, w_mod, b_mod.reshape(DEPTH, 1, n))


def _rope_tile(p, cos, sin, rope_groups):
    lane = lax.broadcasted_iota(I32, (1, LANE), 1)
    first_half = (lane % 32) < 16
    pieces = []
    for k in range(p.shape[1] // LANE):
        xg = p[:, k * LANE:(k + 1) * LANE]
        if k in rope_groups:
            partner = jnp.where(first_half, pltpu.roll(xg, LANE - 16, 1), pltpu.roll(xg, 16, 1))
            xg = xg * cos + partner * sin
        pieces.append(xg)
    return jnp.concatenate(pieces, axis=1)


def _inproj_kernel(*refs, rope_groups):
    if rope_groups:
        xc_ref, xl_ref, mod_ref, w_ref, cos_ref, sin_ref, o_ref, wbf_ref = refs
    else:
        xc_ref, xl_ref, mod_ref, w_ref, o_ref, wbf_ref = refs
    i = pl.program_id(0)

    @pl.when(i == 0)
    def _():
        wbf_ref[...] = w_ref[0].astype(BF16)

    m = mod_ref[0, 0]
    u = _pick(i, xc_ref, xl_ref) * (1.0 + m[1:2, :]) + m[0:1, :]
    p = _dot(u.astype(BF16), wbf_ref[...])
    if rope_groups:
        @pl.when(i < CTX_TILES)
        def _():
            o_ref[...] = p

        @pl.when(i >= CTX_TILES)
        def _():
            o_ref[...] = _rope_tile(p, cos_ref[...], sin_ref[...], rope_groups)
    else:
        o_ref[...] = p


def _inproj(x_pair, mod, layer, w_in, j, rope=None):
    n = w_in.shape[-1]
    rope_groups = ()
    combined = x_pair[0] is x_pair[1]
    in_specs = _pair_specs(D_MODEL, CTX_TILES if combined else 0) + [
        pl.BlockSpec((1, 1, 6, D_MODEL), lambda i: (layer, _seg_of_tile(i), 0, 0)),
        pl.BlockSpec((1, D_MODEL, n), lambda i: (j, 0, 0)),
    ]
    args = [x_pair[0], x_pair[1], mod, w_in]
    if rope is not None:
        cos, sin, rope_groups = rope
        tab = pl.BlockSpec((TM, LANE), lambda i: (jnp.maximum(i - CTX_TILES, 0) % LAT_TILES_PER_SEQ, 0))
        in_specs += [tab, tab]
        args += [cos, sin]
    return pl.pallas_call(
        functools.partial(_inproj_kernel, rope_groups=rope_groups),
        grid=(N_TILES,),
        in_specs=in_specs,
        out_specs=pl.BlockSpec((TM, n), lambda i: (i, 0)),
        out_shape=jax.ShapeDtypeStruct((N_TOK, n), F32),
        scratch_shapes=[pltpu.VMEM((D_MODEL, n), BF16)],
        compiler_params=_cparams(("arbitrary",)),
        name="inproj_rope" if rope is not None else "inproj",
    )(*args)


def _rope_tables():
    s = np.arange(DEC_SEQ)
    row, col = (s // GRID_W).astype(np.float64), (s % GRID_W).astype(np.float64)
    nf = HEAD_DIM // 4
    inv = np.power(ROPE_BASE, -np.arange(nf, dtype=np.float64) / nf)
    ang = np.concatenate([row[:, None] * inv, row[:, None] * inv, col[:, None] * inv, col[:, None] * inv], axis=1)
    sign = np.concatenate([-np.ones(nf), np.ones(nf), -np.ones(nf), np.ones(nf)])
    cos = np.tile(np.cos(ang), (1, LANE // HEAD_DIM))
    sin = np.tile(np.sin(ang) * sign, (1, LANE // HEAD_DIM))
    return jnp.asarray(cos, F32), jnp.asarray(sin, F32)


RG_T = 256
RG_TILES = D_RG // MXU_DIM


def _sigmoid(x):
    return 0.5 * jnp.tanh(0.5 * x) + 0.5


def _softplus(x):
    u = jnp.exp(-jnp.abs(x))
    w = 1.0 + u
    l1p = jnp.where(w == 1.0, u, jnp.log(w) * (u / jnp.where(w == 1.0, 1.0, w - 1.0)))
    return jnp.maximum(x, 0.0) + l1p


def _rglru_kernel(xa_ref, ga_ref, h0_ref, cw_ref, cb_ref, wg_ref, br_ref, bi_ref, lam_ref,
                  y_ref, st_ref, hf_s, a_s, b_s, hs_s, *, seq_len):
    nchunk = seq_len // RG_T
    cw = cw_ref[0]
    cb = cb_ref[0]

    def conv_chunk(c):
        base = pl.multiple_of(c * RG_T, RG_T)
        cur = xa_ref[pl.ds(base, RG_T), :]
        prev = xa_ref[pl.ds(pl.multiple_of(jnp.maximum(base - SUBLANE, 0), SUBLANE), SUBLANE), :]
        prev = jnp.where(c > 0, prev, 0.0)
        nxt = xa_ref[pl.ds(pl.multiple_of(jnp.minimum(base + RG_T, seq_len - SUBLANE), SUBLANE), SUBLANE), :]
        nxt = jnp.where(c < nchunk - 1, nxt, 0.0)
        win = jnp.concatenate([prev, cur, nxt], axis=0)
        xc = cb
        for k in range(CONV_W):
            off = SUBLANE - 1 + k
            xc = xc + win[off:off + RG_T] * cw[k:k + 1, :]
        return xc

    def gates(xc, z):
        xb = xc.astype(BF16)
        rs, gs = [], []
        for t in range(RG_TILES):
            zz = _dot(xb[:, t * MXU_DIM:(t + 1) * MXU_DIM], wg_ref[0, z, t].astype(BF16))
            rs.append(zz[:, :MXU_DIM])
            gs.append(zz[:, MXU_DIM:])
        r = _sigmoid(jnp.concatenate(rs, axis=1) + br_ref[0, z:z + 1, :])
        g = _sigmoid(jnp.concatenate(gs, axis=1) + bi_ref[0, z:z + 1, :])
        log_a = (-RG_C * _softplus(-lam_ref[0, z:z + 1, :])) * r
        a = jnp.exp(log_a)
        bt = jnp.sqrt(-jnp.tanh(log_a) * (a * a + 1.0)) * g * xc
        return a, bt

    def scan_chunk(h, reverse):
        def step(t, h):
            tt = RG_T - 1 - t if reverse else t
            h = a_s[pl.ds(tt, 1), :] * h + b_s[pl.ds(tt, 1), :]
            hs_s[pl.ds(tt, 1), :] = h
            return h
        return lax.fori_loop(0, RG_T, step, h, unroll=8)

    def fwd_chunk(c, h):
        a, bt = gates(conv_chunk(c), 0)
        a_s[...] = a
        b_s[...] = bt
        h = scan_chunk(h, False)
        hf_s[pl.ds(pl.multiple_of(c * RG_T, RG_T), RG_T), :] = hs_s[...]
        return h

    def bwd_chunk(k, h):
        c = nchunk - 1 - k
        a, bt = gates(conv_chunk(c), 1)
        a_s[...] = a
        b_s[...] = bt
        h = scan_chunk(h, True)
        rows = pl.ds(pl.multiple_of(c * RG_T, RG_T), RG_T)
        y_ref[rows, :] = ((hf_s[rows, :] + hs_s[...]) * jax.nn.gelu(ga_ref[rows, :])).astype(BF16)
        return h

    hf = lax.fori_loop(0, nchunk, fwd_chunk, h0_ref[0, 0:1, :])
    hb = lax.fori_loop(0, nchunk, bwd_chunk, h0_ref[0, 1:2, :])
    st_ref[0, 0:1, :] = hf
    st_ref[0, 1:2, :] = hb


def _rglru(proj, h0, j, cw, cb, wg, br, bi, lam, *, seq_len, nseq, row_block0):
    wspec3 = lambda shape: pl.BlockSpec((1,) + shape, lambda b: (j,) + (0,) * len(shape))
    return pl.pallas_call(
        functools.partial(_rglru_kernel, seq_len=seq_len),
        grid=(nseq,),
        in_specs=[
            pl.BlockSpec((seq_len, D_RG), lambda b: (row_block0 + b, 0)),
            pl.BlockSpec((seq_len, D_RG), lambda b: (row_block0 + b, 1)),
            pl.BlockSpec((1, 2, D_RG), lambda b: (b, 0, 0)),
            wspec3((CONV_W, D_RG)),
            wspec3((1, D_RG)),
            wspec3((2, RG_TILES, MXU_DIM, 2 * MXU_DIM)),
            wspec3((2, D_RG)),
            wspec3((2, D_RG)),
            wspec3((2, D_RG)),
        ],
        out_specs=[
            pl.BlockSpec((seq_len, D_RG), lambda b: (b, 0)),
            pl.BlockSpec((1, 2, D_RG), lambda b: (b, 0, 0)),
        ],
        out_shape=[
            jax.ShapeDtypeStruct((nseq * seq_len, D_RG), BF16),
            jax.ShapeDtypeStruct((nseq, 2, D_RG), F32),
        ],
        scratch_shapes=[
            pltpu.VMEM((seq_len, D_RG), F32),
            pltpu.VMEM((RG_T, D_RG), F32),
            pltpu.VMEM((RG_T, D_RG), F32),
            pltpu.VMEM((RG_T, D_RG), F32),
        ],
        compiler_params=_cparams(("arbitrary",)),
        name=f"rglru_s{seq_len}",
    )(proj, proj, h0, cw, cb, wg, br, bi, lam)


def _gate_tiles(w_r, w_i):
    per_tile = MXU_DIM // RG_BLOCK
    eye = jnp.eye(per_tile, dtype=F32)

    def tiles(w):
        w = w.reshape(w.shape[0], 2, RG_TILES, per_tile, RG_BLOCK, RG_BLOCK)
        return jnp.einsum('nztbcd,be->nztbced', w, eye).reshape(w.shape[0], 2, RG_TILES, MXU_DIM, MXU_DIM)

    return jnp.concatenate([tiles(w_r), tiles(w_i)], axis=-1)


def _fnet_kernel(z_ref, cc_ref, sc_ref, w_ref, cs_ref, ns_ref, y_ref, p_s, q_s):
    @pl.when(pl.program_id(1) == 0)
    def _():
        w = w_ref[0]
        a = _dot3(cc_ref[...], w).astype(BF16)
        b = _dot3(sc_ref[...], w).astype(BF16)
        zb = z_ref[...].astype(BF16)
        p_s[...] = _dot(zb, a).astype(BF16)
        q_s[...] = _dot(zb, b).astype(BF16)

    y_ref[...] = (_dot(cs_ref[...].astype(BF16), p_s[...])
                  + _dot(ns_ref[...].astype(BF16), q_s[...])).astype(BF16)


def _fnet(proj, j, wbd, *, seq_len, nseq, row_block0):
    tr = min(seq_len, 512)
    nrt = seq_len // tr
    cc, sc = _dft_tables(FNET_GROUP_DIM)
    ccbd = jnp.asarray(np.kron(np.eye(FNET_GROUPS), cc), F32)
    scbd = jnp.asarray(np.kron(np.eye(FNET_GROUPS), sc), F32)
    cs, ss = _dft_tables(seq_len)
    cs_b, ns_b = jnp.asarray(cs, F32), jnp.asarray(-ss, F32)
    zcol = (2 * D_RG) // D_FNET
    const = lambda shape: pl.BlockSpec(shape, lambda b, r: (0,) * len(shape))
    return pl.pallas_call(
        _fnet_kernel,
        grid=(nseq, nrt),
        in_specs=[
            pl.BlockSpec((seq_len, D_FNET), lambda b, r: (row_block0 + b, zcol)),
            const((D_FNET, D_FNET)),
            const((D_FNET, D_FNET)),
            pl.BlockSpec((1, D_FNET, D_FNET), lambda b, r: (j, 0, 0)),
            pl.BlockSpec((tr, seq_len), lambda b, r: (r, 0)),
            pl.BlockSpec((tr, seq_len), lambda b, r: (r, 0)),
        ],
        out_specs=pl.BlockSpec((tr, D_FNET), lambda b, r: (b * nrt + r, 0)),
        out_shape=jax.ShapeDtypeStruct((nseq * seq_len, D_FNET), BF16),
        scratch_shapes=[pltpu.VMEM((seq_len, D_FNET), BF16), pltpu.VMEM((seq_len, D_FNET), BF16)],
        compiler_params=_cparams(("arbitrary", "arbitrary")),
        name=f"fnet_s{seq_len}",
    )(proj, ccbd, scbd, wbd, cs_b, ns_b)


def _dft_tables(n):
    k = np.arange(n)
    ang = 2.0 * np.pi * ((k[:, None] * k[None, :]) % n) / n
    return np.cos(ang) / np.sqrt(n), np.sin(ang) / np.sqrt(n)


def _fnet_blockdiag(w_f):
    eye = jnp.eye(FNET_GROUPS, dtype=F32)
    return jnp.einsum('ngcd,gh->ngchd', w_f, eye).reshape(w_f.shape[0], D_FNET, D_FNET)


def _outproj_kernel(ac_ref, al_ref, bc_ref, bl_ref, w_ref, xc_ref, xl_ref, mod_ref, g_ref, be_ref, wr_ref,
                    x1_ref, u2_ref, aff_ref, wbf_ref, *, ka):
    i = pl.program_id(0)

    @pl.when(i == 0)
    def _():
        wbf_ref[...] = w_ref[0].astype(BF16)

    out = _dot(_pick(i, ac_ref, al_ref), wbf_ref[0:ka, :]) + _dot(_pick(i, bc_ref, bl_ref), wbf_ref[ka:, :])
    m = mod_ref[0, 0]
    x1 = _layernorm(ALPHA * _pick(i, xc_ref, xl_ref) + m[2:3, :] * out, g_ref[0, 0:1, :], be_ref[0, 0:1, :])
    x1_ref[...] = x1
    u2 = x1 * (1.0 + m[4:5, :]) + m[3:4, :]
    u2_ref[...] = u2.astype(BF16)
    uh, ul = _split(u2)
    wh, wl = _split(wr_ref[0])
    lg = _dot(jnp.concatenate([uh, ul], axis=0), jnp.concatenate([wh, wl], axis=1))
    logits = (lg[:TM, :N_EXPERTS] + lg[:TM, N_EXPERTS:]) + (lg[TM:, :N_EXPERTS] + lg[TM:, N_EXPERTS:])
    mx = jnp.max(logits, axis=-1, keepdims=True)
    ex = jnp.exp(logits - mx)
    aff_ref[...] = ex / jnp.sum(ex, axis=-1, keepdims=True)


def _outproj(a_pair, b_pair, w_out, j, x_pair, mod, layer, ln_g, ln_b, w_router):
    ka, kb = a_pair[0].shape[1], b_pair[0].shape[1]
    combined = x_pair[0] is x_pair[1]
    return pl.pallas_call(
        functools.partial(_outproj_kernel, ka=ka),
        grid=(N_TILES,),
        in_specs=_pair_specs(ka) + _pair_specs(kb) + [
            pl.BlockSpec((1, D_MODEL, D_MODEL), lambda i: (j, 0, 0)),
        ] + _pair_specs(D_MODEL, CTX_TILES if combined else 0) + [
            pl.BlockSpec((1, 1, 6, D_MODEL), lambda i: (layer, _seg_of_tile(i), 0, 0)),
            pl.BlockSpec((1, 2, D_MODEL), lambda i: (layer, 0, 0)),
            pl.BlockSpec((1, 2, D_MODEL), lambda i: (layer, 0, 0)),
            pl.BlockSpec((1, D_MODEL, N_EXPERTS), lambda i: (layer, 0, 0)),
        ],
        out_specs=[
            pl.BlockSpec((TM, D_MODEL), lambda i: (i, 0)),
            pl.BlockSpec((TM, D_MODEL), lambda i: (i, 0)),
            pl.BlockSpec((TM, N_EXPERTS), lambda i: (i, 0)),
        ],
        out_shape=[
            jax.ShapeDtypeStruct((N_TOK, D_MODEL), F32),
            jax.ShapeDtypeStruct((N_TOK, D_MODEL), BF16),
            jax.ShapeDtypeStruct((N_TOK, N_EXPERTS), F32),
        ],
        scratch_shapes=[pltpu.VMEM((D_MODEL, D_MODEL), BF16)],
        compiler_params=_cparams(("arbitrary",)),
        name="outproj_ln_router",
    )(*a_pair, *b_pair, w_out, *x_pair, mod, ln_g, ln_b, w_router)


def _topk_kernel(aff_ref, pos_ref, cnt_ref):
    aff = aff_ref[...]
    cap = jnp.float32(CAP)

    def count_ge(t):
        return jnp.sum(jnp.where(aff >= t, 1.0, 0.0), axis=1, keepdims=True)

    lo_v = jnp.zeros((N_EXPERTS, 1), F32)
    hi_v = jnp.full((N_EXPERTS, 1), 2.0, F32)
    lo_t = jnp.full((N_EXPERTS, 1), TOPK_MIN_EXP, F32)
    hi_t = jnp.full((N_EXPERTS, 1), 1.0, F32)
    for _ in range(TOPK_GEO_STEPS):
        mid_t = 0.5 * (lo_t + hi_t)
        cand = jnp.exp2(mid_t)
        ok = count_ge(cand) >= cap
        lo_t = jnp.where(ok, mid_t, lo_t)
        hi_t = jnp.where(ok, hi_t, mid_t)
        lo_v = jnp.where(ok, jnp.maximum(lo_v, cand), lo_v)
        hi_v = jnp.where(ok, hi_v, jnp.minimum(hi_v, cand))
    for _ in range(TOPK_LIN_STEPS):
        cand = lo_v + 0.5 * (hi_v - lo_v)
        ok = count_ge(cand) >= cap
        lo_v = jnp.where(ok, cand, lo_v)
        hi_v = jnp.where(ok, hi_v, cand)
    gt = aff >= hi_v
    eq = (aff >= lo_v) & (aff < hi_v)
    need = cap - jnp.sum(jnp.where(gt, 1.0, 0.0), axis=1, keepdims=True)

    tri = (lax.broadcasted_iota(I32, (LANE, LANE), 0) <= lax.broadcasted_iota(I32, (LANE, LANE), 1))
    tri = jnp.where(tri, 1.0, 0.0).astype(BF16)

    def cumsum_blocks(mask_f32):
        carry = jnp.zeros((N_EXPERTS, 1), F32)
        outs = []
        for k in range(mask_f32.shape[1] // LANE):
            blk = mask_f32[:, k * LANE:(k + 1) * LANE]
            inc = _dot(blk.astype(BF16), tri) + carry
            outs.append(inc)
            carry = carry + jnp.sum(blk, axis=1, keepdims=True)
        return outs

    eq_f = jnp.where(eq, 1.0, 0.0)
    gt_f = jnp.where(gt, 1.0, 0.0)
    eq_rank = cumsum_blocks(eq_f)
    sel_blocks = []
    for k, rk in enumerate(eq_rank):
        lanes = slice(k * LANE, (k + 1) * LANE)
        sel_blocks.append(jnp.where((eq_f[:, lanes] > 0.0) & (rk <= need), 1.0, gt_f[:, lanes]))
    sel = jnp.concatenate(sel_blocks, axis=1)
    sel_rank = cumsum_blocks(sel)
    for k, rk in enumerate(sel_rank):
        lanes = slice(k * LANE, (k + 1) * LANE)
        pos_ref[:, lanes] = jnp.where(sel[:, lanes] > 0.0, rk.astype(I32) - 1, -1)
    lane = lax.broadcasted_iota(I32, (N_EXPERTS, LANE), 1)
    cnt = jnp.zeros((N_EXPERTS, LANE), F32)
    for c in range(1, N_CHUNKS + 1):
        cnt = jnp.where(lane == c, sel_rank[c * (GC // LANE) - 1][:, LANE - 1:LANE], cnt)
    cnt_ref[0] = cnt.astype(I32)


def _topk(aff_t):
    return pl.pallas_call(
        _topk_kernel,
        grid=(2,),
        in_specs=[pl.BlockSpec((N_EXPERTS, N_CTX), lambda g: (0, g))],
        out_specs=[
            pl.BlockSpec((N_EXPERTS, N_CTX), lambda g: (0, g)),
            pl.BlockSpec((1, N_EXPERTS, LANE), lambda g: (g, 0, 0)),
        ],
        out_shape=[
            jax.ShapeDtypeStruct((N_EXPERTS, N_TOK), I32),
            jax.ShapeDtypeStruct((2, N_EXPERTS, LANE), I32),
        ],
        compiler_params=_cparams(("arbitrary",)),
        name="topk_select",
    )(aff_t)


def _chunk_windows(cnt_ref, g, e, c):
    c0 = cnt_ref[g, e, c]
    c1 = cnt_ref[g, e, c + 1]
    base = (c0 // WIN_ALIGN) * WIN_ALIGN
    k = jnp.where(c1 > c0, (c1 - base + WIN - 1) // WIN, 0)
    return base, k


def _gather_kernel(cnt_ref, pos_ref, aff_ref, u_ref, xe_ref, gs_ref, acc_s, gacc_s):
    g = pl.program_id(0)
    e = pl.program_id(1)
    acc_s[...] = jnp.zeros_like(acc_s)
    gacc_s[...] = jnp.zeros_like(gacc_s)
    prow = pos_ref[pl.ds(e, 1), :]
    arow = aff_ref[pl.ds(e, 1), :]
    slot0 = lax.broadcasted_iota(I32, (WIN, GC), 0)

    def add_window(c, lo):
        cols = slice(c * GC, (c + 1) * GC)
        pc = prow[:, cols]
        start = pl.multiple_of(jnp.minimum(lo, CAP - WIN), WIN_ALIGN)
        hit = (slot0 + start) == jnp.where(pc >= lo, pc, -1)
        rows = pl.ds(start, WIN)
        acc_s[rows, :] += _dot(jnp.where(hit, 1.0, 0.0).astype(BF16), u_ref[cols, :])
        gacc_s[rows, :] += jnp.sum(jnp.where(hit, arow[:, cols], 0.0), axis=1, keepdims=True)

    windows = [_chunk_windows(cnt_ref, g, e, c) for c in range(N_CHUNKS)]
    for c, (base, _) in enumerate(windows):
        add_window(c, base)
    for c, (base, k) in enumerate(windows):
        def extra(w, carry, c=c, base=base):
            add_window(c, base + w * WIN)
            return carry
        lax.fori_loop(1, k, extra, 0)
    xe_ref[0, 0] = acc_s[...].astype(BF16)
    gs_ref[0, 0] = gacc_s[...]


def _gather(cnt, pos_t, aff_t, u2):
    return pl.pallas_call(
        _gather_kernel,
        grid_spec=pltpu.PrefetchScalarGridSpec(
            num_scalar_prefetch=1,
            grid=(2, N_EXPERTS),
            in_specs=[
                pl.BlockSpec((N_EXPERTS, N_CTX), lambda g, e, cnt: (0, g)),
                pl.BlockSpec((N_EXPERTS, N_CTX), lambda g, e, cnt: (0, g)),
                pl.BlockSpec((N_CTX, D_MODEL), lambda g, e, cnt: (g, 0)),
            ],
            out_specs=[
                pl.BlockSpec((1, 1, CAP, D_MODEL), lambda g, e, cnt: (e, g, 0, 0)),
                pl.BlockSpec((1, 1, CAP, 1), lambda g, e, cnt: (e, g, 0, 0)),
            ],
            scratch_shapes=[pltpu.VMEM((CAP, D_MODEL), F32), pltpu.VMEM((CAP, 1), F32)],
        ),
        out_shape=[
            jax.ShapeDtypeStruct((N_EXPERTS, 2, CAP, D_MODEL), BF16),
            jax.ShapeDtypeStruct((N_EXPERTS, 2, CAP, 1), F32),
        ],
        compiler_params=_cparams(("arbitrary", "arbitrary")),
        name="moe_gather",
    )(cnt, pos_t, aff_t, u2)


TF = 1024


def _ffn_kernel(xe_ref, gs_ref, wg_ref, wu_ref, wd_ref, ye_ref, acc_ref):
    f = pl.program_id(1)
    x = xe_ref[0]
    hg = _dot(x, wg_ref[0, 0].astype(BF16))
    hu = _dot(x, wu_ref[0, 0].astype(BF16))
    h = (hg * _sigmoid(hg) * hu).astype(BF16)

    @pl.when(f == 0)
    def _():
        acc_ref[...] = jnp.zeros_like(acc_ref)

    acc_ref[...] += _dot(h, wd_ref[0, 0].astype(BF16))

    @pl.when(f == pl.num_programs(1) - 1)
    def _():
        ye_ref[0] = (acc_ref[...] * gs_ref[0]).astype(BF16)


def _ffn(xe, gs, layer, w_gate, w_up, w_down):
    rows = 2 * CAP
    return pl.pallas_call(
        _ffn_kernel,
        grid=(N_EXPERTS, D_EXPERT // TF),
        in_specs=[
            pl.BlockSpec((1, rows, D_MODEL), lambda e, f: (e, 0, 0)),
            pl.BlockSpec((1, rows, 1), lambda e, f: (e, 0, 0)),
            pl.BlockSpec((1, 1, D_MODEL, TF), lambda e, f: (layer, e, 0, f)),
            pl.BlockSpec((1, 1, D_MODEL, TF), lambda e, f: (layer, e, 0, f)),
            pl.BlockSpec((1, 1, TF, D_MODEL), lambda e, f: (layer, e, f, 0)),
        ],
        out_specs=pl.BlockSpec((1, rows, D_MODEL), lambda e, f: (e, 0, 0)),
        out_shape=jax.ShapeDtypeStruct((N_EXPERTS, rows, D_MODEL), BF16),
        scratch_shapes=[pltpu.VMEM((rows, D_MODEL), F32)],
        compiler_params=_cparams(("arbitrary", "arbitrary")),
        name="moe_ffn",
    )(xe, gs, w_gate, w_up, w_down)


def _combine_kernel(*refs, split_out):
    if split_out:
        cnt_ref, posn_ref, ye_ref, x1_ref, mod_ref, g_ref, be_ref, oc_ref, ol_ref, acc_s = refs
    else:
        cnt_ref, posn_ref, ye_ref, x1_ref, mod_ref, g_ref, be_ref, o_ref, acc_s = refs
    i = pl.program_id(0)
    g = i // N_CHUNKS
    c = i % N_CHUNKS
    posn = posn_ref[...]
    slot0 = lax.broadcasted_iota(I32, (TM, WIN), 1)

    def window(e, lo):
        col = posn[:, e:e + 1]
        start = pl.multiple_of(jnp.minimum(lo, CAP - WIN), WIN_ALIGN)
        hit = (slot0 + start) == jnp.where(col >= lo, col, -1)
        return jnp.where(hit, 1.0, 0.0).astype(BF16), ye_ref[e, 0, pl.ds(start, WIN), :]

    windows = [_chunk_windows(cnt_ref, g, e, c) for e in range(N_EXPERTS)]
    acc = None
    for e in range(0, N_EXPERTS, 2):
        oh0, ye0 = window(e, windows[e][0])
        oh1, ye1 = window(e + 1, windows[e + 1][0])
        d = _dot(jnp.concatenate([oh0, oh1], axis=1), jnp.concatenate([ye0, ye1], axis=0))
        acc = d if acc is None else acc + d
    acc_s[...] = acc
    for e, (base, k) in enumerate(windows):
        def extra(w, carry, e=e, base=base):
            oh, ye = window(e, base + w * WIN)
            acc_s[...] += _dot(oh, ye)
            return carry
        lax.fori_loop(1, k, extra, 0)
    m = mod_ref[0, 0]
    y = ALPHA * x1_ref[...] + m[5:6, :] * acc_s[...]
    res = _layernorm(y, g_ref[0, 1:2, :], be_ref[0, 1:2, :])
    if split_out:
        @pl.when(i < CTX_TILES)
        def _():
            oc_ref[...] = res

        @pl.when(i >= CTX_TILES)
        def _():
            ol_ref[...] = res
    else:
        o_ref[...] = res


def _combine(cnt, pos_n, ye, x1, mod, layer, ln_g, ln_b, split_out):
    if split_out:
        out_specs = _pair_specs(D_MODEL)
        out_shape = [jax.ShapeDtypeStruct((N_CTX, D_MODEL), F32), jax.ShapeDtypeStruct((N_LAT, D_MODEL), F32)]
    else:
        out_specs = pl.BlockSpec((TM, D_MODEL), lambda i, cnt: (i, 0))
        out_shape = jax.ShapeDtypeStruct((N_TOK, D_MODEL), F32)
    return pl.pallas_call(
        functools.partial(_combine_kernel, split_out=split_out),
        grid_spec=pltpu.PrefetchScalarGridSpec(
            num_scalar_prefetch=1,
            grid=(N_TILES,),
            in_specs=[
                pl.BlockSpec((TM, N_EXPERTS), lambda i, cnt: (i, 0)),
                pl.BlockSpec((N_EXPERTS, 1, CAP, D_MODEL), lambda i, cnt: (0, i // N_CHUNKS, 0, 0)),
                pl.BlockSpec((TM, D_MODEL), lambda i, cnt: (i, 0)),
                pl.BlockSpec((1, 1, 6, D_MODEL), lambda i, cnt: (layer, _seg_of_tile(i), 0, 0)),
                pl.BlockSpec((1, 2, D_MODEL), lambda i, cnt: (layer, 0, 0)),
                pl.BlockSpec((1, 2, D_MODEL), lambda i, cnt: (layer, 0, 0)),
            ],
            out_specs=out_specs,
            scratch_shapes=[pltpu.VMEM((TM, D_MODEL), F32)],
        ),
        out_shape=out_shape,
        compiler_params=_cparams(("arbitrary",)),
        name="moe_combine_ln",
    )(cnt, pos_n, ye, x1, mod, ln_g, ln_b)


def _moe(x1, u2, aff, mod, layer, ln_g, ln_b, w_gate, w_up, w_down, split_out=False):
    aff_t = aff.T
    pos_t, cnt = _topk(aff_t)
    xe, gs = _gather(cnt, pos_t, aff_t, u2)
    ye = _ffn(xe.reshape(N_EXPERTS, 2 * CAP, D_MODEL), gs.reshape(N_EXPERTS, 2 * CAP, 1),
              layer, w_gate, w_up, w_down)
    return _combine(cnt, pos_t.T, ye.reshape(N_EXPERTS, 2, CAP, D_MODEL), x1, mod, layer, ln_g, ln_b, split_out)


def _sink_softmax_pv(s, sink_col, v):
    m = jnp.maximum(jnp.max(s, axis=-1, keepdims=True), sink_col)
    p = jnp.exp(s - m)
    den = jnp.sum(p, axis=-1, keepdims=True) + jnp.exp(sink_col - m)
    return _dot(p.astype(BF16), v) / den


def _sink_column(sink_ref, h, rows_per_group):
    rid = lax.broadcasted_iota(I32, (GQA_GROUP * rows_per_group, 1), 0) // rows_per_group
    col = jnp.zeros((GQA_GROUP * rows_per_group, 1), F32)
    for g in range(GQA_GROUP):
        col = jnp.where(rid == g, sink_ref[h * GQA_GROUP + g], col)
    return col


def _stack_groups(q, h):
    return jnp.concatenate(
        [q[:, (h * GQA_GROUP + g) * HEAD_DIM:(h * GQA_GROUP + g + 1) * HEAD_DIM] * ATTN_SCALE
         for g in range(GQA_GROUP)], axis=0).astype(BF16)


def _unstack_groups(outs, rows):
    return jnp.concatenate([o[g * rows:(g + 1) * rows] for o in outs for g in range(GQA_GROUP)], axis=1)


def _swa_ctx_kernel(sink_ref, q_ref, k_ref, v_ref, o_ref):
    q, k, v = q_ref[...], k_ref[...], v_ref[...]
    outs = []
    for h in range(SWA_KV_HEADS):
        cols = slice(h * HEAD_DIM, (h + 1) * HEAD_DIM)
        s = _dot_nt(_stack_groups(q, h), k[:, cols].astype(BF16))
        outs.append(_sink_softmax_pv(s, _sink_column(sink_ref, h, SEQ), v[:, cols].astype(BF16)))
    o_ref[...] = _unstack_groups(outs, SEQ).astype(BF16)


def _swa_ctx(proj, sink):
    return pl.pallas_call(
        _swa_ctx_kernel,
        grid=(BATCH,),
        in_specs=[
            pl.BlockSpec(memory_space=pltpu.SMEM),
            pl.BlockSpec((SEQ, 512), lambda b: (b, 0)),
            pl.BlockSpec((SEQ, LANE), lambda b: (b, 4)),
            pl.BlockSpec((SEQ, LANE), lambda b: (b, 5)),
        ],
        out_specs=pl.BlockSpec((SEQ, 512), lambda b: (b, 0)),
        out_shape=jax.ShapeDtypeStruct((N_CTX, 512), BF16),
        compiler_params=_cparams(("arbitrary",)),
        name="swa_context",
    )(sink, proj, proj, proj)


QB = 128


def _swa_lat_kernel(sink_ref, q_ref, kp_ref, kc_ref, kn_ref, vp_ref, vc_ref, vn_ref, ck_ref, cv_ref, o_ref):
    qb = pl.program_id(1)
    q = q_ref[...]
    kall = jnp.concatenate([ck_ref[0], kp_ref[...], kc_ref[...], kn_ref[...]], axis=0).astype(BF16)
    vall = jnp.concatenate([cv_ref[0], vp_ref[...], vc_ref[...], vn_ref[...]], axis=0).astype(BF16)
    nk = PAST_LEN + 3 * QB
    rows = GQA_GROUP * QB
    col = lax.broadcasted_iota(I32, (rows, nk), 1)
    qpos = qb * QB + lax.broadcasted_iota(I32, (rows, nk), 0) % QB
    kpos = (qb - 1) * QB + col - PAST_LEN
    ok = (col < PAST_LEN) | ((jnp.abs(qpos - kpos) <= WINDOW) & (kpos >= 0) & (kpos < DEC_SEQ))
    outs = []
    for h in range(SWA_KV_HEADS):
        cols = slice(h * HEAD_DIM, (h + 1) * HEAD_DIM)
        s = _dot_nt(_stack_groups(q, h), kall[:, cols])
        s = jnp.where(ok, s, NEG_INF)
        outs.append(_sink_softmax_pv(s, _sink_column(sink_ref, h, QB), vall[:, cols]))
    o_ref[...] = _unstack_groups(outs, QB).astype(BF16)


def _swa_lat(proj, cache_k, cache_v, sink):
    nqb = DEC_SEQ // QB
    row0 = N_CTX // QB

    def blk(col, shift):
        return pl.BlockSpec((QB, LANE), lambda b, i: (row0 + b * nqb + jnp.clip(i + shift, 0, nqb - 1), col))

    return pl.pallas_call(
        _swa_lat_kernel,
        grid=(DEC_BATCH, nqb),
        in_specs=[
            pl.BlockSpec(memory_space=pltpu.SMEM),
            pl.BlockSpec((QB, 512), lambda b, i: (row0 + b * nqb + i, 0)),
            blk(4, -1), blk(4, 0), blk(4, 1),
            blk(5, -1), blk(5, 0), blk(5, 1),
            pl.BlockSpec((1, PAST_LEN, LANE), lambda b, i: (b, 0, 0)),
            pl.BlockSpec((1, PAST_LEN, LANE), lambda b, i: (b, 0, 0)),
        ],
        out_specs=pl.BlockSpec((QB, 512), lambda b, i: (b * nqb + i, 0)),
        out_shape=jax.ShapeDtypeStruct((N_LAT, 512), BF16),
        compiler_params=_cparams(("arbitrary", "arbitrary")),
        name="swa_latent",
    )(sink, proj, proj, proj, proj, proj, proj, proj, cache_k, cache_v)


def _diff_lambda_col(lam_ref, lam_init):
    l1 = jnp.sum(lam_ref[0, 0] * lam_ref[0, 1], axis=-1, keepdims=True)
    l2 = jnp.sum(lam_ref[0, 2] * lam_ref[0, 3], axis=-1, keepdims=True)
    return jnp.exp(l1) - jnp.exp(l2) + lam_init


def _softmax_pv(q, k, v):
    s = _dot_nt(q, k)
    p = jnp.exp(s - jnp.max(s, axis=-1, keepdims=True))
    return _dot(p.astype(BF16), v) / jnp.sum(p, axis=-1, keepdims=True)


def _diff_head(q, k, v, lam, subln, lam_init):
    o = [_softmax_pv((q[:, m * HEAD_DIM:(m + 1) * HEAD_DIM] * ATTN_SCALE).astype(BF16),
                     k[:, m * HEAD_DIM:(m + 1) * HEAD_DIM], v) for m in range(2)]
    o = o[0] - lam * o[1]
    return o * lax.rsqrt(jnp.mean(o * o, axis=-1, keepdims=True) + LN_EPS) * subln * (1.0 - lam_init)


def _diff_kernel(*refs, lam_init, has_cache):
    if has_cache:
        lam_ref, sg_ref, q_ref, k_ref, v_ref, ck_ref, cv_ref, o_ref = refs
    else:
        lam_ref, sg_ref, q_ref, k_ref, v_ref, o_ref = refs
    h = pl.program_id(1)
    lam_all = _diff_lambda_col(lam_ref, lam_init)
    hid = lax.broadcasted_iota(I32, (DIFF_HEADS, 1), 0)
    lam = jnp.sum(jnp.where(hid == h, lam_all, 0.0), axis=0, keepdims=True)
    k, v = k_ref[...], v_ref[...]
    if has_cache:
        k = jnp.concatenate([k, ck_ref[0]], axis=0)
        v = jnp.concatenate([v, cv_ref[0]], axis=0)
    o_ref[...] = _diff_head(q_ref[...], k.astype(BF16), v.astype(BF16), lam, sg_ref[...],
                            lam_init).astype(BF16)


def _diff_ctx(proj, lam_p, subln, j, lam_init):
    return pl.pallas_call(
        functools.partial(_diff_kernel, lam_init=lam_init, has_cache=False),
        grid=(BATCH, DIFF_HEADS),
        in_specs=[
            pl.BlockSpec((1, 4, DIFF_HEADS, HEAD_DIM), lambda b, h: (j, 0, 0, 0)),
            pl.BlockSpec((1, DIFF_V_DIM), lambda b, h: (j, 0)),
            pl.BlockSpec((SEQ, LANE), lambda b, h: (b, 6 + h)),
            pl.BlockSpec((SEQ, LANE), lambda b, h: (b, 10 + h)),
            pl.BlockSpec((SEQ, LANE), lambda b, h: (b, 14 + h)),
        ],
        out_specs=pl.BlockSpec((SEQ, LANE), lambda b, h: (b, h)),
        out_shape=jax.ShapeDtypeStruct((N_CTX, 512), BF16),
        compiler_params=_cparams(("arbitrary", "arbitrary")),
        name="diff_context",
    )(lam_p, subln, proj, proj, proj)


DQ = 256


def _diff_lat(proj, cache_k, cache_v, lam_p, subln, j, lam_init):
    nq = DEC_SEQ // DQ
    row0 = N_CTX // DQ
    seq0 = N_CTX // DEC_SEQ
    return pl.pallas_call(
        functools.partial(_diff_kernel, lam_init=lam_init, has_cache=True),
        grid=(DEC_BATCH, DIFF_HEADS, nq),
        in_specs=[
            pl.BlockSpec((1, 4, DIFF_HEADS, HEAD_DIM), lambda b, h, i: (j, 0, 0, 0)),
            pl.BlockSpec((1, DIFF_V_DIM), lambda b, h, i: (j, 0)),
            pl.BlockSpec((DQ, LANE), lambda b, h, i: (row0 + b * nq + i, 6 + h)),
            pl.BlockSpec((DEC_SEQ, LANE), lambda b, h, i: (seq0 + b, 10 + h)),
            pl.BlockSpec((DEC_SEQ, LANE), lambda b, h, i: (seq0 + b, 14 + h)),
            pl.BlockSpec((1, PAST_LEN, LANE), lambda b, h, i: (b, 0, h)),
            pl.BlockSpec((1, PAST_LEN, LANE), lambda b, h, i: (b, 0, h)),
        ],
        out_specs=pl.BlockSpec((DQ, LANE), lambda b, h, i: (b * nq + i, h)),
        out_shape=jax.ShapeDtypeStruct((N_LAT, 512), BF16),
        compiler_params=_cparams(("arbitrary", "arbitrary", "arbitrary")),
        name="diff_latent",
    )(lam_p, subln, proj, proj, proj, cache_k, cache_v)


def kernel(x_prompt, x_sample, state_rglru, cache_swa_k, cache_swa_v, cache_diff_k, cache_diff_v, c, c_ctx,
           w_mod, b_mod, ln_g, ln_b, e_w_in, e_conv_w, e_conv_b, e_w_rgate, e_b_rgate, e_w_igate, e_b_igate,
           e_lambda, e_w_fnet, e_w_out, o_w_in, o_sink, o_lambda, o_subln_g, o_w_out,
           w_router, w_gate, w_up, w_down):
    x_pair = (x_prompt.reshape(N_CTX, D_MODEL), x_sample.reshape(N_LAT, D_MODEL))
    cvec8 = jnp.concatenate([c_ctx[None, :], c, jnp.zeros((SUBLANE - N_SEG, D_MODEL), F32)], axis=0)
    mod = _modulation(cvec8, w_mod, b_mod).reshape(DEPTH, SUBLANE, 6, D_MODEL)

    j = 0
    proj = _inproj(x_pair, mod, 0, e_w_in, j)
    wg = _gate_tiles(e_w_rgate, e_w_igate)
    cb = e_conv_b.reshape(-1, 1, D_RG)
    rg_args = (j, e_conv_w, cb, wg, e_b_rgate, e_b_igate, e_lambda)
    y_rg_c, st_c = _rglru(proj, jnp.zeros((BATCH, 2, D_RG), F32), *rg_args,
                          seq_len=SEQ, nseq=BATCH, row_block0=0)
    y_rg_l, _ = _rglru(proj, state_rglru[:, j], *rg_args,
                       seq_len=DEC_SEQ, nseq=DEC_BATCH, row_block0=N_CTX // DEC_SEQ)
    wbd = _fnet_blockdiag(e_w_fnet)
    y_fn_c = _fnet(proj, j, wbd, seq_len=SEQ, nseq=BATCH, row_block0=0)
    y_fn_l = _fnet(proj, j, wbd, seq_len=DEC_SEQ, nseq=DEC_BATCH, row_block0=N_CTX // DEC_SEQ)
    x1, u2, aff = _outproj((y_rg_c, y_rg_l), (y_fn_c, y_fn_l), e_w_out, j, x_pair, mod, 0, ln_g, ln_b, w_router)
    x = _moe(x1, u2, aff, mod, 0, ln_g, ln_b, w_gate, w_up, w_down)
    new_state_rglru = st_c[:, None]

    layer = 1
    lam_init = 0.8 - 0.6 * math.exp(-0.3 * layer)
    cos, sin = _rope_tables()
    rope_groups = (0, 1, 2, 3, 4, 6, 7, 8, 9, 10, 11, 12, 13)
    proj = _inproj((x, x), mod, 1, o_w_in, j, rope=(cos, sin, rope_groups))
    sink = o_sink[j]
    ys_c = _swa_ctx(proj, sink)
    yd_c = _diff_ctx(proj, o_lambda, o_subln_g, j, lam_init)
    ck = cache_swa_k[:, j].reshape(DEC_BATCH, PAST_LEN, SWA_KV_HEADS * HEAD_DIM)
    cv = cache_swa_v[:, j].reshape(DEC_BATCH, PAST_LEN, SWA_KV_HEADS * HEAD_DIM)
    ys_l = _swa_lat(proj, ck, cv, sink)
    cdk = cache_diff_k[:, j].reshape(DEC_BATCH, PAST_LEN, DIFF_HEADS * 2 * HEAD_DIM)
    cdv = cache_diff_v[:, j].reshape(DEC_BATCH, PAST_LEN, DIFF_HEADS * DIFF_V_DIM)
    yd_l = _diff_lat(proj, cdk, cdv, o_lambda, o_subln_g, j, lam_init)
    x1, u2, aff = _outproj((ys_c, ys_l), (yd_c, yd_l), o_w_out, j, (x, x), mod, 1, ln_g, ln_b, w_router)
    y_c, y_l = _moe(x1, u2, aff, mod, 1, ln_g, ln_b, w_gate, w_up, w_down, split_out=True)

    pc = proj[:N_CTX]
    new_cache_swa_k = pc[:, 512:640].reshape(BATCH, 1, SEQ, SWA_KV_HEADS, HEAD_DIM)
    new_cache_swa_v = pc[:, 640:768].reshape(BATCH, 1, SEQ, SWA_KV_HEADS, HEAD_DIM)
    new_cache_diff_k = pc[:, 1280:1792].reshape(BATCH, 1, SEQ, DIFF_HEADS, 2, HEAD_DIM)
    new_cache_diff_v = pc[:, 1792:2304].reshape(BATCH, 1, SEQ, DIFF_HEADS, DIFF_V_DIM)
    y_prompt = y_c.reshape(BATCH, SEQ, D_MODEL)
    y_sample = y_l.reshape(DEC_BATCH, DEC_SEQ, D_MODEL)
    return (y_prompt, y_sample, new_state_rglru, new_cache_swa_k, new_cache_swa_v,
            new_cache_diff_k, new_cache_diff_v)
```

```python
import functools
import math

import numpy as np
import jax
import jax.numpy as jnp
from jax import lax
from jax.experimental import pallas as pl
from jax.experimental.pallas import tpu as pltpu

F32 = jnp.float32
BF16 = jnp.bfloat16
I32 = jnp.int32

D_MODEL = 1024
BATCH, SEQ = 16, 256
DEC_BATCH, DEC_SEQ = 2, 2048
PAST_LEN = 256
DEPTH = 2
GRID_W = 64
HEAD_DIM = 64
D_RG = 768
RG_BLOCK = 64
RG_C = 8.0
CONV_W = 4
D_FNET = 256
FNET_GROUP_DIM = 64
FNET_GROUPS = 4
SWA_HEADS = 8
SWA_KV_HEADS = 2
GQA_GROUP = 4
WINDOW = 128
DIFF_HEADS = 4
DIFF_V_DIM = 128
N_EXPERTS = 16
EC_FACTOR = 2
D_EXPERT = 2048
ROPE_BASE = 10000.0
LN_EPS = 1e-5
NEG_INF = -1e30
ATTN_SCALE = HEAD_DIM ** -0.5
ALPHA = (2 * DEPTH) ** 0.25
EVEN_IN = 2 * D_RG + D_FNET
ODD_IN = 2304

N_CTX = BATCH * SEQ
N_LAT = DEC_BATCH * DEC_SEQ
N_TOK = N_CTX + N_LAT
N_SEG = 1 + DEC_BATCH
CAP = EC_FACTOR * N_CTX // N_EXPERTS
assert N_CTX == N_LAT

LANE = 128
SUBLANE = 8
MXU_DIM = 256
VMEM_LIMIT = 56 * 1024 * 1024
TM = 512
GC = TM
N_CHUNKS = N_CTX // GC
WIN = 128
WIN_ALIGN = 16
TOPK_MIN_EXP = -150.0
TOPK_GEO_STEPS = 24
TOPK_LIN_STEPS = 24
N_TILES = N_TOK // TM
CTX_TILES = N_CTX // TM
LAT_TILES_PER_SEQ = DEC_SEQ // TM


def _cparams(sem):
    return pltpu.CompilerParams(dimension_semantics=sem, vmem_limit_bytes=VMEM_LIMIT)


def _dot(a, b):
    return jnp.dot(a, b, preferred_element_type=F32)


def _dot_nt(a, b):
    return lax.dot_general(a, b, (((1,), (1,)), ((), ())), preferred_element_type=F32)


def _split(a):
    hi = a.astype(BF16)
    lo = (a - hi.astype(F32)).astype(BF16)
    return hi, lo


def _dot3(a, b):
    ah, al = _split(a)
    bh, bl = _split(b)
    return _dot(ah, bh) + (_dot(ah, bl) + _dot(al, bh))


def _layernorm(y, g, b):
    mu = jnp.mean(y, axis=-1, keepdims=True)
    d = y - mu
    var = jnp.mean(d * d, axis=-1, keepdims=True)
    return d * lax.rsqrt(var + LN_EPS) * g + b


def _pair_specs(width, lat_block0=0):
    return [
        pl.BlockSpec((TM, width), lambda i, *_: (jnp.minimum(i, CTX_TILES - 1), 0)),
        pl.BlockSpec((TM, width), lambda i, *_: (jnp.maximum(i - CTX_TILES, 0) + lat_block0, 0)),
    ]


def _pick(i, ctx_ref, lat_ref):
    return jnp.where(i < CTX_TILES, ctx_ref[...], lat_ref[...])


def _seg_of_tile(i):
    return jnp.where(i < CTX_TILES, 0, (i - CTX_TILES) // LAT_TILES_PER_SEQ + 1)


MOD_TN = 1536


def _mod_kernel(c_ref, w_ref, b_ref, o_ref):
    c = c_ref[...]
    s = c * jax.nn.sigmoid(c)
    o_ref[0] = _dot3(s, w_ref[0]) + b_ref[0]


def _modulation(cvec8, w_mod, b_mod):
    n = 6 * D_MODEL
    return pl.pallas_call(
        _mod_kernel,
        grid=(DEPTH, n // MOD_TN),
        in_specs=[
            pl.BlockSpec((SUBLANE, D_MODEL), lambda l, j: (0, 0)),
            pl.BlockSpec((1, D_MODEL, MOD_TN), lambda l, j: (l, 0, j)),
            pl.BlockSpec((1, 1, MOD_TN), lambda l, j: (l, 0, j)),
        ],
        out_specs=pl.BlockSpec((1, SUBLANE, MOD_TN), lambda l, j: (l, 0, j)),
        out_shape=jax.ShapeDtypeStruct((DEPTH, SUBLANE, n), F32),
        compiler_params=_cparams(("arbitrary", "arbitrary")),
        name="modulation",
    )(cvec8, w_mod, b_mod.reshape(DEPTH, 1, n))


def _rope_tile(p, cos, sin, rope_groups):
    lane = lax.broadcasted_iota(I32, (1, LANE), 1)
    first_half = (lane % 32) < 16
    pieces = []
    for k in range(p.shape[1] // LANE):
        xg = p[:, k * LANE:(k + 1) * LANE]
        if k in rope_groups:
            partner = jnp.where(first_half, pltpu.roll(xg, LANE - 16, 1), pltpu.roll(xg, 16, 1))
            xg = xg * cos + partner * sin
        pieces.append(xg)
    return jnp.concatenate(pieces, axis=1)


def _inproj_kernel(*refs, rope_groups, cache_cols):
    if rope_groups:
        xc_ref, xl_ref, mod_ref, w_ref, cos_ref, sin_ref, o_ref = refs[:7]
        cache_refs, wbf_ref = refs[7:-1], refs[-1]
    else:
        xc_ref, xl_ref, mod_ref, w_ref, o_ref, wbf_ref = refs
    i = pl.program_id(0)

    @pl.when(i == 0)
    def _():
        wbf_ref[...] = w_ref[0].astype(BF16)

    m = mod_ref[0, 0]
    u = _pick(i, xc_ref, xl_ref) * (1.0 + m[1:2, :]) + m[0:1, :]
    p = _dot(u.astype(BF16), wbf_ref[...])
    if rope_groups:
        @pl.when(i < CTX_TILES)
        def _():
            o_ref[...] = p.astype(BF16)
            for ref, (start, width) in zip(cache_refs, cache_cols):
                ref[...] = p[:, start:start + width]

        @pl.when(i >= CTX_TILES)
        def _():
            o_ref[...] = _rope_tile(p, cos_ref[...], sin_ref[...], rope_groups).astype(BF16)
    else:
        o_ref[...] = p


def _inproj(x_pair, mod, layer, w_in, j, rope=None, cache_cols=()):
    n = w_in.shape[-1]
    rope_groups = ()
    combined = x_pair[0] is x_pair[1]
    in_specs = _pair_specs(D_MODEL, CTX_TILES if combined else 0) + [
        pl.BlockSpec((1, 1, 6, D_MODEL), lambda i: (layer, _seg_of_tile(i), 0, 0)),
        pl.BlockSpec((1, D_MODEL, n), lambda i: (j, 0, 0)),
    ]
    args = [x_pair[0], x_pair[1], mod, w_in]
    out_specs = pl.BlockSpec((TM, n), lambda i: (i, 0))
    out_shape = jax.ShapeDtypeStruct((N_TOK, n), F32)
    if rope is not None:
        cos, sin, rope_groups = rope
        tab = pl.BlockSpec((TM, LANE), lambda i: (jnp.maximum(i - CTX_TILES, 0) % LAT_TILES_PER_SEQ, 0))
        in_specs += [tab, tab]
        args += [cos, sin]
        out_specs = [out_specs] + [_pair_specs(width)[0] for _, width in cache_cols]
        out_shape = [jax.ShapeDtypeStruct((N_TOK, n), BF16)] + [
            jax.ShapeDtypeStruct((N_CTX, width), F32) for _, width in cache_cols]
    return pl.pallas_call(
        functools.partial(_inproj_kernel, rope_groups=rope_groups, cache_cols=cache_cols),
        grid=(N_TILES,),
        in_specs=in_specs,
        out_specs=out_specs,
        out_shape=out_shape,
        scratch_shapes=[pltpu.VMEM((D_MODEL, n), BF16)],
        compiler_params=_cparams(("arbitrary",)),
        name="inproj_rope" if rope is not None else "inproj",
    )(*args)


def _rope_tables():
    s = np.arange(DEC_SEQ)
    row, col = (s // GRID_W).astype(np.float64), (s % GRID_W).astype(np.float64)
    nf = HEAD_DIM // 4
    inv = np.power(ROPE_BASE, -np.arange(nf, dtype=np.float64) / nf)
    ang = np.concatenate([row[:, None] * inv, row[:, None] * inv, col[:, None] * inv, col[:, None] * inv], axis=1)
    sign = np.concatenate([-np.ones(nf), np.ones(nf), -np.ones(nf), np.ones(nf)])
    cos = np.tile(np.cos(ang), (1, LANE // HEAD_DIM))
    sin = np.tile(np.sin(ang) * sign, (1, LANE // HEAD_DIM))
    return jnp.asarray(cos, F32), jnp.asarray(sin, F32)


RG_T = 256
RG_TILES = D_RG // MXU_DIM


def _sigmoid(x):
    return 0.5 * jnp.tanh(0.5 * x) + 0.5


def _softplus(x):
    u = jnp.exp(-jnp.abs(x))
    w = 1.0 + u
    l1p = jnp.where(w == 1.0, u, jnp.log(w) * (u / jnp.where(w == 1.0, 1.0, w - 1.0)))
    return jnp.maximum(x, 0.0) + l1p


def _rglru_kernel(xa_ref, ga_ref, h0_ref, cw_ref, cb_ref, wg_ref, br_ref, bi_ref, lam_ref,
                  y_ref, st_ref, hf_s, xc_s, a_s, b_s, hs_s, *, seq_len):
    nchunk = seq_len // RG_T
    cw = cw_ref[0]
    cb = cb_ref[0]

    def conv_chunk(c):
        base = pl.multiple_of(c * RG_T, RG_T)
        cur = xa_ref[pl.ds(base, RG_T), :]
        prev = xa_ref[pl.ds(pl.multiple_of(jnp.maximum(base - SUBLANE, 0), SUBLANE), SUBLANE), :]
        prev = jnp.where(c > 0, prev, 0.0)
        nxt = xa_ref[pl.ds(pl.multiple_of(jnp.minimum(base + RG_T, seq_len - SUBLANE), SUBLANE), SUBLANE), :]
        nxt = jnp.where(c < nchunk - 1, nxt, 0.0)
        win = jnp.concatenate([prev, cur, nxt], axis=0)
        xc = cb
        for k in range(CONV_W):
            off = SUBLANE - 1 + k
            xc = xc + win[off:off + RG_T] * cw[k:k + 1, :]
        return xc

    def gates(xc, z):
        xb = xc.astype(BF16)
        rs, gs = [], []
        for t in range(RG_TILES):
            zz = _dot(xb[:, t * MXU_DIM:(t + 1) * MXU_DIM], wg_ref[0, z, t].astype(BF16))
            rs.append(zz[:, :MXU_DIM])
            gs.append(zz[:, MXU_DIM:])
        r = _sigmoid(jnp.concatenate(rs, axis=1) + br_ref[0, z:z + 1, :])
        g = _sigmoid(jnp.concatenate(gs, axis=1) + bi_ref[0, z:z + 1, :])
        log_a = (-RG_C * _softplus(-lam_ref[0, z:z + 1, :])) * r
        a = jnp.exp(log_a)
        bt = jnp.sqrt(-jnp.tanh(log_a) * (a * a + 1.0)) * g * xc
        return a, bt

    def scan_chunk(h, reverse):
        def step(t, h):
            tt = RG_T - 1 - t if reverse else t
            h = a_s[pl.ds(tt, 1), :] * h + b_s[pl.ds(tt, 1), :]
            hs_s[pl.ds(tt, 1), :] = h
            return h
        return lax.fori_loop(0, RG_T, step, h, unroll=8)

    def fwd_chunk(c, h):
        rows = pl.ds(pl.multiple_of(c * RG_T, RG_T), RG_T)
        xc = conv_chunk(c)
        xc_s[rows, :] = xc
        a, bt = gates(xc, 0)
        a_s[...] = a
        b_s[...] = bt
        h = scan_chunk(h, False)
        hf_s[rows, :] = hs_s[...]
        return h

    def bwd_chunk(k, h):
        c = nchunk - 1 - k
        rows = pl.ds(pl.multiple_of(c * RG_T, RG_T), RG_T)
        a, bt = gates(xc_s[rows, :], 1)
        a_s[...] = a
        b_s[...] = bt
        h = scan_chunk(h, True)
        y_ref[rows, :] = ((hf_s[rows, :] + hs_s[...]) * jax.nn.gelu(ga_ref[rows, :])).astype(BF16)
        return h

    hf = lax.fori_loop(0, nchunk, fwd_chunk, h0_ref[0, 0:1, :])
    hb = lax.fori_loop(0, nchunk, bwd_chunk, h0_ref[0, 1:2, :])
    st_ref[0, 0:1, :] = hf
    st_ref[0, 1:2, :] = hb


def _rglru(proj, h0, j, cw, cb, wg, br, bi, lam, *, seq_len, nseq, row_block0):
    wspec3 = lambda shape: pl.BlockSpec((1,) + shape, lambda b: (j,) + (0,) * len(shape))
    return pl.pallas_call(
        functools.partial(_rglru_kernel, seq_len=seq_len),
        grid=(nseq,),
        in_specs=[
            pl.BlockSpec((seq_len, D_RG), lambda b: (row_block0 + b, 0)),
            pl.BlockSpec((seq_len, D_RG), lambda b: (row_block0 + b, 1)),
            pl.BlockSpec((1, 2, D_RG), lambda b: (b, 0, 0)),
            wspec3((CONV_W, D_RG)),
            wspec3((1, D_RG)),
            wspec3((2, RG_TILES, MXU_DIM, 2 * MXU_DIM)),
            wspec3((2, D_RG)),
            wspec3((2, D_RG)),
            wspec3((2, D_RG)),
        ],
        out_specs=[
            pl.BlockSpec((seq_len, D_RG), lambda b: (b, 0)),
            pl.BlockSpec((1, 2, D_RG), lambda b: (b, 0, 0)),
        ],
        out_shape=[
            jax.ShapeDtypeStruct((nseq * seq_len, D_RG), BF16),
            jax.ShapeDtypeStruct((nseq, 2, D_RG), F32),
        ],
        scratch_shapes=[
            pltpu.VMEM((seq_len, D_RG), F32),
            pltpu.VMEM((seq_len, D_RG), F32),
            pltpu.VMEM((RG_T, D_RG), F32),
            pltpu.VMEM((RG_T, D_RG), F32),
            pltpu.VMEM((RG_T, D_RG), F32),
        ],
        compiler_params=_cparams(("arbitrary",)),
        name=f"rglru_s{seq_len}",
    )(proj, proj, h0, cw, cb, wg, br, bi, lam)


def _gate_tiles(w_r, w_i):
    per_tile = MXU_DIM // RG_BLOCK
    eye = jnp.eye(per_tile, dtype=F32)

    def tiles(w):
        w = w.reshape(w.shape[0], 2, RG_TILES, per_tile, RG_BLOCK, RG_BLOCK)
        return jnp.einsum('nztbcd,be->nztbced', w, eye).reshape(w.shape[0], 2, RG_TILES, MXU_DIM, MXU_DIM)

    return jnp.concatenate([tiles(w_r), tiles(w_i)], axis=-1)


def _fnet_kernel(z_ref, cc_ref, sc_ref, w_ref, cs_ref, ns_ref, y_ref, p_s, q_s):
    @pl.when(pl.program_id(1) == 0)
    def _():
        w = w_ref[0]
        a = _dot3(cc_ref[...], w).astype(BF16)
        b = _dot3(sc_ref[...], w).astype(BF16)
        zb = z_ref[...].astype(BF16)
        p_s[...] = _dot(zb, a).astype(BF16)
        q_s[...] = _dot(zb, b).astype(BF16)

    y_ref[...] = (_dot(cs_ref[...].astype(BF16), p_s[...])
                  + _dot(ns_ref[...].astype(BF16), q_s[...])).astype(BF16)


def _fnet(proj, j, wbd, *, seq_len, nseq, row_block0):
    tr = min(seq_len, 512)
    nrt = seq_len // tr
    cc, sc = _dft_tables(FNET_GROUP_DIM)
    ccbd = jnp.asarray(np.kron(np.eye(FNET_GROUPS), cc), F32)
    scbd = jnp.asarray(np.kron(np.eye(FNET_GROUPS), sc), F32)
    cs, ss = _dft_tables(seq_len)
    cs_b, ns_b = jnp.asarray(cs, F32), jnp.asarray(-ss, F32)
    zcol = (2 * D_RG) // D_FNET
    const = lambda shape: pl.BlockSpec(shape, lambda b, r: (0,) * len(shape))
    return pl.pallas_call(
        _fnet_kernel,
        grid=(nseq, nrt),
        in_specs=[
            pl.BlockSpec((seq_len, D_FNET), lambda b, r: (row_block0 + b, zcol)),
            const((D_FNET, D_FNET)),
            const((D_FNET, D_FNET)),
            pl.BlockSpec((1, D_FNET, D_FNET), lambda b, r: (j, 0, 0)),
            pl.BlockSpec((tr, seq_len), lambda b, r: (r, 0)),
            pl.BlockSpec((tr, seq_len), lambda b, r: (r, 0)),
        ],
        out_specs=pl.BlockSpec((tr, D_FNET), lambda b, r: (b * nrt + r, 0)),
        out_shape=jax.ShapeDtypeStruct((nseq * seq_len, D_FNET), BF16),
        scratch_shapes=[pltpu.VMEM((seq_len, D_FNET), BF16), pltpu.VMEM((seq_len, D_FNET), BF16)],
        compiler_params=_cparams(("arbitrary", "arbitrary")),
        name=f"fnet_s{seq_len}",
    )(proj, ccbd, scbd, wbd, cs_b, ns_b)


def _dft_tables(n):
    k = np.arange(n)
    ang = 2.0 * np.pi * ((k[:, None] * k[None, :]) % n) / n
    return np.cos(ang) / np.sqrt(n), np.sin(ang) / np.sqrt(n)


def _fnet_blockdiag(w_f):
    eye = jnp.eye(FNET_GROUPS, dtype=F32)
    return jnp.einsum('ngcd,gh->ngchd', w_f, eye).reshape(w_f.shape[0], D_FNET, D_FNET)


def _outproj_kernel(ac_ref, al_ref, bc_ref, bl_ref, w_ref, xc_ref, xl_ref, mod_ref, g_ref, be_ref, wr_ref,
                    x1_ref, u2_ref, aff_ref, wbf_ref, *, ka):
    i = pl.program_id(0)

    @pl.when(i == 0)
    def _():
        wbf_ref[...] = w_ref[0].astype(BF16)

    out = _dot(_pick(i, ac_ref, al_ref), wbf_ref[0:ka, :]) + _dot(_pick(i, bc_ref, bl_ref), wbf_ref[ka:, :])
    m = mod_ref[0, 0]
    x1 = _layernorm(ALPHA * _pick(i, xc_ref, xl_ref) + m[2:3, :] * out, g_ref[0, 0:1, :], be_ref[0, 0:1, :])
    x1_ref[...] = x1
    u2 = x1 * (1.0 + m[4:5, :]) + m[3:4, :]
    u2_ref[...] = u2.astype(BF16)
    uh, ul = _split(u2)
    wh, wl = _split(wr_ref[0])
    lg = _dot(jnp.concatenate([uh, ul], axis=0), jnp.concatenate([wh, wl], axis=1))
    logits = (lg[:TM, :N_EXPERTS] + lg[:TM, N_EXPERTS:]) + (lg[TM:, :N_EXPERTS] + lg[TM:, N_EXPERTS:])
    mx = jnp.max(logits, axis=-1, keepdims=True)
    ex = jnp.exp(logits - mx)
    aff_ref[...] = ex / jnp.sum(ex, axis=-1, keepdims=True)


def _outproj(a_pair, b_pair, w_out, j, x_pair, mod, layer, ln_g, ln_b, w_router):
    ka, kb = a_pair[0].shape[1], b_pair[0].shape[1]
    combined = x_pair[0] is x_pair[1]
    return pl.pallas_call(
        functools.partial(_outproj_kernel, ka=ka),
        grid=(N_TILES,),
        in_specs=_pair_specs(ka) + _pair_specs(kb) + [
            pl.BlockSpec((1, D_MODEL, D_MODEL), lambda i: (j, 0, 0)),
        ] + _pair_specs(D_MODEL, CTX_TILES if combined else 0) + [
            pl.BlockSpec((1, 1, 6, D_MODEL), lambda i: (layer, _seg_of_tile(i), 0, 0)),
            pl.BlockSpec((1, 2, D_MODEL), lambda i: (layer, 0, 0)),
            pl.BlockSpec((1, 2, D_MODEL), lambda i: (layer, 0, 0)),
            pl.BlockSpec((1, D_MODEL, N_EXPERTS), lambda i: (layer, 0, 0)),
        ],
        out_specs=[
            pl.BlockSpec((TM, D_MODEL), lambda i: (i, 0)),
            pl.BlockSpec((TM, D_MODEL), lambda i: (i, 0)),
            pl.BlockSpec((TM, N_EXPERTS), lambda i: (i, 0)),
        ],
        out_shape=[
            jax.ShapeDtypeStruct((N_TOK, D_MODEL), F32),
            jax.ShapeDtypeStruct((N_TOK, D_MODEL), BF16),
            jax.ShapeDtypeStruct((N_TOK, N_EXPERTS), F32),
        ],
        scratch_shapes=[pltpu.VMEM((D_MODEL, D_MODEL), BF16)],
        compiler_params=_cparams(("arbitrary",)),
        name="outproj_ln_router",
    )(*a_pair, *b_pair, w_out, *x_pair, mod, ln_g, ln_b, w_router)


def _topk_kernel(aff_ref, pos_ref, cnt_ref):
    aff = aff_ref[...]
    cap = jnp.float32(CAP)

    def count_ge(t):
        return jnp.sum(jnp.where(aff >= t, 1.0, 0.0), axis=1, keepdims=True)

    lo_v = jnp.zeros((N_EXPERTS, 1), F32)
    hi_v = jnp.full((N_EXPERTS, 1), 2.0, F32)
    lo_t = jnp.full((N_EXPERTS, 1), TOPK_MIN_EXP, F32)
    hi_t = jnp.full((N_EXPERTS, 1), 1.0, F32)
    for _ in range(TOPK_GEO_STEPS):
        mid_t = 0.5 * (lo_t + hi_t)
        cand = jnp.exp2(mid_t)
        ok = count_ge(cand) >= cap
        lo_t = jnp.where(ok, mid_t, lo_t)
        hi_t = jnp.where(ok, hi_t, mid_t)
        lo_v = jnp.where(ok, jnp.maximum(lo_v, cand), lo_v)
        hi_v = jnp.where(ok, hi_v, jnp.minimum(hi_v, cand))
    for _ in range(TOPK_LIN_STEPS):
        cand = lo_v + 0.5 * (hi_v - lo_v)
        ok = count_ge(cand) >= cap
        lo_v = jnp.where(ok, cand, lo_v)
        hi_v = jnp.where(ok, hi_v, cand)
    gt = aff >= hi_v
    eq = (aff >= lo_v) & (aff < hi_v)
    need = cap - jnp.sum(jnp.where(gt, 1.0, 0.0), axis=1, keepdims=True)

    tri = (lax.broadcasted_iota(I32, (LANE, LANE), 0) <= lax.broadcasted_iota(I32, (LANE, LANE), 1))
    tri = jnp.where(tri, 1.0, 0.0).astype(BF16)

    def cumsum_blocks(mask_f32):
        carry = jnp.zeros((N_EXPERTS, 1), F32)
        outs = []
        for k in range(mask_f32.shape[1] // LANE):
            blk = mask_f32[:, k * LANE:(k + 1) * LANE]
            inc = _dot(blk.astype(BF16), tri) + carry
            outs.append(inc)
            carry = carry + jnp.sum(blk, axis=1, keepdims=True)
        return outs

    eq_f = jnp.where(eq, 1.0, 0.0)
    gt_f = jnp.where(gt, 1.0, 0.0)
    eq_rank = cumsum_blocks(eq_f)
    sel_blocks = []
    for k, rk in enumerate(eq_rank):
        lanes = slice(k * LANE, (k + 1) * LANE)
        sel_blocks.append(jnp.where((eq_f[:, lanes] > 0.0) & (rk <= need), 1.0, gt_f[:, lanes]))
    sel = jnp.concatenate(sel_blocks, axis=1)
    sel_rank = cumsum_blocks(sel)
    for k, rk in enumerate(sel_rank):
        lanes = slice(k * LANE, (k + 1) * LANE)
        pos_ref[:, lanes] = jnp.where(sel[:, lanes] > 0.0, rk.astype(I32) - 1, -1)
    lane = lax.broadcasted_iota(I32, (N_EXPERTS, LANE), 1)
    cnt = jnp.zeros((N_EXPERTS, LANE), F32)
    for c in range(1, N_CHUNKS + 1):
        cnt = jnp.where(lane == c, sel_rank[c * (GC // LANE) - 1][:, LANE - 1:LANE], cnt)
    cnt_ref[0] = cnt.astype(I32)


def _topk(aff_t):
    return pl.pallas_call(
        _topk_kernel,
        grid=(2,),
        in_specs=[pl.BlockSpec((N_EXPERTS, N_CTX), lambda g: (0, g))],
        out_specs=[
            pl.BlockSpec((N_EXPERTS, N_CTX), lambda g: (0, g)),
            pl.BlockSpec((1, N_EXPERTS, LANE), lambda g: (g, 0, 0)),
        ],
        out_shape=[
            jax.ShapeDtypeStruct((N_EXPERTS, N_TOK), I32),
            jax.ShapeDtypeStruct((2, N_EXPERTS, LANE), I32),
        ],
        compiler_params=_cparams(("arbitrary",)),
        name="topk_select",
    )(aff_t)


def _chunk_windows(cnt_ref, g, e, c):
    c0 = cnt_ref[g, e, c]
    c1 = cnt_ref[g, e, c + 1]
    base = (c0 // WIN_ALIGN) * WIN_ALIGN
    k = jnp.where(c1 > c0, (c1 - base + WIN - 1) // WIN, 0)
    return base, k


def _gather_kernel(cnt_ref, pos_ref, aff_ref, u_ref, xe_ref, gs_ref, acc_s, gacc_s):
    g = pl.program_id(0)
    e = pl.program_id(1)
    acc_s[...] = jnp.zeros_like(acc_s)
    gacc_s[...] = jnp.zeros_like(gacc_s)
    prow = pos_ref[pl.ds(e, 1), :]
    arow = aff_ref[pl.ds(e, 1), :]
    slot0 = lax.broadcasted_iota(I32, (WIN, GC), 0)

    def add_window(c, lo):
        cols = slice(c * GC, (c + 1) * GC)
        pc = prow[:, cols]
        start = pl.multiple_of(jnp.minimum(lo, CAP - WIN), WIN_ALIGN)
        hit = (slot0 + start) == jnp.where(pc >= lo, pc, -1)
        rows = pl.ds(start, WIN)
        acc_s[rows, :] += _dot(jnp.where(hit, 1.0, 0.0).astype(BF16), u_ref[cols, :])
        gacc_s[rows, :] += jnp.sum(jnp.where(hit, arow[:, cols], 0.0), axis=1, keepdims=True)

    windows = [_chunk_windows(cnt_ref, g, e, c) for c in range(N_CHUNKS)]
    for c, (base, _) in enumerate(windows):
        add_window(c, base)
    for c, (base, k) in enumerate(windows):
        def extra(w, carry, c=c, base=base):
            add_window(c, base + w * WIN)
            return carry
        lax.fori_loop(1, k, extra, 0)
    xe_ref[0, 0] = acc_s[...].astype(BF16)
    gs_ref[0, 0] = gacc_s[...]


def _gather(cnt, pos_t, aff_t, u2):
    return pl.pallas_call(
        _gather_kernel,
        grid_spec=pltpu.PrefetchScalarGridSpec(
            num_scalar_prefetch=1,
            grid=(2, N_EXPERTS),
            in_specs=[
                pl.BlockSpec((N_EXPERTS, N_CTX), lambda g, e, cnt: (0, g)),
                pl.BlockSpec((N_EXPERTS, N_CTX), lambda g, e, cnt: (0, g)),
                pl.BlockSpec((N_CTX, D_MODEL), lambda g, e, cnt: (g, 0)),
            ],
            out_specs=[
                pl.BlockSpec((1, 1, CAP, D_MODEL), lambda g, e, cnt: (e, g, 0, 0)),
                pl.BlockSpec((1, 1, CAP, 1), lambda g, e, cnt: (e, g, 0, 0)),
            ],
            scratch_shapes=[pltpu.VMEM((CAP, D_MODEL), F32), pltpu.VMEM((CAP, 1), F32)],
        ),
        out_shape=[
            jax.ShapeDtypeStruct((N_EXPERTS, 2, CAP, D_MODEL), BF16),
            jax.ShapeDtypeStruct((N_EXPERTS, 2, CAP, 1), F32),
        ],
        compiler_params=_cparams(("arbitrary", "arbitrary")),
        name="moe_gather",
    )(cnt, pos_t, aff_t, u2)


TF = 1024


def _ffn_kernel(xe_ref, gs_ref, wg_ref, wu_ref, wd_ref, ye_ref, acc_ref):
    f = pl.program_id(1)
    x = xe_ref[0]
    hg = _dot(x, wg_ref[0, 0].astype(BF16))
    hu = _dot(x, wu_ref[0, 0].astype(BF16))
    h = (hg * _sigmoid(hg) * hu).astype(BF16)

    @pl.when(f == 0)
    def _():
        acc_ref[...] = jnp.zeros_like(acc_ref)

    acc_ref[...] += _dot(h, wd_ref[0, 0].astype(BF16))

    @pl.when(f == pl.num_programs(1) - 1)
    def _():
        ye_ref[0] = (acc_ref[...] * gs_ref[0]).astype(BF16)


def _ffn(xe, gs, layer, w_gate, w_up, w_down):
    rows = 2 * CAP
    return pl.pallas_call(
        _ffn_kernel,
        grid=(N_EXPERTS, D_EXPERT // TF),
        in_specs=[
            pl.BlockSpec((1, rows, D_MODEL), lambda e, f: (e, 0, 0)),
            pl.BlockSpec((1, rows, 1), lambda e, f: (e, 0, 0)),
            pl.BlockSpec((1, 1, D_MODEL, TF), lambda e, f: (layer, e, 0, f)),
            pl.BlockSpec((1, 1, D_MODEL, TF), lambda e, f: (layer, e, 0, f)),
            pl.BlockSpec((1, 1, TF, D_MODEL), lambda e, f: (layer, e, f, 0)),
        ],
        out_specs=pl.BlockSpec((1, rows, D_MODEL), lambda e, f: (e, 0, 0)),
        out_shape=jax.ShapeDtypeStruct((N_EXPERTS, rows, D_MODEL), BF16),
        scratch_shapes=[pltpu.VMEM((rows, D_MODEL), F32)],
        compiler_params=_cparams(("arbitrary", "arbitrary")),
        name="moe_ffn",
    )(xe, gs, w_gate, w_up, w_down)


def _combine_kernel(*refs, split_out):
    if split_out:
        cnt_ref, posn_ref, ye_ref, x1_ref, mod_ref, g_ref, be_ref, oc_ref, ol_ref, acc_s = refs
    else:
        cnt_ref, posn_ref, ye_ref, x1_ref, mod_ref, g_ref, be_ref, o_ref, acc_s = refs
    i = pl.program_id(0)
    g = i // N_CHUNKS
    c = i % N_CHUNKS
    posn = posn_ref[...]
    slot0 = lax.broadcasted_iota(I32, (TM, WIN), 1)

    def window(e, lo):
        col = posn[:, e:e + 1]
        start = pl.multiple_of(jnp.minimum(lo, CAP - WIN), WIN_ALIGN)
        hit = (slot0 + start) == jnp.where(col >= lo, col, -1)
        return jnp.where(hit, 1.0, 0.0).astype(BF16), ye_ref[e, 0, pl.ds(start, WIN), :]

    windows = [_chunk_windows(cnt_ref, g, e, c) for e in range(N_EXPERTS)]
    acc = None
    for e in range(0, N_EXPERTS, 2):
        oh0, ye0 = window(e, windows[e][0])
        oh1, ye1 = window(e + 1, windows[e + 1][0])
        d = _dot(jnp.concatenate([oh0, oh1], axis=1), jnp.concatenate([ye0, ye1], axis=0))
        acc = d if acc is None else acc + d
    acc_s[...] = acc
    for e, (base, k) in enumerate(windows):
        def extra(w, carry, e=e, base=base):
            oh, ye = window(e, base + w * WIN)
            acc_s[...] += _dot(oh, ye)
            return carry
        lax.fori_loop(1, k, extra, 0)
    m = mod_ref[0, 0]
    y = ALPHA * x1_ref[...] + m[5:6, :] * acc_s[...]
    res = _layernorm(y, g_ref[0, 1:2, :], be_ref[0, 1:2, :])
    if split_out:
        @pl.when(i < CTX_TILES)
        def _():
            oc_ref[...] = res

        @pl.when(i >= CTX_TILES)
        def _():
            ol_ref[...] = res
    else:
        o_ref[...] = res


def _combine(cnt, pos_n, ye, x1, mod, layer, ln_g, ln_b, split_out):
    if split_out:
        out_specs = _pair_specs(D_MODEL)
        out_shape = [jax.ShapeDtypeStruct((N_CTX, D_MODEL), F32), jax.ShapeDtypeStruct((N_LAT, D_MODEL), F32)]
    else:
        out_specs = pl.BlockSpec((TM, D_MODEL), lambda i, cnt: (i, 0))
        out_shape = jax.ShapeDtypeStruct((N_TOK, D_MODEL), F32)
    return pl.pallas_call(
        functools.partial(_combine_kernel, split_out=split_out),
        grid_spec=pltpu.PrefetchScalarGridSpec(
            num_scalar_prefetch=1,
            grid=(N_TILES,),
            in_specs=[
                pl.BlockSpec((TM, N_EXPERTS), lambda i, cnt: (i, 0)),
                pl.BlockSpec((N_EXPERTS, 1, CAP, D_MODEL), lambda i, cnt: (0, i // N_CHUNKS, 0, 0)),
                pl.BlockSpec((TM, D_MODEL), lambda i, cnt: (i, 0)),
                pl.BlockSpec((1, 1, 6, D_MODEL), lambda i, cnt: (layer, _seg_of_tile(i), 0, 0)),
                pl.BlockSpec((1, 2, D_MODEL), lambda i, cnt: (layer, 0, 0)),
                pl.BlockSpec((1, 2, D_MODEL), lambda i, cnt: (layer, 0, 0)),
            ],
            out_specs=out_specs,
            scratch_shapes=[pltpu.VMEM((TM, D_MODEL), F32)],
        ),
        out_shape=out_shape,
        compiler_params=_cparams(("arbitrary",)),
        name="moe_combine_ln",
    )(cnt, pos_n, ye, x1, mod, ln_g, ln_b)


def _moe(x1, u2, aff, mod, layer, ln_g, ln_b, w_gate, w_up, w_down, split_out=False):
    aff_t = aff.T
    pos_t, cnt = _topk(aff_t)
    xe, gs = _gather(cnt, pos_t, aff_t, u2)
    ye = _ffn(xe.reshape(N_EXPERTS, 2 * CAP, D_MODEL), gs.reshape(N_EXPERTS, 2 * CAP, 1),
              layer, w_gate, w_up, w_down)
    return _combine(cnt, pos_t.T, ye.reshape(N_EXPERTS, 2, CAP, D_MODEL), x1, mod, layer, ln_g, ln_b, split_out)


def _sink_softmax_pv(s, sink_col, v):
    m = jnp.maximum(jnp.max(s, axis=-1, keepdims=True), sink_col)
    p = jnp.exp(s - m)
    den = jnp.sum(p, axis=-1, keepdims=True) + jnp.exp(sink_col - m)
    return _dot(p.astype(BF16), v) / den


def _sink_column(sink_ref, h, rows_per_group):
    rid = lax.broadcasted_iota(I32, (GQA_GROUP * rows_per_group, 1), 0) // rows_per_group
    col = jnp.zeros((GQA_GROUP * rows_per_group, 1), F32)
    for g in range(GQA_GROUP):
        col = jnp.where(rid == g, sink_ref[h * GQA_GROUP + g], col)
    return col


def _stack_groups(q, h):
    return jnp.concatenate(
        [q[:, (h * GQA_GROUP + g) * HEAD_DIM:(h * GQA_GROUP + g + 1) * HEAD_DIM] * ATTN_SCALE
         for g in range(GQA_GROUP)], axis=0).astype(BF16)


def _unstack_groups(outs, rows):
    return jnp.concatenate([o[g * rows:(g + 1) * rows] for o in outs for g in range(GQA_GROUP)], axis=1)


def _swa_ctx_kernel(sink_ref, q_ref, k_ref, v_ref, o_ref):
    q, k, v = q_ref[...], k_ref[...], v_ref[...]
    outs = []
    for h in range(SWA_KV_HEADS):
        cols = slice(h * HEAD_DIM, (h + 1) * HEAD_DIM)
        s = _dot_nt(_stack_groups(q, h), k[:, cols].astype(BF16))
        outs.append(_sink_softmax_pv(s, _sink_column(sink_ref, h, SEQ), v[:, cols].astype(BF16)))
    o_ref[...] = _unstack_groups(outs, SEQ).astype(BF16)


def _swa_ctx(proj, sink):
    return pl.pallas_call(
        _swa_ctx_kernel,
        grid=(BATCH,),
        in_specs=[
            pl.BlockSpec(memory_space=pltpu.SMEM),
            pl.BlockSpec((SEQ, 512), lambda b: (b, 0)),
            pl.BlockSpec((SEQ, LANE), lambda b: (b, 4)),
            pl.BlockSpec((SEQ, LANE), lambda b: (b, 5)),
        ],
        out_specs=pl.BlockSpec((SEQ, 512), lambda b: (b, 0)),
        out_shape=jax.ShapeDtypeStruct((N_CTX, 512), BF16),
        compiler_params=_cparams(("arbitrary",)),
        name="swa_context",
    )(sink, proj, proj, proj)


QB = 128


def _swa_lat_kernel(sink_ref, q_ref, kp_ref, kc_ref, kn_ref, vp_ref, vc_ref, vn_ref, ck_ref, cv_ref, o_ref):
    qb = pl.program_id(1)
    q = q_ref[...]
    kall = jnp.concatenate([ck_ref[0].astype(BF16), kp_ref[...], kc_ref[...], kn_ref[...]], axis=0)
    vall = jnp.concatenate([cv_ref[0].astype(BF16), vp_ref[...], vc_ref[...], vn_ref[...]], axis=0)
    nk = PAST_LEN + 3 * QB
    rows = GQA_GROUP * QB
    col = lax.broadcasted_iota(I32, (rows, nk), 1)
    qpos = qb * QB + lax.broadcasted_iota(I32, (rows, nk), 0) % QB
    kpos = (qb - 1) * QB + col - PAST_LEN
    ok = (col < PAST_LEN) | ((jnp.abs(qpos - kpos) <= WINDOW) & (kpos >= 0) & (kpos < DEC_SEQ))
    outs = []
    for h in range(SWA_KV_HEADS):
        cols = slice(h * HEAD_DIM, (h + 1) * HEAD_DIM)
        s = _dot_nt(_stack_groups(q, h), kall[:, cols])
        s = jnp.where(ok, s, NEG_INF)
        outs.append(_sink_softmax_pv(s, _sink_column(sink_ref, h, QB), vall[:, cols]))
    o_ref[...] = _unstack_groups(outs, QB).astype(BF16)


def _swa_lat(proj, cache_k, cache_v, sink):
    nqb = DEC_SEQ // QB
    row0 = N_CTX // QB

    def blk(col, shift):
        return pl.BlockSpec((QB, LANE), lambda b, i: (row0 + b * nqb + jnp.clip(i + shift, 0, nqb - 1), col))

    return pl.pallas_call(
        _swa_lat_kernel,
        grid=(DEC_BATCH, nqb),
        in_specs=[
            pl.BlockSpec(memory_space=pltpu.SMEM),
            pl.BlockSpec((QB, 512), lambda b, i: (row0 + b * nqb + i, 0)),
            blk(4, -1), blk(4, 0), blk(4, 1),
            blk(5, -1), blk(5, 0), blk(5, 1),
            pl.BlockSpec((1, PAST_LEN, LANE), lambda b, i: (b, 0, 0)),
            pl.BlockSpec((1, PAST_LEN, LANE), lambda b, i: (b, 0, 0)),
        ],
        out_specs=pl.BlockSpec((QB, 512), lambda b, i: (b * nqb + i, 0)),
        out_shape=jax.ShapeDtypeStruct((N_LAT, 512), BF16),
        compiler_params=_cparams(("arbitrary", "arbitrary")),
        name="swa_latent",
    )(sink, proj, proj, proj, proj, proj, proj, proj, cache_k, cache_v)


def _diff_lambda_col(lam_ref, lam_init):
    l1 = jnp.sum(lam_ref[0, 0] * lam_ref[0, 1], axis=-1, keepdims=True)
    l2 = jnp.sum(lam_ref[0, 2] * lam_ref[0, 3], axis=-1, keepdims=True)
    return jnp.exp(l1) - jnp.exp(l2) + lam_init


def _softmax_pv(q, k, v):
    s = _dot_nt(q, k)
    p = jnp.exp(s - jnp.max(s, axis=-1, keepdims=True))
    return _dot(p.astype(BF16), v) / jnp.sum(p, axis=-1, keepdims=True)


def _diff_head(q, k, v, lam, subln, lam_init):
    o = [_softmax_pv((q[:, m * HEAD_DIM:(m + 1) * HEAD_DIM] * ATTN_SCALE).astype(BF16),
                     k[:, m * HEAD_DIM:(m + 1) * HEAD_DIM], v) for m in range(2)]
    o = o[0] - lam * o[1]
    return o * lax.rsqrt(jnp.mean(o * o, axis=-1, keepdims=True) + LN_EPS) * subln * (1.0 - lam_init)


def _diff_kernel(*refs, lam_init, has_cache, nh):
    if has_cache:
        lam_ref, sg_ref, q_ref, k_ref, v_ref, ck_ref, cv_ref, o_ref = refs
    else:
        lam_ref, sg_ref, q_ref, k_ref, v_ref, o_ref = refs
    hp = pl.program_id(1)
    lam_all = _diff_lambda_col(lam_ref, lam_init)
    hid = lax.broadcasted_iota(I32, (DIFF_HEADS, 1), 0)
    q, k, v = q_ref[...], k_ref[...], v_ref[...]
    if has_cache:
        k = jnp.concatenate([k, ck_ref[0].astype(BF16)], axis=0)
        v = jnp.concatenate([v, cv_ref[0].astype(BF16)], axis=0)
    outs = []
    for j in range(nh):
        lam = jnp.sum(jnp.where(hid == hp * nh + j, lam_all, 0.0), axis=0, keepdims=True)
        cols = slice(j * DIFF_V_DIM, (j + 1) * DIFF_V_DIM)
        outs.append(_diff_head(q[:, cols], k[:, cols], v[:, cols], lam, sg_ref[...], lam_init))
    o_ref[...] = jnp.concatenate(outs, axis=1).astype(BF16)


DIFF_STEP_HEADS = 2
DQ = 512
Q_COL, K_COL, V_COL = 768, 1280, 1792
assert all(col % (DIFF_STEP_HEADS * DIFF_V_DIM) == 0 for col in (Q_COL, K_COL, V_COL))


def _diff_ctx(proj, lam_p, subln, j, lam_init):
    nh = DIFF_STEP_HEADS
    w = nh * DIFF_V_DIM
    return pl.pallas_call(
        functools.partial(_diff_kernel, lam_init=lam_init, has_cache=False, nh=nh),
        grid=(BATCH, DIFF_HEADS // nh),
        in_specs=[
            pl.BlockSpec((1, 4, DIFF_HEADS, HEAD_DIM), lambda b, h: (j, 0, 0, 0)),
            pl.BlockSpec((1, DIFF_V_DIM), lambda b, h: (j, 0)),
            pl.BlockSpec((SEQ, w), lambda b, h: (b, Q_COL // w + h)),
            pl.BlockSpec((SEQ, w), lambda b, h: (b, K_COL // w + h)),
            pl.BlockSpec((SEQ, w), lambda b, h: (b, V_COL // w + h)),
        ],
        out_specs=pl.BlockSpec((SEQ, w), lambda b, h: (b, h)),
        out_shape=jax.ShapeDtypeStruct((N_CTX, 512), BF16),
        compiler_params=_cparams(("arbitrary", "arbitrary")),
        name="diff_context",
    )(lam_p, subln, proj, proj, proj)


def _diff_lat(proj, cache_k, cache_v, lam_p, subln, j, lam_init):
    nh = DIFF_STEP_HEADS
    w = nh * DIFF_V_DIM
    nq = DEC_SEQ // DQ
    row0 = N_CTX // DQ
    seq0 = N_CTX // DEC_SEQ
    return pl.pallas_call(
        functools.partial(_diff_kernel, lam_init=lam_init, has_cache=True, nh=nh),
        grid=(DEC_BATCH, DIFF_HEADS // nh, nq),
        in_specs=[
            pl.BlockSpec((1, 4, DIFF_HEADS, HEAD_DIM), lambda b, h, i: (j, 0, 0, 0)),
            pl.BlockSpec((1, DIFF_V_DIM), lambda b, h, i: (j, 0)),
            pl.BlockSpec((DQ, w), lambda b, h, i: (row0 + b * nq + i, Q_COL // w + h)),
            pl.BlockSpec((DEC_SEQ, w), lambda b, h, i: (seq0 + b, K_COL // w + h)),
            pl.BlockSpec((DEC_SEQ, w), lambda b, h, i: (seq0 + b, V_COL // w + h)),
            pl.BlockSpec((1, PAST_LEN, w), lambda b, h, i: (b, 0, h)),
            pl.BlockSpec((1, PAST_LEN, w), lambda b, h, i: (b, 0, h)),
        ],
        out_specs=pl.BlockSpec((DQ, w), lambda b, h, i: (b * nq + i, h)),
        out_shape=jax.ShapeDtypeStruct((N_LAT, 512), BF16),
        compiler_params=_cparams(("arbitrary", "arbitrary", "arbitrary")),
        name="diff_latent",
    )(lam_p, subln, proj, proj, proj, cache_k, cache_v)


def kernel(x_prompt, x_sample, state_rglru, cache_swa_k, cache_swa_v, cache_diff_k, cache_diff_v, c, c_ctx,
           w_mod, b_mod, ln_g, ln_b, e_w_in, e_conv_w, e_conv_b, e_w_rgate, e_b_rgate, e_w_igate, e_b_igate,
           e_lambda, e_w_fnet, e_w_out, o_w_in, o_sink, o_lambda, o_subln_g, o_w_out,
           w_router, w_gate, w_up, w_down):
    x_pair = (x_prompt.reshape(N_CTX, D_MODEL), x_sample.reshape(N_LAT, D_MODEL))
    cvec8 = jnp.concatenate([c_ctx[None, :], c, jnp.zeros((SUBLANE - N_SEG, D_MODEL), F32)], axis=0)
    mod = _modulation(cvec8, w_mod, b_mod).reshape(DEPTH, SUBLANE, 6, D_MODEL)

    j = 0
    proj = _inproj(x_pair, mod, 0, e_w_in, j)
    wg = _gate_tiles(e_w_rgate, e_w_igate)
    cb = e_conv_b.reshape(-1, 1, D_RG)
    rg_args = (j, e_conv_w, cb, wg, e_b_rgate, e_b_igate, e_lambda)
    y_rg_c, st_c = _rglru(proj, jnp.zeros((BATCH, 2, D_RG), F32), *rg_args,
                          seq_len=SEQ, nseq=BATCH, row_block0=0)
    y_rg_l, _ = _rglru(proj, state_rglru[:, j], *rg_args,
                       seq_len=DEC_SEQ, nseq=DEC_BATCH, row_block0=N_CTX // DEC_SEQ)
    wbd = _fnet_blockdiag(e_w_fnet)
    y_fn_c = _fnet(proj, j, wbd, seq_len=SEQ, nseq=BATCH, row_block0=0)
    y_fn_l = _fnet(proj, j, wbd, seq_len=DEC_SEQ, nseq=DEC_BATCH, row_block0=N_CTX // DEC_SEQ)
    x1, u2, aff = _outproj((y_rg_c, y_rg_l), (y_fn_c, y_fn_l), e_w_out, j, x_pair, mod, 0, ln_g, ln_b, w_router)
    x = _moe(x1, u2, aff, mod, 0, ln_g, ln_b, w_gate, w_up, w_down)
    new_state_rglru = st_c[:, None]

    layer = 1
    lam_init = 0.8 - 0.6 * math.exp(-0.3 * layer)
    cos, sin = _rope_tables()
    rope_groups = (0, 1, 2, 3, 4, 6, 7, 8, 9, 10, 11, 12, 13)
    cache_cols = ((512, 128), (640, 128), (K_COL, 512), (V_COL, 512))
    proj, ks, vs, kd, vd = _inproj((x, x), mod, 1, o_w_in, j, rope=(cos, sin, rope_groups), cache_cols=cache_cols)
    sink = o_sink[j]
    ys_c = _swa_ctx(proj, sink)
    yd_c = _diff_ctx(proj, o_lambda, o_subln_g, j, lam_init)
    ck = cache_swa_k[:, j].reshape(DEC_BATCH, PAST_LEN, SWA_KV_HEADS * HEAD_DIM)
    cv = cache_swa_v[:, j].reshape(DEC_BATCH, PAST_LEN, SWA_KV_HEADS * HEAD_DIM)
    ys_l = _swa_lat(proj, ck, cv, sink)
    cdk = cache_diff_k[:, j].reshape(DEC_BATCH, PAST_LEN, DIFF_HEADS * 2 * HEAD_DIM)
    cdv = cache_diff_v[:, j].reshape(DEC_BATCH, PAST_LEN, DIFF_HEADS * DIFF_V_DIM)
    yd_l = _diff_lat(proj, cdk, cdv, o_lambda, o_subln_g, j, lam_init)
    x1, u2, aff = _outproj((ys_c, ys_l), (yd_c, yd_l), o_w_out, j, (x, x), mod, 1, ln_g, ln_b, w_router)
    y_c, y_l = _moe(x1, u2, aff, mod, 1, ln_g, ln_b, w_gate, w_up, w_down, split_out=True)

    new_cache_swa_k = ks.reshape(BATCH, 1, SEQ, SWA_KV_HEADS, HEAD_DIM)
    new_cache_swa_v = vs.reshape(BATCH, 1, SEQ, SWA_KV_HEADS, HEAD_DIM)
    new_cache_diff_k = kd.reshape(BATCH, 1, SEQ, DIFF_HEADS, 2, HEAD_DIM)
    new_cache_diff_v = vd.reshape(BATCH, 1, SEQ, DIFF_HEADS, DIFF_V_DIM)
    y_prompt = y_c.reshape(BATCH, SEQ, D_MODEL)
    y_sample = y_l.reshape(DEC_BATCH, DEC_SEQ, D_MODEL)
    return (y_prompt, y_sample, new_state_rglru, new_cache_swa_k, new_cache_swa_v,
            new_cache_diff_k, new_cache_diff_v)
```

```python
import functools
import math

import numpy as np
import jax
import jax.numpy as jnp
from jax import lax
from jax.experimental import pallas as pl
from jax.experimental.pallas import tpu as pltpu

F32 = jnp.float32
BF16 = jnp.bfloat16
I32 = jnp.int32

D_MODEL = 1024
BATCH, SEQ = 16, 256
DEC_BATCH, DEC_SEQ = 2, 2048
PAST_LEN = 256
DEPTH = 2
GRID_W = 64
HEAD_DIM = 64
D_RG = 768
RG_BLOCK = 64
RG_C = 8.0
CONV_W = 4
D_FNET = 256
FNET_GROUP_DIM = 64
FNET_GROUPS = 4
SWA_HEADS = 8
SWA_KV_HEADS = 2
GQA_GROUP = 4
WINDOW = 128
DIFF_HEADS = 4
DIFF_V_DIM = 128
N_EXPERTS = 16
EC_FACTOR = 2
D_EXPERT = 2048
ROPE_BASE = 10000.0
LN_EPS = 1e-5
NEG_INF = -1e30
ATTN_SCALE = HEAD_DIM ** -0.5
ALPHA = (2 * DEPTH) ** 0.25
EVEN_IN = 2 * D_RG + D_FNET
ODD_IN = 2304

N_CTX = BATCH * SEQ
N_LAT = DEC_BATCH * DEC_SEQ
N_TOK = N_CTX + N_LAT
N_SEG = 1 + DEC_BATCH
CAP = EC_FACTOR * N_CTX // N_EXPERTS
assert N_CTX == N_LAT

LANE = 128
SUBLANE = 8
MXU_DIM = 256
VMEM_LIMIT = 56 * 1024 * 1024
TM = 512
GC = 256
N_CHUNKS = N_CTX // GC
WIN = 64
WIN_ALIGN = 16
WINS_PER_DOT = MXU_DIM // WIN
TOPK_MIN_EXP = -150.0
TOPK_GEO_STEPS = 16
TOPK_LIN_STEPS = 24
N_TILES = N_TOK // TM
CTX_TILES = N_CTX // TM
LAT_TILES_PER_SEQ = DEC_SEQ // TM


def _cparams(sem):
    return pltpu.CompilerParams(dimension_semantics=sem, vmem_limit_bytes=VMEM_LIMIT)


def _dot(a, b):
    return jnp.dot(a, b, preferred_element_type=F32)


def _dot_nt(a, b):
    return lax.dot_general(a, b, (((1,), (1,)), ((), ())), preferred_element_type=F32)


def _split(a):
    hi = a.astype(BF16)
    lo = (a - hi.astype(F32)).astype(BF16)
    return hi, lo


def _dot3(a, b):
    ah, al = _split(a)
    bh, bl = _split(b)
    return _dot(ah, bh) + (_dot(ah, bl) + _dot(al, bh))


def _layernorm(y, g, b):
    mu = jnp.mean(y, axis=-1, keepdims=True)
    d = y - mu
    var = jnp.mean(d * d, axis=-1, keepdims=True)
    return d * lax.rsqrt(var + LN_EPS) * g + b


def _pair_specs(width, lat_block0=0, tile=TM):
    ctx_tiles = N_CTX // tile
    return [
        pl.BlockSpec((tile, width), lambda i, *_: (jnp.minimum(i, ctx_tiles - 1), 0)),
        pl.BlockSpec((tile, width), lambda i, *_: (jnp.maximum(i - ctx_tiles, 0) + lat_block0, 0)),
    ]


def _pick(i, ctx_ref, lat_ref):
    return jnp.where(i < CTX_TILES, ctx_ref[...], lat_ref[...])


def _seg_of_tile(i, tile=TM):
    ctx_tiles = N_CTX // tile
    return jnp.where(i < ctx_tiles, 0, (i - ctx_tiles) // (DEC_SEQ // tile) + 1)


MOD_TN = 1536


def _mod_kernel(c_ref, w_ref, b_ref, o_ref):
    c = c_ref[...]
    s = c * jax.nn.sigmoid(c)
    o_ref[0] = _dot3(s, w_ref[0]) + b_ref[0]


def _modulation(cvec8, w_mod, b_mod):
    n = 6 * D_MODEL
    return pl.pallas_call(
        _mod_kernel,
        grid=(DEPTH, n // MOD_TN),
        in_specs=[
            pl.BlockSpec((SUBLANE, D_MODEL), lambda l, j: (0, 0)),
            pl.BlockSpec((1, D_MODEL, MOD_TN), lambda l, j: (l, 0, j)),
            pl.BlockSpec((1, 1, MOD_TN), lambda l, j: (l, 0, j)),
        ],
        out_specs=pl.BlockSpec((1, SUBLANE, MOD_TN), lambda l, j: (l, 0, j)),
        out_shape=jax.ShapeDtypeStruct((DEPTH, SUBLANE, n), F32),
        compiler_params=_cparams(("arbitrary", "arbitrary")),
        name="modulation",
    )(cvec8, w_mod, b_mod.reshape(DEPTH, 1, n))


def _rope_tile(p, cos, sin, rope_groups):
    lane = lax.broadcasted_iota(I32, (1, LANE), 1)
    first_half = (lane % 32) < 16
    pieces = []
    for k in range(p.shape[1] // LANE):
        xg = p[:, k * LANE:(k + 1) * LANE]
        if k in rope_groups:
            partner = jnp.where(first_half, pltpu.roll(xg, LANE - 16, 1), pltpu.roll(xg, 16, 1))
            xg = xg * cos + partner * sin
        pieces.append(xg)
    return jnp.concatenate(pieces, axis=1)


def _inproj_kernel(*refs, rope_groups, cache_cols):
    if rope_groups:
        xc_ref, xl_ref, mod_ref, w_ref, cos_ref, sin_ref, o_ref = refs[:7]
        cache_refs, wbf_ref = refs[7:-1], refs[-1]
    else:
        xc_ref, xl_ref, mod_ref, w_ref, o_ref, wbf_ref = refs
    i = pl.program_id(0)

    @pl.when(i == 0)
    def _():
        wbf_ref[...] = w_ref[0].astype(BF16)

    m = mod_ref[0, 0]
    u = _pick(i, xc_ref, xl_ref) * (1.0 + m[1:2, :]) + m[0:1, :]
    p = _dot(u.astype(BF16), wbf_ref[...])
    if rope_groups:
        @pl.when(i < CTX_TILES)
        def _():
            o_ref[...] = p.astype(BF16)
            for ref, (start, width) in zip(cache_refs, cache_cols):
                ref[...] = p[:, start:start + width]

        @pl.when(i >= CTX_TILES)
        def _():
            o_ref[...] = _rope_tile(p, cos_ref[...], sin_ref[...], rope_groups).astype(BF16)
    else:
        o_ref[...] = p


def _inproj(x_pair, mod, layer, w_in, j, rope=None, cache_cols=()):
    n = w_in.shape[-1]
    rope_groups = ()
    combined = x_pair[0] is x_pair[1]
    in_specs = _pair_specs(D_MODEL, CTX_TILES if combined else 0) + [
        pl.BlockSpec((1, 1, 6, D_MODEL), lambda i: (layer, _seg_of_tile(i), 0, 0)),
        pl.BlockSpec((1, D_MODEL, n), lambda i: (j, 0, 0)),
    ]
    args = [x_pair[0], x_pair[1], mod, w_in]
    out_specs = pl.BlockSpec((TM, n), lambda i: (i, 0))
    out_shape = jax.ShapeDtypeStruct((N_TOK, n), F32)
    if rope is not None:
        cos, sin, rope_groups = rope
        tab = pl.BlockSpec((TM, LANE), lambda i: (jnp.maximum(i - CTX_TILES, 0) % LAT_TILES_PER_SEQ, 0))
        in_specs += [tab, tab]
        args += [cos, sin]
        out_specs = [out_specs] + [_pair_specs(width)[0] for _, width in cache_cols]
        out_shape = [jax.ShapeDtypeStruct((N_TOK, n), BF16)] + [
            jax.ShapeDtypeStruct((N_CTX, width), F32) for _, width in cache_cols]
    return pl.pallas_call(
        functools.partial(_inproj_kernel, rope_groups=rope_groups, cache_cols=cache_cols),
        grid=(N_TILES,),
        in_specs=in_specs,
        out_specs=out_specs,
        out_shape=out_shape,
        scratch_shapes=[pltpu.VMEM((D_MODEL, n), BF16)],
        compiler_params=_cparams(("arbitrary",)),
        name="inproj_rope" if rope is not None else "inproj",
    )(*args)


def _rope_tables():
    s = np.arange(DEC_SEQ)
    row, col = (s // GRID_W).astype(np.float64), (s % GRID_W).astype(np.float64)
    nf = HEAD_DIM // 4
    inv = np.power(ROPE_BASE, -np.arange(nf, dtype=np.float64) / nf)
    ang = np.concatenate([row[:, None] * inv, row[:, None] * inv, col[:, None] * inv, col[:, None] * inv], axis=1)
    sign = np.concatenate([-np.ones(nf), np.ones(nf), -np.ones(nf), np.ones(nf)])
    cos = np.tile(np.cos(ang), (1, LANE // HEAD_DIM))
    sin = np.tile(np.sin(ang) * sign, (1, LANE // HEAD_DIM))
    return jnp.asarray(cos, F32), jnp.asarray(sin, F32)


RG_T = 256
RG_TILES = D_RG // MXU_DIM


def _sigmoid(x):
    return 0.5 * jnp.tanh(0.5 * x) + 0.5


def _softplus(x):
    u = jnp.exp(-jnp.abs(x))
    w = 1.0 + u
    l1p = jnp.where(w == 1.0, u, jnp.log(w) * (u / jnp.where(w == 1.0, 1.0, w - 1.0)))
    return jnp.maximum(x, 0.0) + l1p


def _rglru_kernel(xa_ref, ga_ref, h0_ref, cw_ref, cb_ref, wg_ref, br_ref, bi_ref, lam_ref,
                  y_ref, st_ref, hf_s, xc_s, a_s, b_s, hs_s, *, seq_len):
    nchunk = seq_len // RG_T
    cw = cw_ref[0]
    cb = cb_ref[0]

    def conv_chunk(c):
        base = pl.multiple_of(c * RG_T, RG_T)
        cur = xa_ref[pl.ds(base, RG_T), :]
        prev = xa_ref[pl.ds(pl.multiple_of(jnp.maximum(base - SUBLANE, 0), SUBLANE), SUBLANE), :]
        prev = jnp.where(c > 0, prev, 0.0)
        nxt = xa_ref[pl.ds(pl.multiple_of(jnp.minimum(base + RG_T, seq_len - SUBLANE), SUBLANE), SUBLANE), :]
        nxt = jnp.where(c < nchunk - 1, nxt, 0.0)
        win = jnp.concatenate([prev, cur, nxt], axis=0)
        xc = cb
        for k in range(CONV_W):
            off = SUBLANE - 1 + k
            xc = xc + win[off:off + RG_T] * cw[k:k + 1, :]
        return xc

    def gates(xc, z):
        xb = xc.astype(BF16)
        rs, gs = [], []
        for t in range(RG_TILES):
            zz = _dot(xb[:, t * MXU_DIM:(t + 1) * MXU_DIM], wg_ref[0, z, t].astype(BF16))
            rs.append(zz[:, :MXU_DIM])
            gs.append(zz[:, MXU_DIM:])
        r = _sigmoid(jnp.concatenate(rs, axis=1) + br_ref[0, z:z + 1, :])
        g = _sigmoid(jnp.concatenate(gs, axis=1) + bi_ref[0, z:z + 1, :])
        log_a = (-RG_C * _softplus(-lam_ref[0, z:z + 1, :])) * r
        a = jnp.exp(log_a)
        y = -jnp.tanh(log_a) * (a * a + 1.0)
        bt = jnp.where(y > 0.0, y * lax.rsqrt(y), 0.0) * g * xc
        return a, bt

    def scan_chunk(h, reverse):
        def step(t, h):
            tt = RG_T - 1 - t if reverse else t
            h = a_s[pl.ds(tt, 1), :] * h + b_s[pl.ds(tt, 1), :]
            hs_s[pl.ds(tt, 1), :] = h
            return h
        return lax.fori_loop(0, RG_T, step, h, unroll=8)

    def fwd_chunk(c, h):
        rows = pl.ds(pl.multiple_of(c * RG_T, RG_T), RG_T)
        xc = conv_chunk(c)
        xc_s[rows, :] = xc
        a, bt = gates(xc, 0)
        a_s[...] = a
        b_s[...] = bt
        h = scan_chunk(h, False)
        hf_s[rows, :] = hs_s[...]
        return h

    def bwd_chunk(k, h):
        c = nchunk - 1 - k
        rows = pl.ds(pl.multiple_of(c * RG_T, RG_T), RG_T)
        a, bt = gates(xc_s[rows, :], 1)
        a_s[...] = a
        b_s[...] = bt
        h = scan_chunk(h, True)
        y_ref[rows, :] = ((hf_s[rows, :] + hs_s[...]) * jax.nn.gelu(ga_ref[rows, :])).astype(BF16)
        return h

    hf = lax.fori_loop(0, nchunk, fwd_chunk, h0_ref[0, 0:1, :])
    hb = lax.fori_loop(0, nchunk, bwd_chunk, h0_ref[0, 1:2, :])
    st_ref[0, 0:1, :] = hf
    st_ref[0, 1:2, :] = hb


def _rglru(proj, h0, j, cw, cb, wg, br, bi, lam, *, seq_len, nseq, row_block0):
    wspec3 = lambda shape: pl.BlockSpec((1,) + shape, lambda b: (j,) + (0,) * len(shape))
    return pl.pallas_call(
        functools.partial(_rglru_kernel, seq_len=seq_len),
        grid=(nseq,),
        in_specs=[
            pl.BlockSpec((seq_len, D_RG), lambda b: (row_block0 + b, 0)),
            pl.BlockSpec((seq_len, D_RG), lambda b: (row_block0 + b, 1)),
            pl.BlockSpec((1, 2, D_RG), lambda b: (b, 0, 0)),
            wspec3((CONV_W, D_RG)),
            wspec3((1, D_RG)),
            wspec3((2, RG_TILES, MXU_DIM, 2 * MXU_DIM)),
            wspec3((2, D_RG)),
            wspec3((2, D_RG)),
            wspec3((2, D_RG)),
        ],
        out_specs=[
            pl.BlockSpec((seq_len, D_RG), lambda b: (b, 0)),
            pl.BlockSpec((1, 2, D_RG), lambda b: (b, 0, 0)),
        ],
        out_shape=[
            jax.ShapeDtypeStruct((nseq * seq_len, D_RG), BF16),
            jax.ShapeDtypeStruct((nseq, 2, D_RG), F32),
        ],
        scratch_shapes=[
            pltpu.VMEM((seq_len, D_RG), F32),
            pltpu.VMEM((seq_len, D_RG), F32),
            pltpu.VMEM((RG_T, D_RG), F32),
            pltpu.VMEM((RG_T, D_RG), F32),
            pltpu.VMEM((RG_T, D_RG), F32),
        ],
        compiler_params=_cparams(("arbitrary",)),
        name=f"rglru_s{seq_len}",
    )(proj, proj, h0, cw, cb, wg, br, bi, lam)


def _gate_tiles(w_r, w_i):
    per_tile = MXU_DIM // RG_BLOCK
    eye = jnp.eye(per_tile, dtype=F32)

    def tiles(w):
        w = w.reshape(w.shape[0], 2, RG_TILES, per_tile, RG_BLOCK, RG_BLOCK)
        return jnp.einsum('nztbcd,be->nztbced', w, eye).reshape(w.shape[0], 2, RG_TILES, MXU_DIM, MXU_DIM)

    return jnp.concatenate([tiles(w_r), tiles(w_i)], axis=-1)


def _fnet_kernel(z_ref, cc_ref, sc_ref, w_ref, cs_ref, ns_ref, y_ref, p_s, q_s):
    @pl.when(pl.program_id(1) == 0)
    def _():
        w = w_ref[0]
        a = _dot3(cc_ref[...], w).astype(BF16)
        b = _dot3(sc_ref[...], w).astype(BF16)
        zb = z_ref[...].astype(BF16)
        p_s[...] = _dot(zb, a).astype(BF16)
        q_s[...] = _dot(zb, b).astype(BF16)

    y_ref[...] = (_dot(cs_ref[...].astype(BF16), p_s[...])
                  + _dot(ns_ref[...].astype(BF16), q_s[...])).astype(BF16)


def _fnet(proj, j, wbd, *, seq_len, nseq, row_block0):
    tr = min(seq_len, 512)
    nrt = seq_len // tr
    cc, sc = _dft_tables(FNET_GROUP_DIM)
    ccbd = jnp.asarray(np.kron(np.eye(FNET_GROUPS), cc), F32)
    scbd = jnp.asarray(np.kron(np.eye(FNET_GROUPS), sc), F32)
    cs, ss = _dft_tables(seq_len)
    cs_b, ns_b = jnp.asarray(cs, F32), jnp.asarray(-ss, F32)
    zcol = (2 * D_RG) // D_FNET
    const = lambda shape: pl.BlockSpec(shape, lambda b, r: (0,) * len(shape))
    return pl.pallas_call(
        _fnet_kernel,
        grid=(nseq, nrt),
        in_specs=[
            pl.BlockSpec((seq_len, D_FNET), lambda b, r: (row_block0 + b, zcol)),
            const((D_FNET, D_FNET)),
            const((D_FNET, D_FNET)),
            pl.BlockSpec((1, D_FNET, D_FNET), lambda b, r: (j, 0, 0)),
            pl.BlockSpec((tr, seq_len), lambda b, r: (r, 0)),
            pl.BlockSpec((tr, seq_len), lambda b, r: (r, 0)),
        ],
        out_specs=pl.BlockSpec((tr, D_FNET), lambda b, r: (b * nrt + r, 0)),
        out_shape=jax.ShapeDtypeStruct((nseq * seq_len, D_FNET), BF16),
        scratch_shapes=[pltpu.VMEM((seq_len, D_FNET), BF16), pltpu.VMEM((seq_len, D_FNET), BF16)],
        compiler_params=_cparams(("arbitrary", "arbitrary")),
        name=f"fnet_s{seq_len}",
    )(proj, ccbd, scbd, wbd, cs_b, ns_b)


def _dft_tables(n):
    k = np.arange(n)
    ang = 2.0 * np.pi * ((k[:, None] * k[None, :]) % n) / n
    return np.cos(ang) / np.sqrt(n), np.sin(ang) / np.sqrt(n)


def _fnet_blockdiag(w_f):
    eye = jnp.eye(FNET_GROUPS, dtype=F32)
    return jnp.einsum('ngcd,gh->ngchd', w_f, eye).reshape(w_f.shape[0], D_FNET, D_FNET)


def _outproj_kernel(ac_ref, al_ref, bc_ref, bl_ref, w_ref, xc_ref, xl_ref, mod_ref, g_ref, be_ref, wr_ref,
                    x1_ref, u2_ref, aff_ref, wbf_ref, *, ka):
    i = pl.program_id(0)

    @pl.when(i == 0)
    def _():
        wbf_ref[...] = w_ref[0].astype(BF16)

    out = _dot(_pick(i, ac_ref, al_ref), wbf_ref[0:ka, :]) + _dot(_pick(i, bc_ref, bl_ref), wbf_ref[ka:, :])
    m = mod_ref[0, 0]
    x1 = _layernorm(ALPHA * _pick(i, xc_ref, xl_ref) + m[2:3, :] * out, g_ref[0, 0:1, :], be_ref[0, 0:1, :])
    x1_ref[...] = x1
    u2 = x1 * (1.0 + m[4:5, :]) + m[3:4, :]
    u2_ref[...] = u2.astype(BF16)
    uh, ul = _split(u2)
    wh, wl = _split(wr_ref[0])
    lg = _dot(jnp.concatenate([uh, ul], axis=0), jnp.concatenate([wh, wl], axis=1))
    logits = (lg[:TM, :N_EXPERTS] + lg[:TM, N_EXPERTS:]) + (lg[TM:, :N_EXPERTS] + lg[TM:, N_EXPERTS:])
    mx = jnp.max(logits, axis=-1, keepdims=True)
    ex = jnp.exp(logits - mx)
    aff_ref[...] = ex / jnp.sum(ex, axis=-1, keepdims=True)


def _outproj(a_pair, b_pair, w_out, j, x_pair, mod, layer, ln_g, ln_b, w_router):
    ka, kb = a_pair[0].shape[1], b_pair[0].shape[1]
    combined = x_pair[0] is x_pair[1]
    return pl.pallas_call(
        functools.partial(_outproj_kernel, ka=ka),
        grid=(N_TILES,),
        in_specs=_pair_specs(ka) + _pair_specs(kb) + [
            pl.BlockSpec((1, D_MODEL, D_MODEL), lambda i: (j, 0, 0)),
        ] + _pair_specs(D_MODEL, CTX_TILES if combined else 0) + [
            pl.BlockSpec((1, 1, 6, D_MODEL), lambda i: (layer, _seg_of_tile(i), 0, 0)),
            pl.BlockSpec((1, 2, D_MODEL), lambda i: (layer, 0, 0)),
            pl.BlockSpec((1, 2, D_MODEL), lambda i: (layer, 0, 0)),
            pl.BlockSpec((1, D_MODEL, N_EXPERTS), lambda i: (layer, 0, 0)),
        ],
        out_specs=[
            pl.BlockSpec((TM, D_MODEL), lambda i: (i, 0)),
            pl.BlockSpec((TM, D_MODEL), lambda i: (i, 0)),
            pl.BlockSpec((TM, N_EXPERTS), lambda i: (i, 0)),
        ],
        out_shape=[
            jax.ShapeDtypeStruct((N_TOK, D_MODEL), F32),
            jax.ShapeDtypeStruct((N_TOK, D_MODEL), BF16),
            jax.ShapeDtypeStruct((N_TOK, N_EXPERTS), F32),
        ],
        scratch_shapes=[pltpu.VMEM((D_MODEL, D_MODEL), BF16)],
        compiler_params=_cparams(("arbitrary",)),
        name="outproj_ln_router",
    )(*a_pair, *b_pair, w_out, *x_pair, mod, ln_g, ln_b, w_router)


def _topk_kernel(aff_ref, pos_ref, cnt_ref):
    aff = aff_ref[...]
    cap = jnp.float32(CAP)

    def count_ge(t):
        return jnp.sum(jnp.where(aff >= t, 1.0, 0.0), axis=1, keepdims=True)

    lo_v = jnp.zeros((N_EXPERTS, 1), F32)
    hi_v = jnp.full((N_EXPERTS, 1), 2.0, F32)
    lo_t = jnp.full((N_EXPERTS, 1), TOPK_MIN_EXP, F32)
    hi_t = jnp.full((N_EXPERTS, 1), 1.0, F32)
    for _ in range(TOPK_GEO_STEPS):
        mid_t = 0.5 * (lo_t + hi_t)
        cand = jnp.exp2(mid_t)
        ok = count_ge(cand) >= cap
        lo_t = jnp.where(ok, mid_t, lo_t)
        hi_t = jnp.where(ok, hi_t, mid_t)
        lo_v = jnp.where(ok, jnp.maximum(lo_v, cand), lo_v)
        hi_v = jnp.where(ok, hi_v, jnp.minimum(hi_v, cand))
    for _ in range(TOPK_LIN_STEPS):
        cand = lo_v + 0.5 * (hi_v - lo_v)
        ok = count_ge(cand) >= cap
        lo_v = jnp.where(ok, cand, lo_v)
        hi_v = jnp.where(ok, hi_v, cand)
    gt = aff >= hi_v
    eq = (aff >= lo_v) & (aff < hi_v)
    need = cap - jnp.sum(jnp.where(gt, 1.0, 0.0), axis=1, keepdims=True)

    tri = (lax.broadcasted_iota(I32, (LANE, LANE), 0) <= lax.broadcasted_iota(I32, (LANE, LANE), 1))
    tri = jnp.where(tri, 1.0, 0.0).astype(BF16)

    def cumsum_blocks(mask_f32):
        carry = jnp.zeros((N_EXPERTS, 1), F32)
        outs = []
        for k in range(mask_f32.shape[1] // LANE):
            blk = mask_f32[:, k * LANE:(k + 1) * LANE]
            inc = _dot(blk.astype(BF16), tri) + carry
            outs.append(inc)
            carry = carry + jnp.sum(blk, axis=1, keepdims=True)
        return outs

    eq_f = jnp.where(eq, 1.0, 0.0)
    gt_f = jnp.where(gt, 1.0, 0.0)
    eq_rank = cumsum_blocks(eq_f)
    sel_blocks = []
    for k, rk in enumerate(eq_rank):
        lanes = slice(k * LANE, (k + 1) * LANE)
        sel_blocks.append(jnp.where((eq_f[:, lanes] > 0.0) & (rk <= need), 1.0, gt_f[:, lanes]))
    sel = jnp.concatenate(sel_blocks, axis=1)
    sel_rank = cumsum_blocks(sel)
    for k, rk in enumerate(sel_rank):
        lanes = slice(k * LANE, (k + 1) * LANE)
        pos_ref[:, lanes] = jnp.where(sel[:, lanes] > 0.0, rk.astype(I32) - 1, -1)
    lane = lax.broadcasted_iota(I32, (N_EXPERTS, LANE), 1)
    cnt = jnp.zeros((N_EXPERTS, LANE), F32)
    for c in range(1, N_CHUNKS + 1):
        cnt = jnp.where(lane == c, sel_rank[c * (GC // LANE) - 1][:, LANE - 1:LANE], cnt)
    cnt_ref[0] = cnt.astype(I32)


def _topk(aff_t):
    return pl.pallas_call(
        _topk_kernel,
        grid=(2,),
        in_specs=[pl.BlockSpec((N_EXPERTS, N_CTX), lambda g: (0, g))],
        out_specs=[
            pl.BlockSpec((N_EXPERTS, N_CTX), lambda g: (0, g)),
            pl.BlockSpec((1, N_EXPERTS, LANE), lambda g: (g, 0, 0)),
        ],
        out_shape=[
            jax.ShapeDtypeStruct((N_EXPERTS, N_TOK), I32),
            jax.ShapeDtypeStruct((2, N_EXPERTS, LANE), I32),
        ],
        compiler_params=_cparams(("arbitrary",)),
        name="topk_select",
    )(aff_t)


def _chunk_windows(cnt_ref, g, e, c):
    c0 = cnt_ref[g, e, c]
    c1 = cnt_ref[g, e, c + 1]
    base = (c0 // WIN_ALIGN) * WIN_ALIGN
    k = jnp.where(c1 > c0, (c1 - base + WIN - 1) // WIN, 0)
    return base, k


def _gather_kernel(cnt_ref, pos_ref, aff_ref, u_ref, xe_ref, gs_ref):
    g = pl.program_id(0)
    c = pl.program_id(1)

    @pl.when(c == 0)
    def _():
        xe_ref[...] = jnp.zeros_like(xe_ref)
        gs_ref[...] = jnp.zeros_like(gs_ref)

    pos = pos_ref[...]
    aff = aff_ref[...]
    slot0 = lax.broadcasted_iota(I32, (WIN, GC), 0)

    def window(e, lo):
        start = pl.multiple_of(jnp.minimum(lo, CAP - WIN), WIN_ALIGN)
        pe = pos[e:e + 1, :]
        hit = (slot0 + start) == jnp.where(pe >= lo, pe, -1)
        gates = jnp.sum(jnp.where(hit, aff[e:e + 1, :], 0.0), axis=1, keepdims=True)
        return start, jnp.where(hit, 1.0, 0.0).astype(BF16), gates

    def add_rows(e, start, rows, gates):
        dst = (e, 0, pl.ds(start, WIN), slice(None))
        xe_ref[dst] = (xe_ref[dst].astype(F32) + rows).astype(BF16)
        gs_ref[dst] += gates

    windows = [_chunk_windows(cnt_ref, g, e, c) for e in range(N_EXPERTS)]
    firsts = [window(e, base) for e, (base, _) in enumerate(windows)]
    rows = _dot(jnp.concatenate([oh for _, oh, _ in firsts], axis=0), u_ref[...])
    for e, (start, _, gates) in enumerate(firsts):
        add_rows(e, start, rows[e * WIN:(e + 1) * WIN], gates)
    for e, (base, k) in enumerate(windows):
        def extra(w, carry, e=e, base=base):
            start, oh, gates = window(e, base + w * WIN)
            add_rows(e, start, _dot(oh, u_ref[...]), gates)
            return carry
        lax.fori_loop(1, k, extra, 0)


def _gather(cnt, pos_t, aff_t, u2):
    return pl.pallas_call(
        _gather_kernel,
        grid_spec=pltpu.PrefetchScalarGridSpec(
            num_scalar_prefetch=1,
            grid=(2, N_CHUNKS),
            in_specs=[
                pl.BlockSpec((N_EXPERTS, GC), lambda g, c, cnt: (0, g * N_CHUNKS + c)),
                pl.BlockSpec((N_EXPERTS, GC), lambda g, c, cnt: (0, g * N_CHUNKS + c)),
                pl.BlockSpec((GC, D_MODEL), lambda g, c, cnt: (g * N_CHUNKS + c, 0)),
            ],
            out_specs=[
                pl.BlockSpec((N_EXPERTS, 1, CAP, D_MODEL), lambda g, c, cnt: (0, g, 0, 0)),
                pl.BlockSpec((N_EXPERTS, 1, CAP, 1), lambda g, c, cnt: (0, g, 0, 0)),
            ],
        ),
        out_shape=[
            jax.ShapeDtypeStruct((N_EXPERTS, 2, CAP, D_MODEL), BF16),
            jax.ShapeDtypeStruct((N_EXPERTS, 2, CAP, 1), F32),
        ],
        compiler_params=_cparams(("arbitrary", "arbitrary")),
        name="moe_gather",
    )(cnt, pos_t, aff_t, u2)


TF = 1024


def _ffn_kernel(xe_ref, gs_ref, wg_ref, wu_ref, wd_ref, ye_ref, acc_ref):
    f = pl.program_id(1)
    x = xe_ref[0]
    hg = _dot(x, wg_ref[0, 0].astype(BF16))
    hu = _dot(x, wu_ref[0, 0].astype(BF16))
    h = (hg * _sigmoid(hg) * hu).astype(BF16)

    @pl.when(f == 0)
    def _():
        acc_ref[...] = jnp.zeros_like(acc_ref)

    acc_ref[...] += _dot(h, wd_ref[0, 0].astype(BF16))

    @pl.when(f == pl.num_programs(1) - 1)
    def _():
        ye_ref[0] = (acc_ref[...] * gs_ref[0]).astype(BF16)


def _ffn(xe, gs, layer, w_gate, w_up, w_down):
    rows = 2 * CAP
    return pl.pallas_call(
        _ffn_kernel,
        grid=(N_EXPERTS, D_EXPERT // TF),
        in_specs=[
            pl.BlockSpec((1, rows, D_MODEL), lambda e, f: (e, 0, 0)),
            pl.BlockSpec((1, rows, 1), lambda e, f: (e, 0, 0)),
            pl.BlockSpec((1, 1, D_MODEL, TF), lambda e, f: (layer, e, 0, f)),
            pl.BlockSpec((1, 1, D_MODEL, TF), lambda e, f: (layer, e, 0, f)),
            pl.BlockSpec((1, 1, TF, D_MODEL), lambda e, f: (layer, e, f, 0)),
        ],
        out_specs=pl.BlockSpec((1, rows, D_MODEL), lambda e, f: (e, 0, 0)),
        out_shape=jax.ShapeDtypeStruct((N_EXPERTS, rows, D_MODEL), BF16),
        scratch_shapes=[pltpu.VMEM((rows, D_MODEL), F32)],
        compiler_params=_cparams(("arbitrary", "arbitrary")),
        name="moe_ffn",
    )(xe, gs, w_gate, w_up, w_down)


def _combine_kernel(*refs, split_out):
    if split_out:
        cnt_ref, posn_ref, ye_ref, x1_ref, mod_ref, g_ref, be_ref, oc_ref, ol_ref, acc_s = refs
    else:
        cnt_ref, posn_ref, ye_ref, x1_ref, mod_ref, g_ref, be_ref, o_ref, acc_s = refs
    i = pl.program_id(0)
    g = i // N_CHUNKS
    c = i % N_CHUNKS
    posn = posn_ref[...]
    lane = lax.broadcasted_iota(I32, (1, MXU_DIM), 1)
    part = lane // WIN
    slot_in_win = lane % WIN

    def by_part(vals):
        out = vals[-1]
        for q in range(len(vals) - 2, -1, -1):
            out = jnp.where(part == q, vals[q], out)
        return out

    def start_of(lo):
        return pl.multiple_of(jnp.minimum(lo, CAP - WIN), WIN_ALIGN)

    def group_product(experts, los):
        starts = [start_of(lo) for lo in los]
        slots = by_part(starts) + slot_in_win
        cols = by_part([jnp.where(posn[:, e:e + 1] >= lo, posn[:, e:e + 1], -1) for e, lo in zip(experts, los)])
        onehot = jnp.where(slots == cols, 1.0, 0.0).astype(BF16)
        rows = jnp.concatenate([ye_ref[e, 0, pl.ds(st, WIN), :] for e, st in zip(experts, starts)], axis=0)
        return _dot(onehot, rows)

    windows = [_chunk_windows(cnt_ref, g, e, c) for e in range(N_EXPERTS)]
    acc = None
    for e0 in range(0, N_EXPERTS, WINS_PER_DOT):
        experts = list(range(e0, e0 + WINS_PER_DOT))
        d = group_product(experts, [windows[e][0] for e in experts])
        acc = d if acc is None else acc + d
    acc_s[...] = acc
    slot0 = lax.broadcasted_iota(I32, (GC, WIN), 1)
    for e, (base, k) in enumerate(windows):
        def extra(w, carry, e=e, base=base):
            lo = base + w * WIN
            start = start_of(lo)
            col = posn[:, e:e + 1]
            hit = (slot0 + start) == jnp.where(col >= lo, col, -1)
            acc_s[...] += _dot(jnp.where(hit, 1.0, 0.0).astype(BF16), ye_ref[e, 0, pl.ds(start, WIN), :])
            return carry
        lax.fori_loop(1, k, extra, 0)
    m = mod_ref[0, 0]
    y = ALPHA * x1_ref[...] + m[5:6, :] * acc_s[...]
    res = _layernorm(y, g_ref[0, 1:2, :], be_ref[0, 1:2, :])
    if split_out:
        @pl.when(i < N_CHUNKS)
        def _():
            oc_ref[...] = res

        @pl.when(i >= N_CHUNKS)
        def _():
            ol_ref[...] = res
    else:
        o_ref[...] = res


def _combine(cnt, pos_n, ye, x1, mod, layer, ln_g, ln_b, split_out):
    if split_out:
        out_specs = _pair_specs(D_MODEL, tile=GC)
        out_shape = [jax.ShapeDtypeStruct((N_CTX, D_MODEL), F32), jax.ShapeDtypeStruct((N_LAT, D_MODEL), F32)]
    else:
        out_specs = pl.BlockSpec((GC, D_MODEL), lambda i, cnt: (i, 0))
        out_shape = jax.ShapeDtypeStruct((N_TOK, D_MODEL), F32)
    return pl.pallas_call(
        functools.partial(_combine_kernel, split_out=split_out),
        grid_spec=pltpu.PrefetchScalarGridSpec(
            num_scalar_prefetch=1,
            grid=(N_TOK // GC,),
            in_specs=[
                pl.BlockSpec((GC, N_EXPERTS), lambda i, cnt: (i, 0)),
                pl.BlockSpec((N_EXPERTS, 1, CAP, D_MODEL), lambda i, cnt: (0, i // N_CHUNKS, 0, 0)),
                pl.BlockSpec((GC, D_MODEL), lambda i, cnt: (i, 0)),
                pl.BlockSpec((1, 1, 6, D_MODEL), lambda i, cnt: (layer, _seg_of_tile(i, GC), 0, 0)),
                pl.BlockSpec((1, 2, D_MODEL), lambda i, cnt: (layer, 0, 0)),
                pl.BlockSpec((1, 2, D_MODEL), lambda i, cnt: (layer, 0, 0)),
            ],
            out_specs=out_specs,
            scratch_shapes=[pltpu.VMEM((GC, D_MODEL), F32)],
        ),
        out_shape=out_shape,
        compiler_params=_cparams(("arbitrary",)),
        name="moe_combine_ln",
    )(cnt, pos_n, ye, x1, mod, ln_g, ln_b)


def _moe(x1, u2, aff, mod, layer, ln_g, ln_b, w_gate, w_up, w_down, split_out=False):
    aff_t = aff.T
    pos_t, cnt = _topk(aff_t)
    xe, gs = _gather(cnt, pos_t, aff_t, u2)
    ye = _ffn(xe.reshape(N_EXPERTS, 2 * CAP, D_MODEL), gs.reshape(N_EXPERTS, 2 * CAP, 1),
              layer, w_gate, w_up, w_down)
    return _combine(cnt, pos_t.T, ye.reshape(N_EXPERTS, 2, CAP, D_MODEL), x1, mod, layer, ln_g, ln_b, split_out)


def _sink_softmax_pv(s, sink_col, v):
    m = jnp.maximum(jnp.max(s, axis=-1, keepdims=True), sink_col)
    p = jnp.exp(s - m)
    den = jnp.sum(p, axis=-1, keepdims=True) + jnp.exp(sink_col - m)
    return _dot(p.astype(BF16), v) / den


def _sink_column(sink_ref, h, rows_per_group):
    rid = lax.broadcasted_iota(I32, (GQA_GROUP * rows_per_group, 1), 0) // rows_per_group
    col = jnp.zeros((GQA_GROUP * rows_per_group, 1), F32)
    for g in range(GQA_GROUP):
        col = jnp.where(rid == g, sink_ref[h * GQA_GROUP + g], col)
    return col


def _stack_groups(q, h):
    return jnp.concatenate(
        [q[:, (h * GQA_GROUP + g) * HEAD_DIM:(h * GQA_GROUP + g + 1) * HEAD_DIM] * ATTN_SCALE
         for g in range(GQA_GROUP)], axis=0).astype(BF16)


def _unstack_groups(outs, rows):
    return jnp.concatenate([o[g * rows:(g + 1) * rows] for o in outs for g in range(GQA_GROUP)], axis=1)


def _swa_ctx_kernel(sink_ref, q_ref, k_ref, v_ref, o_ref):
    q, k, v = q_ref[...], k_ref[...], v_ref[...]
    outs = []
    for h in range(SWA_KV_HEADS):
        cols = slice(h * HEAD_DIM, (h + 1) * HEAD_DIM)
        s = _dot_nt(_stack_groups(q, h), k[:, cols].astype(BF16))
        outs.append(_sink_softmax_pv(s, _sink_column(sink_ref, h, SEQ), v[:, cols].astype(BF16)))
    o_ref[...] = _unstack_groups(outs, SEQ).astype(BF16)


def _swa_ctx(proj, sink):
    return pl.pallas_call(
        _swa_ctx_kernel,
        grid=(BATCH,),
        in_specs=[
            pl.BlockSpec(memory_space=pltpu.SMEM),
            pl.BlockSpec((SEQ, 512), lambda b: (b, 0)),
            pl.BlockSpec((SEQ, LANE), lambda b: (b, 4)),
            pl.BlockSpec((SEQ, LANE), lambda b: (b, 5)),
        ],
        out_specs=pl.BlockSpec((SEQ, 512), lambda b: (b, 0)),
        out_shape=jax.ShapeDtypeStruct((N_CTX, 512), BF16),
        compiler_params=_cparams(("arbitrary",)),
        name="swa_context",
    )(sink, proj, proj, proj)


QB = 128


def _swa_lat_kernel(sink_ref, q_ref, kp_ref, kc_ref, kn_ref, vp_ref, vc_ref, vn_ref, ck_ref, cv_ref, o_ref):
    qb = pl.program_id(1)
    q = q_ref[...]
    kall = jnp.concatenate([ck_ref[0].astype(BF16), kp_ref[...], kc_ref[...], kn_ref[...]], axis=0)
    vall = jnp.concatenate([cv_ref[0].astype(BF16), vp_ref[...], vc_ref[...], vn_ref[...]], axis=0)
    nk = PAST_LEN + 3 * QB
    rows = GQA_GROUP * QB
    col = lax.broadcasted_iota(I32, (rows, nk), 1)
    qpos = qb * QB + lax.broadcasted_iota(I32, (rows, nk), 0) % QB
    kpos = (qb - 1) * QB + col - PAST_LEN
    ok = (col < PAST_LEN) | ((jnp.abs(qpos - kpos) <= WINDOW) & (kpos >= 0) & (kpos < DEC_SEQ))
    outs = []
    for h in range(SWA_KV_HEADS):
        cols = slice(h * HEAD_DIM, (h + 1) * HEAD_DIM)
        s = _dot_nt(_stack_groups(q, h), kall[:, cols])
        s = jnp.where(ok, s, NEG_INF)
        outs.append(_sink_softmax_pv(s, _sink_column(sink_ref, h, QB), vall[:, cols]))
    o_ref[...] = _unstack_groups(outs, QB).astype(BF16)


def _swa_lat(proj, cache_k, cache_v, sink):
    nqb = DEC_SEQ // QB
    row0 = N_CTX // QB

    def blk(col, shift):
        return pl.BlockSpec((QB, LANE), lambda b, i: (row0 + b * nqb + jnp.clip(i + shift, 0, nqb - 1), col))

    return pl.pallas_call(
        _swa_lat_kernel,
        grid=(DEC_BATCH, nqb),
        in_specs=[
            pl.BlockSpec(memory_space=pltpu.SMEM),
            pl.BlockSpec((QB, 512), lambda b, i: (row0 + b * nqb + i, 0)),
            blk(4, -1), blk(4, 0), blk(4, 1),
            blk(5, -1), blk(5, 0), blk(5, 1),
            pl.BlockSpec((1, PAST_LEN, LANE), lambda b, i: (b, 0, 0)),
            pl.BlockSpec((1, PAST_LEN, LANE), lambda b, i: (b, 0, 0)),
        ],
        out_specs=pl.BlockSpec((QB, 512), lambda b, i: (b * nqb + i, 0)),
        out_shape=jax.ShapeDtypeStruct((N_LAT, 512), BF16),
        compiler_params=_cparams(("arbitrary", "arbitrary")),
        name="swa_latent",
    )(sink, proj, proj, proj, proj, proj, proj, proj, cache_k, cache_v)


def _diff_lambda_col(lam_ref, lam_init):
    l1 = jnp.sum(lam_ref[0, 0] * lam_ref[0, 1], axis=-1, keepdims=True)
    l2 = jnp.sum(lam_ref[0, 2] * lam_ref[0, 3], axis=-1, keepdims=True)
    return jnp.exp(l1) - jnp.exp(l2) + lam_init


def _softmax_pv(q, k, v):
    s = _dot_nt(q, k)
    p = jnp.exp(s - jnp.max(s, axis=-1, keepdims=True))
    return _dot(p.astype(BF16), v) / jnp.sum(p, axis=-1, keepdims=True)


def _diff_head(q, k, v, lam, subln, lam_init):
    o = [_softmax_pv((q[:, m * HEAD_DIM:(m + 1) * HEAD_DIM] * ATTN_SCALE).astype(BF16),
                     k[:, m * HEAD_DIM:(m + 1) * HEAD_DIM], v) for m in range(2)]
    o = o[0] - lam * o[1]
    return o * lax.rsqrt(jnp.mean(o * o, axis=-1, keepdims=True) + LN_EPS) * subln * (1.0 - lam_init)


def _diff_kernel(*refs, lam_init, has_cache, nh):
    if has_cache:
        lam_ref, sg_ref, q_ref, k_ref, v_ref, ck_ref, cv_ref, o_ref = refs
    else:
        lam_ref, sg_ref, q_ref, k_ref, v_ref, o_ref = refs
    hp = pl.program_id(1)
    lam_all = _diff_lambda_col(lam_ref, lam_init)
    hid = lax.broadcasted_iota(I32, (DIFF_HEADS, 1), 0)
    q, k, v = q_ref[...], k_ref[...], v_ref[...]
    if has_cache:
        k = jnp.concatenate([k, ck_ref[0].astype(BF16)], axis=0)
        v = jnp.concatenate([v, cv_ref[0].astype(BF16)], axis=0)
    outs = []
    for j in range(nh):
        lam = jnp.sum(jnp.where(hid == hp * nh + j, lam_all, 0.0), axis=0, keepdims=True)
        cols = slice(j * DIFF_V_DIM, (j + 1) * DIFF_V_DIM)
        outs.append(_diff_head(q[:, cols], k[:, cols], v[:, cols], lam, sg_ref[...], lam_init))
    o_ref[...] = jnp.concatenate(outs, axis=1).astype(BF16)


DIFF_STEP_HEADS = 2
DQ = 512
Q_COL, K_COL, V_COL = 768, 1280, 1792
assert all(col % (DIFF_STEP_HEADS * DIFF_V_DIM) == 0 for col in (Q_COL, K_COL, V_COL))


def _diff_ctx(proj, lam_p, subln, j, lam_init):
    nh = DIFF_STEP_HEADS
    w = nh * DIFF_V_DIM
    return pl.pallas_call(
        functools.partial(_diff_kernel, lam_init=lam_init, has_cache=False, nh=nh),
        grid=(BATCH, DIFF_HEADS // nh),
        in_specs=[
            pl.BlockSpec((1, 4, DIFF_HEADS, HEAD_DIM), lambda b, h: (j, 0, 0, 0)),
            pl.BlockSpec((1, DIFF_V_DIM), lambda b, h: (j, 0)),
            pl.BlockSpec((SEQ, w), lambda b, h: (b, Q_COL // w + h)),
            pl.BlockSpec((SEQ, w), lambda b, h: (b, K_COL // w + h)),
            pl.BlockSpec((SEQ, w), lambda b, h: (b, V_COL // w + h)),
        ],
        out_specs=pl.BlockSpec((SEQ, w), lambda b, h: (b, h)),
        out_shape=jax.ShapeDtypeStruct((N_CTX, 512), BF16),
        compiler_params=_cparams(("arbitrary", "arbitrary")),
        name="diff_context",
    )(lam_p, subln, proj, proj, proj)


def _diff_lat(proj, cache_k, cache_v, lam_p, subln, j, lam_init):
    nh = DIFF_STEP_HEADS
    w = nh * DIFF_V_DIM
    nq = DEC_SEQ // DQ
    row0 = N_CTX // DQ
    seq0 = N_CTX // DEC_SEQ
    return pl.pallas_call(
        functools.partial(_diff_kernel, lam_init=lam_init, has_cache=True, nh=nh),
        grid=(DEC_BATCH, DIFF_HEADS // nh, nq),
        in_specs=[
            pl.BlockSpec((1, 4, DIFF_HEADS, HEAD_DIM), lambda b, h, i: (j, 0, 0, 0)),
            pl.BlockSpec((1, DIFF_V_DIM), lambda b, h, i: (j, 0)),
            pl.BlockSpec((DQ, w), lambda b, h, i: (row0 + b * nq + i, Q_COL // w + h)),
            pl.BlockSpec((DEC_SEQ, w), lambda b, h, i: (seq0 + b, K_COL // w + h)),
            pl.BlockSpec((DEC_SEQ, w), lambda b, h, i: (seq0 + b, V_COL // w + h)),
            pl.BlockSpec((1, PAST_LEN, w), lambda b, h, i: (b, 0, h)),
            pl.BlockSpec((1, PAST_LEN, w), lambda b, h, i: (b, 0, h)),
        ],
        out_specs=pl.BlockSpec((DQ, w), lambda b, h, i: (b * nq + i, h)),
        out_shape=jax.ShapeDtypeStruct((N_LAT, 512), BF16),
        compiler_params=_cparams(("arbitrary", "arbitrary", "arbitrary")),
        name="diff_latent",
    )(lam_p, subln, proj, proj, proj, cache_k, cache_v)


def kernel(x_prompt, x_sample, state_rglru, cache_swa_k, cache_swa_v, cache_diff_k, cache_diff_v, c, c_ctx,
           w_mod, b_mod, ln_g, ln_b, e_w_in, e_conv_w, e_conv_b, e_w_rgate, e_b_rgate, e_w_igate, e_b_igate,
           e_lambda, e_w_fnet, e_w_out, o_w_in, o_sink, o_lambda, o_subln_g, o_w_out,
           w_router, w_gate, w_up, w_down):
    x_pair = (x_prompt.reshape(N_CTX, D_MODEL), x_sample.reshape(N_LAT, D_MODEL))
    cvec8 = jnp.concatenate([c_ctx[None, :], c, jnp.zeros((SUBLANE - N_SEG, D_MODEL), F32)], axis=0)
    mod = _modulation(cvec8, w_mod, b_mod).reshape(DEPTH, SUBLANE, 6, D_MODEL)

    j = 0
    proj = _inproj(x_pair, mod, 0, e_w_in, j)
    wg = _gate_tiles(e_w_rgate, e_w_igate)
    cb = e_conv_b.reshape(-1, 1, D_RG)
    rg_args = (j, e_conv_w, cb, wg, e_b_rgate, e_b_igate, e_lambda)
    y_rg_c, st_c = _rglru(proj, jnp.zeros((BATCH, 2, D_RG), F32), *rg_args,
                          seq_len=SEQ, nseq=BATCH, row_block0=0)
    y_rg_l, _ = _rglru(proj, state_rglru[:, j], *rg_args,
                       seq_len=DEC_SEQ, nseq=DEC_BATCH, row_block0=N_CTX // DEC_SEQ)
    wbd = _fnet_blockdiag(e_w_fnet)
    y_fn_c = _fnet(proj, j, wbd, seq_len=SEQ, nseq=BATCH, row_block0=0)
    y_fn_l = _fnet(proj, j, wbd, seq_len=DEC_SEQ, nseq=DEC_BATCH, row_block0=N_CTX // DEC_SEQ)
    x1, u2, aff = _outproj((y_rg_c, y_rg_l), (y_fn_c, y_fn_l), e_w_out, j, x_pair, mod, 0, ln_g, ln_b, w_router)
    x = _moe(x1, u2, aff, mod, 0, ln_g, ln_b, w_gate, w_up, w_down)
    new_state_rglru = st_c[:, None]

    layer = 1
    lam_init = 0.8 - 0.6 * math.exp(-0.3 * layer)
    cos, sin = _rope_tables()
    rope_groups = (0, 1, 2, 3, 4, 6, 7, 8, 9, 10, 11, 12, 13)
    cache_cols = ((512, 128), (640, 128), (K_COL, 512), (V_COL, 512))
    proj, ks, vs, kd, vd = _inproj((x, x), mod, 1, o_w_in, j, rope=(cos, sin, rope_groups), cache_cols=cache_cols)
    sink = o_sink[j]
    ys_c = _swa_ctx(proj, sink)
    yd_c = _diff_ctx(proj, o_lambda, o_subln_g, j, lam_init)
    ck = cache_swa_k[:, j].reshape(DEC_BATCH, PAST_LEN, SWA_KV_HEADS * HEAD_DIM)
    cv = cache_swa_v[:, j].reshape(DEC_BATCH, PAST_LEN, SWA_KV_HEADS * HEAD_DIM)
    ys_l = _swa_lat(proj, ck, cv, sink)
    cdk = cache_diff_k[:, j].reshape(DEC_BATCH, PAST_LEN, DIFF_HEADS * 2 * HEAD_DIM)
    cdv = cache_diff_v[:, j].reshape(DEC_BATCH, PAST_LEN, DIFF_HEADS * DIFF_V_DIM)
    yd_l = _diff_lat(proj, cdk, cdv, o_lambda, o_subln_g, j, lam_init)
    x1, u2, aff = _outproj((ys_c, ys_l), (yd_c, yd_l), o_w_out, j, (x, x), mod, 1, ln_g, ln_b, w_router)
    y_c, y_l = _moe(x1, u2, aff, mod, 1, ln_g, ln_b, w_gate, w_up, w_down, split_out=True)

    new_cache_swa_k = ks.reshape(BATCH, 1, SEQ, SWA_KV_HEADS, HEAD_DIM)
    new_cache_swa_v = vs.reshape(BATCH, 1, SEQ, SWA_KV_HEADS, HEAD_DIM)
    new_cache_diff_k = kd.reshape(BATCH, 1, SEQ, DIFF_HEADS, 2, HEAD_DIM)
    new_cache_diff_v = vd.reshape(BATCH, 1, SEQ, DIFF_HEADS, DIFF_V_DIM)
    y_prompt = y_c.reshape(BATCH, SEQ, D_MODEL)
    y_sample = y_l.reshape(DEC_BATCH, DEC_SEQ, D_MODEL)
    return (y_prompt, y_sample, new_state_rglru, new_cache_swa_k, new_cache_swa_v,
            new_cache_diff_k, new_cache_diff_v)
```

```python
import functools
import math

import numpy as np
import jax
import jax.numpy as jnp
from jax import lax
from jax.experimental import pallas as pl
from jax.experimental.pallas import tpu as pltpu

F32 = jnp.float32
BF16 = jnp.bfloat16
I32 = jnp.int32

D_MODEL = 1024
BATCH, SEQ = 16, 256
DEC_BATCH, DEC_SEQ = 2, 2048
PAST_LEN = 256
DEPTH = 2
GRID_W = 64
HEAD_DIM = 64
D_RG = 768
RG_BLOCK = 64
RG_C = 8.0
CONV_W = 4
D_FNET = 256
FNET_GROUP_DIM = 64
FNET_GROUPS = 4
SWA_HEADS = 8
SWA_KV_HEADS = 2
GQA_GROUP = 4
WINDOW = 128
DIFF_HEADS = 4
DIFF_V_DIM = 128
N_EXPERTS = 16
EC_FACTOR = 2
D_EXPERT = 2048
ROPE_BASE = 10000.0
LN_EPS = 1e-5
NEG_INF = -1e30
ATTN_SCALE = HEAD_DIM ** -0.5
ALPHA = (2 * DEPTH) ** 0.25
EVEN_IN = 2 * D_RG + D_FNET
ODD_IN = 2304

N_CTX = BATCH * SEQ
N_LAT = DEC_BATCH * DEC_SEQ
N_TOK = N_CTX + N_LAT
N_SEG = 1 + DEC_BATCH
CAP = EC_FACTOR * N_CTX // N_EXPERTS
assert N_CTX == N_LAT

LANE = 128
SUBLANE = 8
MXU_DIM = 256
VMEM_LIMIT = 56 * 1024 * 1024
TM = 512
GC = 256
N_CHUNKS = N_CTX // GC
WIN = 64
WIN_ALIGN = 16
WINS_PER_DOT = MXU_DIM // WIN
TOPK_MIN_EXP = -150.0
TOPK_GEO_STEPS = 16
TOPK_LIN_STEPS = 24
N_TILES = N_TOK // TM
CTX_TILES = N_CTX // TM
LAT_TILES_PER_SEQ = DEC_SEQ // TM


def _cparams(sem):
    return pltpu.CompilerParams(dimension_semantics=sem, vmem_limit_bytes=VMEM_LIMIT)


def _dot(a, b):
    return jnp.dot(a, b, preferred_element_type=F32)


def _dot_nt(a, b):
    return lax.dot_general(a, b, (((1,), (1,)), ((), ())), preferred_element_type=F32)


def _split(a):
    hi = a.astype(BF16)
    lo = (a - hi.astype(F32)).astype(BF16)
    return hi, lo


def _dot3(a, b):
    ah, al = _split(a)
    bh, bl = _split(b)
    return _dot(ah, bh) + (_dot(ah, bl) + _dot(al, bh))


def _layernorm(y, g, b):
    mu = jnp.mean(y, axis=-1, keepdims=True)
    d = y - mu
    var = jnp.mean(d * d, axis=-1, keepdims=True)
    return d * lax.rsqrt(var + LN_EPS) * g + b


def _pair_specs(width, lat_block0=0, tile=TM):
    ctx_tiles = N_CTX // tile
    return [
        pl.BlockSpec((tile, width), lambda i, *_: (jnp.minimum(i, ctx_tiles - 1), 0)),
        pl.BlockSpec((tile, width), lambda i, *_: (jnp.maximum(i - ctx_tiles, 0) + lat_block0, 0)),
    ]


def _pick(i, ctx_ref, lat_ref):
    return jnp.where(i < CTX_TILES, ctx_ref[...], lat_ref[...])


def _seg_of_tile(i, tile=TM):
    ctx_tiles = N_CTX // tile
    return jnp.where(i < ctx_tiles, 0, (i - ctx_tiles) // (DEC_SEQ // tile) + 1)


MOD_TN = 1536


def _mod_kernel(ct_ref, w_ref, b_ref, o_ref):
    c = ct_ref[...]
    s = c * _sigmoid(c)
    w = w_ref[0]
    rows = [jnp.sum(s[:, v:v + 1] * w, axis=0, keepdims=True) for v in range(N_SEG)]
    rows.append(jnp.zeros((SUBLANE - N_SEG, w.shape[1]), F32))
    o_ref[0] = jnp.concatenate(rows, axis=0) + b_ref[0]


def _modulation(cvec8, w_mod, b_mod):
    n = 6 * D_MODEL
    return pl.pallas_call(
        _mod_kernel,
        grid=(DEPTH, n // MOD_TN),
        in_specs=[
            pl.BlockSpec((D_MODEL, SUBLANE), lambda l, j: (0, 0)),
            pl.BlockSpec((1, D_MODEL, MOD_TN), lambda l, j: (l, 0, j)),
            pl.BlockSpec((1, 1, MOD_TN), lambda l, j: (l, 0, j)),
        ],
        out_specs=pl.BlockSpec((1, SUBLANE, MOD_TN), lambda l, j: (l, 0, j)),
        out_shape=jax.ShapeDtypeStruct((DEPTH, SUBLANE, n), F32),
        compiler_params=_cparams(("arbitrary", "arbitrary")),
        name="modulation",
    )(cvec8, w_mod, b_mod.reshape(DEPTH, 1, n))


def _rope_tile(p, cos, sin, rope_groups):
    lane = lax.broadcasted_iota(I32, (1, LANE), 1)
    first_half = (lane % 32) < 16
    pieces = []
    for k in range(p.shape[1] // LANE):
        xg = p[:, k * LANE:(k + 1) * LANE]
        if k in rope_groups:
            partner = jnp.where(first_half, pltpu.roll(xg, LANE - 16, 1), pltpu.roll(xg, 16, 1))
            xg = xg * cos + partner * sin
        pieces.append(xg)
    return jnp.concatenate(pieces, axis=1)


def _inproj_kernel(*refs, rope_groups, cache_cols):
    if rope_groups:
        xc_ref, xl_ref, mod_ref, w_ref, cos_ref, sin_ref, o_ref = refs[:7]
        cache_refs, wbf_ref = refs[7:-1], refs[-1]
    else:
        xc_ref, xl_ref, mod_ref, w_ref, o_ref, wbf_ref = refs
    i = pl.program_id(0)

    @pl.when(i == 0)
    def _():
        wbf_ref[...] = w_ref[0].astype(BF16)

    m = mod_ref[0, 0]
    u = _pick(i, xc_ref, xl_ref) * (1.0 + m[1:2, :]) + m[0:1, :]
    p = _dot(u.astype(BF16), wbf_ref[...])
    if rope_groups:
        @pl.when(i < CTX_TILES)
        def _():
            o_ref[...] = p.astype(BF16)
            for ref, (start, width) in zip(cache_refs, cache_cols):
                ref[...] = p[:, start:start + width]

        @pl.when(i >= CTX_TILES)
        def _():
            o_ref[...] = _rope_tile(p, cos_ref[...], sin_ref[...], rope_groups).astype(BF16)
    else:
        o_ref[...] = p


def _inproj(x_pair, mod, layer, w_in, j, rope=None, cache_cols=()):
    n = w_in.shape[-1]
    rope_groups = ()
    combined = x_pair[0] is x_pair[1]
    in_specs = _pair_specs(D_MODEL, CTX_TILES if combined else 0) + [
        pl.BlockSpec((1, 1, 6, D_MODEL), lambda i: (layer, _seg_of_tile(i), 0, 0)),
        pl.BlockSpec((1, D_MODEL, n), lambda i: (j, 0, 0)),
    ]
    args = [x_pair[0], x_pair[1], mod, w_in]
    out_specs = pl.BlockSpec((TM, n), lambda i: (i, 0))
    out_shape = jax.ShapeDtypeStruct((N_TOK, n), F32)
    if rope is not None:
        cos, sin, rope_groups = rope
        tab = pl.BlockSpec((TM, LANE), lambda i: (jnp.maximum(i - CTX_TILES, 0) % LAT_TILES_PER_SEQ, 0))
        in_specs += [tab, tab]
        args += [cos, sin]
        out_specs = [out_specs] + [_pair_specs(width)[0] for _, width in cache_cols]
        out_shape = [jax.ShapeDtypeStruct((N_TOK, n), BF16)] + [
            jax.ShapeDtypeStruct((N_CTX, width), F32) for _, width in cache_cols]
    return pl.pallas_call(
        functools.partial(_inproj_kernel, rope_groups=rope_groups, cache_cols=cache_cols),
        grid=(N_TILES,),
        in_specs=in_specs,
        out_specs=out_specs,
        out_shape=out_shape,
        scratch_shapes=[pltpu.VMEM((D_MODEL, n), BF16)],
        compiler_params=_cparams(("arbitrary",)),
        name="inproj_rope" if rope is not None else "inproj",
    )(*args)


def _rope_tables():
    s = np.arange(DEC_SEQ)
    row, col = (s // GRID_W).astype(np.float64), (s % GRID_W).astype(np.float64)
    nf = HEAD_DIM // 4
    inv = np.power(ROPE_BASE, -np.arange(nf, dtype=np.float64) / nf)
    ang = np.concatenate([row[:, None] * inv, row[:, None] * inv, col[:, None] * inv, col[:, None] * inv], axis=1)
    sign = np.concatenate([-np.ones(nf), np.ones(nf), -np.ones(nf), np.ones(nf)])
    cos = np.tile(np.cos(ang), (1, LANE // HEAD_DIM))
    sin = np.tile(np.sin(ang) * sign, (1, LANE // HEAD_DIM))
    return jnp.asarray(cos, F32), jnp.asarray(sin, F32)


RG_T = 256
RG_TILES = D_RG // MXU_DIM


def _sigmoid(x):
    return 0.5 * jnp.tanh(0.5 * x) + 0.5


def _softplus(x):
    u = jnp.exp(-jnp.abs(x))
    w = 1.0 + u
    l1p = jnp.where(w == 1.0, u, jnp.log(w) * (u / jnp.where(w == 1.0, 1.0, w - 1.0)))
    return jnp.maximum(x, 0.0) + l1p


def _rglru_kernel(xa_ref, ga_ref, h0_ref, cw_ref, cb_ref, wg_ref, br_ref, bi_ref, lam_ref,
                  y_ref, st_ref, hf_s, xc_s, a_s, b_s, hs_s, *, seq_len):
    nchunk = seq_len // RG_T
    cw = cw_ref[0]
    cb = cb_ref[0]

    def conv_chunk(c):
        base = pl.multiple_of(c * RG_T, RG_T)
        cur = xa_ref[pl.ds(base, RG_T), :]
        prev = xa_ref[pl.ds(pl.multiple_of(jnp.maximum(base - SUBLANE, 0), SUBLANE), SUBLANE), :]
        prev = jnp.where(c > 0, prev, 0.0)
        nxt = xa_ref[pl.ds(pl.multiple_of(jnp.minimum(base + RG_T, seq_len - SUBLANE), SUBLANE), SUBLANE), :]
        nxt = jnp.where(c < nchunk - 1, nxt, 0.0)
        win = jnp.concatenate([prev, cur, nxt], axis=0)
        xc = cb
        for k in range(CONV_W):
            off = SUBLANE - 1 + k
            xc = xc + win[off:off + RG_T] * cw[k:k + 1, :]
        return xc

    def gates(xc, z):
        xb = xc.astype(BF16)
        rs, gs = [], []
        for t in range(RG_TILES):
            zz = _dot(xb[:, t * MXU_DIM:(t + 1) * MXU_DIM], wg_ref[0, z, t].astype(BF16))
            rs.append(zz[:, :MXU_DIM])
            gs.append(zz[:, MXU_DIM:])
        r = _sigmoid(jnp.concatenate(rs, axis=1) + br_ref[0, z:z + 1, :])
        g = _sigmoid(jnp.concatenate(gs, axis=1) + bi_ref[0, z:z + 1, :])
        log_a = (-RG_C * _softplus(-lam_ref[0, z:z + 1, :])) * r
        a = jnp.exp(log_a)
        y = -jnp.tanh(log_a) * (a * a + 1.0)
        bt = jnp.where(y > 0.0, y * lax.rsqrt(y), 0.0) * g * xc
        return a, bt

    def scan_chunk(h, reverse):
        def step(t, h):
            tt = RG_T - 1 - t if reverse else t
            h = a_s[pl.ds(tt, 1), :] * h + b_s[pl.ds(tt, 1), :]
            hs_s[pl.ds(tt, 1), :] = h
            return h
        return lax.fori_loop(0, RG_T, step, h, unroll=8)

    def fwd_chunk(c, h):
        rows = pl.ds(pl.multiple_of(c * RG_T, RG_T), RG_T)
        xc = conv_chunk(c)
        xc_s[rows, :] = xc
        a, bt = gates(xc, 0)
        a_s[...] = a
        b_s[...] = bt
        h = scan_chunk(h, False)
        hf_s[rows, :] = hs_s[...]
        return h

    def bwd_chunk(k, h):
        c = nchunk - 1 - k
        rows = pl.ds(pl.multiple_of(c * RG_T, RG_T), RG_T)
        a, bt = gates(xc_s[rows, :], 1)
        a_s[...] = a
        b_s[...] = bt
        h = scan_chunk(h, True)
        y_ref[rows, :] = ((hf_s[rows, :] + hs_s[...]) * jax.nn.gelu(ga_ref[rows, :])).astype(BF16)
        return h

    hf = lax.fori_loop(0, nchunk, fwd_chunk, h0_ref[0, 0:1, :])
    hb = lax.fori_loop(0, nchunk, bwd_chunk, h0_ref[0, 1:2, :])
    st_ref[0, 0:1, :] = hf
    st_ref[0, 1:2, :] = hb


def _rglru(proj, h0, j, cw, cb, wg, br, bi, lam, *, seq_len, nseq, row_block0):
    wspec3 = lambda shape: pl.BlockSpec((1,) + shape, lambda b: (j,) + (0,) * len(shape))
    return pl.pallas_call(
        functools.partial(_rglru_kernel, seq_len=seq_len),
        grid=(nseq,),
        in_specs=[
            pl.BlockSpec((seq_len, D_RG), lambda b: (row_block0 + b, 0)),
            pl.BlockSpec((seq_len, D_RG), lambda b: (row_block0 + b, 1)),
            pl.BlockSpec((1, 2, D_RG), lambda b: (b, 0, 0)),
            wspec3((CONV_W, D_RG)),
            wspec3((1, D_RG)),
            wspec3((2, RG_TILES, MXU_DIM, 2 * MXU_DIM)),
            wspec3((2, D_RG)),
            wspec3((2, D_RG)),
            wspec3((2, D_RG)),
        ],
        out_specs=[
            pl.BlockSpec((seq_len, D_RG), lambda b: (b, 0)),
            pl.BlockSpec((1, 2, D_RG), lambda b: (b, 0, 0)),
        ],
        out_shape=[
            jax.ShapeDtypeStruct((nseq * seq_len, D_RG), BF16),
            jax.ShapeDtypeStruct((nseq, 2, D_RG), F32),
        ],
        scratch_shapes=[
            pltpu.VMEM((seq_len, D_RG), F32),
            pltpu.VMEM((seq_len, D_RG), F32),
            pltpu.VMEM((RG_T, D_RG), F32),
            pltpu.VMEM((RG_T, D_RG), F32),
            pltpu.VMEM((RG_T, D_RG), F32),
        ],
        compiler_params=_cparams(("arbitrary",)),
        name=f"rglru_s{seq_len}",
    )(proj, proj, h0, cw, cb, wg, br, bi, lam)


def _gate_tiles(w_r, w_i):
    per_tile = MXU_DIM // RG_BLOCK
    eye = jnp.eye(per_tile, dtype=F32)

    def tiles(w):
        w = w.reshape(w.shape[0], 2, RG_TILES, per_tile, RG_BLOCK, RG_BLOCK)
        return jnp.einsum('nztbcd,be->nztbced', w, eye).reshape(w.shape[0], 2, RG_TILES, MXU_DIM, MXU_DIM)

    return jnp.concatenate([tiles(w_r), tiles(w_i)], axis=-1)


def _fnet_kernel(z_ref, cc_ref, sc_ref, w_ref, cs_ref, ns_ref, y_ref, p_s, q_s):
    @pl.when(pl.program_id(1) == 0)
    def _():
        w = w_ref[0]
        a = _dot3(cc_ref[...], w).astype(BF16)
        b = _dot3(sc_ref[...], w).astype(BF16)
        zb = z_ref[...].astype(BF16)
        p_s[...] = _dot(zb, a).astype(BF16)
        q_s[...] = _dot(zb, b).astype(BF16)

    y_ref[...] = (_dot(cs_ref[...].astype(BF16), p_s[...])
                  + _dot(ns_ref[...].astype(BF16), q_s[...])).astype(BF16)


def _fnet(proj, j, wbd, *, seq_len, nseq, row_block0):
    tr = min(seq_len, 512)
    nrt = seq_len // tr
    cc, sc = _dft_tables(FNET_GROUP_DIM)
    ccbd = jnp.asarray(np.kron(np.eye(FNET_GROUPS), cc), F32)
    scbd = jnp.asarray(np.kron(np.eye(FNET_GROUPS), sc), F32)
    cs, ss = _dft_tables(seq_len)
    cs_b, ns_b = jnp.asarray(cs, F32), jnp.asarray(-ss, F32)
    zcol = (2 * D_RG) // D_FNET
    const = lambda shape: pl.BlockSpec(shape, lambda b, r: (0,) * len(shape))
    return pl.pallas_call(
        _fnet_kernel,
        grid=(nseq, nrt),
        in_specs=[
            pl.BlockSpec((seq_len, D_FNET), lambda b, r: (row_block0 + b, zcol)),
            const((D_FNET, D_FNET)),
            const((D_FNET, D_FNET)),
            pl.BlockSpec((1, D_FNET, D_FNET), lambda b, r: (j, 0, 0)),
            pl.BlockSpec((tr, seq_len), lambda b, r: (r, 0)),
            pl.BlockSpec((tr, seq_len), lambda b, r: (r, 0)),
        ],
        out_specs=pl.BlockSpec((tr, D_FNET), lambda b, r: (b * nrt + r, 0)),
        out_shape=jax.ShapeDtypeStruct((nseq * seq_len, D_FNET), BF16),
        scratch_shapes=[pltpu.VMEM((seq_len, D_FNET), BF16), pltpu.VMEM((seq_len, D_FNET), BF16)],
        compiler_params=_cparams(("arbitrary", "arbitrary")),
        name=f"fnet_s{seq_len}",
    )(proj, ccbd, scbd, wbd, cs_b, ns_b)


def _dft_tables(n):
    k = np.arange(n)
    ang = 2.0 * np.pi * ((k[:, None] * k[None, :]) % n) / n
    return np.cos(ang) / np.sqrt(n), np.sin(ang) / np.sqrt(n)


def _fnet_blockdiag(w_f):
    eye = jnp.eye(FNET_GROUPS, dtype=F32)
    return jnp.einsum('ngcd,gh->ngchd', w_f, eye).reshape(w_f.shape[0], D_FNET, D_FNET)


def _outproj_kernel(ac_ref, al_ref, bc_ref, bl_ref, w_ref, xc_ref, xl_ref, mod_ref, g_ref, be_ref, wr_ref,
                    x1_ref, u2_ref, aff_ref, wbf_ref, *, ka):
    i = pl.program_id(0)

    @pl.when(i == 0)
    def _():
        wbf_ref[...] = w_ref[0].astype(BF16)

    out = _dot(_pick(i, ac_ref, al_ref), wbf_ref[0:ka, :]) + _dot(_pick(i, bc_ref, bl_ref), wbf_ref[ka:, :])
    m = mod_ref[0, 0]
    x1 = _layernorm(ALPHA * _pick(i, xc_ref, xl_ref) + m[2:3, :] * out, g_ref[0, 0:1, :], be_ref[0, 0:1, :])
    x1_ref[...] = x1
    u2 = x1 * (1.0 + m[4:5, :]) + m[3:4, :]
    u2_ref[...] = u2.astype(BF16)
    uh, ul = _split(u2)
    wh, wl = _split(wr_ref[0])
    lg = _dot(jnp.concatenate([uh, ul], axis=0), jnp.concatenate([wh, wl], axis=1))
    logits = (lg[:TM, :N_EXPERTS] + lg[:TM, N_EXPERTS:]) + (lg[TM:, :N_EXPERTS] + lg[TM:, N_EXPERTS:])
    mx = jnp.max(logits, axis=-1, keepdims=True)
    ex = jnp.exp(logits - mx)
    aff_ref[...] = ex / jnp.sum(ex, axis=-1, keepdims=True)


def _outproj(a_pair, b_pair, w_out, j, x_pair, mod, layer, ln_g, ln_b, w_router):
    ka, kb = a_pair[0].shape[1], b_pair[0].shape[1]
    combined = x_pair[0] is x_pair[1]
    return pl.pallas_call(
        functools.partial(_outproj_kernel, ka=ka),
        grid=(N_TILES,),
        in_specs=_pair_specs(ka) + _pair_specs(kb) + [
            pl.BlockSpec((1, D_MODEL, D_MODEL), lambda i: (j, 0, 0)),
        ] + _pair_specs(D_MODEL, CTX_TILES if combined else 0) + [
            pl.BlockSpec((1, 1, 6, D_MODEL), lambda i: (layer, _seg_of_tile(i), 0, 0)),
            pl.BlockSpec((1, 2, D_MODEL), lambda i: (layer, 0, 0)),
            pl.BlockSpec((1, 2, D_MODEL), lambda i: (layer, 0, 0)),
            pl.BlockSpec((1, D_MODEL, N_EXPERTS), lambda i: (layer, 0, 0)),
        ],
        out_specs=[
            pl.BlockSpec((TM, D_MODEL), lambda i: (i, 0)),
            pl.BlockSpec((TM, D_MODEL), lambda i: (i, 0)),
            pl.BlockSpec((TM, N_EXPERTS), lambda i: (i, 0)),
        ],
        out_shape=[
            jax.ShapeDtypeStruct((N_TOK, D_MODEL), F32),
            jax.ShapeDtypeStruct((N_TOK, D_MODEL), BF16),
            jax.ShapeDtypeStruct((N_TOK, N_EXPERTS), F32),
        ],
        scratch_shapes=[pltpu.VMEM((D_MODEL, D_MODEL), BF16)],
        compiler_params=_cparams(("arbitrary",)),
        name="outproj_ln_router",
    )(*a_pair, *b_pair, w_out, *x_pair, mod, ln_g, ln_b, w_router)


def _topk_kernel(aff_ref, pos_ref, cnt_ref):
    aff = aff_ref[...]
    cap = jnp.float32(CAP)

    def count_ge(t):
        return jnp.sum(jnp.where(aff >= t, 1.0, 0.0), axis=1, keepdims=True)

    lo_v = jnp.zeros((N_EXPERTS, 1), F32)
    hi_v = jnp.full((N_EXPERTS, 1), 2.0, F32)
    lo_t = jnp.full((N_EXPERTS, 1), TOPK_MIN_EXP, F32)
    hi_t = jnp.full((N_EXPERTS, 1), 1.0, F32)
    for _ in range(TOPK_GEO_STEPS):
        mid_t = 0.5 * (lo_t + hi_t)
        cand = jnp.exp2(mid_t)
        ok = count_ge(cand) >= cap
        lo_t = jnp.where(ok, mid_t, lo_t)
        hi_t = jnp.where(ok, hi_t, mid_t)
        lo_v = jnp.where(ok, jnp.maximum(lo_v, cand), lo_v)
        hi_v = jnp.where(ok, hi_v, jnp.minimum(hi_v, cand))
    for _ in range(TOPK_LIN_STEPS):
        cand = lo_v + 0.5 * (hi_v - lo_v)
        ok = count_ge(cand) >= cap
        lo_v = jnp.where(ok, cand, lo_v)
        hi_v = jnp.where(ok, hi_v, cand)
    gt = aff >= hi_v
    eq = (aff >= lo_v) & (aff < hi_v)
    need = cap - jnp.sum(jnp.where(gt, 1.0, 0.0), axis=1, keepdims=True)

    tri = (lax.broadcasted_iota(I32, (LANE, LANE), 0) <= lax.broadcasted_iota(I32, (LANE, LANE), 1))
    tri = jnp.where(tri, 1.0, 0.0).astype(BF16)

    def cumsum_blocks(mask_f32):
        carry = jnp.zeros((N_EXPERTS, 1), F32)
        outs = []
        for k in range(mask_f32.shape[1] // LANE):
            blk = mask_f32[:, k * LANE:(k + 1) * LANE]
            inc = _dot(blk.astype(BF16), tri) + carry
            outs.append(inc)
            carry = carry + jnp.sum(blk, axis=1, keepdims=True)
        return outs

    eq_f = jnp.where(eq, 1.0, 0.0)
    gt_f = jnp.where(gt, 1.0, 0.0)
    eq_rank = cumsum_blocks(eq_f)
    sel_blocks = []
    for k, rk in enumerate(eq_rank):
        lanes = slice(k * LANE, (k + 1) * LANE)
        sel_blocks.append(jnp.where((eq_f[:, lanes] > 0.0) & (rk <= need), 1.0, gt_f[:, lanes]))
    sel = jnp.concatenate(sel_blocks, axis=1)
    sel_rank = cumsum_blocks(sel)
    for k, rk in enumerate(sel_rank):
        lanes = slice(k * LANE, (k + 1) * LANE)
        pos_ref[:, lanes] = jnp.where(sel[:, lanes] > 0.0, rk.astype(I32) - 1, -1)
    lane = lax.broadcasted_iota(I32, (N_EXPERTS, LANE), 1)
    cnt = jnp.zeros((N_EXPERTS, LANE), F32)
    for c in range(1, N_CHUNKS + 1):
        cnt = jnp.where(lane == c, sel_rank[c * (GC // LANE) - 1][:, LANE - 1:LANE], cnt)
    cnt_ref[0] = cnt.astype(I32)


def _topk(aff_t):
    return pl.pallas_call(
        _topk_kernel,
        grid=(2,),
        in_specs=[pl.BlockSpec((N_EXPERTS, N_CTX), lambda g: (0, g))],
        out_specs=[
            pl.BlockSpec((N_EXPERTS, N_CTX), lambda g: (0, g)),
            pl.BlockSpec((1, N_EXPERTS, LANE), lambda g: (g, 0, 0)),
        ],
        out_shape=[
            jax.ShapeDtypeStruct((N_EXPERTS, N_TOK), I32),
            jax.ShapeDtypeStruct((2, N_EXPERTS, LANE), I32),
        ],
        compiler_params=_cparams(("arbitrary",)),
        name="topk_select",
    )(aff_t)


def _chunk_windows(cnt_ref, g, e, c):
    c0 = cnt_ref[g, e, c]
    c1 = cnt_ref[g, e, c + 1]
    base = (c0 // WIN_ALIGN) * WIN_ALIGN
    k = jnp.where(c1 > c0, (c1 - base + WIN - 1) // WIN, 0)
    return base, k


def _gather_kernel(cnt_ref, pos_ref, aff_ref, u_ref, xe_ref, gs_ref):
    g = pl.program_id(0)
    c = pl.program_id(1)

    @pl.when(c == 0)
    def _():
        xe_ref[...] = jnp.zeros_like(xe_ref)
        gs_ref[...] = jnp.zeros_like(gs_ref)

    pos = pos_ref[...]
    aff = aff_ref[...]
    slot0 = lax.broadcasted_iota(I32, (WIN, GC), 0)

    def window(e, lo):
        start = pl.multiple_of(jnp.minimum(lo, CAP - WIN), WIN_ALIGN)
        pe = pos[e:e + 1, :]
        hit = (slot0 + start) == jnp.where(pe >= lo, pe, -1)
        gates = jnp.sum(jnp.where(hit, aff[e:e + 1, :], 0.0), axis=1, keepdims=True)
        return start, jnp.where(hit, 1.0, 0.0).astype(BF16), gates

    def add_rows(e, start, rows, gates):
        dst = (e, 0, pl.ds(start, WIN), slice(None))
        xe_ref[dst] = (xe_ref[dst].astype(F32) + rows).astype(BF16)
        gs_ref[dst] += gates

    windows = [_chunk_windows(cnt_ref, g, e, c) for e in range(N_EXPERTS)]
    firsts = [window(e, base) for e, (base, _) in enumerate(windows)]
    rows = _dot(jnp.concatenate([oh for _, oh, _ in firsts], axis=0), u_ref[...])
    for e, (start, _, gates) in enumerate(firsts):
        add_rows(e, start, rows[e * WIN:(e + 1) * WIN], gates)
    for e, (base, k) in enumerate(windows):
        def extra(w, carry, e=e, base=base):
            start, oh, gates = window(e, base + w * WIN)
            add_rows(e, start, _dot(oh, u_ref[...]), gates)
            return carry
        lax.fori_loop(1, k, extra, 0)


def _gather(cnt, pos_t, aff_t, u2):
    return pl.pallas_call(
        _gather_kernel,
        grid_spec=pltpu.PrefetchScalarGridSpec(
            num_scalar_prefetch=1,
            grid=(2, N_CHUNKS),
            in_specs=[
                pl.BlockSpec((N_EXPERTS, GC), lambda g, c, cnt: (0, g * N_CHUNKS + c)),
                pl.BlockSpec((N_EXPERTS, GC), lambda g, c, cnt: (0, g * N_CHUNKS + c)),
                pl.BlockSpec((GC, D_MODEL), lambda g, c, cnt: (g * N_CHUNKS + c, 0)),
            ],
            out_specs=[
                pl.BlockSpec((N_EXPERTS, 1, CAP, D_MODEL), lambda g, c, cnt: (0, g, 0, 0)),
                pl.BlockSpec((N_EXPERTS, 1, CAP, 1), lambda g, c, cnt: (0, g, 0, 0)),
            ],
        ),
        out_shape=[
            jax.ShapeDtypeStruct((N_EXPERTS, 2, CAP, D_MODEL), BF16),
            jax.ShapeDtypeStruct((N_EXPERTS, 2, CAP, 1), F32),
        ],
        compiler_params=_cparams(("arbitrary", "arbitrary")),
        name="moe_gather",
    )(cnt, pos_t, aff_t, u2)


TF = 1024


def _ffn_kernel(xe_ref, gs_ref, wg_ref, wu_ref, wd_ref, ye_ref, acc_ref):
    f = pl.program_id(1)
    x = xe_ref[0]
    hg = _dot(x, wg_ref[0, 0].astype(BF16))
    hu = _dot(x, wu_ref[0, 0].astype(BF16))
    h = (hg * _sigmoid(hg) * hu).astype(BF16)

    @pl.when(f == 0)
    def _():
        acc_ref[...] = jnp.zeros_like(acc_ref)

    acc_ref[...] += _dot(h, wd_ref[0, 0].astype(BF16))

    @pl.when(f == pl.num_programs(1) - 1)
    def _():
        ye_ref[0] = (acc_ref[...] * gs_ref[0]).astype(BF16)


def _ffn(xe, gs, layer, w_gate, w_up, w_down):
    rows = 2 * CAP
    return pl.pallas_call(
        _ffn_kernel,
        grid=(N_EXPERTS, D_EXPERT // TF),
        in_specs=[
            pl.BlockSpec((1, rows, D_MODEL), lambda e, f: (e, 0, 0)),
            pl.BlockSpec((1, rows, 1), lambda e, f: (e, 0, 0)),
            pl.BlockSpec((1, 1, D_MODEL, TF), lambda e, f: (layer, e, 0, f)),
            pl.BlockSpec((1, 1, D_MODEL, TF), lambda e, f: (layer, e, 0, f)),
            pl.BlockSpec((1, 1, TF, D_MODEL), lambda e, f: (layer, e, f, 0)),
        ],
        out_specs=pl.BlockSpec((1, rows, D_MODEL), lambda e, f: (e, 0, 0)),
        out_shape=jax.ShapeDtypeStruct((N_EXPERTS, rows, D_MODEL), BF16),
        scratch_shapes=[pltpu.VMEM((rows, D_MODEL), F32)],
        compiler_params=_cparams(("arbitrary", "arbitrary")),
        name="moe_ffn",
    )(xe, gs, w_gate, w_up, w_down)


def _combine_kernel(*refs, split_out):
    if split_out:
        cnt_ref, posn_ref, ye_ref, x1_ref, mod_ref, g_ref, be_ref, oc_ref, ol_ref, acc_s = refs
    else:
        cnt_ref, posn_ref, ye_ref, x1_ref, mod_ref, g_ref, be_ref, o_ref, acc_s = refs
    i = pl.program_id(0)
    g = i // N_CHUNKS
    c = i % N_CHUNKS
    posn = posn_ref[...]
    lane = lax.broadcasted_iota(I32, (1, MXU_DIM), 1)
    part = lane // WIN
    slot_in_win = lane % WIN

    def by_part(vals):
        out = vals[-1]
        for q in range(len(vals) - 2, -1, -1):
            out = jnp.where(part == q, vals[q], out)
        return out

    def start_of(lo):
        return pl.multiple_of(jnp.minimum(lo, CAP - WIN), WIN_ALIGN)

    def group_product(experts, los):
        starts = [start_of(lo) for lo in los]
        slots = by_part(starts) + slot_in_win
        cols = by_part([jnp.where(posn[:, e:e + 1] >= lo, posn[:, e:e + 1], -1) for e, lo in zip(experts, los)])
        onehot = jnp.where(slots == cols, 1.0, 0.0).astype(BF16)
        rows = jnp.concatenate([ye_ref[e, 0, pl.ds(st, WIN), :] for e, st in zip(experts, starts)], axis=0)
        return _dot(onehot, rows)

    windows = [_chunk_windows(cnt_ref, g, e, c) for e in range(N_EXPERTS)]
    acc = None
    for e0 in range(0, N_EXPERTS, WINS_PER_DOT):
        experts = list(range(e0, e0 + WINS_PER_DOT))
        d = group_product(experts, [windows[e][0] for e in experts])
        acc = d if acc is None else acc + d
    acc_s[...] = acc
    slot0 = lax.broadcasted_iota(I32, (GC, WIN), 1)
    for e, (base, k) in enumerate(windows):
        def extra(w, carry, e=e, base=base):
            lo = base + w * WIN
            start = start_of(lo)
            col = posn[:, e:e + 1]
            hit = (slot0 + start) == jnp.where(col >= lo, col, -1)
            acc_s[...] += _dot(jnp.where(hit, 1.0, 0.0).astype(BF16), ye_ref[e, 0, pl.ds(start, WIN), :])
            return carry
        lax.fori_loop(1, k, extra, 0)
    m = mod_ref[0, 0]
    y = ALPHA * x1_ref[...] + m[5:6, :] * acc_s[...]
    res = _layernorm(y, g_ref[0, 1:2, :], be_ref[0, 1:2, :])
    if split_out:
        @pl.when(i < N_CHUNKS)
        def _():
            oc_ref[...] = res

        @pl.when(i >= N_CHUNKS)
        def _():
            ol_ref[...] = res
    else:
        o_ref[...] = res


def _combine(cnt, pos_n, ye, x1, mod, layer, ln_g, ln_b, split_out):
    if split_out:
        out_specs = _pair_specs(D_MODEL, tile=GC)
        out_shape = [jax.ShapeDtypeStruct((N_CTX, D_MODEL), F32), jax.ShapeDtypeStruct((N_LAT, D_MODEL), F32)]
    else:
        out_specs = pl.BlockSpec((GC, D_MODEL), lambda i, cnt: (i, 0))
        out_shape = jax.ShapeDtypeStruct((N_TOK, D_MODEL), F32)
    return pl.pallas_call(
        functools.partial(_combine_kernel, split_out=split_out),
        grid_spec=pltpu.PrefetchScalarGridSpec(
            num_scalar_prefetch=1,
            grid=(N_TOK // GC,),
            in_specs=[
                pl.BlockSpec((GC, N_EXPERTS), lambda i, cnt: (i, 0)),
                pl.BlockSpec((N_EXPERTS, 1, CAP, D_MODEL), lambda i, cnt: (0, i // N_CHUNKS, 0, 0)),
                pl.BlockSpec((GC, D_MODEL), lambda i, cnt: (i, 0)),
                pl.BlockSpec((1, 1, 6, D_MODEL), lambda i, cnt: (layer, _seg_of_tile(i, GC), 0, 0)),
                pl.BlockSpec((1, 2, D_MODEL), lambda i, cnt: (layer, 0, 0)),
                pl.BlockSpec((1, 2, D_MODEL), lambda i, cnt: (layer, 0, 0)),
            ],
            out_specs=out_specs,
            scratch_shapes=[pltpu.VMEM((GC, D_MODEL), F32)],
        ),
        out_shape=out_shape,
        compiler_params=_cparams(("arbitrary",)),
        name="moe_combine_ln",
    )(cnt, pos_n, ye, x1, mod, ln_g, ln_b)


def _moe(x1, u2, aff, mod, layer, ln_g, ln_b, w_gate, w_up, w_down, split_out=False):
    aff_t = aff.T
    pos_t, cnt = _topk(aff_t)
    xe, gs = _gather(cnt, pos_t, aff_t, u2)
    ye = _ffn(xe.reshape(N_EXPERTS, 2 * CAP, D_MODEL), gs.reshape(N_EXPERTS, 2 * CAP, 1),
              layer, w_gate, w_up, w_down)
    return _combine(cnt, pos_t.T, ye.reshape(N_EXPERTS, 2, CAP, D_MODEL), x1, mod, layer, ln_g, ln_b, split_out)


def _sink_softmax_pv(s, sink_col, v):
    m = jnp.maximum(jnp.max(s, axis=-1, keepdims=True), sink_col)
    p = jnp.exp(s - m)
    den = jnp.sum(p, axis=-1, keepdims=True) + jnp.exp(sink_col - m)
    return _dot(p.astype(BF16), v) / den


def _sink_column(sink_ref, h, rows_per_group):
    rid = lax.broadcasted_iota(I32, (GQA_GROUP * rows_per_group, 1), 0) // rows_per_group
    col = jnp.zeros((GQA_GROUP * rows_per_group, 1), F32)
    for g in range(GQA_GROUP):
        col = jnp.where(rid == g, sink_ref[h * GQA_GROUP + g], col)
    return col


def _stack_groups(q, h):
    return jnp.concatenate(
        [q[:, (h * GQA_GROUP + g) * HEAD_DIM:(h * GQA_GROUP + g + 1) * HEAD_DIM] * ATTN_SCALE
         for g in range(GQA_GROUP)], axis=0).astype(BF16)


def _unstack_groups(outs, rows):
    return jnp.concatenate([o[g * rows:(g + 1) * rows] for o in outs for g in range(GQA_GROUP)], axis=1)


def _swa_ctx_kernel(sink_ref, q_ref, k_ref, v_ref, o_ref):
    q, k, v = q_ref[...], k_ref[...], v_ref[...]
    outs = []
    for h in range(SWA_KV_HEADS):
        cols = slice(h * HEAD_DIM, (h + 1) * HEAD_DIM)
        s = _dot_nt(_stack_groups(q, h), k[:, cols].astype(BF16))
        outs.append(_sink_softmax_pv(s, _sink_column(sink_ref, h, SEQ), v[:, cols].astype(BF16)))
    o_ref[...] = _unstack_groups(outs, SEQ).astype(BF16)


def _swa_ctx(proj, sink):
    return pl.pallas_call(
        _swa_ctx_kernel,
        grid=(BATCH,),
        in_specs=[
            pl.BlockSpec(memory_space=pltpu.SMEM),
            pl.BlockSpec((SEQ, 512), lambda b: (b, 0)),
            pl.BlockSpec((SEQ, LANE), lambda b: (b, 4)),
            pl.BlockSpec((SEQ, LANE), lambda b: (b, 5)),
        ],
        out_specs=pl.BlockSpec((SEQ, 512), lambda b: (b, 0)),
        out_shape=jax.ShapeDtypeStruct((N_CTX, 512), BF16),
        compiler_params=_cparams(("arbitrary",)),
        name="swa_context",
    )(sink, proj, proj, proj)


QB = 128


def _swa_lat_kernel(sink_ref, q_ref, kp_ref, kc_ref, kn_ref, vp_ref, vc_ref, vn_ref, ck_ref, cv_ref, o_ref):
    qb = pl.program_id(1)
    q = q_ref[...]
    kall = jnp.concatenate([ck_ref[0].astype(BF16), kp_ref[...], kc_ref[...], kn_ref[...]], axis=0)
    vall = jnp.concatenate([cv_ref[0].astype(BF16), vp_ref[...], vc_ref[...], vn_ref[...]], axis=0)
    nk = PAST_LEN + 3 * QB
    col = lax.broadcasted_iota(I32, (QB, nk), 1)
    qpos = qb * QB + lax.broadcasted_iota(I32, (QB, nk), 0)
    kpos = (qb - 1) * QB + col - PAST_LEN
    ok = (col < PAST_LEN) | ((jnp.abs(qpos - kpos) <= WINDOW) & (kpos >= 0) & (kpos < DEC_SEQ))
    bias = jnp.concatenate([jnp.where(ok, 0.0, NEG_INF)] * GQA_GROUP, axis=0)
    outs = []
    for h in range(SWA_KV_HEADS):
        cols = slice(h * HEAD_DIM, (h + 1) * HEAD_DIM)
        s = _dot_nt(_stack_groups(q, h), kall[:, cols]) + bias
        outs.append(_sink_softmax_pv(s, _sink_column(sink_ref, h, QB), vall[:, cols]))
    o_ref[...] = _unstack_groups(outs, QB).astype(BF16)


def _swa_lat(proj, cache_k, cache_v, sink):
    nqb = DEC_SEQ // QB
    row0 = N_CTX // QB

    def blk(col, shift):
        return pl.BlockSpec((QB, LANE), lambda b, i: (row0 + b * nqb + jnp.clip(i + shift, 0, nqb - 1), col))

    return pl.pallas_call(
        _swa_lat_kernel,
        grid=(DEC_BATCH, nqb),
        in_specs=[
            pl.BlockSpec(memory_space=pltpu.SMEM),
            pl.BlockSpec((QB, 512), lambda b, i: (row0 + b * nqb + i, 0)),
            blk(4, -1), blk(4, 0), blk(4, 1),
            blk(5, -1), blk(5, 0), blk(5, 1),
            pl.BlockSpec((1, PAST_LEN, LANE), lambda b, i: (b, 0, 0)),
            pl.BlockSpec((1, PAST_LEN, LANE), lambda b, i: (b, 0, 0)),
        ],
        out_specs=pl.BlockSpec((QB, 512), lambda b, i: (b * nqb + i, 0)),
        out_shape=jax.ShapeDtypeStruct((N_LAT, 512), BF16),
        compiler_params=_cparams(("arbitrary", "arbitrary")),
        name="swa_latent",
    )(sink, proj, proj, proj, proj, proj, proj, proj, cache_k, cache_v)


def _diff_lambda_col(lam_ref, lam_init):
    l1 = jnp.sum(lam_ref[0, 0] * lam_ref[0, 1], axis=-1, keepdims=True)
    l2 = jnp.sum(lam_ref[0, 2] * lam_ref[0, 3], axis=-1, keepdims=True)
    return jnp.exp(l1) - jnp.exp(l2) + lam_init


def _softmax_pv(q, k, v):
    s = _dot_nt(q, k)
    p = jnp.exp(s - jnp.max(s, axis=-1, keepdims=True))
    return _dot(p.astype(BF16), v) / jnp.sum(p, axis=-1, keepdims=True)


def _diff_head(q, k, v, lam, subln, lam_init):
    o = [_softmax_pv((q[:, m * HEAD_DIM:(m + 1) * HEAD_DIM] * ATTN_SCALE).astype(BF16),
                     k[:, m * HEAD_DIM:(m + 1) * HEAD_DIM], v) for m in range(2)]
    o = o[0] - lam * o[1]
    return o * lax.rsqrt(jnp.mean(o * o, axis=-1, keepdims=True) + LN_EPS) * subln * (1.0 - lam_init)


def _diff_kernel(*refs, lam_init, has_cache, nh):
    if has_cache:
        lam_ref, sg_ref, q_ref, k_ref, v_ref, ck_ref, cv_ref, o_ref = refs
    else:
        lam_ref, sg_ref, q_ref, k_ref, v_ref, o_ref = refs
    hp = pl.program_id(1)
    lam_all = _diff_lambda_col(lam_ref, lam_init)
    hid = lax.broadcasted_iota(I32, (DIFF_HEADS, 1), 0)
    q, k, v = q_ref[...], k_ref[...], v_ref[...]
    if has_cache:
        k = jnp.concatenate([k, ck_ref[0].astype(BF16)], axis=0)
        v = jnp.concatenate([v, cv_ref[0].astype(BF16)], axis=0)
    outs = []
    for j in range(nh):
        lam = jnp.sum(jnp.where(hid == hp * nh + j, lam_all, 0.0), axis=0, keepdims=True)
        cols = slice(j * DIFF_V_DIM, (j + 1) * DIFF_V_DIM)
        outs.append(_diff_head(q[:, cols], k[:, cols], v[:, cols], lam, sg_ref[...], lam_init))
    o_ref[...] = jnp.concatenate(outs, axis=1).astype(BF16)


DIFF_STEP_HEADS = 2
DQ = 512
Q_COL, K_COL, V_COL = 768, 1280, 1792
assert all(col % (DIFF_STEP_HEADS * DIFF_V_DIM) == 0 for col in (Q_COL, K_COL, V_COL))


def _diff_ctx(proj, lam_p, subln, j, lam_init):
    nh = DIFF_STEP_HEADS
    w = nh * DIFF_V_DIM
    return pl.pallas_call(
        functools.partial(_diff_kernel, lam_init=lam_init, has_cache=False, nh=nh),
        grid=(BATCH, DIFF_HEADS // nh),
        in_specs=[
            pl.BlockSpec((1, 4, DIFF_HEADS, HEAD_DIM), lambda b, h: (j, 0, 0, 0)),
            pl.BlockSpec((1, DIFF_V_DIM), lambda b, h: (j, 0)),
            pl.BlockSpec((SEQ, w), lambda b, h: (b, Q_COL // w + h)),
            pl.BlockSpec((SEQ, w), lambda b, h: (b, K_COL // w + h)),
            pl.BlockSpec((SEQ, w), lambda b, h: (b, V_COL // w + h)),
        ],
        out_specs=pl.BlockSpec((SEQ, w), lambda b, h: (b, h)),
        out_shape=jax.ShapeDtypeStruct((N_CTX, 512), BF16),
        compiler_params=_cparams(("arbitrary", "arbitrary")),
        name="diff_context",
    )(lam_p, subln, proj, proj, proj)


def _diff_lat(proj, cache_k, cache_v, lam_p, subln, j, lam_init):
    nh = DIFF_STEP_HEADS
    w = nh * DIFF_V_DIM
    nq = DEC_SEQ // DQ
    row0 = N_CTX // DQ
    seq0 = N_CTX // DEC_SEQ
    return pl.pallas_call(
        functools.partial(_diff_kernel, lam_init=lam_init, has_cache=True, nh=nh),
        grid=(DEC_BATCH, DIFF_HEADS // nh, nq),
        in_specs=[
            pl.BlockSpec((1, 4, DIFF_HEADS, HEAD_DIM), lambda b, h, i: (j, 0, 0, 0)),
            pl.BlockSpec((1, DIFF_V_DIM), lambda b, h, i: (j, 0)),
            pl.BlockSpec((DQ, w), lambda b, h, i: (row0 + b * nq + i, Q_COL // w + h)),
            pl.BlockSpec((DEC_SEQ, w), lambda b, h, i: (seq0 + b, K_COL // w + h)),
            pl.BlockSpec((DEC_SEQ, w), lambda b, h, i: (seq0 + b, V_COL // w + h)),
            pl.BlockSpec((1, PAST_LEN, w), lambda b, h, i: (b, 0, h)),
            pl.BlockSpec((1, PAST_LEN, w), lambda b, h, i: (b, 0, h)),
        ],
        out_specs=pl.BlockSpec((DQ, w), lambda b, h, i: (b * nq + i, h)),
        out_shape=jax.ShapeDtypeStruct((N_LAT, 512), BF16),
        compiler_params=_cparams(("arbitrary", "arbitrary", "arbitrary")),
        name="diff_latent",
    )(lam_p, subln, proj, proj, proj, cache_k, cache_v)


def kernel(x_prompt, x_sample, state_rglru, cache_swa_k, cache_swa_v, cache_diff_k, cache_diff_v, c, c_ctx,
           w_mod, b_mod, ln_g, ln_b, e_w_in, e_conv_w, e_conv_b, e_w_rgate, e_b_rgate, e_w_igate, e_b_igate,
           e_lambda, e_w_fnet, e_w_out, o_w_in, o_sink, o_lambda, o_subln_g, o_w_out,
           w_router, w_gate, w_up, w_down):
    x_pair = (x_prompt.reshape(N_CTX, D_MODEL), x_sample.reshape(N_LAT, D_MODEL))
    cvec8 = jnp.concatenate([c_ctx[None, :], c, jnp.zeros((SUBLANE - N_SEG, D_MODEL), F32)], axis=0).T
    mod = _modulation(cvec8, w_mod, b_mod).reshape(DEPTH, SUBLANE, 6, D_MODEL)

    j = 0
    proj = _inproj(x_pair, mod, 0, e_w_in, j)
    wg = _gate_tiles(e_w_rgate, e_w_igate)
    cb = e_conv_b.reshape(-1, 1, D_RG)
    rg_args = (j, e_conv_w, cb, wg, e_b_rgate, e_b_igate, e_lambda)
    y_rg_c, st_c = _rglru(proj, jnp.zeros((BATCH, 2, D_RG), F32), *rg_args,
                          seq_len=SEQ, nseq=BATCH, row_block0=0)
    y_rg_l, _ = _rglru(proj, state_rglru[:, j], *rg_args,
                       seq_len=DEC_SEQ, nseq=DEC_BATCH, row_block0=N_CTX // DEC_SEQ)
    wbd = _fnet_blockdiag(e_w_fnet)
    y_fn_c = _fnet(proj, j, wbd, seq_len=SEQ, nseq=BATCH, row_block0=0)
    y_fn_l = _fnet(proj, j, wbd, seq_len=DEC_SEQ, nseq=DEC_BATCH, row_block0=N_CTX // DEC_SEQ)
    x1, u2, aff = _outproj((y_rg_c, y_rg_l), (y_fn_c, y_fn_l), e_w_out, j, x_pair, mod, 0, ln_g, ln_b, w_router)
    x = _moe(x1, u2, aff, mod, 0, ln_g, ln_b, w_gate, w_up, w_down)
    new_state_rglru = st_c[:, None]

    layer = 1
    lam_init = 0.8 - 0.6 * math.exp(-0.3 * layer)
    cos, sin = _rope_tables()
    rope_groups = (0, 1, 2, 3, 4, 6, 7, 8, 9, 10, 11, 12, 13)
    cache_cols = ((512, 128), (640, 128), (K_COL, 512), (V_COL, 512))
    proj, ks, vs, kd, vd = _inproj((x, x), mod, 1, o_w_in, j, rope=(cos, sin, rope_groups), cache_cols=cache_cols)
    sink = o_sink[j]
    ys_c = _swa_ctx(proj, sink)
    yd_c = _diff_ctx(proj, o_lambda, o_subln_g, j, lam_init)
    ck = cache_swa_k[:, j].reshape(DEC_BATCH, PAST_LEN, SWA_KV_HEADS * HEAD_DIM)
    cv = cache_swa_v[:, j].reshape(DEC_BATCH, PAST_LEN, SWA_KV_HEADS * HEAD_DIM)
    ys_l = _swa_lat(proj, ck, cv, sink)
    cdk = cache_diff_k[:, j].reshape(DEC_BATCH, PAST_LEN, DIFF_HEADS * 2 * HEAD_DIM)
    cdv = cache_diff_v[:, j].reshape(DEC_BATCH, PAST_LEN, DIFF_HEADS * DIFF_V_DIM)
    yd_l = _diff_lat(proj, cdk, cdv, o_lambda, o_subln_g, j, lam_init)
    x1, u2, aff = _outproj((ys_c, ys_l), (yd_c, yd_l), o_w_out, j, (x, x), mod, 1, ln_g, ln_b, w_router)
    y_c, y_l = _moe(x1, u2, aff, mod, 1, ln_g, ln_b, w_gate, w_up, w_down, split_out=True)

    new_cache_swa_k = ks.reshape(BATCH, 1, SEQ, SWA_KV_HEADS, HEAD_DIM)
    new_cache_swa_v = vs.reshape(BATCH, 1, SEQ, SWA_KV_HEADS, HEAD_DIM)
    new_cache_diff_k = kd.reshape(BATCH, 1, SEQ, DIFF_HEADS, 2, HEAD_DIM)
    new_cache_diff_v = vd.reshape(BATCH, 1, SEQ, DIFF_HEADS, DIFF_V_DIM)
    y_prompt = y_c.reshape(BATCH, SEQ, D_MODEL)
    y_sample = y_l.reshape(DEC_BATCH, DEC_SEQ, D_MODEL)
    return (y_prompt, y_sample, new_state_rglru, new_cache_swa_k, new_cache_swa_v,
            new_cache_diff_k, new_cache_diff_v)
```

```python
import functools
import math

import numpy as np
import jax
import jax.numpy as jnp
from jax import lax
from jax.experimental import pallas as pl
from jax.experimental.pallas import tpu as pltpu

F32 = jnp.float32
BF16 = jnp.bfloat16
I32 = jnp.int32

D_MODEL = 1024
BATCH, SEQ = 16, 256
DEC_BATCH, DEC_SEQ = 2, 2048
PAST_LEN = 256
DEPTH = 2
GRID_W = 64
HEAD_DIM = 64
D_RG = 768
RG_BLOCK = 64
RG_C = 8.0
CONV_W = 4
D_FNET = 256
FNET_GROUP_DIM = 64
FNET_GROUPS = 4
SWA_HEADS = 8
SWA_KV_HEADS = 2
GQA_GROUP = 4
WINDOW = 128
DIFF_HEADS = 4
DIFF_V_DIM = 128
N_EXPERTS = 16
EC_FACTOR = 2
D_EXPERT = 2048
ROPE_BASE = 10000.0
LN_EPS = 1e-5
NEG_INF = -1e30
ATTN_SCALE = HEAD_DIM ** -0.5
ALPHA = (2 * DEPTH) ** 0.25
EVEN_IN = 2 * D_RG + D_FNET
ODD_IN = 2304

N_CTX = BATCH * SEQ
N_LAT = DEC_BATCH * DEC_SEQ
N_TOK = N_CTX + N_LAT
N_SEG = 1 + DEC_BATCH
CAP = EC_FACTOR * N_CTX // N_EXPERTS
assert N_CTX == N_LAT

LANE = 128
SUBLANE = 8
MXU_DIM = 256
VMEM_LIMIT = 56 * 1024 * 1024
TM = 512
GC = 256
N_CHUNKS = N_CTX // GC
WIN = 64
WIN_ALIGN = 16
WINS_PER_DOT = MXU_DIM // WIN
TOPK_MIN_EXP = -150.0
TOPK_GEO_STEPS = 16
TOPK_LIN_STEPS = 24
N_TILES = N_TOK // TM
CTX_TILES = N_CTX // TM
LAT_TILES_PER_SEQ = DEC_SEQ // TM


def _cparams(sem):
    return pltpu.CompilerParams(dimension_semantics=sem, vmem_limit_bytes=VMEM_LIMIT)


def _dot(a, b):
    return jnp.dot(a, b, preferred_element_type=F32)


def _dot_nt(a, b):
    return lax.dot_general(a, b, (((1,), (1,)), ((), ())), preferred_element_type=F32)


def _split(a):
    hi = a.astype(BF16)
    lo = (a - hi.astype(F32)).astype(BF16)
    return hi, lo


def _dot3(a, b):
    ah, al = _split(a)
    bh, bl = _split(b)
    return _dot(ah, bh) + (_dot(ah, bl) + _dot(al, bh))


def _layernorm(y, g, b):
    mu = jnp.mean(y, axis=-1, keepdims=True)
    d = y - mu
    var = jnp.mean(d * d, axis=-1, keepdims=True)
    return d * lax.rsqrt(var + LN_EPS) * g + b


def _pair_specs(width, lat_block0=0, tile=TM):
    ctx_tiles = N_CTX // tile
    return [
        pl.BlockSpec((tile, width), lambda i, *_: (jnp.minimum(i, ctx_tiles - 1), 0)),
        pl.BlockSpec((tile, width), lambda i, *_: (jnp.maximum(i - ctx_tiles, 0) + lat_block0, 0)),
    ]


def _pick(i, ctx_ref, lat_ref):
    return jnp.where(i < CTX_TILES, ctx_ref[...], lat_ref[...])


def _seg_of_tile(i, tile=TM):
    ctx_tiles = N_CTX // tile
    return jnp.where(i < ctx_tiles, 0, (i - ctx_tiles) // (DEC_SEQ // tile) + 1)


MOD_TN = 1536


def _mod_kernel(ct_ref, w_ref, b_ref, o_ref):
    c = ct_ref[...]
    s = c * _sigmoid(c)
    w = w_ref[0]
    rows = [jnp.sum(s[:, v:v + 1] * w, axis=0, keepdims=True) for v in range(N_SEG)]
    rows.append(jnp.zeros((SUBLANE - N_SEG, w.shape[1]), F32))
    o_ref[0] = jnp.concatenate(rows, axis=0) + b_ref[0]


def _modulation(cvec8, w_mod, b_mod):
    n = 6 * D_MODEL
    return pl.pallas_call(
        _mod_kernel,
        grid=(DEPTH, n // MOD_TN),
        in_specs=[
            pl.BlockSpec((D_MODEL, SUBLANE), lambda l, j: (0, 0)),
            pl.BlockSpec((1, D_MODEL, MOD_TN), lambda l, j: (l, 0, j)),
            pl.BlockSpec((1, 1, MOD_TN), lambda l, j: (l, 0, j)),
        ],
        out_specs=pl.BlockSpec((1, SUBLANE, MOD_TN), lambda l, j: (l, 0, j)),
        out_shape=jax.ShapeDtypeStruct((DEPTH, SUBLANE, n), F32),
        compiler_params=_cparams(("arbitrary", "arbitrary")),
        name="modulation",
    )(cvec8, w_mod, b_mod.reshape(DEPTH, 1, n))


def _rope_tile(p, cos, sin, rope_groups):
    lane = lax.broadcasted_iota(I32, (1, LANE), 1)
    first_half = (lane % 32) < 16
    pieces = []
    for k in range(p.shape[1] // LANE):
        xg = p[:, k * LANE:(k + 1) * LANE]
        if k in rope_groups:
            partner = jnp.where(first_half, pltpu.roll(xg, LANE - 16, 1), pltpu.roll(xg, 16, 1))
            xg = xg * cos + partner * sin
        pieces.append(xg)
    return jnp.concatenate(pieces, axis=1)


def _inproj_kernel(*refs, rope_groups, cache_cols):
    if rope_groups:
        xc_ref, xl_ref, mod_ref, w_ref, cos_ref, sin_ref, o_ref = refs[:7]
        cache_refs, wbf_ref = refs[7:-1], refs[-1]
    else:
        xc_ref, xl_ref, mod_ref, w_ref, o_ref, wbf_ref = refs
    i = pl.program_id(0)

    @pl.when(i == 0)
    def _():
        wbf_ref[...] = w_ref[0].astype(BF16)

    m = mod_ref[0, 0]
    u = _pick(i, xc_ref, xl_ref) * (1.0 + m[1:2, :]) + m[0:1, :]
    p = _dot(u.astype(BF16), wbf_ref[...])
    if rope_groups:
        @pl.when(i < CTX_TILES)
        def _():
            o_ref[...] = p.astype(BF16)
            for ref, (start, width, transposed) in zip(cache_refs, cache_cols):
                if transposed:
                    for b in range(TM // SEQ):
                        ref[b] = p[b * SEQ:(b + 1) * SEQ, start:start + width].T
                else:
                    ref[...] = p[:, start:start + width]

        @pl.when(i >= CTX_TILES)
        def _():
            o_ref[...] = _rope_tile(p, cos_ref[...], sin_ref[...], rope_groups).astype(BF16)
    else:
        o_ref[...] = p


def _inproj(x_pair, mod, layer, w_in, j, rope=None, cache_cols=()):
    n = w_in.shape[-1]
    rope_groups = ()
    combined = x_pair[0] is x_pair[1]
    in_specs = _pair_specs(D_MODEL, CTX_TILES if combined else 0) + [
        pl.BlockSpec((1, 1, 6, D_MODEL), lambda i: (layer, _seg_of_tile(i), 0, 0)),
        pl.BlockSpec((1, D_MODEL, n), lambda i: (j, 0, 0)),
    ]
    args = [x_pair[0], x_pair[1], mod, w_in]
    out_specs = pl.BlockSpec((TM, n), lambda i: (i, 0))
    out_shape = jax.ShapeDtypeStruct((N_TOK, n), F32)
    if rope is not None:
        cos, sin, rope_groups = rope
        tab = pl.BlockSpec((TM, LANE), lambda i: (jnp.maximum(i - CTX_TILES, 0) % LAT_TILES_PER_SEQ, 0))
        in_specs += [tab, tab]
        args += [cos, sin]
        seqs = TM // SEQ
        out_specs = [out_specs] + [
            pl.BlockSpec((seqs, width, SEQ), lambda i: (jnp.minimum(i, CTX_TILES - 1), 0, 0)) if transposed
            else _pair_specs(width)[0] for _, width, transposed in cache_cols]
        out_shape = [jax.ShapeDtypeStruct((N_TOK, n), BF16)] + [
            jax.ShapeDtypeStruct((BATCH, width, SEQ) if transposed else (N_CTX, width), F32)
            for _, width, transposed in cache_cols]
    return pl.pallas_call(
        functools.partial(_inproj_kernel, rope_groups=rope_groups, cache_cols=cache_cols),
        grid=(N_TILES,),
        in_specs=in_specs,
        out_specs=out_specs,
        out_shape=out_shape,
        scratch_shapes=[pltpu.VMEM((D_MODEL, n), BF16)],
        compiler_params=_cparams(("arbitrary",)),
        name="inproj_rope" if rope is not None else "inproj",
    )(*args)


def _rope_tables():
    s = np.arange(DEC_SEQ)
    row, col = (s // GRID_W).astype(np.float64), (s % GRID_W).astype(np.float64)
    nf = HEAD_DIM // 4
    inv = np.power(ROPE_BASE, -np.arange(nf, dtype=np.float64) / nf)
    ang = np.concatenate([row[:, None] * inv, row[:, None] * inv, col[:, None] * inv, col[:, None] * inv], axis=1)
    sign = np.concatenate([-np.ones(nf), np.ones(nf), -np.ones(nf), np.ones(nf)])
    cos = np.tile(np.cos(ang), (1, LANE // HEAD_DIM))
    sin = np.tile(np.sin(ang) * sign, (1, LANE // HEAD_DIM))
    return jnp.asarray(cos, F32), jnp.asarray(sin, F32)


RG_T = 256
RG_TILES = D_RG // MXU_DIM


def _sigmoid(x):
    return 0.5 * jnp.tanh(0.5 * x) + 0.5


def _softplus(x):
    u = jnp.exp(-jnp.abs(x))
    w = 1.0 + u
    l1p = jnp.where(w == 1.0, u, jnp.log(w) * (u / jnp.where(w == 1.0, 1.0, w - 1.0)))
    return jnp.maximum(x, 0.0) + l1p


def _rglru_kernel(xa_ref, ga_ref, h0_ref, cw_ref, cb_ref, wg_ref, br_ref, bi_ref, lam_ref,
                  y_ref, st_ref, hf_s, xc_s, a_s, b_s, hs_s, *, seq_len):
    nchunk = seq_len // RG_T
    cw = cw_ref[0]
    cb = cb_ref[0]

    def conv_chunk(c):
        base = pl.multiple_of(c * RG_T, RG_T)
        cur = xa_ref[pl.ds(base, RG_T), :]
        prev = xa_ref[pl.ds(pl.multiple_of(jnp.maximum(base - SUBLANE, 0), SUBLANE), SUBLANE), :]
        prev = jnp.where(c > 0, prev, 0.0)
        nxt = xa_ref[pl.ds(pl.multiple_of(jnp.minimum(base + RG_T, seq_len - SUBLANE), SUBLANE), SUBLANE), :]
        nxt = jnp.where(c < nchunk - 1, nxt, 0.0)
        win = jnp.concatenate([prev, cur, nxt], axis=0)
        xc = cb
        for k in range(CONV_W):
            off = SUBLANE - 1 + k
            xc = xc + win[off:off + RG_T] * cw[k:k + 1, :]
        return xc

    def gates(xc, z):
        xb = xc.astype(BF16)
        rs, gs = [], []
        for t in range(RG_TILES):
            zz = _dot(xb[:, t * MXU_DIM:(t + 1) * MXU_DIM], wg_ref[0, z, t].astype(BF16))
            rs.append(zz[:, :MXU_DIM])
            gs.append(zz[:, MXU_DIM:])
        r = _sigmoid(jnp.concatenate(rs, axis=1) + br_ref[0, z:z + 1, :])
        g = _sigmoid(jnp.concatenate(gs, axis=1) + bi_ref[0, z:z + 1, :])
        log_a = (-RG_C * _softplus(-lam_ref[0, z:z + 1, :])) * r
        a = jnp.exp(log_a)
        y = -jnp.tanh(log_a) * (a * a + 1.0)
        bt = jnp.where(y > 0.0, y * lax.rsqrt(y), 0.0) * g * xc
        return a, bt

    def scan_chunk(h, reverse):
        def step(t, h):
            tt = RG_T - 1 - t if reverse else t
            h = a_s[pl.ds(tt, 1), :] * h + b_s[pl.ds(tt, 1), :]
            hs_s[pl.ds(tt, 1), :] = h
            return h
        return lax.fori_loop(0, RG_T, step, h, unroll=8)

    def fwd_chunk(c, h):
        rows = pl.ds(pl.multiple_of(c * RG_T, RG_T), RG_T)
        xc = conv_chunk(c)
        xc_s[rows, :] = xc
        a, bt = gates(xc, 0)
        a_s[...] = a
        b_s[...] = bt
        h = scan_chunk(h, False)
        hf_s[rows, :] = hs_s[...]
        return h

    def bwd_chunk(k, h):
        c = nchunk - 1 - k
        rows = pl.ds(pl.multiple_of(c * RG_T, RG_T), RG_T)
        a, bt = gates(xc_s[rows, :], 1)
        a_s[...] = a
        b_s[...] = bt
        h = scan_chunk(h, True)
        y_ref[rows, :] = ((hf_s[rows, :] + hs_s[...]) * jax.nn.gelu(ga_ref[rows, :])).astype(BF16)
        return h

    hf = lax.fori_loop(0, nchunk, fwd_chunk, h0_ref[0, 0:1, :])
    hb = lax.fori_loop(0, nchunk, bwd_chunk, h0_ref[0, 1:2, :])
    st_ref[0, 0:1, :] = hf
    st_ref[0, 1:2, :] = hb


def _rglru(proj, h0, j, cw, cb, wg, br, bi, lam, *, seq_len, nseq, row_block0):
    wspec3 = lambda shape: pl.BlockSpec((1,) + shape, lambda b: (j,) + (0,) * len(shape))
    return pl.pallas_call(
        functools.partial(_rglru_kernel, seq_len=seq_len),
        grid=(nseq,),
        in_specs=[
            pl.BlockSpec((seq_len, D_RG), lambda b: (row_block0 + b, 0)),
            pl.BlockSpec((seq_len, D_RG), lambda b: (row_block0 + b, 1)),
            pl.BlockSpec((1, 2, D_RG), lambda b: (b, 0, 0)),
            wspec3((CONV_W, D_RG)),
            wspec3((1, D_RG)),
            wspec3((2, RG_TILES, MXU_DIM, 2 * MXU_DIM)),
            wspec3((2, D_RG)),
            wspec3((2, D_RG)),
            wspec3((2, D_RG)),
        ],
        out_specs=[
            pl.BlockSpec((seq_len, D_RG), lambda b: (b, 0)),
            pl.BlockSpec((1, 2, D_RG), lambda b: (b, 0, 0)),
        ],
        out_shape=[
            jax.ShapeDtypeStruct((nseq * seq_len, D_RG), BF16),
            jax.ShapeDtypeStruct((nseq, 2, D_RG), F32),
        ],
        scratch_shapes=[
            pltpu.VMEM((seq_len, D_RG), F32),
            pltpu.VMEM((seq_len, D_RG), F32),
            pltpu.VMEM((RG_T, D_RG), F32),
            pltpu.VMEM((RG_T, D_RG), F32),
            pltpu.VMEM((RG_T, D_RG), F32),
        ],
        compiler_params=_cparams(("arbitrary",)),
        name=f"rglru_s{seq_len}",
    )(proj, proj, h0, cw, cb, wg, br, bi, lam)


def _gate_tiles(w_r, w_i):
    per_tile = MXU_DIM // RG_BLOCK
    eye = jnp.eye(per_tile, dtype=F32)

    def tiles(w):
        w = w.reshape(w.shape[0], 2, RG_TILES, per_tile, RG_BLOCK, RG_BLOCK)
        return jnp.einsum('nztbcd,be->nztbced', w, eye).reshape(w.shape[0], 2, RG_TILES, MXU_DIM, MXU_DIM)

    return jnp.concatenate([tiles(w_r), tiles(w_i)], axis=-1)


def _fnet_kernel(z_ref, cc_ref, sc_ref, w_ref, cs_ref, ns_ref, y_ref, p_s, q_s):
    @pl.when(pl.program_id(1) == 0)
    def _():
        w = w_ref[0]
        a = _dot3(cc_ref[...], w).astype(BF16)
        b = _dot3(sc_ref[...], w).astype(BF16)
        zb = z_ref[...].astype(BF16)
        p_s[...] = _dot(zb, a).astype(BF16)
        q_s[...] = _dot(zb, b).astype(BF16)

    y_ref[...] = (_dot(cs_ref[...].astype(BF16), p_s[...])
                  + _dot(ns_ref[...].astype(BF16), q_s[...])).astype(BF16)


def _fnet(proj, j, wbd, *, seq_len, nseq, row_block0):
    tr = min(seq_len, 512)
    nrt = seq_len // tr
    cc, sc = _dft_tables(FNET_GROUP_DIM)
    ccbd = jnp.asarray(np.kron(np.eye(FNET_GROUPS), cc), F32)
    scbd = jnp.asarray(np.kron(np.eye(FNET_GROUPS), sc), F32)
    cs, ss = _dft_tables(seq_len)
    cs_b, ns_b = jnp.asarray(cs, F32), jnp.asarray(-ss, F32)
    zcol = (2 * D_RG) // D_FNET
    const = lambda shape: pl.BlockSpec(shape, lambda b, r: (0,) * len(shape))
    return pl.pallas_call(
        _fnet_kernel,
        grid=(nseq, nrt),
        in_specs=[
            pl.BlockSpec((seq_len, D_FNET), lambda b, r: (row_block0 + b, zcol)),
            const((D_FNET, D_FNET)),
            const((D_FNET, D_FNET)),
            pl.BlockSpec((1, D_FNET, D_FNET), lambda b, r: (j, 0, 0)),
            pl.BlockSpec((tr, seq_len), lambda b, r: (r, 0)),
            pl.BlockSpec((tr, seq_len), lambda b, r: (r, 0)),
        ],
        out_specs=pl.BlockSpec((tr, D_FNET), lambda b, r: (b * nrt + r, 0)),
        out_shape=jax.ShapeDtypeStruct((nseq * seq_len, D_FNET), BF16),
        scratch_shapes=[pltpu.VMEM((seq_len, D_FNET), BF16), pltpu.VMEM((seq_len, D_FNET), BF16)],
        compiler_params=_cparams(("arbitrary", "arbitrary")),
        name=f"fnet_s{seq_len}",
    )(proj, ccbd, scbd, wbd, cs_b, ns_b)


def _dft_tables(n):
    k = np.arange(n)
    ang = 2.0 * np.pi * ((k[:, None] * k[None, :]) % n) / n
    return np.cos(ang) / np.sqrt(n), np.sin(ang) / np.sqrt(n)


def _fnet_blockdiag(w_f):
    eye = jnp.eye(FNET_GROUPS, dtype=F32)
    return jnp.einsum('ngcd,gh->ngchd', w_f, eye).reshape(w_f.shape[0], D_FNET, D_FNET)


def _outproj_kernel(ac_ref, al_ref, bc_ref, bl_ref, w_ref, xc_ref, xl_ref, mod_ref, g_ref, be_ref, wr_ref,
                    x1_ref, u2_ref, aff_ref, wbf_ref, *, ka):
    i = pl.program_id(0)

    @pl.when(i == 0)
    def _():
        wbf_ref[...] = w_ref[0].astype(BF16)

    out = _dot(_pick(i, ac_ref, al_ref), wbf_ref[0:ka, :]) + _dot(_pick(i, bc_ref, bl_ref), wbf_ref[ka:, :])
    m = mod_ref[0, 0]
    x1 = _layernorm(ALPHA * _pick(i, xc_ref, xl_ref) + m[2:3, :] * out, g_ref[0, 0:1, :], be_ref[0, 0:1, :])
    x1_ref[...] = x1
    u2 = x1 * (1.0 + m[4:5, :]) + m[3:4, :]
    u2_ref[...] = u2.astype(BF16)
    uh, ul = _split(u2)
    wh, wl = _split(wr_ref[0])
    lg = _dot(jnp.concatenate([uh, ul], axis=0), jnp.concatenate([wh, wl], axis=1))
    logits = (lg[:TM, :N_EXPERTS] + lg[:TM, N_EXPERTS:]) + (lg[TM:, :N_EXPERTS] + lg[TM:, N_EXPERTS:])
    mx = jnp.max(logits, axis=-1, keepdims=True)
    ex = jnp.exp(logits - mx)
    aff_ref[...] = ex / jnp.sum(ex, axis=-1, keepdims=True)


def _outproj(a_pair, b_pair, w_out, j, x_pair, mod, layer, ln_g, ln_b, w_router):
    ka, kb = a_pair[0].shape[1], b_pair[0].shape[1]
    combined = x_pair[0] is x_pair[1]
    return pl.pallas_call(
        functools.partial(_outproj_kernel, ka=ka),
        grid=(N_TILES,),
        in_specs=_pair_specs(ka) + _pair_specs(kb) + [
            pl.BlockSpec((1, D_MODEL, D_MODEL), lambda i: (j, 0, 0)),
        ] + _pair_specs(D_MODEL, CTX_TILES if combined else 0) + [
            pl.BlockSpec((1, 1, 6, D_MODEL), lambda i: (layer, _seg_of_tile(i), 0, 0)),
            pl.BlockSpec((1, 2, D_MODEL), lambda i: (layer, 0, 0)),
            pl.BlockSpec((1, 2, D_MODEL), lambda i: (layer, 0, 0)),
            pl.BlockSpec((1, D_MODEL, N_EXPERTS), lambda i: (layer, 0, 0)),
        ],
        out_specs=[
            pl.BlockSpec((TM, D_MODEL), lambda i: (i, 0)),
            pl.BlockSpec((TM, D_MODEL), lambda i: (i, 0)),
            pl.BlockSpec((TM, N_EXPERTS), lambda i: (i, 0)),
        ],
        out_shape=[
            jax.ShapeDtypeStruct((N_TOK, D_MODEL), F32),
            jax.ShapeDtypeStruct((N_TOK, D_MODEL), BF16),
            jax.ShapeDtypeStruct((N_TOK, N_EXPERTS), F32),
        ],
        scratch_shapes=[pltpu.VMEM((D_MODEL, D_MODEL), BF16)],
        compiler_params=_cparams(("arbitrary",)),
        name="outproj_ln_router",
    )(*a_pair, *b_pair, w_out, *x_pair, mod, ln_g, ln_b, w_router)


def _topk_kernel(aff_ref, pos_ref, cnt_ref):
    aff = aff_ref[...]
    cap = jnp.float32(CAP)

    def count_ge(t):
        return jnp.sum(jnp.where(aff >= t, 1.0, 0.0), axis=1, keepdims=True)

    lo_v = jnp.zeros((N_EXPERTS, 1), F32)
    hi_v = jnp.full((N_EXPERTS, 1), 2.0, F32)
    lo_t = jnp.full((N_EXPERTS, 1), TOPK_MIN_EXP, F32)
    hi_t = jnp.full((N_EXPERTS, 1), 1.0, F32)
    for _ in range(TOPK_GEO_STEPS):
        mid_t = 0.5 * (lo_t + hi_t)
        cand = jnp.exp2(mid_t)
        ok = count_ge(cand) >= cap
        lo_t = jnp.where(ok, mid_t, lo_t)
        hi_t = jnp.where(ok, hi_t, mid_t)
        lo_v = jnp.where(ok, jnp.maximum(lo_v, cand), lo_v)
        hi_v = jnp.where(ok, hi_v, jnp.minimum(hi_v, cand))
    for _ in range(TOPK_LIN_STEPS):
        cand = lo_v + 0.5 * (hi_v - lo_v)
        ok = count_ge(cand) >= cap
        lo_v = jnp.where(ok, cand, lo_v)
        hi_v = jnp.where(ok, hi_v, cand)
    gt = aff >= hi_v
    eq = (aff >= lo_v) & (aff < hi_v)
    need = cap - jnp.sum(jnp.where(gt, 1.0, 0.0), axis=1, keepdims=True)

    tri = (lax.broadcasted_iota(I32, (LANE, LANE), 0) <= lax.broadcasted_iota(I32, (LANE, LANE), 1))
    tri = jnp.where(tri, 1.0, 0.0).astype(BF16)

    def cumsum_blocks(mask_f32):
        carry = jnp.zeros((N_EXPERTS, 1), F32)
        outs = []
        for k in range(mask_f32.shape[1] // LANE):
            blk = mask_f32[:, k * LANE:(k + 1) * LANE]
            inc = _dot(blk.astype(BF16), tri) + carry
            outs.append(inc)
            carry = carry + jnp.sum(blk, axis=1, keepdims=True)
        return outs

    eq_f = jnp.where(eq, 1.0, 0.0)
    gt_f = jnp.where(gt, 1.0, 0.0)
    eq_rank = cumsum_blocks(eq_f)
    sel_blocks = []
    for k, rk in enumerate(eq_rank):
        lanes = slice(k * LANE, (k + 1) * LANE)
        sel_blocks.append(jnp.where((eq_f[:, lanes] > 0.0) & (rk <= need), 1.0, gt_f[:, lanes]))
    sel = jnp.concatenate(sel_blocks, axis=1)
    sel_rank = cumsum_blocks(sel)
    for k, rk in enumerate(sel_rank):
        lanes = slice(k * LANE, (k + 1) * LANE)
        pos_ref[:, lanes] = jnp.where(sel[:, lanes] > 0.0, rk.astype(I32) - 1, -1)
    lane = lax.broadcasted_iota(I32, (N_EXPERTS, LANE), 1)
    cnt = jnp.zeros((N_EXPERTS, LANE), F32)
    for c in range(1, N_CHUNKS + 1):
        cnt = jnp.where(lane == c, sel_rank[c * (GC // LANE) - 1][:, LANE - 1:LANE], cnt)
    cnt_ref[0] = cnt.astype(I32)


def _topk(aff_t):
    return pl.pallas_call(
        _topk_kernel,
        grid=(2,),
        in_specs=[pl.BlockSpec((N_EXPERTS, N_CTX), lambda g: (0, g))],
        out_specs=[
            pl.BlockSpec((N_EXPERTS, N_CTX), lambda g: (0, g)),
            pl.BlockSpec((1, N_EXPERTS, LANE), lambda g: (g, 0, 0)),
        ],
        out_shape=[
            jax.ShapeDtypeStruct((N_EXPERTS, N_TOK), I32),
            jax.ShapeDtypeStruct((2, N_EXPERTS, LANE), I32),
        ],
        compiler_params=_cparams(("arbitrary",)),
        name="topk_select",
    )(aff_t)


def _chunk_windows(cnt_ref, g, e, c):
    c0 = cnt_ref[g, e, c]
    c1 = cnt_ref[g, e, c + 1]
    base = (c0 // WIN_ALIGN) * WIN_ALIGN
    k = jnp.where(c1 > c0, (c1 - base + WIN - 1) // WIN, 0)
    return base, k


def _gather_kernel(cnt_ref, pos_ref, aff_ref, u_ref, xe_ref, gs_ref):
    g = pl.program_id(0)
    c = pl.program_id(1)

    @pl.when(c == 0)
    def _():
        xe_ref[...] = jnp.zeros_like(xe_ref)
        gs_ref[...] = jnp.zeros_like(gs_ref)

    pos = pos_ref[...]
    aff = aff_ref[...]
    slot0 = lax.broadcasted_iota(I32, (WIN, GC), 0)

    def window(e, lo):
        start = pl.multiple_of(jnp.minimum(lo, CAP - WIN), WIN_ALIGN)
        pe = pos[e:e + 1, :]
        hit = (slot0 + start) == jnp.where(pe >= lo, pe, -1)
        gates = jnp.sum(jnp.where(hit, aff[e:e + 1, :], 0.0), axis=1, keepdims=True)
        return start, jnp.where(hit, 1.0, 0.0).astype(BF16), gates

    def add_rows(e, start, rows, gates):
        dst = (e, 0, pl.ds(start, WIN), slice(None))
        xe_ref[dst] = (xe_ref[dst].astype(F32) + rows).astype(BF16)
        gs_ref[dst] += gates

    windows = [_chunk_windows(cnt_ref, g, e, c) for e in range(N_EXPERTS)]
    firsts = [window(e, base) for e, (base, _) in enumerate(windows)]
    rows = _dot(jnp.concatenate([oh for _, oh, _ in firsts], axis=0), u_ref[...])
    for e, (start, _, gates) in enumerate(firsts):
        add_rows(e, start, rows[e * WIN:(e + 1) * WIN], gates)
    for e, (base, k) in enumerate(windows):
        def extra(w, carry, e=e, base=base):
            start, oh, gates = window(e, base + w * WIN)
            add_rows(e, start, _dot(oh, u_ref[...]), gates)
            return carry
        lax.fori_loop(1, k, extra, 0)


def _gather(cnt, pos_t, aff_t, u2):
    return pl.pallas_call(
        _gather_kernel,
        grid_spec=pltpu.PrefetchScalarGridSpec(
            num_scalar_prefetch=1,
            grid=(2, N_CHUNKS),
            in_specs=[
                pl.BlockSpec((N_EXPERTS, GC), lambda g, c, cnt: (0, g * N_CHUNKS + c)),
                pl.BlockSpec((N_EXPERTS, GC), lambda g, c, cnt: (0, g * N_CHUNKS + c)),
                pl.BlockSpec((GC, D_MODEL), lambda g, c, cnt: (g * N_CHUNKS + c, 0)),
            ],
            out_specs=[
                pl.BlockSpec((N_EXPERTS, 1, CAP, D_MODEL), lambda g, c, cnt: (0, g, 0, 0)),
                pl.BlockSpec((N_EXPERTS, 1, CAP, 1), lambda g, c, cnt: (0, g, 0, 0)),
            ],
        ),
        out_shape=[
            jax.ShapeDtypeStruct((N_EXPERTS, 2, CAP, D_MODEL), BF16),
            jax.ShapeDtypeStruct((N_EXPERTS, 2, CAP, 1), F32),
        ],
        compiler_params=_cparams(("arbitrary", "arbitrary")),
        name="moe_gather",
    )(cnt, pos_t, aff_t, u2)


TF = 1024


def _ffn_kernel(xe_ref, gs_ref, wg_ref, wu_ref, wd_ref, ye_ref, acc_ref):
    f = pl.program_id(1)
    x = xe_ref[0]
    hg = _dot(x, wg_ref[0, 0].astype(BF16))
    hu = _dot(x, wu_ref[0, 0].astype(BF16))
    h = (hg * _sigmoid(hg) * hu).astype(BF16)

    @pl.when(f == 0)
    def _():
        acc_ref[...] = jnp.zeros_like(acc_ref)

    acc_ref[...] += _dot(h, wd_ref[0, 0].astype(BF16))

    @pl.when(f == pl.num_programs(1) - 1)
    def _():
        ye_ref[0] = (acc_ref[...] * gs_ref[0]).astype(BF16)


def _ffn(xe, gs, layer, w_gate, w_up, w_down):
    rows = 2 * CAP
    return pl.pallas_call(
        _ffn_kernel,
        grid=(N_EXPERTS, D_EXPERT // TF),
        in_specs=[
            pl.BlockSpec((1, rows, D_MODEL), lambda e, f: (e, 0, 0)),
            pl.BlockSpec((1, rows, 1), lambda e, f: (e, 0, 0)),
            pl.BlockSpec((1, 1, D_MODEL, TF), lambda e, f: (layer, e, 0, f)),
            pl.BlockSpec((1, 1, D_MODEL, TF), lambda e, f: (layer, e, 0, f)),
            pl.BlockSpec((1, 1, TF, D_MODEL), lambda e, f: (layer, e, f, 0)),
        ],
        out_specs=pl.BlockSpec((1, rows, D_MODEL), lambda e, f: (e, 0, 0)),
        out_shape=jax.ShapeDtypeStruct((N_EXPERTS, rows, D_MODEL), BF16),
        scratch_shapes=[pltpu.VMEM((rows, D_MODEL), F32)],
        compiler_params=_cparams(("arbitrary", "arbitrary")),
        name="moe_ffn",
    )(xe, gs, w_gate, w_up, w_down)


def _combine_kernel(*refs, split_out):
    if split_out:
        cnt_ref, posn_ref, ye_ref, x1_ref, mod_ref, g_ref, be_ref, oc_ref, ol_ref, acc_s = refs
    else:
        cnt_ref, posn_ref, ye_ref, x1_ref, mod_ref, g_ref, be_ref, o_ref, acc_s = refs
    i = pl.program_id(0)
    g = i // N_CHUNKS
    c = i % N_CHUNKS
    posn = posn_ref[...]
    lane = lax.broadcasted_iota(I32, (1, MXU_DIM), 1)
    part = lane // WIN
    slot_in_win = lane % WIN

    def by_part(vals):
        out = vals[-1]
        for q in range(len(vals) - 2, -1, -1):
            out = jnp.where(part == q, vals[q], out)
        return out

    def start_of(lo):
        return pl.multiple_of(jnp.minimum(lo, CAP - WIN), WIN_ALIGN)

    def group_product(experts, los):
        starts = [start_of(lo) for lo in los]
        slots = by_part(starts) + slot_in_win
        cols = by_part([jnp.where(posn[:, e:e + 1] >= lo, posn[:, e:e + 1], -1) for e, lo in zip(experts, los)])
        onehot = jnp.where(slots == cols, 1.0, 0.0).astype(BF16)
        rows = jnp.concatenate([ye_ref[e, 0, pl.ds(st, WIN), :] for e, st in zip(experts, starts)], axis=0)
        return _dot(onehot, rows)

    windows = [_chunk_windows(cnt_ref, g, e, c) for e in range(N_EXPERTS)]
    acc = None
    for e0 in range(0, N_EXPERTS, WINS_PER_DOT):
        experts = list(range(e0, e0 + WINS_PER_DOT))
        d = group_product(experts, [windows[e][0] for e in experts])
        acc = d if acc is None else acc + d
    acc_s[...] = acc
    slot0 = lax.broadcasted_iota(I32, (GC, WIN), 1)
    for e, (base, k) in enumerate(windows):
        def extra(w, carry, e=e, base=base):
            lo = base + w * WIN
            start = start_of(lo)
            col = posn[:, e:e + 1]
            hit = (slot0 + start) == jnp.where(col >= lo, col, -1)
            acc_s[...] += _dot(jnp.where(hit, 1.0, 0.0).astype(BF16), ye_ref[e, 0, pl.ds(start, WIN), :])
            return carry
        lax.fori_loop(1, k, extra, 0)
    m = mod_ref[0, 0]
    y = ALPHA * x1_ref[...] + m[5:6, :] * acc_s[...]
    res = _layernorm(y, g_ref[0, 1:2, :], be_ref[0, 1:2, :])
    if split_out:
        @pl.when(i < N_CHUNKS)
        def _():
            oc_ref[...] = res

        @pl.when(i >= N_CHUNKS)
        def _():
            ol_ref[...] = res
    else:
        o_ref[...] = res


def _combine(cnt, pos_n, ye, x1, mod, layer, ln_g, ln_b, split_out):
    if split_out:
        out_specs = _pair_specs(D_MODEL, tile=GC)
        out_shape = [jax.ShapeDtypeStruct((N_CTX, D_MODEL), F32), jax.ShapeDtypeStruct((N_LAT, D_MODEL), F32)]
    else:
        out_specs = pl.BlockSpec((GC, D_MODEL), lambda i, cnt: (i, 0))
        out_shape = jax.ShapeDtypeStruct((N_TOK, D_MODEL), F32)
    return pl.pallas_call(
        functools.partial(_combine_kernel, split_out=split_out),
        grid_spec=pltpu.PrefetchScalarGridSpec(
            num_scalar_prefetch=1,
            grid=(N_TOK // GC,),
            in_specs=[
                pl.BlockSpec((GC, N_EXPERTS), lambda i, cnt: (i, 0)),
                pl.BlockSpec((N_EXPERTS, 1, CAP, D_MODEL), lambda i, cnt: (0, i // N_CHUNKS, 0, 0)),
                pl.BlockSpec((GC, D_MODEL), lambda i, cnt: (i, 0)),
                pl.BlockSpec((1, 1, 6, D_MODEL), lambda i, cnt: (layer, _seg_of_tile(i, GC), 0, 0)),
                pl.BlockSpec((1, 2, D_MODEL), lambda i, cnt: (layer, 0, 0)),
                pl.BlockSpec((1, 2, D_MODEL), lambda i, cnt: (layer, 0, 0)),
            ],
            out_specs=out_specs,
            scratch_shapes=[pltpu.VMEM((GC, D_MODEL), F32)],
        ),
        out_shape=out_shape,
        compiler_params=_cparams(("arbitrary",)),
        name="moe_combine_ln",
    )(cnt, pos_n, ye, x1, mod, ln_g, ln_b)


def _moe(x1, u2, aff, mod, layer, ln_g, ln_b, w_gate, w_up, w_down, split_out=False):
    aff_t = aff.T
    pos_t, cnt = _topk(aff_t)
    xe, gs = _gather(cnt, pos_t, aff_t, u2)
    ye = _ffn(xe.reshape(N_EXPERTS, 2 * CAP, D_MODEL), gs.reshape(N_EXPERTS, 2 * CAP, 1),
              layer, w_gate, w_up, w_down)
    return _combine(cnt, pos_t.T, ye.reshape(N_EXPERTS, 2, CAP, D_MODEL), x1, mod, layer, ln_g, ln_b, split_out)


def _sink_softmax_pv(s, sink_col, v):
    m = jnp.maximum(jnp.max(s, axis=-1, keepdims=True), sink_col)
    p = jnp.exp(s - m)
    den = jnp.sum(p, axis=-1, keepdims=True) + jnp.exp(sink_col - m)
    return _dot(p.astype(BF16), v) / den


def _sink_column(sink_ref, h, rows_per_group):
    rid = lax.broadcasted_iota(I32, (GQA_GROUP * rows_per_group, 1), 0) // rows_per_group
    col = jnp.zeros((GQA_GROUP * rows_per_group, 1), F32)
    for g in range(GQA_GROUP):
        col = jnp.where(rid == g, sink_ref[h * GQA_GROUP + g], col)
    return col


def _stack_groups(q, h):
    return jnp.concatenate(
        [q[:, (h * GQA_GROUP + g) * HEAD_DIM:(h * GQA_GROUP + g + 1) * HEAD_DIM] * ATTN_SCALE
         for g in range(GQA_GROUP)], axis=0).astype(BF16)


def _unstack_groups(outs, rows):
    return jnp.concatenate([o[g * rows:(g + 1) * rows] for o in outs for g in range(GQA_GROUP)], axis=1)


def _swa_ctx_kernel(sink_ref, q_ref, k_ref, v_ref, o_ref):
    q, k, v = q_ref[...], k_ref[...], v_ref[...]
    outs = []
    for h in range(SWA_KV_HEADS):
        cols = slice(h * HEAD_DIM, (h + 1) * HEAD_DIM)
        s = _dot_nt(_stack_groups(q, h), k[:, cols].astype(BF16))
        outs.append(_sink_softmax_pv(s, _sink_column(sink_ref, h, SEQ), v[:, cols].astype(BF16)))
    o_ref[...] = _unstack_groups(outs, SEQ).astype(BF16)


def _swa_ctx(proj, sink):
    return pl.pallas_call(
        _swa_ctx_kernel,
        grid=(BATCH,),
        in_specs=[
            pl.BlockSpec(memory_space=pltpu.SMEM),
            pl.BlockSpec((SEQ, 512), lambda b: (b, 0)),
            pl.BlockSpec((SEQ, LANE), lambda b: (b, 4)),
            pl.BlockSpec((SEQ, LANE), lambda b: (b, 5)),
        ],
        out_specs=pl.BlockSpec((SEQ, 512), lambda b: (b, 0)),
        out_shape=jax.ShapeDtypeStruct((N_CTX, 512), BF16),
        compiler_params=_cparams(("arbitrary",)),
        name="swa_context",
    )(sink, proj, proj, proj)


QB = 128


def _swa_lat_kernel(sink_ref, q_ref, kp_ref, kc_ref, kn_ref, vp_ref, vc_ref, vn_ref, ck_ref, cv_ref, o_ref):
    qb = pl.program_id(1)
    q = q_ref[...]
    kall = jnp.concatenate([ck_ref[0].astype(BF16), kp_ref[...], kc_ref[...], kn_ref[...]], axis=0)
    vall = jnp.concatenate([cv_ref[0].astype(BF16), vp_ref[...], vc_ref[...], vn_ref[...]], axis=0)
    nk = PAST_LEN + 3 * QB
    col = lax.broadcasted_iota(I32, (QB, nk), 1)
    qpos = qb * QB + lax.broadcasted_iota(I32, (QB, nk), 0)
    kpos = (qb - 1) * QB + col - PAST_LEN
    ok = (col < PAST_LEN) | ((jnp.abs(qpos - kpos) <= WINDOW) & (kpos >= 0) & (kpos < DEC_SEQ))
    bias = jnp.concatenate([jnp.where(ok, 0.0, NEG_INF)] * GQA_GROUP, axis=0)
    outs = []
    for h in range(SWA_KV_HEADS):
        cols = slice(h * HEAD_DIM, (h + 1) * HEAD_DIM)
        s = _dot_nt(_stack_groups(q, h), kall[:, cols]) + bias
        outs.append(_sink_softmax_pv(s, _sink_column(sink_ref, h, QB), vall[:, cols]))
    o_ref[...] = _unstack_groups(outs, QB).astype(BF16)


def _swa_lat(proj, cache_k, cache_v, sink):
    nqb = DEC_SEQ // QB
    row0 = N_CTX // QB

    def blk(col, shift):
        return pl.BlockSpec((QB, LANE), lambda b, i: (row0 + b * nqb + jnp.clip(i + shift, 0, nqb - 1), col))

    return pl.pallas_call(
        _swa_lat_kernel,
        grid=(DEC_BATCH, nqb),
        in_specs=[
            pl.BlockSpec(memory_space=pltpu.SMEM),
            pl.BlockSpec((QB, 512), lambda b, i: (row0 + b * nqb + i, 0)),
            blk(4, -1), blk(4, 0), blk(4, 1),
            blk(5, -1), blk(5, 0), blk(5, 1),
            pl.BlockSpec((1, PAST_LEN, LANE), lambda b, i: (b, 0, 0)),
            pl.BlockSpec((1, PAST_LEN, LANE), lambda b, i: (b, 0, 0)),
        ],
        out_specs=pl.BlockSpec((QB, 512), lambda b, i: (b * nqb + i, 0)),
        out_shape=jax.ShapeDtypeStruct((N_LAT, 512), BF16),
        compiler_params=_cparams(("arbitrary", "arbitrary")),
        name="swa_latent",
    )(sink, proj, proj, proj, proj, proj, proj, proj, cache_k, cache_v)


def _diff_lambda_col(lam_ref, lam_init):
    l1 = jnp.sum(lam_ref[0, 0] * lam_ref[0, 1], axis=-1, keepdims=True)
    l2 = jnp.sum(lam_ref[0, 2] * lam_ref[0, 3], axis=-1, keepdims=True)
    return jnp.exp(l1) - jnp.exp(l2) + lam_init


def _softmax_pv(q, k, v):
    s = _dot_nt(q, k)
    p = jnp.exp(s - jnp.max(s, axis=-1, keepdims=True))
    return _dot(p.astype(BF16), v) / jnp.sum(p, axis=-1, keepdims=True)


def _diff_head(q, k, v, lam, subln, lam_init):
    o = [_softmax_pv((q[:, m * HEAD_DIM:(m + 1) * HEAD_DIM] * ATTN_SCALE).astype(BF16),
                     k[:, m * HEAD_DIM:(m + 1) * HEAD_DIM], v) for m in range(2)]
    o = o[0] - lam * o[1]
    return o * lax.rsqrt(jnp.mean(o * o, axis=-1, keepdims=True) + LN_EPS) * subln * (1.0 - lam_init)


def _diff_kernel(*refs, lam_init, has_cache, nh):
    if has_cache:
        lam_ref, sg_ref, q_ref, k_ref, v_ref, ck_ref, cv_ref, o_ref = refs
    else:
        lam_ref, sg_ref, q_ref, k_ref, v_ref, o_ref = refs
    hp = pl.program_id(1)
    lam_all = _diff_lambda_col(lam_ref, lam_init)
    hid = lax.broadcasted_iota(I32, (DIFF_HEADS, 1), 0)
    q, k, v = q_ref[...], k_ref[...], v_ref[...]
    if has_cache:
        k = jnp.concatenate([k, ck_ref[0].astype(BF16)], axis=0)
        v = jnp.concatenate([v, cv_ref[0].astype(BF16)], axis=0)
    outs = []
    for j in range(nh):
        lam = jnp.sum(jnp.where(hid == hp * nh + j, lam_all, 0.0), axis=0, keepdims=True)
        cols = slice(j * DIFF_V_DIM, (j + 1) * DIFF_V_DIM)
        outs.append(_diff_head(q[:, cols], k[:, cols], v[:, cols], lam, sg_ref[...], lam_init))
    o_ref[...] = jnp.concatenate(outs, axis=1).astype(BF16)


DIFF_STEP_HEADS = 2
DQ = 512
Q_COL, K_COL, V_COL = 768, 1280, 1792
assert all(col % (DIFF_STEP_HEADS * DIFF_V_DIM) == 0 for col in (Q_COL, K_COL, V_COL))


def _diff_ctx(proj, lam_p, subln, j, lam_init):
    nh = DIFF_STEP_HEADS
    w = nh * DIFF_V_DIM
    return pl.pallas_call(
        functools.partial(_diff_kernel, lam_init=lam_init, has_cache=False, nh=nh),
        grid=(BATCH, DIFF_HEADS // nh),
        in_specs=[
            pl.BlockSpec((1, 4, DIFF_HEADS, HEAD_DIM), lambda b, h: (j, 0, 0, 0)),
            pl.BlockSpec((1, DIFF_V_DIM), lambda b, h: (j, 0)),
            pl.BlockSpec((SEQ, w), lambda b, h: (b, Q_COL // w + h)),
            pl.BlockSpec((SEQ, w), lambda b, h: (b, K_COL // w + h)),
            pl.BlockSpec((SEQ, w), lambda b, h: (b, V_COL // w + h)),
        ],
        out_specs=pl.BlockSpec((SEQ, w), lambda b, h: (b, h)),
        out_shape=jax.ShapeDtypeStruct((N_CTX, 512), BF16),
        compiler_params=_cparams(("arbitrary", "arbitrary")),
        name="diff_context",
    )(lam_p, subln, proj, proj, proj)


def _diff_lat(proj, cache_k, cache_v, lam_p, subln, j, lam_init):
    nh = DIFF_STEP_HEADS
    w = nh * DIFF_V_DIM
    nq = DEC_SEQ // DQ
    row0 = N_CTX // DQ
    seq0 = N_CTX // DEC_SEQ
    return pl.pallas_call(
        functools.partial(_diff_kernel, lam_init=lam_init, has_cache=True, nh=nh),
        grid=(DEC_BATCH, DIFF_HEADS // nh, nq),
        in_specs=[
            pl.BlockSpec((1, 4, DIFF_HEADS, HEAD_DIM), lambda b, h, i: (j, 0, 0, 0)),
            pl.BlockSpec((1, DIFF_V_DIM), lambda b, h, i: (j, 0)),
            pl.BlockSpec((DQ, w), lambda b, h, i: (row0 + b * nq + i, Q_COL // w + h)),
            pl.BlockSpec((DEC_SEQ, w), lambda b, h, i: (seq0 + b, K_COL // w + h)),
            pl.BlockSpec((DEC_SEQ, w), lambda b, h, i: (seq0 + b, V_COL // w + h)),
            pl.BlockSpec((1, PAST_LEN, w), lambda b, h, i: (b, 0, h)),
            pl.BlockSpec((1, PAST_LEN, w), lambda b, h, i: (b, 0, h)),
        ],
        out_specs=pl.BlockSpec((DQ, w), lambda b, h, i: (b * nq + i, h)),
        out_shape=jax.ShapeDtypeStruct((N_LAT, 512), BF16),
        compiler_params=_cparams(("arbitrary", "arbitrary", "arbitrary")),
        name="diff_latent",
    )(lam_p, subln, proj, proj, proj, cache_k, cache_v)


def kernel(x_prompt, x_sample, state_rglru, cache_swa_k, cache_swa_v, cache_diff_k, cache_diff_v, c, c_ctx,
           w_mod, b_mod, ln_g, ln_b, e_w_in, e_conv_w, e_conv_b, e_w_rgate, e_b_rgate, e_w_igate, e_b_igate,
           e_lambda, e_w_fnet, e_w_out, o_w_in, o_sink, o_lambda, o_subln_g, o_w_out,
           w_router, w_gate, w_up, w_down):
    x_pair = (x_prompt.reshape(N_CTX, D_MODEL), x_sample.reshape(N_LAT, D_MODEL))
    cvec8 = jnp.concatenate([c_ctx[None, :], c, jnp.zeros((SUBLANE - N_SEG, D_MODEL), F32)], axis=0).T
    mod = _modulation(cvec8, w_mod, b_mod).reshape(DEPTH, SUBLANE, 6, D_MODEL)

    j = 0
    proj = _inproj(x_pair, mod, 0, e_w_in, j)
    wg = _gate_tiles(e_w_rgate, e_w_igate)
    cb = e_conv_b.reshape(-1, 1, D_RG)
    rg_args = (j, e_conv_w, cb, wg, e_b_rgate, e_b_igate, e_lambda)
    y_rg_c, st_c = _rglru(proj, jnp.zeros((BATCH, 2, D_RG), F32), *rg_args,
                          seq_len=SEQ, nseq=BATCH, row_block0=0)
    y_rg_l, _ = _rglru(proj, state_rglru[:, j], *rg_args,
                       seq_len=DEC_SEQ, nseq=DEC_BATCH, row_block0=N_CTX // DEC_SEQ)
    wbd = _fnet_blockdiag(e_w_fnet)
    y_fn_c = _fnet(proj, j, wbd, seq_len=SEQ, nseq=BATCH, row_block0=0)
    y_fn_l = _fnet(proj, j, wbd, seq_len=DEC_SEQ, nseq=DEC_BATCH, row_block0=N_CTX // DEC_SEQ)
    x1, u2, aff = _outproj((y_rg_c, y_rg_l), (y_fn_c, y_fn_l), e_w_out, j, x_pair, mod, 0, ln_g, ln_b, w_router)
    x = _moe(x1, u2, aff, mod, 0, ln_g, ln_b, w_gate, w_up, w_down)
    new_state_rglru = st_c[:, None]

    layer = 1
    lam_init = 0.8 - 0.6 * math.exp(-0.3 * layer)
    cos, sin = _rope_tables()
    rope_groups = (0, 1, 2, 3, 4, 6, 7, 8, 9, 10, 11, 12, 13)
    cache_cols = ((512, 128, True), (640, 128, True), (K_COL, 512, True), (V_COL, 512, False))
    proj, ks, vs, kd, vd = _inproj((x, x), mod, 1, o_w_in, j, rope=(cos, sin, rope_groups), cache_cols=cache_cols)
    sink = o_sink[j]
    ys_c = _swa_ctx(proj, sink)
    yd_c = _diff_ctx(proj, o_lambda, o_subln_g, j, lam_init)
    ck = cache_swa_k[:, j].reshape(DEC_BATCH, PAST_LEN, SWA_KV_HEADS * HEAD_DIM)
    cv = cache_swa_v[:, j].reshape(DEC_BATCH, PAST_LEN, SWA_KV_HEADS * HEAD_DIM)
    ys_l = _swa_lat(proj, ck, cv, sink)
    cdk = cache_diff_k[:, j].reshape(DEC_BATCH, PAST_LEN, DIFF_HEADS * 2 * HEAD_DIM)
    cdv = cache_diff_v[:, j].reshape(DEC_BATCH, PAST_LEN, DIFF_HEADS * DIFF_V_DIM)
    yd_l = _diff_lat(proj, cdk, cdv, o_lambda, o_subln_g, j, lam_init)
    x1, u2, aff = _outproj((ys_c, ys_l), (yd_c, yd_l), o_w_out, j, (x, x), mod, 1, ln_g, ln_b, w_router)
    y_c, y_l = _moe(x1, u2, aff, mod, 1, ln_g, ln_b, w_gate, w_up, w_down, split_out=True)

    new_cache_swa_k = ks.reshape(BATCH, 1, SWA_KV_HEADS, HEAD_DIM, SEQ).transpose(0, 1, 4, 2, 3)
    new_cache_swa_v = vs.reshape(BATCH, 1, SWA_KV_HEADS, HEAD_DIM, SEQ).transpose(0, 1, 4, 2, 3)
    new_cache_diff_k = kd.reshape(BATCH, 1, DIFF_HEADS, 2, HEAD_DIM, SEQ).transpose(0, 1, 5, 2, 3, 4)
    new_cache_diff_v = vd.reshape(BATCH, 1, SEQ, DIFF_HEADS, DIFF_V_DIM)
    y_prompt = y_c.reshape(BATCH, SEQ, D_MODEL)
    y_sample = y_l.reshape(DEC_BATCH, DEC_SEQ, D_MODEL)
    return (y_prompt, y_sample, new_state_rglru, new_cache_swa_k, new_cache_swa_v,
            new_cache_diff_k, new_cache_diff_v)
```

```python
import functools
import math

import numpy as np
import jax
import jax.numpy as jnp
from jax import lax
from jax.experimental import pallas as pl
from jax.experimental.pallas import tpu as pltpu

F32 = jnp.float32
BF16 = jnp.bfloat16
I32 = jnp.int32

D_MODEL = 1024
BATCH, SEQ = 16, 256
DEC_BATCH, DEC_SEQ = 2, 2048
PAST_LEN = 256
DEPTH = 2
GRID_W = 64
HEAD_DIM = 64
D_RG = 768
RG_BLOCK = 64
RG_C = 8.0
CONV_W = 4
D_FNET = 256
FNET_GROUP_DIM = 64
FNET_GROUPS = 4
SWA_HEADS = 8
SWA_KV_HEADS = 2
GQA_GROUP = 4
WINDOW = 128
DIFF_HEADS = 4
DIFF_V_DIM = 128
N_EXPERTS = 16
EC_FACTOR = 2
D_EXPERT = 2048
ROPE_BASE = 10000.0
LN_EPS = 1e-5
NEG_INF = -1e30
ATTN_SCALE = HEAD_DIM ** -0.5
ALPHA = (2 * DEPTH) ** 0.25
EVEN_IN = 2 * D_RG + D_FNET
ODD_IN = 2304

N_CTX = BATCH * SEQ
N_LAT = DEC_BATCH * DEC_SEQ
N_TOK = N_CTX + N_LAT
N_SEG = 1 + DEC_BATCH
CAP = EC_FACTOR * N_CTX // N_EXPERTS
assert N_CTX == N_LAT

LANE = 128
SUBLANE = 8
MXU_DIM = 256
VMEM_LIMIT = 56 * 1024 * 1024
TM = 512
GC = 256
N_CHUNKS = N_CTX // GC
WIN = 64
WIN_ALIGN = 16
WINS_PER_DOT = MXU_DIM // WIN
TOPK_MIN_EXP = -150.0
TOPK_GEO_STEPS = 16
TOPK_LIN_STEPS = 24
N_TILES = N_TOK // TM
CTX_TILES = N_CTX // TM
LAT_TILES_PER_SEQ = DEC_SEQ // TM


def _cparams(sem):
    return pltpu.CompilerParams(dimension_semantics=sem, vmem_limit_bytes=VMEM_LIMIT)


def _dot(a, b):
    return jnp.dot(a, b, preferred_element_type=F32)


def _dot_nt(a, b):
    return lax.dot_general(a, b, (((1,), (1,)), ((), ())), preferred_element_type=F32)


def _split(a):
    hi = a.astype(BF16)
    lo = (a - hi.astype(F32)).astype(BF16)
    return hi, lo


def _dot3(a, b):
    ah, al = _split(a)
    bh, bl = _split(b)
    return _dot(ah, bh) + (_dot(ah, bl) + _dot(al, bh))


def _layernorm(y, g, b):
    mu = jnp.mean(y, axis=-1, keepdims=True)
    d = y - mu
    var = jnp.mean(d * d, axis=-1, keepdims=True)
    return d * lax.rsqrt(var + LN_EPS) * g + b


def _pair_specs(width, lat_block0=0, tile=TM):
    ctx_tiles = N_CTX // tile
    return [
        pl.BlockSpec((tile, width), lambda i, *_: (jnp.minimum(i, ctx_tiles - 1), 0)),
        pl.BlockSpec((tile, width), lambda i, *_: (jnp.maximum(i - ctx_tiles, 0) + lat_block0, 0)),
    ]


def _pick(i, ctx_ref, lat_ref):
    return jnp.where(i < CTX_TILES, ctx_ref[...], lat_ref[...])


def _seg_of_tile(i, tile=TM):
    ctx_tiles = N_CTX // tile
    return jnp.where(i < ctx_tiles, 0, (i - ctx_tiles) // (DEC_SEQ // tile) + 1)


MOD_TN = 1536


def _mod_kernel(ct_ref, w_ref, b_ref, o_ref):
    c = ct_ref[...]
    s = c * _sigmoid(c)
    w = w_ref[0]
    rows = [jnp.sum(s[:, v:v + 1] * w, axis=0, keepdims=True) for v in range(N_SEG)]
    rows.append(jnp.zeros((SUBLANE - N_SEG, w.shape[1]), F32))
    o_ref[0] = jnp.concatenate(rows, axis=0) + b_ref[0]


def _modulation(cvec8, w_mod, b_mod):
    n = 6 * D_MODEL
    return pl.pallas_call(
        _mod_kernel,
        grid=(DEPTH, n // MOD_TN),
        in_specs=[
            pl.BlockSpec((D_MODEL, SUBLANE), lambda l, j: (0, 0)),
            pl.BlockSpec((1, D_MODEL, MOD_TN), lambda l, j: (l, 0, j)),
            pl.BlockSpec((1, 1, MOD_TN), lambda l, j: (l, 0, j)),
        ],
        out_specs=pl.BlockSpec((1, SUBLANE, MOD_TN), lambda l, j: (l, 0, j)),
        out_shape=jax.ShapeDtypeStruct((DEPTH, SUBLANE, n), F32),
        compiler_params=_cparams(("arbitrary", "arbitrary")),
        name="modulation",
    )(cvec8, w_mod, b_mod.reshape(DEPTH, 1, n))


def _rope_tile(p, cos, sin, rope_groups):
    lane = lax.broadcasted_iota(I32, (1, LANE), 1)
    first_half = (lane % 32) < 16
    pieces = []
    for k in range(p.shape[1] // LANE):
        xg = p[:, k * LANE:(k + 1) * LANE]
        if k in rope_groups:
            partner = jnp.where(first_half, pltpu.roll(xg, LANE - 16, 1), pltpu.roll(xg, 16, 1))
            xg = xg * cos + partner * sin
        pieces.append(xg)
    return jnp.concatenate(pieces, axis=1)


def _inproj_kernel(*refs, rope_groups, cache_cols):
    if rope_groups:
        xc_ref, xl_ref, mod_ref, w_ref, cos_ref, sin_ref, o_ref = refs[:7]
        cache_refs, wbf_ref = refs[7:-1], refs[-1]
    else:
        xc_ref, xl_ref, mod_ref, w_ref, o_ref, gz_ref, wbf_ref = refs
    i = pl.program_id(0)

    @pl.when(i == 0)
    def _():
        wbf_ref[...] = w_ref[0].astype(BF16)

    m = mod_ref[0, 0]
    u = _pick(i, xc_ref, xl_ref) * (1.0 + m[1:2, :]) + m[0:1, :]
    p = _dot(u.astype(BF16), wbf_ref[...])
    if rope_groups:
        @pl.when(i < CTX_TILES)
        def _():
            o_ref[...] = p.astype(BF16)
            for ref, (start, width, transposed) in zip(cache_refs, cache_cols):
                if transposed:
                    for b in range(TM // SEQ):
                        ref[b] = p[b * SEQ:(b + 1) * SEQ, start:start + width].T
                else:
                    ref[...] = p[:, start:start + width]

        @pl.when(i >= CTX_TILES)
        def _():
            o_ref[...] = _rope_tile(p, cos_ref[...], sin_ref[...], rope_groups).astype(BF16)
    else:
        o_ref[...] = p[:, :D_RG]
        gz_ref[...] = p[:, D_RG:].astype(BF16)


def _inproj(x_pair, mod, layer, w_in, j, rope=None, cache_cols=()):
    n = w_in.shape[-1]
    rope_groups = ()
    combined = x_pair[0] is x_pair[1]
    in_specs = _pair_specs(D_MODEL, CTX_TILES if combined else 0) + [
        pl.BlockSpec((1, 1, 6, D_MODEL), lambda i: (layer, _seg_of_tile(i), 0, 0)),
        pl.BlockSpec((1, D_MODEL, n), lambda i: (j, 0, 0)),
    ]
    args = [x_pair[0], x_pair[1], mod, w_in]
    out_specs = pl.BlockSpec((TM, n), lambda i: (i, 0))
    if rope is None:
        out_specs = [pl.BlockSpec((TM, D_RG), lambda i: (i, 0)), pl.BlockSpec((TM, n - D_RG), lambda i: (i, 0))]
        out_shape = [jax.ShapeDtypeStruct((N_TOK, D_RG), F32), jax.ShapeDtypeStruct((N_TOK, n - D_RG), BF16)]
    else:
        cos, sin, rope_groups = rope
        tab = pl.BlockSpec((TM, LANE), lambda i: (jnp.maximum(i - CTX_TILES, 0) % LAT_TILES_PER_SEQ, 0))
        in_specs += [tab, tab]
        args += [cos, sin]
        seqs = TM // SEQ
        out_specs = [out_specs] + [
            pl.BlockSpec((seqs, width, SEQ), lambda i: (jnp.minimum(i, CTX_TILES - 1), 0, 0)) if transposed
            else _pair_specs(width)[0] for _, width, transposed in cache_cols]
        out_shape = [jax.ShapeDtypeStruct((N_TOK, n), BF16)] + [
            jax.ShapeDtypeStruct((BATCH, width, SEQ) if transposed else (N_CTX, width), F32)
            for _, width, transposed in cache_cols]
    return pl.pallas_call(
        functools.partial(_inproj_kernel, rope_groups=rope_groups, cache_cols=cache_cols),
        grid=(N_TILES,),
        in_specs=in_specs,
        out_specs=out_specs,
        out_shape=out_shape,
        scratch_shapes=[pltpu.VMEM((D_MODEL, n), BF16)],
        compiler_params=_cparams(("arbitrary",)),
        name="inproj_rope" if rope is not None else "inproj",
    )(*args)


def _rope_tables():
    s = np.arange(DEC_SEQ)
    row, col = (s // GRID_W).astype(np.float64), (s % GRID_W).astype(np.float64)
    nf = HEAD_DIM // 4
    inv = np.power(ROPE_BASE, -np.arange(nf, dtype=np.float64) / nf)
    ang = np.concatenate([row[:, None] * inv, row[:, None] * inv, col[:, None] * inv, col[:, None] * inv], axis=1)
    sign = np.concatenate([-np.ones(nf), np.ones(nf), -np.ones(nf), np.ones(nf)])
    cos = np.tile(np.cos(ang), (1, LANE // HEAD_DIM))
    sin = np.tile(np.sin(ang) * sign, (1, LANE // HEAD_DIM))
    return jnp.asarray(cos, F32), jnp.asarray(sin, F32)


RG_T = 256
RG_TILES = D_RG // MXU_DIM


def _sigmoid(x):
    return 0.5 * jnp.tanh(0.5 * x) + 0.5


def _softplus(x):
    u = jnp.exp(-jnp.abs(x))
    w = 1.0 + u
    l1p = jnp.where(w == 1.0, u, jnp.log(w) * (u / jnp.where(w == 1.0, 1.0, w - 1.0)))
    return jnp.maximum(x, 0.0) + l1p


def _rglru_kernel(xa_ref, ga_ref, h0_ref, cw_ref, cb_ref, wg_ref, br_ref, bi_ref, lam_ref,
                  y_ref, st_ref, hf_s, xc_s, a_s, b_s, hs_s, *, seq_len):
    nchunk = seq_len // RG_T
    cw = cw_ref[0]
    cb = cb_ref[0]

    def conv_chunk(c):
        base = pl.multiple_of(c * RG_T, RG_T)
        cur = xa_ref[pl.ds(base, RG_T), :]
        prev = xa_ref[pl.ds(pl.multiple_of(jnp.maximum(base - SUBLANE, 0), SUBLANE), SUBLANE), :]
        prev = jnp.where(c > 0, prev, 0.0)
        nxt = xa_ref[pl.ds(pl.multiple_of(jnp.minimum(base + RG_T, seq_len - SUBLANE), SUBLANE), SUBLANE), :]
        nxt = jnp.where(c < nchunk - 1, nxt, 0.0)
        win = jnp.concatenate([prev, cur, nxt], axis=0)
        xc = cb
        for k in range(CONV_W):
            off = SUBLANE - 1 + k
            xc = xc + win[off:off + RG_T] * cw[k:k + 1, :]
        return xc

    def gates(xc, z):
        xb = xc.astype(BF16)
        rs, gs = [], []
        for t in range(RG_TILES):
            zz = _dot(xb[:, t * MXU_DIM:(t + 1) * MXU_DIM], wg_ref[0, z, t].astype(BF16))
            rs.append(zz[:, :MXU_DIM])
            gs.append(zz[:, MXU_DIM:])
        r = _sigmoid(jnp.concatenate(rs, axis=1) + br_ref[0, z:z + 1, :])
        g = _sigmoid(jnp.concatenate(gs, axis=1) + bi_ref[0, z:z + 1, :])
        log_a = (-RG_C * _softplus(-lam_ref[0, z:z + 1, :])) * r
        a = jnp.exp(log_a)
        y = -jnp.tanh(log_a) * (a * a + 1.0)
        bt = jnp.where(y > 0.0, y * lax.rsqrt(y), 0.0) * g * xc
        return a, bt

    def scan_chunk(h, reverse):
        def step(t, h):
            tt = RG_T - 1 - t if reverse else t
            h = a_s[pl.ds(tt, 1), :] * h + b_s[pl.ds(tt, 1), :]
            hs_s[pl.ds(tt, 1), :] = h
            return h
        return lax.fori_loop(0, RG_T, step, h, unroll=8)

    def fwd_chunk(c, h):
        rows = pl.ds(pl.multiple_of(c * RG_T, RG_T), RG_T)
        xc = conv_chunk(c)
        xc_s[rows, :] = xc
        a, bt = gates(xc, 0)
        a_s[...] = a
        b_s[...] = bt
        h = scan_chunk(h, False)
        hf_s[rows, :] = hs_s[...]
        return h

    def bwd_chunk(k, h):
        c = nchunk - 1 - k
        rows = pl.ds(pl.multiple_of(c * RG_T, RG_T), RG_T)
        a, bt = gates(xc_s[rows, :], 1)
        a_s[...] = a
        b_s[...] = bt
        h = scan_chunk(h, True)
        y_ref[rows, :] = ((hf_s[rows, :] + hs_s[...]) * jax.nn.gelu(ga_ref[rows, :].astype(F32))).astype(BF16)
        return h

    hf = lax.fori_loop(0, nchunk, fwd_chunk, h0_ref[0, 0:1, :])
    hb = lax.fori_loop(0, nchunk, bwd_chunk, h0_ref[0, 1:2, :])
    st_ref[0, 0:1, :] = hf
    st_ref[0, 1:2, :] = hb


def _rglru(xa, gz, h0, j, cw, cb, wg, br, bi, lam, *, seq_len, nseq, row_block0):
    wspec3 = lambda shape: pl.BlockSpec((1,) + shape, lambda b: (j,) + (0,) * len(shape))
    return pl.pallas_call(
        functools.partial(_rglru_kernel, seq_len=seq_len),
        grid=(nseq,),
        in_specs=[
            pl.BlockSpec((seq_len, D_RG), lambda b: (row_block0 + b, 0)),
            pl.BlockSpec((seq_len, D_RG), lambda b: (row_block0 + b, 0)),
            pl.BlockSpec((1, 2, D_RG), lambda b: (b, 0, 0)),
            wspec3((CONV_W, D_RG)),
            wspec3((1, D_RG)),
            wspec3((2, RG_TILES, MXU_DIM, 2 * MXU_DIM)),
            wspec3((2, D_RG)),
            wspec3((2, D_RG)),
            wspec3((2, D_RG)),
        ],
        out_specs=[
            pl.BlockSpec((seq_len, D_RG), lambda b: (b, 0)),
            pl.BlockSpec((1, 2, D_RG), lambda b: (b, 0, 0)),
        ],
        out_shape=[
            jax.ShapeDtypeStruct((nseq * seq_len, D_RG), BF16),
            jax.ShapeDtypeStruct((nseq, 2, D_RG), F32),
        ],
        scratch_shapes=[
            pltpu.VMEM((seq_len, D_RG), F32),
            pltpu.VMEM((seq_len, D_RG), F32),
            pltpu.VMEM((RG_T, D_RG), F32),
            pltpu.VMEM((RG_T, D_RG), F32),
            pltpu.VMEM((RG_T, D_RG), F32),
        ],
        compiler_params=_cparams(("arbitrary",)),
        name=f"rglru_s{seq_len}",
    )(xa, gz, h0, cw, cb, wg, br, bi, lam)


def _gate_tiles(w_r, w_i):
    per_tile = MXU_DIM // RG_BLOCK
    eye = jnp.eye(per_tile, dtype=F32)

    def tiles(w):
        w = w.reshape(w.shape[0], 2, RG_TILES, per_tile, RG_BLOCK, RG_BLOCK)
        return jnp.einsum('nztbcd,be->nztbced', w, eye).reshape(w.shape[0], 2, RG_TILES, MXU_DIM, MXU_DIM)

    return jnp.concatenate([tiles(w_r), tiles(w_i)], axis=-1)


def _fnet_kernel(z_ref, cc_ref, sc_ref, w_ref, cs_ref, ns_ref, y_ref, p_s, q_s):
    @pl.when(pl.program_id(1) == 0)
    def _():
        w = w_ref[0]
        a = _dot3(cc_ref[...], w).astype(BF16)
        b = _dot3(sc_ref[...], w).astype(BF16)
        zb = z_ref[...].astype(BF16)
        p_s[...] = _dot(zb, a).astype(BF16)
        q_s[...] = _dot(zb, b).astype(BF16)

    y_ref[...] = (_dot(cs_ref[...].astype(BF16), p_s[...])
                  + _dot(ns_ref[...].astype(BF16), q_s[...])).astype(BF16)


def _fnet(gz, j, wbd, *, seq_len, nseq, row_block0):
    tr = min(seq_len, 512)
    nrt = seq_len // tr
    cc, sc = _dft_tables(FNET_GROUP_DIM)
    ccbd = jnp.asarray(np.kron(np.eye(FNET_GROUPS), cc), F32)
    scbd = jnp.asarray(np.kron(np.eye(FNET_GROUPS), sc), F32)
    cs, ss = _dft_tables(seq_len)
    cs_b, ns_b = jnp.asarray(cs, F32), jnp.asarray(-ss, F32)
    zcol = D_RG // D_FNET
    const = lambda shape: pl.BlockSpec(shape, lambda b, r: (0,) * len(shape))
    return pl.pallas_call(
        _fnet_kernel,
        grid=(nseq, nrt),
        in_specs=[
            pl.BlockSpec((seq_len, D_FNET), lambda b, r: (row_block0 + b, zcol)),
            const((D_FNET, D_FNET)),
            const((D_FNET, D_FNET)),
            pl.BlockSpec((1, D_FNET, D_FNET), lambda b, r: (j, 0, 0)),
            pl.BlockSpec((tr, seq_len), lambda b, r: (r, 0)),
            pl.BlockSpec((tr, seq_len), lambda b, r: (r, 0)),
        ],
        out_specs=pl.BlockSpec((tr, D_FNET), lambda b, r: (b * nrt + r, 0)),
        out_shape=jax.ShapeDtypeStruct((nseq * seq_len, D_FNET), BF16),
        scratch_shapes=[pltpu.VMEM((seq_len, D_FNET), BF16), pltpu.VMEM((seq_len, D_FNET), BF16)],
        compiler_params=_cparams(("arbitrary", "arbitrary")),
        name=f"fnet_s{seq_len}",
    )(gz, ccbd, scbd, wbd, cs_b, ns_b)


def _dft_tables(n):
    k = np.arange(n)
    ang = 2.0 * np.pi * ((k[:, None] * k[None, :]) % n) / n
    return np.cos(ang) / np.sqrt(n), np.sin(ang) / np.sqrt(n)


def _fnet_blockdiag(w_f):
    eye = jnp.eye(FNET_GROUPS, dtype=F32)
    return jnp.einsum('ngcd,gh->ngchd', w_f, eye).reshape(w_f.shape[0], D_FNET, D_FNET)


def _outproj_kernel(ac_ref, al_ref, bc_ref, bl_ref, w_ref, xc_ref, xl_ref, mod_ref, g_ref, be_ref, wr_ref,
                    x1_ref, u2_ref, aff_ref, wbf_ref, *, ka):
    i = pl.program_id(0)

    @pl.when(i == 0)
    def _():
        wbf_ref[...] = w_ref[0].astype(BF16)

    out = _dot(_pick(i, ac_ref, al_ref), wbf_ref[0:ka, :]) + _dot(_pick(i, bc_ref, bl_ref), wbf_ref[ka:, :])
    m = mod_ref[0, 0]
    x1 = _layernorm(ALPHA * _pick(i, xc_ref, xl_ref) + m[2:3, :] * out, g_ref[0, 0:1, :], be_ref[0, 0:1, :])
    x1_ref[...] = x1
    u2 = x1 * (1.0 + m[4:5, :]) + m[3:4, :]
    u2_ref[...] = u2.astype(BF16)
    uh, ul = _split(u2)
    wh, wl = _split(wr_ref[0])
    lg = _dot(jnp.concatenate([uh, ul], axis=0), jnp.concatenate([wh, wl], axis=1))
    logits = (lg[:TM, :N_EXPERTS] + lg[:TM, N_EXPERTS:]) + (lg[TM:, :N_EXPERTS] + lg[TM:, N_EXPERTS:])
    mx = jnp.max(logits, axis=-1, keepdims=True)
    ex = jnp.exp(logits - mx)
    aff = ex / jnp.sum(ex, axis=-1, keepdims=True)
    aff = jnp.concatenate([aff, jnp.zeros((TM, LANE - N_EXPERTS), F32)], axis=1)
    aff_ref[...] = aff.T[:N_EXPERTS, :]


def _outproj(a_pair, b_pair, w_out, j, x_pair, mod, layer, ln_g, ln_b, w_router):
    ka, kb = a_pair[0].shape[1], b_pair[0].shape[1]
    combined = x_pair[0] is x_pair[1]
    return pl.pallas_call(
        functools.partial(_outproj_kernel, ka=ka),
        grid=(N_TILES,),
        in_specs=_pair_specs(ka) + _pair_specs(kb) + [
            pl.BlockSpec((1, D_MODEL, D_MODEL), lambda i: (j, 0, 0)),
        ] + _pair_specs(D_MODEL, CTX_TILES if combined else 0) + [
            pl.BlockSpec((1, 1, 6, D_MODEL), lambda i: (layer, _seg_of_tile(i), 0, 0)),
            pl.BlockSpec((1, 2, D_MODEL), lambda i: (layer, 0, 0)),
            pl.BlockSpec((1, 2, D_MODEL), lambda i: (layer, 0, 0)),
            pl.BlockSpec((1, D_MODEL, N_EXPERTS), lambda i: (layer, 0, 0)),
        ],
        out_specs=[
            pl.BlockSpec((TM, D_MODEL), lambda i: (i, 0)),
            pl.BlockSpec((TM, D_MODEL), lambda i: (i, 0)),
            pl.BlockSpec((N_EXPERTS, TM), lambda i: (0, i)),
        ],
        out_shape=[
            jax.ShapeDtypeStruct((N_TOK, D_MODEL), F32),
            jax.ShapeDtypeStruct((N_TOK, D_MODEL), BF16),
            jax.ShapeDtypeStruct((N_EXPERTS, N_TOK), F32),
        ],
        scratch_shapes=[pltpu.VMEM((D_MODEL, D_MODEL), BF16)],
        compiler_params=_cparams(("arbitrary",)),
        name="outproj_ln_router",
    )(*a_pair, *b_pair, w_out, *x_pair, mod, ln_g, ln_b, w_router)


def _topk_kernel(aff_ref, pos_ref, posn_ref, cnt_ref):
    aff = aff_ref[...]
    cap = jnp.float32(CAP)

    def count_ge(t):
        return jnp.sum(jnp.where(aff >= t, 1.0, 0.0), axis=1, keepdims=True)

    lo_v = jnp.zeros((N_EXPERTS, 1), F32)
    hi_v = jnp.full((N_EXPERTS, 1), 2.0, F32)
    lo_t = jnp.full((N_EXPERTS, 1), TOPK_MIN_EXP, F32)
    hi_t = jnp.full((N_EXPERTS, 1), 1.0, F32)
    for _ in range(TOPK_GEO_STEPS):
        mid_t = 0.5 * (lo_t + hi_t)
        cand = jnp.exp2(mid_t)
        ok = count_ge(cand) >= cap
        lo_t = jnp.where(ok, mid_t, lo_t)
        hi_t = jnp.where(ok, hi_t, mid_t)
        lo_v = jnp.where(ok, jnp.maximum(lo_v, cand), lo_v)
        hi_v = jnp.where(ok, hi_v, jnp.minimum(hi_v, cand))
    for _ in range(TOPK_LIN_STEPS):
        cand = lo_v + 0.5 * (hi_v - lo_v)
        ok = count_ge(cand) >= cap
        lo_v = jnp.where(ok, cand, lo_v)
        hi_v = jnp.where(ok, hi_v, cand)
    gt = aff >= hi_v
    eq = (aff >= lo_v) & (aff < hi_v)
    need = cap - jnp.sum(jnp.where(gt, 1.0, 0.0), axis=1, keepdims=True)

    tri = (lax.broadcasted_iota(I32, (LANE, LANE), 0) <= lax.broadcasted_iota(I32, (LANE, LANE), 1))
    tri = jnp.where(tri, 1.0, 0.0).astype(BF16)

    def cumsum_blocks(mask_f32):
        carry = jnp.zeros((N_EXPERTS, 1), F32)
        outs = []
        for k in range(mask_f32.shape[1] // LANE):
            blk = mask_f32[:, k * LANE:(k + 1) * LANE]
            inc = _dot(blk.astype(BF16), tri) + carry
            outs.append(inc)
            carry = carry + jnp.sum(blk, axis=1, keepdims=True)
        return outs

    eq_f = jnp.where(eq, 1.0, 0.0)
    gt_f = jnp.where(gt, 1.0, 0.0)
    eq_rank = cumsum_blocks(eq_f)
    sel_blocks = []
    for k, rk in enumerate(eq_rank):
        lanes = slice(k * LANE, (k + 1) * LANE)
        sel_blocks.append(jnp.where((eq_f[:, lanes] > 0.0) & (rk <= need), 1.0, gt_f[:, lanes]))
    sel = jnp.concatenate(sel_blocks, axis=1)
    sel_rank = cumsum_blocks(sel)
    pad = jnp.zeros((LANE - N_EXPERTS, LANE), F32)
    for k, rk in enumerate(sel_rank):
        lanes = slice(k * LANE, (k + 1) * LANE)
        pos = jnp.where(sel[:, lanes] > 0.0, rk - 1.0, -1.0)
        pos_ref[:, lanes] = pos.astype(I32)
        posn_ref[lanes, :] = jnp.concatenate([pos, pad], axis=0).T[:, :N_EXPERTS].astype(I32)
    lane = lax.broadcasted_iota(I32, (N_EXPERTS, LANE), 1)
    cnt = jnp.zeros((N_EXPERTS, LANE), F32)
    for c in range(1, N_CHUNKS + 1):
        cnt = jnp.where(lane == c, sel_rank[c * (GC // LANE) - 1][:, LANE - 1:LANE], cnt)
    cnt_ref[0] = cnt.astype(I32)


def _topk(aff_t):
    return pl.pallas_call(
        _topk_kernel,
        grid=(2,),
        in_specs=[pl.BlockSpec((N_EXPERTS, N_CTX), lambda g: (0, g))],
        out_specs=[
            pl.BlockSpec((N_EXPERTS, N_CTX), lambda g: (0, g)),
            pl.BlockSpec((N_CTX, N_EXPERTS), lambda g: (g, 0)),
            pl.BlockSpec((1, N_EXPERTS, LANE), lambda g: (g, 0, 0)),
        ],
        out_shape=[
            jax.ShapeDtypeStruct((N_EXPERTS, N_TOK), I32),
            jax.ShapeDtypeStruct((N_TOK, N_EXPERTS), I32),
            jax.ShapeDtypeStruct((2, N_EXPERTS, LANE), I32),
        ],
        compiler_params=_cparams(("arbitrary",)),
        name="topk_select",
    )(aff_t)


def _chunk_windows(cnt_ref, g, e, c):
    c0 = cnt_ref[g, e, c]
    c1 = cnt_ref[g, e, c + 1]
    base = (c0 // WIN_ALIGN) * WIN_ALIGN
    k = jnp.where(c1 > c0, (c1 - base + WIN - 1) // WIN, 0)
    return base, k


def _gather_kernel(cnt_ref, pos_ref, aff_ref, u_ref, xe_ref, gs_ref):
    g = pl.program_id(0)
    c = pl.program_id(1)

    @pl.when(c == 0)
    def _():
        xe_ref[...] = jnp.zeros_like(xe_ref)
        gs_ref[...] = jnp.zeros_like(gs_ref)

    pos = pos_ref[...]
    aff = aff_ref[...]
    slot0 = lax.broadcasted_iota(I32, (WIN, GC), 0)

    def window(e, lo):
        start = pl.multiple_of(jnp.minimum(lo, CAP - WIN), WIN_ALIGN)
        pe = pos[e:e + 1, :]
        hit = (slot0 + start) == jnp.where(pe >= lo, pe, -1)
        gates = jnp.sum(jnp.where(hit, aff[e:e + 1, :], 0.0), axis=1, keepdims=True)
        return start, jnp.where(hit, 1.0, 0.0).astype(BF16), gates

    def add_rows(e, start, rows, gates):
        dst = (e, 0, pl.ds(start, WIN), slice(None))
        xe_ref[dst] = (xe_ref[dst].astype(F32) + rows).astype(BF16)
        gs_ref[dst] += gates

    windows = [_chunk_windows(cnt_ref, g, e, c) for e in range(N_EXPERTS)]
    firsts = [window(e, base) for e, (base, _) in enumerate(windows)]
    rows = _dot(jnp.concatenate([oh for _, oh, _ in firsts], axis=0), u_ref[...])
    for e, (start, _, gates) in enumerate(firsts):
        add_rows(e, start, rows[e * WIN:(e + 1) * WIN], gates)
    for e, (base, k) in enumerate(windows):
        def extra(w, carry, e=e, base=base):
            start, oh, gates = window(e, base + w * WIN)
            add_rows(e, start, _dot(oh, u_ref[...]), gates)
            return carry
        lax.fori_loop(1, k, extra, 0)


def _gather(cnt, pos_t, aff_t, u2):
    return pl.pallas_call(
        _gather_kernel,
        grid_spec=pltpu.PrefetchScalarGridSpec(
            num_scalar_prefetch=1,
            grid=(2, N_CHUNKS),
            in_specs=[
                pl.BlockSpec((N_EXPERTS, GC), lambda g, c, cnt: (0, g * N_CHUNKS + c)),
                pl.BlockSpec((N_EXPERTS, GC), lambda g, c, cnt: (0, g * N_CHUNKS + c)),
                pl.BlockSpec((GC, D_MODEL), lambda g, c, cnt: (g * N_CHUNKS + c, 0)),
            ],
            out_specs=[
                pl.BlockSpec((N_EXPERTS, 1, CAP, D_MODEL), lambda g, c, cnt: (0, g, 0, 0)),
                pl.BlockSpec((N_EXPERTS, 1, CAP, 1), lambda g, c, cnt: (0, g, 0, 0)),
            ],
        ),
        out_shape=[
            jax.ShapeDtypeStruct((N_EXPERTS, 2, CAP, D_MODEL), BF16),
            jax.ShapeDtypeStruct((N_EXPERTS, 2, CAP, 1), F32),
        ],
        compiler_params=_cparams(("arbitrary", "arbitrary")),
        name="moe_gather",
    )(cnt, pos_t, aff_t, u2)


TF = 1024


def _ffn_kernel(xe_ref, gs_ref, wg_ref, wu_ref, wd_ref, ye_ref, acc_ref):
    f = pl.program_id(1)
    x = xe_ref[0]
    hg = _dot(x, wg_ref[0, 0].astype(BF16))
    hu = _dot(x, wu_ref[0, 0].astype(BF16))
    h = (hg * _sigmoid(hg) * hu).astype(BF16)

    @pl.when(f == 0)
    def _():
        acc_ref[...] = jnp.zeros_like(acc_ref)

    acc_ref[...] += _dot(h, wd_ref[0, 0].astype(BF16))

    @pl.when(f == pl.num_programs(1) - 1)
    def _():
        ye_ref[0] = (acc_ref[...] * gs_ref[0]).astype(BF16)


def _ffn(xe, gs, layer, w_gate, w_up, w_down):
    rows = 2 * CAP
    return pl.pallas_call(
        _ffn_kernel,
        grid=(N_EXPERTS, D_EXPERT // TF),
        in_specs=[
            pl.BlockSpec((1, rows, D_MODEL), lambda e, f: (e, 0, 0)),
            pl.BlockSpec((1, rows, 1), lambda e, f: (e, 0, 0)),
            pl.BlockSpec((1, 1, D_MODEL, TF), lambda e, f: (layer, e, 0, f)),
            pl.BlockSpec((1, 1, D_MODEL, TF), lambda e, f: (layer, e, 0, f)),
            pl.BlockSpec((1, 1, TF, D_MODEL), lambda e, f: (layer, e, f, 0)),
        ],
        out_specs=pl.BlockSpec((1, rows, D_MODEL), lambda e, f: (e, 0, 0)),
        out_shape=jax.ShapeDtypeStruct((N_EXPERTS, rows, D_MODEL), BF16),
        scratch_shapes=[pltpu.VMEM((rows, D_MODEL), F32)],
        compiler_params=_cparams(("arbitrary", "arbitrary")),
        name="moe_ffn",
    )(xe, gs, w_gate, w_up, w_down)


def _combine_kernel(*refs, split_out):
    if split_out:
        cnt_ref, posn_ref, ye_ref, x1_ref, mod_ref, g_ref, be_ref, oc_ref, ol_ref, acc_s = refs
    else:
        cnt_ref, posn_ref, ye_ref, x1_ref, mod_ref, g_ref, be_ref, o_ref, acc_s = refs
    i = pl.program_id(0)
    g = i // N_CHUNKS
    c = i % N_CHUNKS
    posn = posn_ref[...]
    lane = lax.broadcasted_iota(I32, (1, MXU_DIM), 1)
    part = lane // WIN
    slot_in_win = lane % WIN

    def by_part(vals):
        out = vals[-1]
        for q in range(len(vals) - 2, -1, -1):
            out = jnp.where(part == q, vals[q], out)
        return out

    def start_of(lo):
        return pl.multiple_of(jnp.minimum(lo, CAP - WIN), WIN_ALIGN)

    def group_product(experts, los):
        starts = [start_of(lo) for lo in los]
        slots = by_part(starts) + slot_in_win
        cols = by_part([jnp.where(posn[:, e:e + 1] >= lo, posn[:, e:e + 1], -1) for e, lo in zip(experts, los)])
        onehot = jnp.where(slots == cols, 1.0, 0.0).astype(BF16)
        rows = jnp.concatenate([ye_ref[e, 0, pl.ds(st, WIN), :] for e, st in zip(experts, starts)], axis=0)
        return _dot(onehot, rows)

    windows = [_chunk_windows(cnt_ref, g, e, c) for e in range(N_EXPERTS)]
    acc = None
    for e0 in range(0, N_EXPERTS, WINS_PER_DOT):
        experts = list(range(e0, e0 + WINS_PER_DOT))
        d = group_product(experts, [windows[e][0] for e in experts])
        acc = d if acc is None else acc + d
    acc_s[...] = acc
    slot0 = lax.broadcasted_iota(I32, (GC, WIN), 1)
    for e, (base, k) in enumerate(windows):
        def extra(w, carry, e=e, base=base):
            lo = base + w * WIN
            start = start_of(lo)
            col = posn[:, e:e + 1]
            hit = (slot0 + start) == jnp.where(col >= lo, col, -1)
            acc_s[...] += _dot(jnp.where(hit, 1.0, 0.0).astype(BF16), ye_ref[e, 0, pl.ds(start, WIN), :])
            return carry
        lax.fori_loop(1, k, extra, 0)
    m = mod_ref[0, 0]
    y = ALPHA * x1_ref[...] + m[5:6, :] * acc_s[...]
    res = _layernorm(y, g_ref[0, 1:2, :], be_ref[0, 1:2, :])
    if split_out:
        @pl.when(i < N_CHUNKS)
        def _():
            oc_ref[...] = res

        @pl.when(i >= N_CHUNKS)
        def _():
            ol_ref[...] = res
    else:
        o_ref[...] = res


def _combine(cnt, pos_n, ye, x1, mod, layer, ln_g, ln_b, split_out):
    if split_out:
        out_specs = _pair_specs(D_MODEL, tile=GC)
        out_shape = [jax.ShapeDtypeStruct((N_CTX, D_MODEL), F32), jax.ShapeDtypeStruct((N_LAT, D_MODEL), F32)]
    else:
        out_specs = pl.BlockSpec((GC, D_MODEL), lambda i, cnt: (i, 0))
        out_shape = jax.ShapeDtypeStruct((N_TOK, D_MODEL), F32)
    return pl.pallas_call(
        functools.partial(_combine_kernel, split_out=split_out),
        grid_spec=pltpu.PrefetchScalarGridSpec(
            num_scalar_prefetch=1,
            grid=(N_TOK // GC,),
            in_specs=[
                pl.BlockSpec((GC, N_EXPERTS), lambda i, cnt: (i, 0)),
                pl.BlockSpec((N_EXPERTS, 1, CAP, D_MODEL), lambda i, cnt: (0, i // N_CHUNKS, 0, 0)),
                pl.BlockSpec((GC, D_MODEL), lambda i, cnt: (i, 0)),
                pl.BlockSpec((1, 1, 6, D_MODEL), lambda i, cnt: (layer, _seg_of_tile(i, GC), 0, 0)),
                pl.BlockSpec((1, 2, D_MODEL), lambda i, cnt: (layer, 0, 0)),
                pl.BlockSpec((1, 2, D_MODEL), lambda i, cnt: (layer, 0, 0)),
            ],
            out_specs=out_specs,
            scratch_shapes=[pltpu.VMEM((GC, D_MODEL), F32)],
        ),
        out_shape=out_shape,
        compiler_params=_cparams(("arbitrary",)),
        name="moe_combine_ln",
    )(cnt, pos_n, ye, x1, mod, ln_g, ln_b)


def _moe(x1, u2, aff_t, mod, layer, ln_g, ln_b, w_gate, w_up, w_down, split_out=False):
    pos_t, pos_n, cnt = _topk(aff_t)
    xe, gs = _gather(cnt, pos_t, aff_t, u2)
    ye = _ffn(xe.reshape(N_EXPERTS, 2 * CAP, D_MODEL), gs.reshape(N_EXPERTS, 2 * CAP, 1),
              layer, w_gate, w_up, w_down)
    return _combine(cnt, pos_n, ye.reshape(N_EXPERTS, 2, CAP, D_MODEL), x1, mod, layer, ln_g, ln_b, split_out)


def _sink_softmax_pv(s, sink_col, v):
    m = jnp.maximum(jnp.max(s, axis=-1, keepdims=True), sink_col)
    p = jnp.exp(s - m)
    den = jnp.sum(p, axis=-1, keepdims=True) + jnp.exp(sink_col - m)
    return _dot(p.astype(BF16), v) / den


def _sink_column(sink_ref, h, rows_per_group):
    rid = lax.broadcasted_iota(I32, (GQA_GROUP * rows_per_group, 1), 0) // rows_per_group
    col = jnp.zeros((GQA_GROUP * rows_per_group, 1), F32)
    for g in range(GQA_GROUP):
        col = jnp.where(rid == g, sink_ref[h * GQA_GROUP + g], col)
    return col


def _stack_groups(q, h):
    return jnp.concatenate(
        [q[:, (h * GQA_GROUP + g) * HEAD_DIM:(h * GQA_GROUP + g + 1) * HEAD_DIM] * ATTN_SCALE
         for g in range(GQA_GROUP)], axis=0).astype(BF16)


def _unstack_groups(outs, rows):
    return jnp.concatenate([o[g * rows:(g + 1) * rows] for o in outs for g in range(GQA_GROUP)], axis=1)


def _swa_ctx_kernel(sink_ref, q_ref, k_ref, v_ref, o_ref):
    q, k, v = q_ref[...], k_ref[...], v_ref[...]
    outs = []
    for h in range(SWA_KV_HEADS):
        cols = slice(h * HEAD_DIM, (h + 1) * HEAD_DIM)
        s = _dot_nt(_stack_groups(q, h), k[:, cols].astype(BF16))
        outs.append(_sink_softmax_pv(s, _sink_column(sink_ref, h, SEQ), v[:, cols].astype(BF16)))
    o_ref[...] = _unstack_groups(outs, SEQ).astype(BF16)


def _swa_ctx(proj, sink):
    return pl.pallas_call(
        _swa_ctx_kernel,
        grid=(BATCH,),
        in_specs=[
            pl.BlockSpec(memory_space=pltpu.SMEM),
            pl.BlockSpec((SEQ, 512), lambda b: (b, 0)),
            pl.BlockSpec((SEQ, LANE), lambda b: (b, 4)),
            pl.BlockSpec((SEQ, LANE), lambda b: (b, 5)),
        ],
        out_specs=pl.BlockSpec((SEQ, 512), lambda b: (b, 0)),
        out_shape=jax.ShapeDtypeStruct((N_CTX, 512), BF16),
        compiler_params=_cparams(("arbitrary",)),
        name="swa_context",
    )(sink, proj, proj, proj)


QB = 128


def _swa_lat_kernel(sink_ref, q_ref, kp_ref, kc_ref, kn_ref, vp_ref, vc_ref, vn_ref, ck_ref, cv_ref, o_ref):
    qb = pl.program_id(1)
    q = q_ref[...]
    kall = jnp.concatenate([ck_ref[0].astype(BF16), kp_ref[...], kc_ref[...], kn_ref[...]], axis=0)
    vall = jnp.concatenate([cv_ref[0].astype(BF16), vp_ref[...], vc_ref[...], vn_ref[...]], axis=0)
    nk = PAST_LEN + 3 * QB
    col = lax.broadcasted_iota(I32, (QB, nk), 1)
    qpos = qb * QB + lax.broadcasted_iota(I32, (QB, nk), 0)
    kpos = (qb - 1) * QB + col - PAST_LEN
    ok = (col < PAST_LEN) | ((jnp.abs(qpos - kpos) <= WINDOW) & (kpos >= 0) & (kpos < DEC_SEQ))
    bias = jnp.concatenate([jnp.where(ok, 0.0, NEG_INF)] * GQA_GROUP, axis=0)
    outs = []
    for h in range(SWA_KV_HEADS):
        cols = slice(h * HEAD_DIM, (h + 1) * HEAD_DIM)
        s = _dot_nt(_stack_groups(q, h), kall[:, cols]) + bias
        outs.append(_sink_softmax_pv(s, _sink_column(sink_ref, h, QB), vall[:, cols]))
    o_ref[...] = _unstack_groups(outs, QB).astype(BF16)


def _swa_lat(proj, cache_k, cache_v, sink):
    nqb = DEC_SEQ // QB
    row0 = N_CTX // QB

    def blk(col, shift):
        return pl.BlockSpec((QB, LANE), lambda b, i: (row0 + b * nqb + jnp.clip(i + shift, 0, nqb - 1), col))

    return pl.pallas_call(
        _swa_lat_kernel,
        grid=(DEC_BATCH, nqb),
        in_specs=[
            pl.BlockSpec(memory_space=pltpu.SMEM),
            pl.BlockSpec((QB, 512), lambda b, i: (row0 + b * nqb + i, 0)),
            blk(4, -1), blk(4, 0), blk(4, 1),
            blk(5, -1), blk(5, 0), blk(5, 1),
            pl.BlockSpec((1, PAST_LEN, LANE), lambda b, i: (b, 0, 0)),
            pl.BlockSpec((1, PAST_LEN, LANE), lambda b, i: (b, 0, 0)),
        ],
        out_specs=pl.BlockSpec((QB, 512), lambda b, i: (b * nqb + i, 0)),
        out_shape=jax.ShapeDtypeStruct((N_LAT, 512), BF16),
        compiler_params=_cparams(("arbitrary", "arbitrary")),
        name="swa_latent",
    )(sink, proj, proj, proj, proj, proj, proj, proj, cache_k, cache_v)


def _diff_lambda_col(lam_ref, lam_init):
    l1 = jnp.sum(lam_ref[0, 0] * lam_ref[0, 1], axis=-1, keepdims=True)
    l2 = jnp.sum(lam_ref[0, 2] * lam_ref[0, 3], axis=-1, keepdims=True)
    return jnp.exp(l1) - jnp.exp(l2) + lam_init


def _softmax_pv(q, k, v):
    s = _dot_nt(q, k)
    p = jnp.exp(s - jnp.max(s, axis=-1, keepdims=True))
    return _dot(p.astype(BF16), v) / jnp.sum(p, axis=-1, keepdims=True)


def _diff_head(q, k, v, lam, subln, lam_init):
    o = [_softmax_pv((q[:, m * HEAD_DIM:(m + 1) * HEAD_DIM] * ATTN_SCALE).astype(BF16),
                     k[:, m * HEAD_DIM:(m + 1) * HEAD_DIM], v) for m in range(2)]
    o = o[0] - lam * o[1]
    return o * lax.rsqrt(jnp.mean(o * o, axis=-1, keepdims=True) + LN_EPS) * subln * (1.0 - lam_init)


def _diff_kernel(*refs, lam_init, has_cache, nh):
    if has_cache:
        lam_ref, sg_ref, q_ref, k_ref, v_ref, ck_ref, cv_ref, o_ref = refs
    else:
        lam_ref, sg_ref, q_ref, k_ref, v_ref, o_ref = refs
    hp = pl.program_id(1)
    lam_all = _diff_lambda_col(lam_ref, lam_init)
    hid = lax.broadcasted_iota(I32, (DIFF_HEADS, 1), 0)
    q, k, v = q_ref[...], k_ref[...], v_ref[...]
    if has_cache:
        k = jnp.concatenate([k, ck_ref[0].astype(BF16)], axis=0)
        v = jnp.concatenate([v, cv_ref[0].astype(BF16)], axis=0)
    outs = []
    for j in range(nh):
        lam = jnp.sum(jnp.where(hid == hp * nh + j, lam_all, 0.0), axis=0, keepdims=True)
        cols = slice(j * DIFF_V_DIM, (j + 1) * DIFF_V_DIM)
        outs.append(_diff_head(q[:, cols], k[:, cols], v[:, cols], lam, sg_ref[...], lam_init))
    o_ref[...] = jnp.concatenate(outs, axis=1).astype(BF16)


DIFF_STEP_HEADS = 2
DQ = 512
Q_COL, K_COL, V_COL = 768, 1280, 1792
assert all(col % (DIFF_STEP_HEADS * DIFF_V_DIM) == 0 for col in (Q_COL, K_COL, V_COL))


def _diff_ctx(proj, lam_p, subln, j, lam_init):
    nh = DIFF_STEP_HEADS
    w = nh * DIFF_V_DIM
    return pl.pallas_call(
        functools.partial(_diff_kernel, lam_init=lam_init, has_cache=False, nh=nh),
        grid=(BATCH, DIFF_HEADS // nh),
        in_specs=[
            pl.BlockSpec((1, 4, DIFF_HEADS, HEAD_DIM), lambda b, h: (j, 0, 0, 0)),
            pl.BlockSpec((1, DIFF_V_DIM), lambda b, h: (j, 0)),
            pl.BlockSpec((SEQ, w), lambda b, h: (b, Q_COL // w + h)),
            pl.BlockSpec((SEQ, w), lambda b, h: (b, K_COL // w + h)),
            pl.BlockSpec((SEQ, w), lambda b, h: (b, V_COL // w + h)),
        ],
        out_specs=pl.BlockSpec((SEQ, w), lambda b, h: (b, h)),
        out_shape=jax.ShapeDtypeStruct((N_CTX, 512), BF16),
        compiler_params=_cparams(("arbitrary", "arbitrary")),
        name="diff_context",
    )(lam_p, subln, proj, proj, proj)


def _diff_lat(proj, cache_k, cache_v, lam_p, subln, j, lam_init):
    nh = DIFF_STEP_HEADS
    w = nh * DIFF_V_DIM
    nq = DEC_SEQ // DQ
    row0 = N_CTX // DQ
    seq0 = N_CTX // DEC_SEQ
    return pl.pallas_call(
        functools.partial(_diff_kernel, lam_init=lam_init, has_cache=True, nh=nh),
        grid=(DEC_BATCH, DIFF_HEADS // nh, nq),
        in_specs=[
            pl.BlockSpec((1, 4, DIFF_HEADS, HEAD_DIM), lambda b, h, i: (j, 0, 0, 0)),
            pl.BlockSpec((1, DIFF_V_DIM), lambda b, h, i: (j, 0)),
            pl.BlockSpec((DQ, w), lambda b, h, i: (row0 + b * nq + i, Q_COL // w + h)),
            pl.BlockSpec((DEC_SEQ, w), lambda b, h, i: (seq0 + b, K_COL // w + h)),
            pl.BlockSpec((DEC_SEQ, w), lambda b, h, i: (seq0 + b, V_COL // w + h)),
            pl.BlockSpec((1, PAST_LEN, w), lambda b, h, i: (b, 0, h)),
            pl.BlockSpec((1, PAST_LEN, w), lambda b, h, i: (b, 0, h)),
        ],
        out_specs=pl.BlockSpec((DQ, w), lambda b, h, i: (b * nq + i, h)),
        out_shape=jax.ShapeDtypeStruct((N_LAT, 512), BF16),
        compiler_params=_cparams(("arbitrary", "arbitrary", "arbitrary")),
        name="diff_latent",
    )(lam_p, subln, proj, proj, proj, cache_k, cache_v)


def kernel(x_prompt, x_sample, state_rglru, cache_swa_k, cache_swa_v, cache_diff_k, cache_diff_v, c, c_ctx,
           w_mod, b_mod, ln_g, ln_b, e_w_in, e_conv_w, e_conv_b, e_w_rgate, e_b_rgate, e_w_igate, e_b_igate,
           e_lambda, e_w_fnet, e_w_out, o_w_in, o_sink, o_lambda, o_subln_g, o_w_out,
           w_router, w_gate, w_up, w_down):
    x_pair = (x_prompt.reshape(N_CTX, D_MODEL), x_sample.reshape(N_LAT, D_MODEL))
    cvec8 = jnp.concatenate([c_ctx[None, :], c, jnp.zeros((SUBLANE - N_SEG, D_MODEL), F32)], axis=0).T
    mod = _modulation(cvec8, w_mod, b_mod).reshape(DEPTH, SUBLANE, 6, D_MODEL)

    j = 0
    xa, gz = _inproj(x_pair, mod, 0, e_w_in, j)
    wg = _gate_tiles(e_w_rgate, e_w_igate)
    cb = e_conv_b.reshape(-1, 1, D_RG)
    rg_args = (j, e_conv_w, cb, wg, e_b_rgate, e_b_igate, e_lambda)
    y_rg_c, st_c = _rglru(xa, gz, jnp.zeros((BATCH, 2, D_RG), F32), *rg_args,
                          seq_len=SEQ, nseq=BATCH, row_block0=0)
    y_rg_l, _ = _rglru(xa, gz, state_rglru[:, j], *rg_args,
                       seq_len=DEC_SEQ, nseq=DEC_BATCH, row_block0=N_CTX // DEC_SEQ)
    wbd = _fnet_blockdiag(e_w_fnet)
    y_fn_c = _fnet(gz, j, wbd, seq_len=SEQ, nseq=BATCH, row_block0=0)
    y_fn_l = _fnet(gz, j, wbd, seq_len=DEC_SEQ, nseq=DEC_BATCH, row_block0=N_CTX // DEC_SEQ)
    x1, u2, aff = _outproj((y_rg_c, y_rg_l), (y_fn_c, y_fn_l), e_w_out, j, x_pair, mod, 0, ln_g, ln_b, w_router)
    x = _moe(x1, u2, aff, mod, 0, ln_g, ln_b, w_gate, w_up, w_down)
    new_state_rglru = st_c[:, None]

    layer = 1
    lam_init = 0.8 - 0.6 * math.exp(-0.3 * layer)
    cos, sin = _rope_tables()
    rope_groups = (0, 1, 2, 3, 4, 6, 7, 8, 9, 10, 11, 12, 13)
    cache_cols = ((512, 128, True), (640, 128, True), (K_COL, 512, True), (V_COL, 512, False))
    proj, ks, vs, kd, vd = _inproj((x, x), mod, 1, o_w_in, j, rope=(cos, sin, rope_groups), cache_cols=cache_cols)
    sink = o_sink[j]
    ys_c = _swa_ctx(proj, sink)
    yd_c = _diff_ctx(proj, o_lambda, o_subln_g, j, lam_init)
    ck = cache_swa_k[:, j].reshape(DEC_BATCH, PAST_LEN, SWA_KV_HEADS * HEAD_DIM)
    cv = cache_swa_v[:, j].reshape(DEC_BATCH, PAST_LEN, SWA_KV_HEADS * HEAD_DIM)
    ys_l = _swa_lat(proj, ck, cv, sink)
    cdk = cache_diff_k[:, j].reshape(DEC_BATCH, PAST_LEN, DIFF_HEADS * 2 * HEAD_DIM)
    cdv = cache_diff_v[:, j].reshape(DEC_BATCH, PAST_LEN, DIFF_HEADS * DIFF_V_DIM)
    yd_l = _diff_lat(proj, cdk, cdv, o_lambda, o_subln_g, j, lam_init)
    x1, u2, aff = _outproj((ys_c, ys_l), (yd_c, yd_l), o_w_out, j, (x, x), mod, 1, ln_g, ln_b, w_router)
    y_c, y_l = _moe(x1, u2, aff, mod, 1, ln_g, ln_b, w_gate, w_up, w_down, split_out=True)

    new_cache_swa_k = ks.reshape(BATCH, 1, SWA_KV_HEADS, HEAD_DIM, SEQ).transpose(0, 1, 4, 2, 3)
    new_cache_swa_v = vs.reshape(BATCH, 1, SWA_KV_HEADS, HEAD_DIM, SEQ).transpose(0, 1, 4, 2, 3)
    new_cache_diff_k = kd.reshape(BATCH, 1, DIFF_HEADS, 2, HEAD_DIM, SEQ).transpose(0, 1, 5, 2, 3, 4)
    new_cache_diff_v = vd.reshape(BATCH, 1, SEQ, DIFF_HEADS, DIFF_V_DIM)
    y_prompt = y_c.reshape(BATCH, SEQ, D_MODEL)
    y_sample = y_l.reshape(DEC_BATCH, DEC_SEQ, D_MODEL)
    return (y_prompt, y_sample, new_state_rglru, new_cache_swa_k, new_cache_swa_v,
            new_cache_diff_k, new_cache_diff_v)
```

```python
import functools
import math

import numpy as np
import jax
import jax.numpy as jnp
from jax import lax
from jax.experimental import pallas as pl
from jax.experimental.pallas import tpu as pltpu

F32 = jnp.float32
BF16 = jnp.bfloat16
I32 = jnp.int32

D_MODEL = 1024
BATCH, SEQ = 16, 256
DEC_BATCH, DEC_SEQ = 2, 2048
PAST_LEN = 256
DEPTH = 2
GRID_W = 64
HEAD_DIM = 64
D_RG = 768
RG_BLOCK = 64
RG_C = 8.0
CONV_W = 4
D_FNET = 256
FNET_GROUP_DIM = 64
FNET_GROUPS = 4
SWA_HEADS = 8
SWA_KV_HEADS = 2
GQA_GROUP = 4
WINDOW = 128
DIFF_HEADS = 4
DIFF_V_DIM = 128
N_EXPERTS = 16
EC_FACTOR = 2
D_EXPERT = 2048
ROPE_BASE = 10000.0
LN_EPS = 1e-5
NEG_INF = -1e30
ATTN_SCALE = HEAD_DIM ** -0.5
ALPHA = (2 * DEPTH) ** 0.25
EVEN_IN = 2 * D_RG + D_FNET
ODD_IN = 2304

N_CTX = BATCH * SEQ
N_LAT = DEC_BATCH * DEC_SEQ
N_TOK = N_CTX + N_LAT
N_SEG = 1 + DEC_BATCH
CAP = EC_FACTOR * N_CTX // N_EXPERTS
assert N_CTX == N_LAT

LANE = 128
SUBLANE = 8
MXU_DIM = 256
VMEM_LIMIT = 56 * 1024 * 1024
TM = 512
GC = 256
N_CHUNKS = N_CTX // GC
WIN = 64
WIN_ALIGN = 16
WINS_PER_DOT = MXU_DIM // WIN
TOPK_MIN_EXP = -150.0
TOPK_GEO_STEPS = 16
TOPK_LIN_STEPS = 24
N_TILES = N_TOK // TM
CTX_TILES = N_CTX // TM
LAT_TILES_PER_SEQ = DEC_SEQ // TM


def _cparams(sem):
    return pltpu.CompilerParams(dimension_semantics=sem, vmem_limit_bytes=VMEM_LIMIT)


def _dot(a, b):
    return jnp.dot(a, b, preferred_element_type=F32)


def _dot_nt(a, b):
    return lax.dot_general(a, b, (((1,), (1,)), ((), ())), preferred_element_type=F32)


def _split(a):
    hi = a.astype(BF16)
    lo = (a - hi.astype(F32)).astype(BF16)
    return hi, lo


def _dot3(a, b):
    ah, al = _split(a)
    bh, bl = _split(b)
    return _dot(ah, bh) + (_dot(ah, bl) + _dot(al, bh))


def _layernorm(y, g, b):
    mu = jnp.mean(y, axis=-1, keepdims=True)
    d = y - mu
    var = jnp.mean(d * d, axis=-1, keepdims=True)
    return d * lax.rsqrt(var + LN_EPS) * g + b


def _pair_specs(width, tile=TM):
    ctx_tiles = N_CTX // tile
    return [
        pl.BlockSpec((tile, width), lambda i, *_: (jnp.minimum(i, ctx_tiles - 1), 0)),
        pl.BlockSpec((tile, width), lambda i, *_: (jnp.maximum(i - ctx_tiles, 0), 0)),
    ]


def _x_operand(x):
    if isinstance(x, tuple):
        return _pair_specs(D_MODEL), list(x)
    return [pl.BlockSpec((TM, D_MODEL), lambda i: (i, 0))], [x]


def _pick(i, ctx_ref, lat_ref):
    return jnp.where(i < CTX_TILES, ctx_ref[...], lat_ref[...])


def _seg_of_tile(i, tile=TM):
    ctx_tiles = N_CTX // tile
    return jnp.where(i < ctx_tiles, 0, (i - ctx_tiles) // (DEC_SEQ // tile) + 1)


MOD_TN = 1536


def _mod_kernel(ct_ref, w_ref, b_ref, o_ref):
    c = ct_ref[...]
    s = c * _sigmoid(c)
    w = w_ref[0]
    rows = [jnp.sum(s[:, v:v + 1] * w, axis=0, keepdims=True) for v in range(N_SEG)]
    rows.append(jnp.zeros((SUBLANE - N_SEG, w.shape[1]), F32))
    o_ref[0] = jnp.concatenate(rows, axis=0) + b_ref[0]


def _modulation(cvec8, w_mod, b_mod):
    n = 6 * D_MODEL
    return pl.pallas_call(
        _mod_kernel,
        grid=(DEPTH, n // MOD_TN),
        in_specs=[
            pl.BlockSpec((D_MODEL, SUBLANE), lambda l, j: (0, 0)),
            pl.BlockSpec((1, D_MODEL, MOD_TN), lambda l, j: (l, 0, j)),
            pl.BlockSpec((1, 1, MOD_TN), lambda l, j: (l, 0, j)),
        ],
        out_specs=pl.BlockSpec((1, SUBLANE, MOD_TN), lambda l, j: (l, 0, j)),
        out_shape=jax.ShapeDtypeStruct((DEPTH, SUBLANE, n), F32),
        compiler_params=_cparams(("arbitrary", "arbitrary")),
        name="modulation",
    )(cvec8, w_mod, b_mod.reshape(DEPTH, 1, n))


def _rope_tile(p, cos, sin, rope_groups):
    lane = lax.broadcasted_iota(I32, (1, LANE), 1)
    first_half = (lane % 32) < 16
    pieces = []
    for k in range(p.shape[1] // LANE):
        xg = p[:, k * LANE:(k + 1) * LANE]
        if k in rope_groups:
            partner = jnp.where(first_half, pltpu.roll(xg, LANE - 16, 1), pltpu.roll(xg, 16, 1))
            xg = xg * cos + partner * sin
        pieces.append(xg)
    return jnp.concatenate(pieces, axis=1)


def _inproj_kernel(*refs, rope_groups, cache_cols):
    i = pl.program_id(0)
    if rope_groups:
        x_ref, mod_ref, w_ref, cos_ref, sin_ref, o_ref = refs[:6]
        cache_refs, wbf_ref = refs[6:-1], refs[-1]
    else:
        xc_ref, xl_ref, mod_ref, w_ref, o_ref, gz_ref, wbf_ref = refs

    @pl.when(i == 0)
    def _():
        wbf_ref[...] = w_ref[0].astype(BF16)

    m = mod_ref[0, 0]
    x = x_ref[...] if rope_groups else _pick(i, xc_ref, xl_ref)
    u = x * (1.0 + m[1:2, :]) + m[0:1, :]
    p = _dot(u.astype(BF16), wbf_ref[...])
    if rope_groups:
        @pl.when(i < CTX_TILES)
        def _():
            o_ref[...] = p.astype(BF16)
            for ref, (start, width, transposed) in zip(cache_refs, cache_cols):
                if transposed:
                    for b in range(TM // SEQ):
                        ref[b] = p[b * SEQ:(b + 1) * SEQ, start:start + width].T
                else:
                    ref[...] = p[:, start:start + width]

        @pl.when(i >= CTX_TILES)
        def _():
            o_ref[...] = _rope_tile(p, cos_ref[...], sin_ref[...], rope_groups).astype(BF16)
    else:
        o_ref[...] = p[:, :D_RG]
        gz_ref[...] = p[:, D_RG:].astype(BF16)


def _inproj(x, mod, layer, w_in, j, rope=None, cache_cols=()):
    n = w_in.shape[-1]
    rope_groups = ()
    x_specs, x_args = _x_operand(x)
    in_specs = x_specs + [
        pl.BlockSpec((1, 1, 6, D_MODEL), lambda i: (layer, _seg_of_tile(i), 0, 0)),
        pl.BlockSpec((1, D_MODEL, n), lambda i: (j, 0, 0)),
    ]
    args = x_args + [mod, w_in]
    out_specs = pl.BlockSpec((TM, n), lambda i: (i, 0))
    if rope is None:
        out_specs = [pl.BlockSpec((TM, D_RG), lambda i: (i, 0)), pl.BlockSpec((TM, n - D_RG), lambda i: (i, 0))]
        out_shape = [jax.ShapeDtypeStruct((N_TOK, D_RG), F32), jax.ShapeDtypeStruct((N_TOK, n - D_RG), BF16)]
    else:
        cos, sin, rope_groups = rope
        tab = pl.BlockSpec((TM, LANE), lambda i: (jnp.maximum(i - CTX_TILES, 0) % LAT_TILES_PER_SEQ, 0))
        in_specs += [tab, tab]
        args += [cos, sin]
        seqs = TM // SEQ
        out_specs = [out_specs] + [
            pl.BlockSpec((seqs, width, SEQ), lambda i: (jnp.minimum(i, CTX_TILES - 1), 0, 0)) if transposed
            else _pair_specs(width)[0] for _, width, transposed in cache_cols]
        out_shape = [jax.ShapeDtypeStruct((N_TOK, n), BF16)] + [
            jax.ShapeDtypeStruct((BATCH, width, SEQ) if transposed else (N_CTX, width), F32)
            for _, width, transposed in cache_cols]
    return pl.pallas_call(
        functools.partial(_inproj_kernel, rope_groups=rope_groups, cache_cols=cache_cols),
        grid=(N_TILES,),
        in_specs=in_specs,
        out_specs=out_specs,
        out_shape=out_shape,
        scratch_shapes=[pltpu.VMEM((D_MODEL, n), BF16)],
        compiler_params=_cparams(("arbitrary",)),
        name="inproj_rope" if rope is not None else "inproj",
    )(*args)


def _rope_tables():
    s = np.arange(DEC_SEQ)
    row, col = (s // GRID_W).astype(np.float64), (s % GRID_W).astype(np.float64)
    nf = HEAD_DIM // 4
    inv = np.power(ROPE_BASE, -np.arange(nf, dtype=np.float64) / nf)
    ang = np.concatenate([row[:, None] * inv, row[:, None] * inv, col[:, None] * inv, col[:, None] * inv], axis=1)
    sign = np.concatenate([-np.ones(nf), np.ones(nf), -np.ones(nf), np.ones(nf)])
    cos = np.tile(np.cos(ang), (1, LANE // HEAD_DIM))
    sin = np.tile(np.sin(ang) * sign, (1, LANE // HEAD_DIM))
    return jnp.asarray(cos, F32), jnp.asarray(sin, F32)


RG_T = 256
RG_TILES = D_RG // MXU_DIM


def _sigmoid(x):
    return 0.5 * jnp.tanh(0.5 * x) + 0.5


def _softplus(x):
    u = jnp.exp(-jnp.abs(x))
    w = 1.0 + u
    l1p = jnp.where(w == 1.0, u, jnp.log(w) * (u / jnp.where(w == 1.0, 1.0, w - 1.0)))
    return jnp.maximum(x, 0.0) + l1p


def _rglru_kernel(xa_ref, ga_ref, h0_ref, cw_ref, cb_ref, wg_ref, br_ref, bi_ref, lam_ref,
                  y_ref, st_ref, hf_s, xc_s, a_s, b_s, hs_s, *, seq_len):
    nchunk = seq_len // RG_T
    cw = cw_ref[0]
    cb = cb_ref[0]

    def conv_chunk(c):
        base = pl.multiple_of(c * RG_T, RG_T)
        cur = xa_ref[pl.ds(base, RG_T), :]
        prev = xa_ref[pl.ds(pl.multiple_of(jnp.maximum(base - SUBLANE, 0), SUBLANE), SUBLANE), :]
        prev = jnp.where(c > 0, prev, 0.0)
        nxt = xa_ref[pl.ds(pl.multiple_of(jnp.minimum(base + RG_T, seq_len - SUBLANE), SUBLANE), SUBLANE), :]
        nxt = jnp.where(c < nchunk - 1, nxt, 0.0)
        win = jnp.concatenate([prev, cur, nxt], axis=0)
        xc = cb
        for k in range(CONV_W):
            off = SUBLANE - 1 + k
            xc = xc + win[off:off + RG_T] * cw[k:k + 1, :]
        return xc

    def gates(xc, z):
        xb = xc.astype(BF16)
        rs, gs = [], []
        for t in range(RG_TILES):
            zz = _dot(xb[:, t * MXU_DIM:(t + 1) * MXU_DIM], wg_ref[0, z, t].astype(BF16))
            rs.append(zz[:, :MXU_DIM])
            gs.append(zz[:, MXU_DIM:])
        r = _sigmoid(jnp.concatenate(rs, axis=1) + br_ref[0, z:z + 1, :])
        g = _sigmoid(jnp.concatenate(gs, axis=1) + bi_ref[0, z:z + 1, :])
        log_a = (-RG_C * _softplus(-lam_ref[0, z:z + 1, :])) * r
        a = jnp.exp(log_a)
        y = -jnp.tanh(log_a) * (a * a + 1.0)
        bt = jnp.where(y > 0.0, y * lax.rsqrt(y), 0.0) * g * xc
        return a, bt

    def scan_chunk(h, reverse):
        def step(t, h):
            tt = RG_T - 1 - t if reverse else t
            h = a_s[pl.ds(tt, 1), :] * h + b_s[pl.ds(tt, 1), :]
            hs_s[pl.ds(tt, 1), :] = h
            return h
        return lax.fori_loop(0, RG_T, step, h, unroll=8)

    def fwd_chunk(c, h):
        rows = pl.ds(pl.multiple_of(c * RG_T, RG_T), RG_T)
        xc = conv_chunk(c)
        xc_s[rows, :] = xc
        a, bt = gates(xc, 0)
        a_s[...] = a
        b_s[...] = bt
        h = scan_chunk(h, False)
        hf_s[rows, :] = hs_s[...]
        return h

    def bwd_chunk(k, h):
        c = nchunk - 1 - k
        rows = pl.ds(pl.multiple_of(c * RG_T, RG_T), RG_T)
        a, bt = gates(xc_s[rows, :], 1)
        a_s[...] = a
        b_s[...] = bt
        h = scan_chunk(h, True)
        y_ref[rows, :] = ((hf_s[rows, :] + hs_s[...]) * jax.nn.gelu(ga_ref[rows, :].astype(F32))).astype(BF16)
        return h

    hf = lax.fori_loop(0, nchunk, fwd_chunk, h0_ref[0, 0:1, :])
    hb = lax.fori_loop(0, nchunk, bwd_chunk, h0_ref[0, 1:2, :])
    st_ref[0, 0:1, :] = hf
    st_ref[0, 1:2, :] = hb


def _rglru(xa, gz, h0, j, cw, cb, wg, br, bi, lam, *, seq_len, nseq, row_block0):
    wspec3 = lambda shape: pl.BlockSpec((1,) + shape, lambda b: (j,) + (0,) * len(shape))
    return pl.pallas_call(
        functools.partial(_rglru_kernel, seq_len=seq_len),
        grid=(nseq,),
        in_specs=[
            pl.BlockSpec((seq_len, D_RG), lambda b: (row_block0 + b, 0)),
            pl.BlockSpec((seq_len, D_RG), lambda b: (row_block0 + b, 0)),
            pl.BlockSpec((1, 2, D_RG), lambda b: (b, 0, 0)),
            wspec3((CONV_W, D_RG)),
            wspec3((1, D_RG)),
            wspec3((2, RG_TILES, MXU_DIM, 2 * MXU_DIM)),
            wspec3((2, D_RG)),
            wspec3((2, D_RG)),
            wspec3((2, D_RG)),
        ],
        out_specs=[
            pl.BlockSpec((seq_len, D_RG), lambda b: (b, 0)),
            pl.BlockSpec((1, 2, D_RG), lambda b: (b, 0, 0)),
        ],
        out_shape=[
            jax.ShapeDtypeStruct((nseq * seq_len, D_RG), BF16),
            jax.ShapeDtypeStruct((nseq, 2, D_RG), F32),
        ],
        scratch_shapes=[
            pltpu.VMEM((seq_len, D_RG), F32),
            pltpu.VMEM((seq_len, D_RG), F32),
            pltpu.VMEM((RG_T, D_RG), F32),
            pltpu.VMEM((RG_T, D_RG), F32),
            pltpu.VMEM((RG_T, D_RG), F32),
        ],
        compiler_params=_cparams(("arbitrary",)),
        name=f"rglru_s{seq_len}",
    )(xa, gz, h0, cw, cb, wg, br, bi, lam)


def _gate_tiles(w_r, w_i):
    per_tile = MXU_DIM // RG_BLOCK
    eye = jnp.eye(per_tile, dtype=F32)

    def tiles(w):
        w = w.reshape(w.shape[0], 2, RG_TILES, per_tile, RG_BLOCK, RG_BLOCK)
        return jnp.einsum('nztbcd,be->nztbced', w, eye).reshape(w.shape[0], 2, RG_TILES, MXU_DIM, MXU_DIM)

    return jnp.concatenate([tiles(w_r), tiles(w_i)], axis=-1)


def _fnet_kernel(z_ref, cc_ref, sc_ref, w_ref, cs_ref, ns_ref, y_ref, p_s, q_s):
    b = pl.program_id(1)

    @pl.when(pl.program_id(0) == 0)
    def _():
        w = w_ref[0]
        a = _dot3(cc_ref[...], w).astype(BF16)
        bm = _dot3(sc_ref[...], w).astype(BF16)
        zb = z_ref[...].astype(BF16)
        p_s[b] = _dot(zb, a).astype(BF16)
        q_s[b] = _dot(zb, bm).astype(BF16)

    y_ref[...] = (_dot(cs_ref[...].astype(BF16), p_s[b])
                  + _dot(ns_ref[...].astype(BF16), q_s[b])).astype(BF16)


def _fnet(gz, j, wbd, *, seq_len, nseq, row_block0):
    tr = min(seq_len, 512)
    nrt = seq_len // tr
    cc, sc = _dft_tables(FNET_GROUP_DIM)
    ccbd = jnp.asarray(np.kron(np.eye(FNET_GROUPS), cc), F32)
    scbd = jnp.asarray(np.kron(np.eye(FNET_GROUPS), sc), F32)
    cs, ss = _dft_tables(seq_len)
    cs_b, ns_b = jnp.asarray(cs, F32), jnp.asarray(-ss, F32)
    zcol = D_RG // D_FNET
    const = lambda shape: pl.BlockSpec(shape, lambda r, b: (0,) * len(shape))
    return pl.pallas_call(
        _fnet_kernel,
        grid=(nrt, nseq),
        in_specs=[
            pl.BlockSpec((seq_len, D_FNET), lambda r, b: (row_block0 + b, zcol)),
            const((D_FNET, D_FNET)),
            const((D_FNET, D_FNET)),
            pl.BlockSpec((1, D_FNET, D_FNET), lambda r, b: (j, 0, 0)),
            pl.BlockSpec((tr, seq_len), lambda r, b: (r, 0)),
            pl.BlockSpec((tr, seq_len), lambda r, b: (r, 0)),
        ],
        out_specs=pl.BlockSpec((tr, D_FNET), lambda r, b: (b * nrt + r, 0)),
        out_shape=jax.ShapeDtypeStruct((nseq * seq_len, D_FNET), BF16),
        scratch_shapes=[pltpu.VMEM((nseq, seq_len, D_FNET), BF16), pltpu.VMEM((nseq, seq_len, D_FNET), BF16)],
        compiler_params=_cparams(("arbitrary", "arbitrary")),
        name=f"fnet_s{seq_len}",
    )(gz, ccbd, scbd, wbd, cs_b, ns_b)


def _dft_tables(n):
    k = np.arange(n)
    ang = 2.0 * np.pi * ((k[:, None] * k[None, :]) % n) / n
    return np.cos(ang) / np.sqrt(n), np.sin(ang) / np.sqrt(n)


def _fnet_blockdiag(w_f):
    eye = jnp.eye(FNET_GROUPS, dtype=F32)
    return jnp.einsum('ngcd,gh->ngchd', w_f, eye).reshape(w_f.shape[0], D_FNET, D_FNET)


def _outproj_kernel(ac_ref, al_ref, bc_ref, bl_ref, w_ref, *refs, ka):
    i = pl.program_id(0)
    x_refs, refs = refs[:-8], refs[-8:]
    mod_ref, g_ref, be_ref, wr_ref, x1_ref, u2_ref, aff_ref, wbf_ref = refs

    @pl.when(i == 0)
    def _():
        wbf_ref[...] = w_ref[0].astype(BF16)

    out = _dot(_pick(i, ac_ref, al_ref), wbf_ref[0:ka, :]) + _dot(_pick(i, bc_ref, bl_ref), wbf_ref[ka:, :])
    m = mod_ref[0, 0]
    x = _pick(i, *x_refs) if len(x_refs) == 2 else x_refs[0][...]
    x1 = _layernorm(ALPHA * x + m[2:3, :] * out, g_ref[0, 0:1, :], be_ref[0, 0:1, :])
    x1_ref[...] = x1
    u2 = x1 * (1.0 + m[4:5, :]) + m[3:4, :]
    u2_ref[...] = u2.astype(BF16)
    uh, ul = _split(u2)
    wh, wl = _split(wr_ref[0])
    lg = _dot(jnp.concatenate([uh, ul], axis=0), jnp.concatenate([wh, wl], axis=1))
    logits = (lg[:TM, :N_EXPERTS] + lg[:TM, N_EXPERTS:]) + (lg[TM:, :N_EXPERTS] + lg[TM:, N_EXPERTS:])
    mx = jnp.max(logits, axis=-1, keepdims=True)
    ex = jnp.exp(logits - mx)
    aff = ex / jnp.sum(ex, axis=-1, keepdims=True)
    aff = jnp.concatenate([aff, jnp.zeros((TM, LANE - N_EXPERTS), F32)], axis=1)
    aff_ref[...] = aff.T[:N_EXPERTS, :]


def _outproj(a_pair, b_pair, w_out, j, x, mod, layer, ln_g, ln_b, w_router):
    ka, kb = a_pair[0].shape[1], b_pair[0].shape[1]
    x_specs, x_args = _x_operand(x)
    return pl.pallas_call(
        functools.partial(_outproj_kernel, ka=ka),
        grid=(N_TILES,),
        in_specs=_pair_specs(ka) + _pair_specs(kb) + [
            pl.BlockSpec((1, D_MODEL, D_MODEL), lambda i: (j, 0, 0)),
        ] + x_specs + [
            pl.BlockSpec((1, 1, 6, D_MODEL), lambda i: (layer, _seg_of_tile(i), 0, 0)),
            pl.BlockSpec((1, 2, D_MODEL), lambda i: (layer, 0, 0)),
            pl.BlockSpec((1, 2, D_MODEL), lambda i: (layer, 0, 0)),
            pl.BlockSpec((1, D_MODEL, N_EXPERTS), lambda i: (layer, 0, 0)),
        ],
        out_specs=[
            pl.BlockSpec((TM, D_MODEL), lambda i: (i, 0)),
            pl.BlockSpec((TM, D_MODEL), lambda i: (i, 0)),
            pl.BlockSpec((N_EXPERTS, TM), lambda i: (0, i)),
        ],
        out_shape=[
            jax.ShapeDtypeStruct((N_TOK, D_MODEL), F32),
            jax.ShapeDtypeStruct((N_TOK, D_MODEL), BF16),
            jax.ShapeDtypeStruct((N_EXPERTS, N_TOK), F32),
        ],
        scratch_shapes=[pltpu.VMEM((D_MODEL, D_MODEL), BF16)],
        compiler_params=_cparams(("arbitrary",)),
        name="outproj_ln_router",
    )(*a_pair, *b_pair, w_out, *x_args, mod, ln_g, ln_b, w_router)


def _topk_kernel(aff_ref, pos_ref, posn_ref, cnt_ref):
    aff = aff_ref[...]
    cap = jnp.float32(CAP)

    def count_ge(t):
        return jnp.sum(jnp.where(aff >= t, 1.0, 0.0), axis=1, keepdims=True)

    lo_v = jnp.zeros((N_EXPERTS, 1), F32)
    hi_v = jnp.full((N_EXPERTS, 1), 2.0, F32)
    lo_t = jnp.full((N_EXPERTS, 1), TOPK_MIN_EXP, F32)
    hi_t = jnp.full((N_EXPERTS, 1), 1.0, F32)
    for _ in range(TOPK_GEO_STEPS):
        mid_t = 0.5 * (lo_t + hi_t)
        cand = jnp.exp2(mid_t)
        ok = count_ge(cand) >= cap
        lo_t = jnp.where(ok, mid_t, lo_t)
        hi_t = jnp.where(ok, hi_t, mid_t)
        lo_v = jnp.where(ok, jnp.maximum(lo_v, cand), lo_v)
        hi_v = jnp.where(ok, hi_v, jnp.minimum(hi_v, cand))
    for _ in range(TOPK_LIN_STEPS):
        cand = lo_v + 0.5 * (hi_v - lo_v)
        ok = count_ge(cand) >= cap
        lo_v = jnp.where(ok, cand, lo_v)
        hi_v = jnp.where(ok, hi_v, cand)
    gt = aff >= hi_v
    eq = (aff >= lo_v) & (aff < hi_v)
    need = cap - jnp.sum(jnp.where(gt, 1.0, 0.0), axis=1, keepdims=True)

    tri = (lax.broadcasted_iota(I32, (LANE, LANE), 0) <= lax.broadcasted_iota(I32, (LANE, LANE), 1))
    tri = jnp.where(tri, 1.0, 0.0).astype(BF16)

    def cumsum_blocks(mask_f32):
        carry = jnp.zeros((N_EXPERTS, 1), F32)
        outs = []
        for k in range(mask_f32.shape[1] // LANE):
            blk = mask_f32[:, k * LANE:(k + 1) * LANE]
            inc = _dot(blk.astype(BF16), tri) + carry
            outs.append(inc)
            carry = carry + jnp.sum(blk, axis=1, keepdims=True)
        return outs

    eq_f = jnp.where(eq, 1.0, 0.0)
    gt_f = jnp.where(gt, 1.0, 0.0)
    eq_rank = cumsum_blocks(eq_f)
    sel_blocks = []
    for k, rk in enumerate(eq_rank):
        lanes = slice(k * LANE, (k + 1) * LANE)
        sel_blocks.append(jnp.where((eq_f[:, lanes] > 0.0) & (rk <= need), 1.0, gt_f[:, lanes]))
    sel = jnp.concatenate(sel_blocks, axis=1)
    sel_rank = cumsum_blocks(sel)
    pad = jnp.zeros((LANE - N_EXPERTS, LANE), F32)
    for k, rk in enumerate(sel_rank):
        lanes = slice(k * LANE, (k + 1) * LANE)
        pos = jnp.where(sel[:, lanes] > 0.0, rk - 1.0, -1.0)
        pos_ref[:, lanes] = pos.astype(I32)
        posn_ref[lanes, :] = jnp.concatenate([pos, pad], axis=0).T[:, :N_EXPERTS].astype(I32)
    lane = lax.broadcasted_iota(I32, (N_EXPERTS, LANE), 1)
    cnt = jnp.zeros((N_EXPERTS, LANE), F32)
    for c in range(1, N_CHUNKS + 1):
        cnt = jnp.where(lane == c, sel_rank[c * (GC // LANE) - 1][:, LANE - 1:LANE], cnt)
    cnt_ref[0] = cnt.astype(I32)


def _topk(aff_t):
    return pl.pallas_call(
        _topk_kernel,
        grid=(2,),
        in_specs=[pl.BlockSpec((N_EXPERTS, N_CTX), lambda g: (0, g))],
        out_specs=[
            pl.BlockSpec((N_EXPERTS, N_CTX), lambda g: (0, g)),
            pl.BlockSpec((N_CTX, N_EXPERTS), lambda g: (g, 0)),
            pl.BlockSpec((1, N_EXPERTS, LANE), lambda g: (g, 0, 0)),
        ],
        out_shape=[
            jax.ShapeDtypeStruct((N_EXPERTS, N_TOK), I32),
            jax.ShapeDtypeStruct((N_TOK, N_EXPERTS), I32),
            jax.ShapeDtypeStruct((2, N_EXPERTS, LANE), I32),
        ],
        compiler_params=_cparams(("arbitrary",)),
        name="topk_select",
    )(aff_t)


def _chunk_windows(cnt_ref, g, e, c):
    c0 = cnt_ref[g, e, c]
    c1 = cnt_ref[g, e, c + 1]
    base = (c0 // WIN_ALIGN) * WIN_ALIGN
    k = jnp.where(c1 > c0, (c1 - base + WIN - 1) // WIN, 0)
    return base, k


def _gather_kernel(cnt_ref, pos_ref, aff_ref, u_ref, xe_ref, gs_ref):
    g = pl.program_id(0)
    c = pl.program_id(1)

    @pl.when(c == 0)
    def _():
        xe_ref[...] = jnp.zeros_like(xe_ref)
        gs_ref[...] = jnp.zeros_like(gs_ref)

    pos = pos_ref[...]
    aff = aff_ref[...]
    slot0 = lax.broadcasted_iota(I32, (WIN, GC), 0)

    def window(e, lo):
        start = pl.multiple_of(jnp.minimum(lo, CAP - WIN), WIN_ALIGN)
        pe = pos[e:e + 1, :]
        hit = (slot0 + start) == jnp.where(pe >= lo, pe, -1)
        gates = jnp.sum(jnp.where(hit, aff[e:e + 1, :], 0.0), axis=1, keepdims=True)
        return start, jnp.where(hit, 1.0, 0.0).astype(BF16), gates

    def add_rows(e, start, rows, gates):
        dst = (e, 0, pl.ds(start, WIN), slice(None))
        xe_ref[dst] = (xe_ref[dst].astype(F32) + rows).astype(BF16)
        gs_ref[dst] += gates

    windows = [_chunk_windows(cnt_ref, g, e, c) for e in range(N_EXPERTS)]
    firsts = [window(e, base) for e, (base, _) in enumerate(windows)]
    rows = _dot(jnp.concatenate([oh for _, oh, _ in firsts], axis=0), u_ref[...])
    for e, (start, _, gates) in enumerate(firsts):
        add_rows(e, start, rows[e * WIN:(e + 1) * WIN], gates)
    for e, (base, k) in enumerate(windows):
        def extra(w, carry, e=e, base=base):
            start, oh, gates = window(e, base + w * WIN)
            add_rows(e, start, _dot(oh, u_ref[...]), gates)
            return carry
        lax.fori_loop(1, k, extra, 0)


def _gather(cnt, pos_t, aff_t, u2):
    return pl.pallas_call(
        _gather_kernel,
        grid_spec=pltpu.PrefetchScalarGridSpec(
            num_scalar_prefetch=1,
            grid=(2, N_CHUNKS),
            in_specs=[
                pl.BlockSpec((N_EXPERTS, GC), lambda g, c, cnt: (0, g * N_CHUNKS + c)),
                pl.BlockSpec((N_EXPERTS, GC), lambda g, c, cnt: (0, g * N_CHUNKS + c)),
                pl.BlockSpec((GC, D_MODEL), lambda g, c, cnt: (g * N_CHUNKS + c, 0)),
            ],
            out_specs=[
                pl.BlockSpec((N_EXPERTS, 1, CAP, D_MODEL), lambda g, c, cnt: (0, g, 0, 0)),
                pl.BlockSpec((N_EXPERTS, 1, CAP, 1), lambda g, c, cnt: (0, g, 0, 0)),
            ],
        ),
        out_shape=[
            jax.ShapeDtypeStruct((N_EXPERTS, 2, CAP, D_MODEL), BF16),
            jax.ShapeDtypeStruct((N_EXPERTS, 2, CAP, 1), F32),
        ],
        compiler_params=_cparams(("arbitrary", "arbitrary")),
        name="moe_gather",
    )(cnt, pos_t, aff_t, u2)


TF = 1024


def _ffn_kernel(xe_ref, gs_ref, wg_ref, wu_ref, wd_ref, ye_ref, acc_ref):
    f = pl.program_id(1)
    x = xe_ref[0]
    hg = _dot(x, wg_ref[0, 0].astype(BF16))
    hu = _dot(x, wu_ref[0, 0].astype(BF16))
    h = (hg * _sigmoid(hg) * hu).astype(BF16)

    @pl.when(f == 0)
    def _():
        acc_ref[...] = jnp.zeros_like(acc_ref)

    acc_ref[...] += _dot(h, wd_ref[0, 0].astype(BF16))

    @pl.when(f == pl.num_programs(1) - 1)
    def _():
        ye_ref[0] = (acc_ref[...] * gs_ref[0]).astype(BF16)


def _ffn(xe, gs, layer, w_gate, w_up, w_down):
    rows = 2 * CAP
    return pl.pallas_call(
        _ffn_kernel,
        grid=(N_EXPERTS, D_EXPERT // TF),
        in_specs=[
            pl.BlockSpec((1, rows, D_MODEL), lambda e, f: (e, 0, 0)),
            pl.BlockSpec((1, rows, 1), lambda e, f: (e, 0, 0)),
            pl.BlockSpec((1, 1, D_MODEL, TF), lambda e, f: (layer, e, 0, f)),
            pl.BlockSpec((1, 1, D_MODEL, TF), lambda e, f: (layer, e, 0, f)),
            pl.BlockSpec((1, 1, TF, D_MODEL), lambda e, f: (layer, e, f, 0)),
        ],
        out_specs=pl.BlockSpec((1, rows, D_MODEL), lambda e, f: (e, 0, 0)),
        out_shape=jax.ShapeDtypeStruct((N_EXPERTS, rows, D_MODEL), BF16),
        scratch_shapes=[pltpu.VMEM((rows, D_MODEL), F32)],
        compiler_params=_cparams(("arbitrary", "arbitrary")),
        name="moe_ffn",
    )(xe, gs, w_gate, w_up, w_down)


def _combine_kernel(*refs, split_out):
    if split_out:
        cnt_ref, posn_ref, ye_ref, x1_ref, mod_ref, g_ref, be_ref, oc_ref, ol_ref, acc_s = refs
    else:
        cnt_ref, posn_ref, ye_ref, x1_ref, mod_ref, g_ref, be_ref, o_ref, acc_s = refs
    i = pl.program_id(0)
    g = i // N_CHUNKS
    c = i % N_CHUNKS
    posn = posn_ref[...]
    lane = lax.broadcasted_iota(I32, (1, MXU_DIM), 1)
    part = lane // WIN
    slot_in_win = lane % WIN

    def by_part(vals):
        out = vals[-1]
        for q in range(len(vals) - 2, -1, -1):
            out = jnp.where(part == q, vals[q], out)
        return out

    def start_of(lo):
        return pl.multiple_of(jnp.minimum(lo, CAP - WIN), WIN_ALIGN)

    def group_product(experts, los):
        starts = [start_of(lo) for lo in los]
        slots = by_part(starts) + slot_in_win
        cols = by_part([jnp.where(posn[:, e:e + 1] >= lo, posn[:, e:e + 1], -1) for e, lo in zip(experts, los)])
        onehot = jnp.where(slots == cols, 1.0, 0.0).astype(BF16)
        rows = jnp.concatenate([ye_ref[e, 0, pl.ds(st, WIN), :] for e, st in zip(experts, starts)], axis=0)
        return _dot(onehot, rows)

    windows = [_chunk_windows(cnt_ref, g, e, c) for e in range(N_EXPERTS)]
    acc = None
    for e0 in range(0, N_EXPERTS, WINS_PER_DOT):
        experts = list(range(e0, e0 + WINS_PER_DOT))
        d = group_product(experts, [windows[e][0] for e in experts])
        acc = d if acc is None else acc + d
    acc_s[...] = acc
    slot0 = lax.broadcasted_iota(I32, (GC, WIN), 1)
    for e, (base, k) in enumerate(windows):
        def extra(w, carry, e=e, base=base):
            lo = base + w * WIN
            start = start_of(lo)
            col = posn[:, e:e + 1]
            hit = (slot0 + start) == jnp.where(col >= lo, col, -1)
            acc_s[...] += _dot(jnp.where(hit, 1.0, 0.0).astype(BF16), ye_ref[e, 0, pl.ds(start, WIN), :])
            return carry
        lax.fori_loop(1, k, extra, 0)
    m = mod_ref[0, 0]
    y = ALPHA * x1_ref[...] + m[5:6, :] * acc_s[...]
    res = _layernorm(y, g_ref[0, 1:2, :], be_ref[0, 1:2, :])
    if split_out:
        @pl.when(i < N_CHUNKS)
        def _():
            oc_ref[...] = res

        @pl.when(i >= N_CHUNKS)
        def _():
            ol_ref[...] = res
    else:
        o_ref[...] = res


def _combine(cnt, pos_n, ye, x1, mod, layer, ln_g, ln_b, split_out):
    if split_out:
        out_specs = _pair_specs(D_MODEL, tile=GC)
        out_shape = [jax.ShapeDtypeStruct((N_CTX, D_MODEL), F32), jax.ShapeDtypeStruct((N_LAT, D_MODEL), F32)]
    else:
        out_specs = pl.BlockSpec((GC, D_MODEL), lambda i, cnt: (i, 0))
        out_shape = jax.ShapeDtypeStruct((N_TOK, D_MODEL), F32)
    return pl.pallas_call(
        functools.partial(_combine_kernel, split_out=split_out),
        grid_spec=pltpu.PrefetchScalarGridSpec(
            num_scalar_prefetch=1,
            grid=(N_TOK // GC,),
            in_specs=[
                pl.BlockSpec((GC, N_EXPERTS), lambda i, cnt: (i, 0)),
                pl.BlockSpec((N_EXPERTS, 1, CAP, D_MODEL), lambda i, cnt: (0, i // N_CHUNKS, 0, 0)),
                pl.BlockSpec((GC, D_MODEL), lambda i, cnt: (i, 0)),
                pl.BlockSpec((1, 1, 6, D_MODEL), lambda i, cnt: (layer, _seg_of_tile(i, GC), 0, 0)),
                pl.BlockSpec((1, 2, D_MODEL), lambda i, cnt: (layer, 0, 0)),
                pl.BlockSpec((1, 2, D_MODEL), lambda i, cnt: (layer, 0, 0)),
            ],
            out_specs=out_specs,
            scratch_shapes=[pltpu.VMEM((GC, D_MODEL), F32)],
        ),
        out_shape=out_shape,
        compiler_params=_cparams(("arbitrary",)),
        name="moe_combine_ln",
    )(cnt, pos_n, ye, x1, mod, ln_g, ln_b)


def _moe(x1, u2, aff_t, mod, layer, ln_g, ln_b, w_gate, w_up, w_down, split_out=False):
    pos_t, pos_n, cnt = _topk(aff_t)
    xe, gs = _gather(cnt, pos_t, aff_t, u2)
    ye = _ffn(xe.reshape(N_EXPERTS, 2 * CAP, D_MODEL), gs.reshape(N_EXPERTS, 2 * CAP, 1),
              layer, w_gate, w_up, w_down)
    return _combine(cnt, pos_n, ye.reshape(N_EXPERTS, 2, CAP, D_MODEL), x1, mod, layer, ln_g, ln_b, split_out)


def _sink_softmax_pv(s, sink_col, v):
    m = jnp.maximum(jnp.max(s, axis=-1, keepdims=True), sink_col)
    p = jnp.exp(s - m)
    den = jnp.sum(p, axis=-1, keepdims=True) + jnp.exp(sink_col - m)
    return _dot(p.astype(BF16), v) / den


def _sink_column(sink_ref, h, rows_per_group):
    rid = lax.broadcasted_iota(I32, (GQA_GROUP * rows_per_group, 1), 0) // rows_per_group
    col = jnp.zeros((GQA_GROUP * rows_per_group, 1), F32)
    for g in range(GQA_GROUP):
        col = jnp.where(rid == g, sink_ref[h * GQA_GROUP + g], col)
    return col


def _stack_groups(q, h):
    return jnp.concatenate(
        [q[:, (h * GQA_GROUP + g) * HEAD_DIM:(h * GQA_GROUP + g + 1) * HEAD_DIM] * ATTN_SCALE
         for g in range(GQA_GROUP)], axis=0).astype(BF16)


def _unstack_groups(outs, rows):
    return jnp.concatenate([o[g * rows:(g + 1) * rows] for o in outs for g in range(GQA_GROUP)], axis=1)


def _swa_ctx_kernel(sink_ref, q_ref, k_ref, v_ref, o_ref):
    q, k, v = q_ref[...], k_ref[...], v_ref[...]
    outs = []
    for h in range(SWA_KV_HEADS):
        cols = slice(h * HEAD_DIM, (h + 1) * HEAD_DIM)
        s = _dot_nt(_stack_groups(q, h), k[:, cols].astype(BF16))
        outs.append(_sink_softmax_pv(s, _sink_column(sink_ref, h, SEQ), v[:, cols].astype(BF16)))
    o_ref[...] = _unstack_groups(outs, SEQ).astype(BF16)


def _swa_ctx(proj, sink):
    return pl.pallas_call(
        _swa_ctx_kernel,
        grid=(BATCH,),
        in_specs=[
            pl.BlockSpec(memory_space=pltpu.SMEM),
            pl.BlockSpec((SEQ, 512), lambda b: (b, 0)),
            pl.BlockSpec((SEQ, LANE), lambda b: (b, 4)),
            pl.BlockSpec((SEQ, LANE), lambda b: (b, 5)),
        ],
        out_specs=pl.BlockSpec((SEQ, 512), lambda b: (b, 0)),
        out_shape=jax.ShapeDtypeStruct((N_CTX, 512), BF16),
        compiler_params=_cparams(("arbitrary",)),
        name="swa_context",
    )(sink, proj, proj, proj)


QB = 128


def _swa_lat_kernel(sink_ref, q_ref, kp_ref, kc_ref, kn_ref, vp_ref, vc_ref, vn_ref, ck_ref, cv_ref, o_ref):
    qb = pl.program_id(1)
    q = q_ref[...]
    kall = jnp.concatenate([ck_ref[0].astype(BF16), kp_ref[...], kc_ref[...], kn_ref[...]], axis=0)
    vall = jnp.concatenate([cv_ref[0].astype(BF16), vp_ref[...], vc_ref[...], vn_ref[...]], axis=0)
    nk = PAST_LEN + 3 * QB
    col = lax.broadcasted_iota(I32, (QB, nk), 1)
    qpos = qb * QB + lax.broadcasted_iota(I32, (QB, nk), 0)
    kpos = (qb - 1) * QB + col - PAST_LEN
    ok = (col < PAST_LEN) | ((jnp.abs(qpos - kpos) <= WINDOW) & (kpos >= 0) & (kpos < DEC_SEQ))
    bias = jnp.concatenate([jnp.where(ok, 0.0, NEG_INF)] * GQA_GROUP, axis=0)
    outs = []
    for h in range(SWA_KV_HEADS):
        cols = slice(h * HEAD_DIM, (h + 1) * HEAD_DIM)
        s = _dot_nt(_stack_groups(q, h), kall[:, cols]) + bias
        outs.append(_sink_softmax_pv(s, _sink_column(sink_ref, h, QB), vall[:, cols]))
    o_ref[...] = _unstack_groups(outs, QB).astype(BF16)


def _swa_lat(proj, cache_k, cache_v, sink):
    nqb = DEC_SEQ // QB
    row0 = N_CTX // QB

    def blk(col, shift):
        return pl.BlockSpec((QB, LANE), lambda b, i: (row0 + b * nqb + jnp.clip(i + shift, 0, nqb - 1), col))

    return pl.pallas_call(
        _swa_lat_kernel,
        grid=(DEC_BATCH, nqb),
        in_specs=[
            pl.BlockSpec(memory_space=pltpu.SMEM),
            pl.BlockSpec((QB, 512), lambda b, i: (row0 + b * nqb + i, 0)),
            blk(4, -1), blk(4, 0), blk(4, 1),
            blk(5, -1), blk(5, 0), blk(5, 1),
            pl.BlockSpec((1, PAST_LEN, LANE), lambda b, i: (b, 0, 0)),
            pl.BlockSpec((1, PAST_LEN, LANE), lambda b, i: (b, 0, 0)),
        ],
        out_specs=pl.BlockSpec((QB, 512), lambda b, i: (b * nqb + i, 0)),
        out_shape=jax.ShapeDtypeStruct((N_LAT, 512), BF16),
        compiler_params=_cparams(("arbitrary", "arbitrary")),
        name="swa_latent",
    )(sink, proj, proj, proj, proj, proj, proj, proj, cache_k, cache_v)


def _diff_lambda_col(lam_ref, lam_init):
    l1 = jnp.sum(lam_ref[0, 0] * lam_ref[0, 1], axis=-1, keepdims=True)
    l2 = jnp.sum(lam_ref[0, 2] * lam_ref[0, 3], axis=-1, keepdims=True)
    return jnp.exp(l1) - jnp.exp(l2) + lam_init


def _softmax_pv(q, k, v):
    s = _dot_nt(q, k)
    p = jnp.exp(s - jnp.max(s, axis=-1, keepdims=True))
    return _dot(p.astype(BF16), v) / jnp.sum(p, axis=-1, keepdims=True)


def _diff_head(q, k, v, lam, subln, lam_init):
    o = [_softmax_pv((q[:, m * HEAD_DIM:(m + 1) * HEAD_DIM] * ATTN_SCALE).astype(BF16),
                     k[:, m * HEAD_DIM:(m + 1) * HEAD_DIM], v) for m in range(2)]
    o = o[0] - lam * o[1]
    return o * lax.rsqrt(jnp.mean(o * o, axis=-1, keepdims=True) + LN_EPS) * subln * (1.0 - lam_init)


def _diff_kernel(*refs, lam_init, has_cache, nh):
    if has_cache:
        lam_ref, sg_ref, q_ref, k_ref, v_ref, ck_ref, cv_ref, o_ref = refs
    else:
        lam_ref, sg_ref, q_ref, k_ref, v_ref, o_ref = refs
    hp = pl.program_id(1)
    lam_all = _diff_lambda_col(lam_ref, lam_init)
    hid = lax.broadcasted_iota(I32, (DIFF_HEADS, 1), 0)
    q, k, v = q_ref[...], k_ref[...], v_ref[...]
    if has_cache:
        k = jnp.concatenate([k, ck_ref[0].astype(BF16)], axis=0)
        v = jnp.concatenate([v, cv_ref[0].astype(BF16)], axis=0)
    outs = []
    for j in range(nh):
        lam = jnp.sum(jnp.where(hid == hp * nh + j, lam_all, 0.0), axis=0, keepdims=True)
        cols = slice(j * DIFF_V_DIM, (j + 1) * DIFF_V_DIM)
        outs.append(_diff_head(q[:, cols], k[:, cols], v[:, cols], lam, sg_ref[...], lam_init))
    o_ref[...] = jnp.concatenate(outs, axis=1).astype(BF16)


DIFF_STEP_HEADS = 2
DQ = 512
Q_COL, K_COL, V_COL = 768, 1280, 1792
assert all(col % (DIFF_STEP_HEADS * DIFF_V_DIM) == 0 for col in (Q_COL, K_COL, V_COL))


def _diff_ctx(proj, lam_p, subln, j, lam_init):
    nh = DIFF_STEP_HEADS
    w = nh * DIFF_V_DIM
    return pl.pallas_call(
        functools.partial(_diff_kernel, lam_init=lam_init, has_cache=False, nh=nh),
        grid=(BATCH, DIFF_HEADS // nh),
        in_specs=[
            pl.BlockSpec((1, 4, DIFF_HEADS, HEAD_DIM), lambda b, h: (j, 0, 0, 0)),
            pl.BlockSpec((1, DIFF_V_DIM), lambda b, h: (j, 0)),
            pl.BlockSpec((SEQ, w), lambda b, h: (b, Q_COL // w + h)),
            pl.BlockSpec((SEQ, w), lambda b, h: (b, K_COL // w + h)),
            pl.BlockSpec((SEQ, w), lambda b, h: (b, V_COL // w + h)),
        ],
        out_specs=pl.BlockSpec((SEQ, w), lambda b, h: (b, h)),
        out_shape=jax.ShapeDtypeStruct((N_CTX, 512), BF16),
        compiler_params=_cparams(("arbitrary", "arbitrary")),
        name="diff_context",
    )(lam_p, subln, proj, proj, proj)


def _diff_lat(proj, cache_k, cache_v, lam_p, subln, j, lam_init):
    nh = DIFF_STEP_HEADS
    w = nh * DIFF_V_DIM
    nq = DEC_SEQ // DQ
    row0 = N_CTX // DQ
    seq0 = N_CTX // DEC_SEQ
    return pl.pallas_call(
        functools.partial(_diff_kernel, lam_init=lam_init, has_cache=True, nh=nh),
        grid=(DEC_BATCH, DIFF_HEADS // nh, nq),
        in_specs=[
            pl.BlockSpec((1, 4, DIFF_HEADS, HEAD_DIM), lambda b, h, i: (j, 0, 0, 0)),
            pl.BlockSpec((1, DIFF_V_DIM), lambda b, h, i: (j, 0)),
            pl.BlockSpec((DQ, w), lambda b, h, i: (row0 + b * nq + i, Q_COL // w + h)),
            pl.BlockSpec((DEC_SEQ, w), lambda b, h, i: (seq0 + b, K_COL // w + h)),
            pl.BlockSpec((DEC_SEQ, w), lambda b, h, i: (seq0 + b, V_COL // w + h)),
            pl.BlockSpec((1, PAST_LEN, w), lambda b, h, i: (b, 0, h)),
            pl.BlockSpec((1, PAST_LEN, w), lambda b, h, i: (b, 0, h)),
        ],
        out_specs=pl.BlockSpec((DQ, w), lambda b, h, i: (b * nq + i, h)),
        out_shape=jax.ShapeDtypeStruct((N_LAT, 512), BF16),
        compiler_params=_cparams(("arbitrary", "arbitrary", "arbitrary")),
        name="diff_latent",
    )(lam_p, subln, proj, proj, proj, cache_k, cache_v)


def kernel(x_prompt, x_sample, state_rglru, cache_swa_k, cache_swa_v, cache_diff_k, cache_diff_v, c, c_ctx,
           w_mod, b_mod, ln_g, ln_b, e_w_in, e_conv_w, e_conv_b, e_w_rgate, e_b_rgate, e_w_igate, e_b_igate,
           e_lambda, e_w_fnet, e_w_out, o_w_in, o_sink, o_lambda, o_subln_g, o_w_out,
           w_router, w_gate, w_up, w_down):
    x_pair = (x_prompt.reshape(N_CTX, D_MODEL), x_sample.reshape(N_LAT, D_MODEL))
    cvec8 = jnp.concatenate([c_ctx[None, :], c, jnp.zeros((SUBLANE - N_SEG, D_MODEL), F32)], axis=0).T
    mod = _modulation(cvec8, w_mod, b_mod).reshape(DEPTH, SUBLANE, 6, D_MODEL)

    j = 0
    xa, gz = _inproj(x_pair, mod, 0, e_w_in, j)
    wg = _gate_tiles(e_w_rgate, e_w_igate)
    cb = e_conv_b.reshape(-1, 1, D_RG)
    rg_args = (j, e_conv_w, cb, wg, e_b_rgate, e_b_igate, e_lambda)
    y_rg_c, st_c = _rglru(xa, gz, jnp.zeros((BATCH, 2, D_RG), F32), *rg_args,
                          seq_len=SEQ, nseq=BATCH, row_block0=0)
    y_rg_l, _ = _rglru(xa, gz, state_rglru[:, j], *rg_args,
                       seq_len=DEC_SEQ, nseq=DEC_BATCH, row_block0=N_CTX // DEC_SEQ)
    wbd = _fnet_blockdiag(e_w_fnet)
    y_fn_c = _fnet(gz, j, wbd, seq_len=SEQ, nseq=BATCH, row_block0=0)
    y_fn_l = _fnet(gz, j, wbd, seq_len=DEC_SEQ, nseq=DEC_BATCH, row_block0=N_CTX // DEC_SEQ)
    x1, u2, aff = _outproj((y_rg_c, y_rg_l), (y_fn_c, y_fn_l), e_w_out, j, x_pair, mod, 0, ln_g, ln_b, w_router)
    x = _moe(x1, u2, aff, mod, 0, ln_g, ln_b, w_gate, w_up, w_down)
    new_state_rglru = st_c[:, None]

    layer = 1
    lam_init = 0.8 - 0.6 * math.exp(-0.3 * layer)
    cos, sin = _rope_tables()
    rope_groups = (0, 1, 2, 3, 4, 6, 7, 8, 9, 10, 11, 12, 13)
    cache_cols = ((512, 128, True), (640, 128, True), (K_COL, 512, True), (V_COL, 512, False))
    proj, ks, vs, kd, vd = _inproj(x, mod, 1, o_w_in, j, rope=(cos, sin, rope_groups), cache_cols=cache_cols)
    sink = o_sink[j]
    ys_c = _swa_ctx(proj, sink)
    yd_c = _diff_ctx(proj, o_lambda, o_subln_g, j, lam_init)
    ck = cache_swa_k[:, j].reshape(DEC_BATCH, PAST_LEN, SWA_KV_HEADS * HEAD_DIM)
    cv = cache_swa_v[:, j].reshape(DEC_BATCH, PAST_LEN, SWA_KV_HEADS * HEAD_DIM)
    ys_l = _swa_lat(proj, ck, cv, sink)
    cdk = cache_diff_k[:, j].reshape(DEC_BATCH, PAST_LEN, DIFF_HEADS * 2 * HEAD_DIM)
    cdv = cache_diff_v[:, j].reshape(DEC_BATCH, PAST_LEN, DIFF_HEADS * DIFF_V_DIM)
    yd_l = _diff_lat(proj, cdk, cdv, o_lambda, o_subln_g, j, lam_init)
    x1, u2, aff = _outproj((ys_c, ys_l), (yd_c, yd_l), o_w_out, j, x, mod, 1, ln_g, ln_b, w_router)
    y_c, y_l = _moe(x1, u2, aff, mod, 1, ln_g, ln_b, w_gate, w_up, w_down, split_out=True)

    new_cache_swa_k = ks.reshape(BATCH, 1, SWA_KV_HEADS, HEAD_DIM, SEQ).transpose(0, 1, 4, 2, 3)
    new_cache_swa_v = vs.reshape(BATCH, 1, SWA_KV_HEADS, HEAD_DIM, SEQ).transpose(0, 1, 4, 2, 3)
    new_cache_diff_k = kd.reshape(BATCH, 1, DIFF_HEADS, 2, HEAD_DIM, SEQ).transpose(0, 1, 5, 2, 3, 4)
    new_cache_diff_v = vd.reshape(BATCH, 1, SEQ, DIFF_HEADS, DIFF_V_DIM)
    y_prompt = y_c.reshape(BATCH, SEQ, D_MODEL)
    y_sample = y_l.reshape(DEC_BATCH, DEC_SEQ, D_MODEL)
    return (y_prompt, y_sample, new_state_rglru, new_cache_swa_k, new_cache_swa_v,
            new_cache_diff_k, new_cache_diff_v)
```

```python
import functools
import math

import numpy as np
import jax
import jax.numpy as jnp
from jax import lax
from jax.experimental import pallas as pl
from jax.experimental.pallas import tpu as pltpu

F32 = jnp.float32
BF16 = jnp.bfloat16
I32 = jnp.int32

D_MODEL = 1024
BATCH, SEQ = 16, 256
DEC_BATCH, DEC_SEQ = 2, 2048
PAST_LEN = 256
DEPTH = 2
GRID_W = 64
HEAD_DIM = 64
D_RG = 768
RG_BLOCK = 64
RG_C = 8.0
CONV_W = 4
D_FNET = 256
FNET_GROUP_DIM = 64
FNET_GROUPS = 4
SWA_HEADS = 8
SWA_KV_HEADS = 2
GQA_GROUP = 4
WINDOW = 128
DIFF_HEADS = 4
DIFF_V_DIM = 128
N_EXPERTS = 16
EC_FACTOR = 2
D_EXPERT = 2048
ROPE_BASE = 10000.0
LN_EPS = 1e-5
NEG_INF = -1e30
ATTN_SCALE = HEAD_DIM ** -0.5
ALPHA = (2 * DEPTH) ** 0.25
SWA_Q_W = SWA_HEADS * HEAD_DIM
SWA_KV_W = SWA_KV_HEADS * HEAD_DIM
DIFF_W = DIFF_HEADS * DIFF_V_DIM
SWA_K_COL = SWA_Q_W
SWA_V_COL = SWA_K_COL + SWA_KV_W
Q_COL = SWA_V_COL + SWA_KV_W
K_COL = Q_COL + DIFF_W
V_COL = K_COL + DIFF_W
ODD_IN = V_COL + DIFF_W

N_CTX = BATCH * SEQ
N_LAT = DEC_BATCH * DEC_SEQ
N_TOK = N_CTX + N_LAT
N_SEG = 1 + DEC_BATCH
CAP = EC_FACTOR * N_CTX // N_EXPERTS
assert N_CTX == N_LAT

LANE = 128
SUBLANE = 8
MXU_DIM = 256
VMEM_LIMIT = 56 * 1024 * 1024
TM = 512
GC = 256
N_CHUNKS = N_CTX // GC
WIN = 64
WIN_ALIGN = 16
WINS_PER_DOT = MXU_DIM // WIN
TOPK_MIN_EXP = -150.0
TOPK_GEO_STEPS = 16
TOPK_LIN_STEPS = 24
N_TILES = N_TOK // TM
CTX_TILES = N_CTX // TM
LAT_TILES_PER_SEQ = DEC_SEQ // TM


def _cparams(sem):
    return pltpu.CompilerParams(dimension_semantics=sem, vmem_limit_bytes=VMEM_LIMIT)


def _dot(a, b):
    return jnp.dot(a, b, preferred_element_type=F32)


def _dot_nt(a, b):
    return lax.dot_general(a, b, (((1,), (1,)), ((), ())), preferred_element_type=F32)


def _split(a):
    hi = a.astype(BF16)
    lo = (a - hi.astype(F32)).astype(BF16)
    return hi, lo


def _dot3(a, b):
    ah, al = _split(a)
    bh, bl = _split(b)
    return _dot(ah, bh) + (_dot(ah, bl) + _dot(al, bh))


def _layernorm(y, g, b):
    mu = jnp.mean(y, axis=-1, keepdims=True)
    d = y - mu
    var = jnp.mean(d * d, axis=-1, keepdims=True)
    return d * lax.rsqrt(var + LN_EPS) * g + b


def _pair_specs(width, tile=TM):
    ctx_tiles = N_CTX // tile
    return [
        pl.BlockSpec((tile, width), lambda i, *_: (jnp.minimum(i, ctx_tiles - 1), 0)),
        pl.BlockSpec((tile, width), lambda i, *_: (jnp.maximum(i - ctx_tiles, 0), 0)),
    ]


def _x_operand(x):
    if isinstance(x, tuple):
        return _pair_specs(D_MODEL), list(x)
    return [pl.BlockSpec((TM, D_MODEL), lambda i: (i, 0))], [x]


def _pick(i, ctx_ref, lat_ref):
    return jnp.where(i < CTX_TILES, ctx_ref[...], lat_ref[...])


def _seg_of_tile(i, tile=TM):
    ctx_tiles = N_CTX // tile
    return jnp.where(i < ctx_tiles, 0, (i - ctx_tiles) // (DEC_SEQ // tile) + 1)


MOD_TN = 1536


def _mod_kernel(ct_ref, w_ref, b_ref, o_ref):
    c = ct_ref[...]
    s = c * _sigmoid(c)
    w = w_ref[0]
    rows = [jnp.sum(s[:, v:v + 1] * w, axis=0, keepdims=True) for v in range(N_SEG)]
    rows.append(jnp.zeros((SUBLANE - N_SEG, w.shape[1]), F32))
    o_ref[0] = jnp.concatenate(rows, axis=0) + b_ref[0]


def _modulation(cvec8, w_mod, b_mod):
    n = 6 * D_MODEL
    return pl.pallas_call(
        _mod_kernel,
        grid=(DEPTH, n // MOD_TN),
        in_specs=[
            pl.BlockSpec((D_MODEL, SUBLANE), lambda l, j: (0, 0)),
            pl.BlockSpec((1, D_MODEL, MOD_TN), lambda l, j: (l, 0, j)),
            pl.BlockSpec((1, 1, MOD_TN), lambda l, j: (l, 0, j)),
        ],
        out_specs=pl.BlockSpec((1, SUBLANE, MOD_TN), lambda l, j: (l, 0, j)),
        out_shape=jax.ShapeDtypeStruct((DEPTH, SUBLANE, n), F32),
        compiler_params=_cparams(("arbitrary", "arbitrary")),
        name="modulation",
    )(cvec8, w_mod, b_mod.reshape(DEPTH, 1, n))


def _rope_tile(p, cos, sin, rope_groups):
    lane = lax.broadcasted_iota(I32, (1, LANE), 1)
    first_half = (lane % (HEAD_DIM // 2)) < HEAD_DIM // 4
    pieces = []
    for k in range(p.shape[1] // LANE):
        xg = p[:, k * LANE:(k + 1) * LANE]
        if k in rope_groups:
            nf = HEAD_DIM // 4
            partner = jnp.where(first_half, pltpu.roll(xg, LANE - nf, 1), pltpu.roll(xg, nf, 1))
            xg = xg * cos + partner * sin
        pieces.append(xg)
    return jnp.concatenate(pieces, axis=1)


def _inproj_kernel(*refs, rope_groups, cache_cols):
    i = pl.program_id(0)
    if rope_groups:
        x_ref, mod_ref, w_ref, cos_ref, sin_ref, o_ref = refs[:6]
        cache_refs, wbf_ref = refs[6:-1], refs[-1]
    else:
        xc_ref, xl_ref, mod_ref, w_ref, o_ref, gz_ref, wbf_ref = refs

    @pl.when(i == 0)
    def _():
        wbf_ref[...] = w_ref[0].astype(BF16)

    m = mod_ref[0, 0]
    x = x_ref[...] if rope_groups else _pick(i, xc_ref, xl_ref)
    u = x * (1.0 + m[1:2, :]) + m[0:1, :]
    p = _dot(u.astype(BF16), wbf_ref[...])
    if rope_groups:
        @pl.when(i < CTX_TILES)
        def _():
            o_ref[...] = p.astype(BF16)
            for ref, (start, width, transposed) in zip(cache_refs, cache_cols):
                if transposed:
                    for b in range(TM // SEQ):
                        ref[b] = p[b * SEQ:(b + 1) * SEQ, start:start + width].T
                else:
                    ref[...] = p[:, start:start + width]

        @pl.when(i >= CTX_TILES)
        def _():
            o_ref[...] = _rope_tile(p, cos_ref[...], sin_ref[...], rope_groups).astype(BF16)
    else:
        o_ref[...] = p[:, :D_RG]
        gz_ref[...] = p[:, D_RG:].astype(BF16)


def _inproj(x, mod, layer, w_in, j, rope=None, cache_cols=()):
    n = w_in.shape[-1]
    rope_groups = ()
    x_specs, x_args = _x_operand(x)
    in_specs = x_specs + [
        pl.BlockSpec((1, 1, 6, D_MODEL), lambda i: (layer, _seg_of_tile(i), 0, 0)),
        pl.BlockSpec((1, D_MODEL, n), lambda i: (j, 0, 0)),
    ]
    args = x_args + [mod, w_in]
    out_specs = pl.BlockSpec((TM, n), lambda i: (i, 0))
    if rope is None:
        out_specs = [pl.BlockSpec((TM, D_RG), lambda i: (i, 0)), pl.BlockSpec((TM, n - D_RG), lambda i: (i, 0))]
        out_shape = [jax.ShapeDtypeStruct((N_TOK, D_RG), F32), jax.ShapeDtypeStruct((N_TOK, n - D_RG), BF16)]
    else:
        cos, sin, rope_groups = rope
        tab = pl.BlockSpec((TM, LANE), lambda i: (jnp.maximum(i - CTX_TILES, 0) % LAT_TILES_PER_SEQ, 0))
        in_specs += [tab, tab]
        args += [cos, sin]
        seqs = TM // SEQ
        out_specs = [out_specs] + [
            pl.BlockSpec((seqs, width, SEQ), lambda i: (jnp.minimum(i, CTX_TILES - 1), 0, 0)) if transposed
            else _pair_specs(width)[0] for _, width, transposed in cache_cols]
        out_shape = [jax.ShapeDtypeStruct((N_TOK, n), BF16)] + [
            jax.ShapeDtypeStruct((BATCH, width, SEQ) if transposed else (N_CTX, width), F32)
            for _, width, transposed in cache_cols]
    return pl.pallas_call(
        functools.partial(_inproj_kernel, rope_groups=rope_groups, cache_cols=cache_cols),
        grid=(N_TILES,),
        in_specs=in_specs,
        out_specs=out_specs,
        out_shape=out_shape,
        scratch_shapes=[pltpu.VMEM((D_MODEL, n), BF16)],
        compiler_params=_cparams(("arbitrary",)),
        name="inproj_rope" if rope is not None else "inproj",
    )(*args)


def _rope_tables():
    s = np.arange(DEC_SEQ)
    row, col = (s // GRID_W).astype(np.float64), (s % GRID_W).astype(np.float64)
    nf = HEAD_DIM // 4
    inv = np.power(ROPE_BASE, -np.arange(nf, dtype=np.float64) / nf)
    ang = np.concatenate([row[:, None] * inv, row[:, None] * inv, col[:, None] * inv, col[:, None] * inv], axis=1)
    sign = np.concatenate([-np.ones(nf), np.ones(nf), -np.ones(nf), np.ones(nf)])
    cos = np.tile(np.cos(ang), (1, LANE // HEAD_DIM))
    sin = np.tile(np.sin(ang) * sign, (1, LANE // HEAD_DIM))
    return jnp.asarray(cos, F32), jnp.asarray(sin, F32)


RG_T = 256
RG_TILES = D_RG // MXU_DIM


def _sigmoid(x):
    return 0.5 * jnp.tanh(0.5 * x) + 0.5


def _softplus(x):
    u = jnp.exp(-jnp.abs(x))
    w = 1.0 + u
    l1p = jnp.where(w == 1.0, u, jnp.log(w) * (u / jnp.where(w == 1.0, 1.0, w - 1.0)))
    return jnp.maximum(x, 0.0) + l1p


def _rglru_kernel(xa_ref, ga_ref, h0_ref, cw_ref, cb_ref, wg_ref, br_ref, bi_ref, lam_ref,
                  y_ref, st_ref, hf_s, xc_s, a_s, b_s, hs_s, *, seq_len):
    nchunk = seq_len // RG_T
    cw = cw_ref[0]
    cb = cb_ref[0]

    def conv_chunk(c):
        base = pl.multiple_of(c * RG_T, RG_T)
        cur = xa_ref[pl.ds(base, RG_T), :]
        prev = xa_ref[pl.ds(pl.multiple_of(jnp.maximum(base - SUBLANE, 0), SUBLANE), SUBLANE), :]
        prev = jnp.where(c > 0, prev, 0.0)
        nxt = xa_ref[pl.ds(pl.multiple_of(jnp.minimum(base + RG_T, seq_len - SUBLANE), SUBLANE), SUBLANE), :]
        nxt = jnp.where(c < nchunk - 1, nxt, 0.0)
        win = jnp.concatenate([prev, cur, nxt], axis=0)
        xc = cb
        for k in range(CONV_W):
            off = SUBLANE - 1 + k
            xc = xc + win[off:off + RG_T] * cw[k:k + 1, :]
        return xc

    def gates(xc, z):
        xb = xc.astype(BF16)
        rs, gs = [], []
        for t in range(RG_TILES):
            zz = _dot(xb[:, t * MXU_DIM:(t + 1) * MXU_DIM], wg_ref[0, z, t].astype(BF16))
            rs.append(zz[:, :MXU_DIM])
            gs.append(zz[:, MXU_DIM:])
        r = _sigmoid(jnp.concatenate(rs, axis=1) + br_ref[0, z:z + 1, :])
        g = _sigmoid(jnp.concatenate(gs, axis=1) + bi_ref[0, z:z + 1, :])
        log_a = (-RG_C * _softplus(-lam_ref[0, z:z + 1, :])) * r
        a = jnp.exp(log_a)
        y = -jnp.tanh(log_a) * (a * a + 1.0)
        bt = jnp.where(y > 0.0, y * lax.rsqrt(y), 0.0) * g * xc
        return a, bt

    def scan_chunk(h, reverse):
        def step(t, h):
            tt = RG_T - 1 - t if reverse else t
            h = a_s[pl.ds(tt, 1), :] * h + b_s[pl.ds(tt, 1), :]
            hs_s[pl.ds(tt, 1), :] = h
            return h
        return lax.fori_loop(0, RG_T, step, h, unroll=8)

    def fwd_chunk(c, h):
        rows = pl.ds(pl.multiple_of(c * RG_T, RG_T), RG_T)
        xc = conv_chunk(c)
        xc_s[rows, :] = xc
        a, bt = gates(xc, 0)
        a_s[...] = a
        b_s[...] = bt
        h = scan_chunk(h, False)
        hf_s[rows, :] = hs_s[...]
        return h

    def bwd_chunk(k, h):
        c = nchunk - 1 - k
        rows = pl.ds(pl.multiple_of(c * RG_T, RG_T), RG_T)
        a, bt = gates(xc_s[rows, :], 1)
        a_s[...] = a
        b_s[...] = bt
        h = scan_chunk(h, True)
        y_ref[rows, :] = ((hf_s[rows, :] + hs_s[...]) * jax.nn.gelu(ga_ref[rows, :].astype(F32))).astype(BF16)
        return h

    hf = lax.fori_loop(0, nchunk, fwd_chunk, h0_ref[0, 0:1, :])
    hb = lax.fori_loop(0, nchunk, bwd_chunk, h0_ref[0, 1:2, :])
    st_ref[0, 0:1, :] = hf
    st_ref[0, 1:2, :] = hb


def _rglru(xa, gz, h0, j, cw, cb, wg, br, bi, lam, *, seq_len, nseq, row_block0):
    wspec3 = lambda shape: pl.BlockSpec((1,) + shape, lambda b: (j,) + (0,) * len(shape))
    return pl.pallas_call(
        functools.partial(_rglru_kernel, seq_len=seq_len),
        grid=(nseq,),
        in_specs=[
            pl.BlockSpec((seq_len, D_RG), lambda b: (row_block0 + b, 0)),
            pl.BlockSpec((seq_len, D_RG), lambda b: (row_block0 + b, 0)),
            pl.BlockSpec((1, 2, D_RG), lambda b: (b, 0, 0)),
            wspec3((CONV_W, D_RG)),
            wspec3((1, D_RG)),
            wspec3((2, RG_TILES, MXU_DIM, 2 * MXU_DIM)),
            wspec3((2, D_RG)),
            wspec3((2, D_RG)),
            wspec3((2, D_RG)),
        ],
        out_specs=[
            pl.BlockSpec((seq_len, D_RG), lambda b: (b, 0)),
            pl.BlockSpec((1, 2, D_RG), lambda b: (b, 0, 0)),
        ],
        out_shape=[
            jax.ShapeDtypeStruct((nseq * seq_len, D_RG), BF16),
            jax.ShapeDtypeStruct((nseq, 2, D_RG), F32),
        ],
        scratch_shapes=[
            pltpu.VMEM((seq_len, D_RG), F32),
            pltpu.VMEM((seq_len, D_RG), F32),
            pltpu.VMEM((RG_T, D_RG), F32),
            pltpu.VMEM((RG_T, D_RG), F32),
            pltpu.VMEM((RG_T, D_RG), F32),
        ],
        compiler_params=_cparams(("arbitrary",)),
        name=f"rglru_s{seq_len}",
    )(xa, gz, h0, cw, cb, wg, br, bi, lam)


def _gate_tiles(w_r, w_i):
    per_tile = MXU_DIM // RG_BLOCK
    eye = jnp.eye(per_tile, dtype=F32)

    def tiles(w):
        w = w.reshape(w.shape[0], 2, RG_TILES, per_tile, RG_BLOCK, RG_BLOCK)
        return jnp.einsum('nztbcd,be->nztbced', w, eye).reshape(w.shape[0], 2, RG_TILES, MXU_DIM, MXU_DIM)

    return jnp.concatenate([tiles(w_r), tiles(w_i)], axis=-1)


FNET_ROW_TILE = 512


def _fnet_kernel(z_ref, cc_ref, sc_ref, w_ref, cs_ref, ns_ref, y_ref, p_s, q_s):
    b = pl.program_id(1)

    @pl.when(pl.program_id(0) == 0)
    def _():
        w = w_ref[0]
        a = _dot3(cc_ref[...], w).astype(BF16)
        bm = _dot3(sc_ref[...], w).astype(BF16)
        zb = z_ref[...].astype(BF16)
        p_s[b] = _dot(zb, a).astype(BF16)
        q_s[b] = _dot(zb, bm).astype(BF16)

    y_ref[...] = (_dot(cs_ref[...].astype(BF16), p_s[b])
                  + _dot(ns_ref[...].astype(BF16), q_s[b])).astype(BF16)


def _fnet(gz, j, wbd, *, seq_len, nseq, row_block0):
    tr = min(seq_len, FNET_ROW_TILE)
    nrt = seq_len // tr
    cc, sc = _dft_tables(FNET_GROUP_DIM)
    ccbd = jnp.asarray(np.kron(np.eye(FNET_GROUPS), cc), F32)
    scbd = jnp.asarray(np.kron(np.eye(FNET_GROUPS), sc), F32)
    cs, ss = _dft_tables(seq_len)
    cs_b, ns_b = jnp.asarray(cs, F32), jnp.asarray(-ss, F32)
    zcol = D_RG // D_FNET
    const = lambda shape: pl.BlockSpec(shape, lambda r, b: (0,) * len(shape))
    return pl.pallas_call(
        _fnet_kernel,
        grid=(nrt, nseq),
        in_specs=[
            pl.BlockSpec((seq_len, D_FNET), lambda r, b: (row_block0 + b, zcol)),
            const((D_FNET, D_FNET)),
            const((D_FNET, D_FNET)),
            pl.BlockSpec((1, D_FNET, D_FNET), lambda r, b: (j, 0, 0)),
            pl.BlockSpec((tr, seq_len), lambda r, b: (r, 0)),
            pl.BlockSpec((tr, seq_len), lambda r, b: (r, 0)),
        ],
        out_specs=pl.BlockSpec((tr, D_FNET), lambda r, b: (b * nrt + r, 0)),
        out_shape=jax.ShapeDtypeStruct((nseq * seq_len, D_FNET), BF16),
        scratch_shapes=[pltpu.VMEM((nseq, seq_len, D_FNET), BF16), pltpu.VMEM((nseq, seq_len, D_FNET), BF16)],
        compiler_params=_cparams(("arbitrary", "arbitrary")),
        name=f"fnet_s{seq_len}",
    )(gz, ccbd, scbd, wbd, cs_b, ns_b)


def _dft_tables(n):
    k = np.arange(n)
    ang = 2.0 * np.pi * ((k[:, None] * k[None, :]) % n) / n
    return np.cos(ang) / np.sqrt(n), np.sin(ang) / np.sqrt(n)


def _fnet_blockdiag(w_f):
    eye = jnp.eye(FNET_GROUPS, dtype=F32)
    return jnp.einsum('ngcd,gh->ngchd', w_f, eye).reshape(w_f.shape[0], D_FNET, D_FNET)


def _outproj_kernel(ac_ref, al_ref, bc_ref, bl_ref, w_ref, *refs, ka):
    i = pl.program_id(0)
    x_refs, refs = refs[:-8], refs[-8:]
    mod_ref, g_ref, be_ref, wr_ref, x1_ref, u2_ref, aff_ref, wbf_ref = refs

    @pl.when(i == 0)
    def _():
        wbf_ref[...] = w_ref[0].astype(BF16)

    out = _dot(_pick(i, ac_ref, al_ref), wbf_ref[0:ka, :]) + _dot(_pick(i, bc_ref, bl_ref), wbf_ref[ka:, :])
    m = mod_ref[0, 0]
    x = _pick(i, *x_refs) if len(x_refs) == 2 else x_refs[0][...]
    x1 = _layernorm(ALPHA * x + m[2:3, :] * out, g_ref[0, 0:1, :], be_ref[0, 0:1, :])
    x1_ref[...] = x1
    u2 = x1 * (1.0 + m[4:5, :]) + m[3:4, :]
    u2_ref[...] = u2.astype(BF16)
    uh, ul = _split(u2)
    wh, wl = _split(wr_ref[0])
    lg = _dot(jnp.concatenate([uh, ul], axis=0), jnp.concatenate([wh, wl], axis=1))
    logits = (lg[:TM, :N_EXPERTS] + lg[:TM, N_EXPERTS:]) + (lg[TM:, :N_EXPERTS] + lg[TM:, N_EXPERTS:])
    mx = jnp.max(logits, axis=-1, keepdims=True)
    ex = jnp.exp(logits - mx)
    aff = ex / jnp.sum(ex, axis=-1, keepdims=True)
    aff = jnp.concatenate([aff, jnp.zeros((TM, LANE - N_EXPERTS), F32)], axis=1)
    aff_ref[...] = aff.T[:N_EXPERTS, :]


def _outproj(a_pair, b_pair, w_out, j, x, mod, layer, ln_g, ln_b, w_router):
    ka, kb = a_pair[0].shape[1], b_pair[0].shape[1]
    x_specs, x_args = _x_operand(x)
    return pl.pallas_call(
        functools.partial(_outproj_kernel, ka=ka),
        grid=(N_TILES,),
        in_specs=_pair_specs(ka) + _pair_specs(kb) + [
            pl.BlockSpec((1, D_MODEL, D_MODEL), lambda i: (j, 0, 0)),
        ] + x_specs + [
            pl.BlockSpec((1, 1, 6, D_MODEL), lambda i: (layer, _seg_of_tile(i), 0, 0)),
            pl.BlockSpec((1, 2, D_MODEL), lambda i: (layer, 0, 0)),
            pl.BlockSpec((1, 2, D_MODEL), lambda i: (layer, 0, 0)),
            pl.BlockSpec((1, D_MODEL, N_EXPERTS), lambda i: (layer, 0, 0)),
        ],
        out_specs=[
            pl.BlockSpec((TM, D_MODEL), lambda i: (i, 0)),
            pl.BlockSpec((TM, D_MODEL), lambda i: (i, 0)),
            pl.BlockSpec((N_EXPERTS, TM), lambda i: (0, i)),
        ],
        out_shape=[
            jax.ShapeDtypeStruct((N_TOK, D_MODEL), F32),
            jax.ShapeDtypeStruct((N_TOK, D_MODEL), BF16),
            jax.ShapeDtypeStruct((N_EXPERTS, N_TOK), F32),
        ],
        scratch_shapes=[pltpu.VMEM((D_MODEL, D_MODEL), BF16)],
        compiler_params=_cparams(("arbitrary",)),
        name="outproj_ln_router",
    )(*a_pair, *b_pair, w_out, *x_args, mod, ln_g, ln_b, w_router)


def _topk_kernel(aff_ref, pos_ref, posn_ref, cnt_ref):
    aff = aff_ref[...]
    cap = jnp.float32(CAP)

    def count_ge(t):
        return jnp.sum(jnp.where(aff >= t, 1.0, 0.0), axis=1, keepdims=True)

    lo_v = jnp.zeros((N_EXPERTS, 1), F32)
    hi_v = jnp.full((N_EXPERTS, 1), 2.0, F32)
    lo_t = jnp.full((N_EXPERTS, 1), TOPK_MIN_EXP, F32)
    hi_t = jnp.full((N_EXPERTS, 1), 1.0, F32)
    for _ in range(TOPK_GEO_STEPS):
        mid_t = 0.5 * (lo_t + hi_t)
        cand = jnp.exp2(mid_t)
        ok = count_ge(cand) >= cap
        lo_t = jnp.where(ok, mid_t, lo_t)
        hi_t = jnp.where(ok, hi_t, mid_t)
        lo_v = jnp.where(ok, jnp.maximum(lo_v, cand), lo_v)
        hi_v = jnp.where(ok, hi_v, jnp.minimum(hi_v, cand))
    for _ in range(TOPK_LIN_STEPS):
        cand = lo_v + 0.5 * (hi_v - lo_v)
        ok = count_ge(cand) >= cap
        lo_v = jnp.where(ok, cand, lo_v)
        hi_v = jnp.where(ok, hi_v, cand)
    gt = aff >= hi_v
    eq = (aff >= lo_v) & (aff < hi_v)
    need = cap - jnp.sum(jnp.where(gt, 1.0, 0.0), axis=1, keepdims=True)

    tri = (lax.broadcasted_iota(I32, (LANE, LANE), 0) <= lax.broadcasted_iota(I32, (LANE, LANE), 1))
    tri = jnp.where(tri, 1.0, 0.0).astype(BF16)

    def cumsum_blocks(mask_f32):
        carry = jnp.zeros((N_EXPERTS, 1), F32)
        outs = []
        for k in range(mask_f32.shape[1] // LANE):
            blk = mask_f32[:, k * LANE:(k + 1) * LANE]
            inc = _dot(blk.astype(BF16), tri) + carry
            outs.append(inc)
            carry = carry + jnp.sum(blk, axis=1, keepdims=True)
        return outs

    eq_f = jnp.where(eq, 1.0, 0.0)
    gt_f = jnp.where(gt, 1.0, 0.0)
    eq_rank = cumsum_blocks(eq_f)
    sel_blocks = []
    for k, rk in enumerate(eq_rank):
        lanes = slice(k * LANE, (k + 1) * LANE)
        sel_blocks.append(jnp.where((eq_f[:, lanes] > 0.0) & (rk <= need), 1.0, gt_f[:, lanes]))
    sel = jnp.concatenate(sel_blocks, axis=1)
    sel_rank = cumsum_blocks(sel)
    pad = jnp.zeros((LANE - N_EXPERTS, LANE), F32)
    for k, rk in enumerate(sel_rank):
        lanes = slice(k * LANE, (k + 1) * LANE)
        pos = jnp.where(sel[:, lanes] > 0.0, rk - 1.0, -1.0)
        pos_ref[:, lanes] = pos.astype(I32)
        posn_ref[lanes, :] = jnp.concatenate([pos, pad], axis=0).T[:, :N_EXPERTS].astype(I32)
    lane = lax.broadcasted_iota(I32, (N_EXPERTS, LANE), 1)
    cnt = jnp.zeros((N_EXPERTS, LANE), F32)
    for c in range(1, N_CHUNKS + 1):
        cnt = jnp.where(lane == c, sel_rank[c * (GC // LANE) - 1][:, LANE - 1:LANE], cnt)
    cnt_ref[0] = cnt.astype(I32)


def _topk(aff_t):
    return pl.pallas_call(
        _topk_kernel,
        grid=(2,),
        in_specs=[pl.BlockSpec((N_EXPERTS, N_CTX), lambda g: (0, g))],
        out_specs=[
            pl.BlockSpec((N_EXPERTS, N_CTX), lambda g: (0, g)),
            pl.BlockSpec((N_CTX, N_EXPERTS), lambda g: (g, 0)),
            pl.BlockSpec((1, N_EXPERTS, LANE), lambda g: (g, 0, 0)),
        ],
        out_shape=[
            jax.ShapeDtypeStruct((N_EXPERTS, N_TOK), I32),
            jax.ShapeDtypeStruct((N_TOK, N_EXPERTS), I32),
            jax.ShapeDtypeStruct((2, N_EXPERTS, LANE), I32),
        ],
        compiler_params=_cparams(("arbitrary",)),
        name="topk_select",
    )(aff_t)


def _chunk_windows(cnt_ref, g, e, c):
    c0 = cnt_ref[g, e, c]
    c1 = cnt_ref[g, e, c + 1]
    base = (c0 // WIN_ALIGN) * WIN_ALIGN
    k = jnp.where(c1 > c0, (c1 - base + WIN - 1) // WIN, 0)
    return base, k


def _gather_kernel(cnt_ref, pos_ref, aff_ref, u_ref, xe_ref, gs_ref):
    g = pl.program_id(0)
    c = pl.program_id(1)

    @pl.when(c == 0)
    def _():
        xe_ref[...] = jnp.zeros_like(xe_ref)
        gs_ref[...] = jnp.zeros_like(gs_ref)

    pos = pos_ref[...]
    aff = aff_ref[...]
    slot0 = lax.broadcasted_iota(I32, (WIN, GC), 0)

    def window(e, lo):
        start = pl.multiple_of(jnp.minimum(lo, CAP - WIN), WIN_ALIGN)
        pe = pos[e:e + 1, :]
        hit = (slot0 + start) == jnp.where(pe >= lo, pe, -1)
        gates = jnp.sum(jnp.where(hit, aff[e:e + 1, :], 0.0), axis=1, keepdims=True)
        return start, jnp.where(hit, 1.0, 0.0).astype(BF16), gates

    def add_rows(e, start, rows, gates):
        dst = (e, 0, pl.ds(start, WIN), slice(None))
        xe_ref[dst] = (xe_ref[dst].astype(F32) + rows).astype(BF16)
        gs_ref[dst] += gates

    windows = [_chunk_windows(cnt_ref, g, e, c) for e in range(N_EXPERTS)]
    firsts = [window(e, base) for e, (base, _) in enumerate(windows)]
    rows = _dot(jnp.concatenate([oh for _, oh, _ in firsts], axis=0), u_ref[...])
    for e, (start, _, gates) in enumerate(firsts):
        add_rows(e, start, rows[e * WIN:(e + 1) * WIN], gates)
    for e, (base, k) in enumerate(windows):
        def extra(w, carry, e=e, base=base):
            start, oh, gates = window(e, base + w * WIN)
            add_rows(e, start, _dot(oh, u_ref[...]), gates)
            return carry
        lax.fori_loop(1, k, extra, 0)


def _gather(cnt, pos_t, aff_t, u2):
    return pl.pallas_call(
        _gather_kernel,
        grid_spec=pltpu.PrefetchScalarGridSpec(
            num_scalar_prefetch=1,
            grid=(2, N_CHUNKS),
            in_specs=[
                pl.BlockSpec((N_EXPERTS, GC), lambda g, c, cnt: (0, g * N_CHUNKS + c)),
                pl.BlockSpec((N_EXPERTS, GC), lambda g, c, cnt: (0, g * N_CHUNKS + c)),
                pl.BlockSpec((GC, D_MODEL), lambda g, c, cnt: (g * N_CHUNKS + c, 0)),
            ],
            out_specs=[
                pl.BlockSpec((N_EXPERTS, 1, CAP, D_MODEL), lambda g, c, cnt: (0, g, 0, 0)),
                pl.BlockSpec((N_EXPERTS, 1, CAP, 1), lambda g, c, cnt: (0, g, 0, 0)),
            ],
        ),
        out_shape=[
            jax.ShapeDtypeStruct((N_EXPERTS, 2, CAP, D_MODEL), BF16),
            jax.ShapeDtypeStruct((N_EXPERTS, 2, CAP, 1), F32),
        ],
        compiler_params=_cparams(("arbitrary", "arbitrary")),
        name="moe_gather",
    )(cnt, pos_t, aff_t, u2)


TF = 1024


def _ffn_kernel(xe_ref, gs_ref, wg_ref, wu_ref, wd_ref, ye_ref, acc_ref):
    f = pl.program_id(1)
    x = xe_ref[0]
    hg = _dot(x, wg_ref[0, 0].astype(BF16))
    hu = _dot(x, wu_ref[0, 0].astype(BF16))
    h = (hg * _sigmoid(hg) * hu).astype(BF16)

    @pl.when(f == 0)
    def _():
        acc_ref[...] = jnp.zeros_like(acc_ref)

    acc_ref[...] += _dot(h, wd_ref[0, 0].astype(BF16))

    @pl.when(f == pl.num_programs(1) - 1)
    def _():
        ye_ref[0] = (acc_ref[...] * gs_ref[0]).astype(BF16)


def _ffn(xe, gs, layer, w_gate, w_up, w_down):
    rows = 2 * CAP
    return pl.pallas_call(
        _ffn_kernel,
        grid=(N_EXPERTS, D_EXPERT // TF),
        in_specs=[
            pl.BlockSpec((1, rows, D_MODEL), lambda e, f: (e, 0, 0)),
            pl.BlockSpec((1, rows, 1), lambda e, f: (e, 0, 0)),
            pl.BlockSpec((1, 1, D_MODEL, TF), lambda e, f: (layer, e, 0, f)),
            pl.BlockSpec((1, 1, D_MODEL, TF), lambda e, f: (layer, e, 0, f)),
            pl.BlockSpec((1, 1, TF, D_MODEL), lambda e, f: (layer, e, f, 0)),
        ],
        out_specs=pl.BlockSpec((1, rows, D_MODEL), lambda e, f: (e, 0, 0)),
        out_shape=jax.ShapeDtypeStruct((N_EXPERTS, rows, D_MODEL), BF16),
        scratch_shapes=[pltpu.VMEM((rows, D_MODEL), F32)],
        compiler_params=_cparams(("arbitrary", "arbitrary")),
        name="moe_ffn",
    )(xe, gs, w_gate, w_up, w_down)


def _combine_kernel(*refs, split_out):
    if split_out:
        cnt_ref, posn_ref, ye_ref, x1_ref, mod_ref, g_ref, be_ref, oc_ref, ol_ref, acc_s = refs
    else:
        cnt_ref, posn_ref, ye_ref, x1_ref, mod_ref, g_ref, be_ref, o_ref, acc_s = refs
    i = pl.program_id(0)
    g = i // N_CHUNKS
    c = i % N_CHUNKS
    posn = posn_ref[...]
    lane = lax.broadcasted_iota(I32, (1, MXU_DIM), 1)
    part = lane // WIN
    slot_in_win = lane % WIN

    def by_part(vals):
        out = vals[-1]
        for q in range(len(vals) - 2, -1, -1):
            out = jnp.where(part == q, vals[q], out)
        return out

    def start_of(lo):
        return pl.multiple_of(jnp.minimum(lo, CAP - WIN), WIN_ALIGN)

    def group_product(experts, los):
        starts = [start_of(lo) for lo in los]
        slots = by_part(starts) + slot_in_win
        cols = by_part([jnp.where(posn[:, e:e + 1] >= lo, posn[:, e:e + 1], -1) for e, lo in zip(experts, los)])
        onehot = jnp.where(slots == cols, 1.0, 0.0).astype(BF16)
        rows = jnp.concatenate([ye_ref[e, 0, pl.ds(st, WIN), :] for e, st in zip(experts, starts)], axis=0)
        return _dot(onehot, rows)

    windows = [_chunk_windows(cnt_ref, g, e, c) for e in range(N_EXPERTS)]
    acc = None
    for e0 in range(0, N_EXPERTS, WINS_PER_DOT):
        experts = list(range(e0, e0 + WINS_PER_DOT))
        d = group_product(experts, [windows[e][0] for e in experts])
        acc = d if acc is None else acc + d
    acc_s[...] = acc
    slot0 = lax.broadcasted_iota(I32, (GC, WIN), 1)
    for e, (base, k) in enumerate(windows):
        def extra(w, carry, e=e, base=base):
            lo = base + w * WIN
            start = start_of(lo)
            col = posn[:, e:e + 1]
            hit = (slot0 + start) == jnp.where(col >= lo, col, -1)
            acc_s[...] += _dot(jnp.where(hit, 1.0, 0.0).astype(BF16), ye_ref[e, 0, pl.ds(start, WIN), :])
            return carry
        lax.fori_loop(1, k, extra, 0)
    m = mod_ref[0, 0]
    y = ALPHA * x1_ref[...] + m[5:6, :] * acc_s[...]
    res = _layernorm(y, g_ref[0, 1:2, :], be_ref[0, 1:2, :])
    if split_out:
        @pl.when(i < N_CHUNKS)
        def _():
            oc_ref[...] = res

        @pl.when(i >= N_CHUNKS)
        def _():
            ol_ref[...] = res
    else:
        o_ref[...] = res


def _combine(cnt, pos_n, ye, x1, mod, layer, ln_g, ln_b, split_out):
    if split_out:
        out_specs = _pair_specs(D_MODEL, tile=GC)
        out_shape = [jax.ShapeDtypeStruct((N_CTX, D_MODEL), F32), jax.ShapeDtypeStruct((N_LAT, D_MODEL), F32)]
    else:
        out_specs = pl.BlockSpec((GC, D_MODEL), lambda i, cnt: (i, 0))
        out_shape = jax.ShapeDtypeStruct((N_TOK, D_MODEL), F32)
    return pl.pallas_call(
        functools.partial(_combine_kernel, split_out=split_out),
        grid_spec=pltpu.PrefetchScalarGridSpec(
            num_scalar_prefetch=1,
            grid=(N_TOK // GC,),
            in_specs=[
                pl.BlockSpec((GC, N_EXPERTS), lambda i, cnt: (i, 0)),
                pl.BlockSpec((N_EXPERTS, 1, CAP, D_MODEL), lambda i, cnt: (0, i // N_CHUNKS, 0, 0)),
                pl.BlockSpec((GC, D_MODEL), lambda i, cnt: (i, 0)),
                pl.BlockSpec((1, 1, 6, D_MODEL), lambda i, cnt: (layer, _seg_of_tile(i, GC), 0, 0)),
                pl.BlockSpec((1, 2, D_MODEL), lambda i, cnt: (layer, 0, 0)),
                pl.BlockSpec((1, 2, D_MODEL), lambda i, cnt: (layer, 0, 0)),
            ],
            out_specs=out_specs,
            scratch_shapes=[pltpu.VMEM((GC, D_MODEL), F32)],
        ),
        out_shape=out_shape,
        compiler_params=_cparams(("arbitrary",)),
        name="moe_combine_ln",
    )(cnt, pos_n, ye, x1, mod, ln_g, ln_b)


def _moe(x1, u2, aff_t, mod, layer, ln_g, ln_b, w_gate, w_up, w_down, split_out=False):
    pos_t, pos_n, cnt = _topk(aff_t)
    xe, gs = _gather(cnt, pos_t, aff_t, u2)
    ye = _ffn(xe.reshape(N_EXPERTS, 2 * CAP, D_MODEL), gs.reshape(N_EXPERTS, 2 * CAP, 1),
              layer, w_gate, w_up, w_down)
    return _combine(cnt, pos_n, ye.reshape(N_EXPERTS, 2, CAP, D_MODEL), x1, mod, layer, ln_g, ln_b, split_out)


def _sink_softmax_pv(s, sink_col, v):
    m = jnp.maximum(jnp.max(s, axis=-1, keepdims=True), sink_col)
    p = jnp.exp(s - m)
    den = jnp.sum(p, axis=-1, keepdims=True) + jnp.exp(sink_col - m)
    return _dot(p.astype(BF16), v) / den


def _sink_column(sink_ref, h, rows_per_group):
    rid = lax.broadcasted_iota(I32, (GQA_GROUP * rows_per_group, 1), 0) // rows_per_group
    col = jnp.zeros((GQA_GROUP * rows_per_group, 1), F32)
    for g in range(GQA_GROUP):
        col = jnp.where(rid == g, sink_ref[h * GQA_GROUP + g], col)
    return col


def _stack_groups(q, h):
    return jnp.concatenate(
        [q[:, (h * GQA_GROUP + g) * HEAD_DIM:(h * GQA_GROUP + g + 1) * HEAD_DIM] * ATTN_SCALE
         for g in range(GQA_GROUP)], axis=0).astype(BF16)


def _unstack_groups(outs, rows):
    return jnp.concatenate([o[g * rows:(g + 1) * rows] for o in outs for g in range(GQA_GROUP)], axis=1)


def _swa_ctx_kernel(sink_ref, q_ref, k_ref, v_ref, o_ref):
    q, k, v = q_ref[...], k_ref[...], v_ref[...]
    outs = []
    for h in range(SWA_KV_HEADS):
        cols = slice(h * HEAD_DIM, (h + 1) * HEAD_DIM)
        s = _dot_nt(_stack_groups(q, h), k[:, cols].astype(BF16))
        outs.append(_sink_softmax_pv(s, _sink_column(sink_ref, h, SEQ), v[:, cols].astype(BF16)))
    o_ref[...] = _unstack_groups(outs, SEQ).astype(BF16)


def _swa_ctx(proj, sink):
    return pl.pallas_call(
        _swa_ctx_kernel,
        grid=(BATCH,),
        in_specs=[
            pl.BlockSpec(memory_space=pltpu.SMEM),
            pl.BlockSpec((SEQ, SWA_Q_W), lambda b: (b, 0)),
            pl.BlockSpec((SEQ, SWA_KV_W), lambda b: (b, SWA_K_COL // SWA_KV_W)),
            pl.BlockSpec((SEQ, SWA_KV_W), lambda b: (b, SWA_V_COL // SWA_KV_W)),
        ],
        out_specs=pl.BlockSpec((SEQ, SWA_Q_W), lambda b: (b, 0)),
        out_shape=jax.ShapeDtypeStruct((N_CTX, SWA_Q_W), BF16),
        compiler_params=_cparams(("arbitrary",)),
        name="swa_context",
    )(sink, proj, proj, proj)


QB = 128


def _swa_lat_kernel(sink_ref, q_ref, kp_ref, kc_ref, kn_ref, vp_ref, vc_ref, vn_ref, ck_ref, cv_ref, o_ref):
    qb = pl.program_id(1)
    q = q_ref[...]
    kall = jnp.concatenate([ck_ref[0].astype(BF16), kp_ref[...], kc_ref[...], kn_ref[...]], axis=0)
    vall = jnp.concatenate([cv_ref[0].astype(BF16), vp_ref[...], vc_ref[...], vn_ref[...]], axis=0)
    nk = PAST_LEN + 3 * QB
    col = lax.broadcasted_iota(I32, (QB, nk), 1)
    qpos = qb * QB + lax.broadcasted_iota(I32, (QB, nk), 0)
    kpos = (qb - 1) * QB + col - PAST_LEN
    ok = (col < PAST_LEN) | ((jnp.abs(qpos - kpos) <= WINDOW) & (kpos >= 0) & (kpos < DEC_SEQ))
    bias = jnp.concatenate([jnp.where(ok, 0.0, NEG_INF)] * GQA_GROUP, axis=0)
    outs = []
    for h in range(SWA_KV_HEADS):
        cols = slice(h * HEAD_DIM, (h + 1) * HEAD_DIM)
        s = _dot_nt(_stack_groups(q, h), kall[:, cols]) + bias
        outs.append(_sink_softmax_pv(s, _sink_column(sink_ref, h, QB), vall[:, cols]))
    o_ref[...] = _unstack_groups(outs, QB).astype(BF16)


def _swa_lat(proj, cache_k, cache_v, sink):
    nqb = DEC_SEQ // QB
    row0 = N_CTX // QB

    def blk(col, shift):
        return pl.BlockSpec((QB, SWA_KV_W),
                            lambda b, i: (row0 + b * nqb + jnp.clip(i + shift, 0, nqb - 1), col // SWA_KV_W))

    return pl.pallas_call(
        _swa_lat_kernel,
        grid=(DEC_BATCH, nqb),
        in_specs=[
            pl.BlockSpec(memory_space=pltpu.SMEM),
            pl.BlockSpec((QB, SWA_Q_W), lambda b, i: (row0 + b * nqb + i, 0)),
            blk(SWA_K_COL, -1), blk(SWA_K_COL, 0), blk(SWA_K_COL, 1),
            blk(SWA_V_COL, -1), blk(SWA_V_COL, 0), blk(SWA_V_COL, 1),
            pl.BlockSpec((1, PAST_LEN, SWA_KV_W), lambda b, i: (b, 0, 0)),
            pl.BlockSpec((1, PAST_LEN, SWA_KV_W), lambda b, i: (b, 0, 0)),
        ],
        out_specs=pl.BlockSpec((QB, SWA_Q_W), lambda b, i: (b * nqb + i, 0)),
        out_shape=jax.ShapeDtypeStruct((N_LAT, SWA_Q_W), BF16),
        compiler_params=_cparams(("arbitrary", "arbitrary")),
        name="swa_latent",
    )(sink, proj, proj, proj, proj, proj, proj, proj, cache_k, cache_v)


def _diff_lambda_col(lam_ref, lam_init):
    l1 = jnp.sum(lam_ref[0, 0] * lam_ref[0, 1], axis=-1, keepdims=True)
    l2 = jnp.sum(lam_ref[0, 2] * lam_ref[0, 3], axis=-1, keepdims=True)
    return jnp.exp(l1) - jnp.exp(l2) + lam_init


def _softmax_pv(q, k, v):
    s = _dot_nt(q, k)
    p = jnp.exp(s - jnp.max(s, axis=-1, keepdims=True))
    return _dot(p.astype(BF16), v) / jnp.sum(p, axis=-1, keepdims=True)


def _diff_head(q, k, v, lam, subln, lam_init):
    o = [_softmax_pv((q[:, m * HEAD_DIM:(m + 1) * HEAD_DIM] * ATTN_SCALE).astype(BF16),
                     k[:, m * HEAD_DIM:(m + 1) * HEAD_DIM], v) for m in range(2)]
    o = o[0] - lam * o[1]
    return o * lax.rsqrt(jnp.mean(o * o, axis=-1, keepdims=True) + LN_EPS) * subln * (1.0 - lam_init)


def _diff_kernel(*refs, lam_init, has_cache, nh):
    if has_cache:
        lam_ref, sg_ref, q_ref, k_ref, v_ref, ck_ref, cv_ref, o_ref = refs
    else:
        lam_ref, sg_ref, q_ref, k_ref, v_ref, o_ref = refs
    hp = pl.program_id(1)
    lam_all = _diff_lambda_col(lam_ref, lam_init)
    hid = lax.broadcasted_iota(I32, (DIFF_HEADS, 1), 0)
    q, k, v = q_ref[...], k_ref[...], v_ref[...]
    if has_cache:
        k = jnp.concatenate([k, ck_ref[0].astype(BF16)], axis=0)
        v = jnp.concatenate([v, cv_ref[0].astype(BF16)], axis=0)
    outs = []
    for j in range(nh):
        lam = jnp.sum(jnp.where(hid == hp * nh + j, lam_all, 0.0), axis=0, keepdims=True)
        cols = slice(j * DIFF_V_DIM, (j + 1) * DIFF_V_DIM)
        outs.append(_diff_head(q[:, cols], k[:, cols], v[:, cols], lam, sg_ref[...], lam_init))
    o_ref[...] = jnp.concatenate(outs, axis=1).astype(BF16)


DIFF_STEP_HEADS = 2
DQ = 512
assert all(col % (DIFF_STEP_HEADS * DIFF_V_DIM) == 0 for col in (Q_COL, K_COL, V_COL))


def _diff_ctx(proj, lam_p, subln, j, lam_init):
    nh = DIFF_STEP_HEADS
    w = nh * DIFF_V_DIM
    return pl.pallas_call(
        functools.partial(_diff_kernel, lam_init=lam_init, has_cache=False, nh=nh),
        grid=(BATCH, DIFF_HEADS // nh),
        in_specs=[
            pl.BlockSpec((1, 4, DIFF_HEADS, HEAD_DIM), lambda b, h: (j, 0, 0, 0)),
            pl.BlockSpec((1, DIFF_V_DIM), lambda b, h: (j, 0)),
            pl.BlockSpec((SEQ, w), lambda b, h: (b, Q_COL // w + h)),
            pl.BlockSpec((SEQ, w), lambda b, h: (b, K_COL // w + h)),
            pl.BlockSpec((SEQ, w), lambda b, h: (b, V_COL // w + h)),
        ],
        out_specs=pl.BlockSpec((SEQ, w), lambda b, h: (b, h)),
        out_shape=jax.ShapeDtypeStruct((N_CTX, DIFF_W), BF16),
        compiler_params=_cparams(("arbitrary", "arbitrary")),
        name="diff_context",
    )(lam_p, subln, proj, proj, proj)


def _diff_lat(proj, cache_k, cache_v, lam_p, subln, j, lam_init):
    nh = DIFF_STEP_HEADS
    w = nh * DIFF_V_DIM
    nq = DEC_SEQ // DQ
    row0 = N_CTX // DQ
    seq0 = N_CTX // DEC_SEQ
    return pl.pallas_call(
        functools.partial(_diff_kernel, lam_init=lam_init, has_cache=True, nh=nh),
        grid=(DEC_BATCH, DIFF_HEADS // nh, nq),
        in_specs=[
            pl.BlockSpec((1, 4, DIFF_HEADS, HEAD_DIM), lambda b, h, i: (j, 0, 0, 0)),
            pl.BlockSpec((1, DIFF_V_DIM), lambda b, h, i: (j, 0)),
            pl.BlockSpec((DQ, w), lambda b, h, i: (row0 + b * nq + i, Q_COL // w + h)),
            pl.BlockSpec((DEC_SEQ, w), lambda b, h, i: (seq0 + b, K_COL // w + h)),
            pl.BlockSpec((DEC_SEQ, w), lambda b, h, i: (seq0 + b, V_COL // w + h)),
            pl.BlockSpec((1, PAST_LEN, w), lambda b, h, i: (b, 0, h)),
            pl.BlockSpec((1, PAST_LEN, w), lambda b, h, i: (b, 0, h)),
        ],
        out_specs=pl.BlockSpec((DQ, w), lambda b, h, i: (b * nq + i, h)),
        out_shape=jax.ShapeDtypeStruct((N_LAT, DIFF_W), BF16),
        compiler_params=_cparams(("arbitrary", "arbitrary", "arbitrary")),
        name="diff_latent",
    )(lam_p, subln, proj, proj, proj, cache_k, cache_v)


def kernel(x_prompt, x_sample, state_rglru, cache_swa_k, cache_swa_v, cache_diff_k, cache_diff_v, c, c_ctx,
           w_mod, b_mod, ln_g, ln_b, e_w_in, e_conv_w, e_conv_b, e_w_rgate, e_b_rgate, e_w_igate, e_b_igate,
           e_lambda, e_w_fnet, e_w_out, o_w_in, o_sink, o_lambda, o_subln_g, o_w_out,
           w_router, w_gate, w_up, w_down):
    x_pair = (x_prompt.reshape(N_CTX, D_MODEL), x_sample.reshape(N_LAT, D_MODEL))
    cvec8 = jnp.concatenate([c_ctx[None, :], c, jnp.zeros((SUBLANE - N_SEG, D_MODEL), F32)], axis=0).T
    mod = _modulation(cvec8, w_mod, b_mod).reshape(DEPTH, SUBLANE, 6, D_MODEL)

    j = 0
    xa, gz = _inproj(x_pair, mod, 0, e_w_in, j)
    wg = _gate_tiles(e_w_rgate, e_w_igate)
    cb = e_conv_b.reshape(-1, 1, D_RG)
    rg_args = (j, e_conv_w, cb, wg, e_b_rgate, e_b_igate, e_lambda)
    y_rg_c, st_c = _rglru(xa, gz, jnp.zeros((BATCH, 2, D_RG), F32), *rg_args,
                          seq_len=SEQ, nseq=BATCH, row_block0=0)
    y_rg_l, _ = _rglru(xa, gz, state_rglru[:, j], *rg_args,
                       seq_len=DEC_SEQ, nseq=DEC_BATCH, row_block0=N_CTX // DEC_SEQ)
    wbd = _fnet_blockdiag(e_w_fnet)
    y_fn_c = _fnet(gz, j, wbd, seq_len=SEQ, nseq=BATCH, row_block0=0)
    y_fn_l = _fnet(gz, j, wbd, seq_len=DEC_SEQ, nseq=DEC_BATCH, row_block0=N_CTX // DEC_SEQ)
    x1, u2, aff = _outproj((y_rg_c, y_rg_l), (y_fn_c, y_fn_l), e_w_out, j, x_pair, mod, 0, ln_g, ln_b, w_router)
    x = _moe(x1, u2, aff, mod, 0, ln_g, ln_b, w_gate, w_up, w_down)
    new_state_rglru = st_c[:, None]

    layer = 1
    lam_init = 0.8 - 0.6 * math.exp(-0.3 * layer)
    cos, sin = _rope_tables()
    rope_groups = tuple(g for g in range(ODD_IN // LANE)
                        if g * LANE < SWA_V_COL or Q_COL <= g * LANE < V_COL)
    cache_cols = ((SWA_K_COL, SWA_KV_W, True), (SWA_V_COL, SWA_KV_W, True),
                  (K_COL, DIFF_W, True), (V_COL, DIFF_W, False))
    proj, ks, vs, kd, vd = _inproj(x, mod, 1, o_w_in, j, rope=(cos, sin, rope_groups), cache_cols=cache_cols)
    sink = o_sink[j]
    ys_c = _swa_ctx(proj, sink)
    yd_c = _diff_ctx(proj, o_lambda, o_subln_g, j, lam_init)
    ck = cache_swa_k[:, j].reshape(DEC_BATCH, PAST_LEN, SWA_KV_HEADS * HEAD_DIM)
    cv = cache_swa_v[:, j].reshape(DEC_BATCH, PAST_LEN, SWA_KV_HEADS * HEAD_DIM)
    ys_l = _swa_lat(proj, ck, cv, sink)
    cdk = cache_diff_k[:, j].reshape(DEC_BATCH, PAST_LEN, DIFF_HEADS * 2 * HEAD_DIM)
    cdv = cache_diff_v[:, j].reshape(DEC_BATCH, PAST_LEN, DIFF_HEADS * DIFF_V_DIM)
    yd_l = _diff_lat(proj, cdk, cdv, o_lambda, o_subln_g, j, lam_init)
    x1, u2, aff = _outproj((ys_c, ys_l), (yd_c, yd_l), o_w_out, j, x, mod, 1, ln_g, ln_b, w_router)
    y_c, y_l = _moe(x1, u2, aff, mod, 1, ln_g, ln_b, w_gate, w_up, w_down, split_out=True)

    new_cache_swa_k = ks.reshape(BATCH, 1, SWA_KV_HEADS, HEAD_DIM, SEQ).transpose(0, 1, 4, 2, 3)
    new_cache_swa_v = vs.reshape(BATCH, 1, SWA_KV_HEADS, HEAD_DIM, SEQ).transpose(0, 1, 4, 2, 3)
    new_cache_diff_k = kd.reshape(BATCH, 1, DIFF_HEADS, 2, HEAD_DIM, SEQ).transpose(0, 1, 5, 2, 3, 4)
    new_cache_diff_v = vd.reshape(BATCH, 1, SEQ, DIFF_HEADS, DIFF_V_DIM)
    y_prompt = y_c.reshape(BATCH, SEQ, D_MODEL)
    y_sample = y_l.reshape(DEC_BATCH, DEC_SEQ, D_MODEL)
    return (y_prompt, y_sample, new_state_rglru, new_cache_swa_k, new_cache_swa_v,
            new_cache_diff_k, new_cache_diff_v)
```

```python
import functools
import math

import numpy as np
import jax
import jax.numpy as jnp
from jax import lax
from jax.experimental import pallas as pl
from jax.experimental.pallas import tpu as pltpu

F32 = jnp.float32
BF16 = jnp.bfloat16
I32 = jnp.int32

D_MODEL = 1024
BATCH, SEQ = 16, 256
DEC_BATCH, DEC_SEQ = 2, 2048
PAST_LEN = 256
DEPTH = 2
GRID_W = 64
HEAD_DIM = 64
D_RG = 768
RG_BLOCK = 64
RG_C = 8.0
CONV_W = 4
D_FNET = 256
FNET_GROUP_DIM = 64
FNET_GROUPS = 4
SWA_HEADS = 8
SWA_KV_HEADS = 2
GQA_GROUP = 4
WINDOW = 128
DIFF_HEADS = 4
DIFF_V_DIM = 128
N_EXPERTS = 16
EC_FACTOR = 2
D_EXPERT = 2048
ROPE_BASE = 10000.0
LN_EPS = 1e-5
NEG_INF = -1e30
ATTN_SCALE = HEAD_DIM ** -0.5
ALPHA = (2 * DEPTH) ** 0.25
SWA_Q_W = SWA_HEADS * HEAD_DIM
SWA_KV_W = SWA_KV_HEADS * HEAD_DIM
DIFF_W = DIFF_HEADS * DIFF_V_DIM
SWA_K_COL = SWA_Q_W
SWA_V_COL = SWA_K_COL + SWA_KV_W
Q_COL = SWA_V_COL + SWA_KV_W
K_COL = Q_COL + DIFF_W
V_COL = K_COL + DIFF_W
ODD_IN = V_COL + DIFF_W

N_CTX = BATCH * SEQ
N_LAT = DEC_BATCH * DEC_SEQ
N_TOK = N_CTX + N_LAT
N_SEG = 1 + DEC_BATCH
CAP = EC_FACTOR * N_CTX // N_EXPERTS
assert N_CTX == N_LAT

LANE = 128
SUBLANE = 8
MXU_DIM = 256
VMEM_LIMIT = 56 * 1024 * 1024
TM = 512
GC = 256
N_CHUNKS = N_CTX // GC
WIN = 64
WIN_ALIGN = 16
WINS_PER_DOT = MXU_DIM // WIN
TOPK_MIN_EXP = -150.0
TOPK_GEO_STEPS = 16
TOPK_LIN_STEPS = 24
N_TILES = N_TOK // TM
CTX_TILES = N_CTX // TM
LAT_TILES_PER_SEQ = DEC_SEQ // TM


def _cparams(sem):
    return pltpu.CompilerParams(dimension_semantics=sem, vmem_limit_bytes=VMEM_LIMIT)


def _dot(a, b):
    return jnp.dot(a, b, preferred_element_type=F32)


def _dot_nt(a, b):
    return lax.dot_general(a, b, (((1,), (1,)), ((), ())), preferred_element_type=F32)


def _split(a):
    hi = a.astype(BF16)
    lo = (a - hi.astype(F32)).astype(BF16)
    return hi, lo


def _dot3(a, b):
    ah, al = _split(a)
    bh, bl = _split(b)
    return _dot(ah, bh) + (_dot(ah, bl) + _dot(al, bh))


def _layernorm(y, g, b):
    mu = jnp.mean(y, axis=-1, keepdims=True)
    d = y - mu
    var = jnp.mean(d * d, axis=-1, keepdims=True)
    return d * lax.rsqrt(var + LN_EPS) * g + b


def _pair_specs(width, tile=TM):
    ctx_tiles = N_CTX // tile
    return [
        pl.BlockSpec((tile, width), lambda i, *_: (jnp.minimum(i, ctx_tiles - 1), 0)),
        pl.BlockSpec((tile, width), lambda i, *_: (jnp.maximum(i - ctx_tiles, 0), 0)),
    ]


def _x_operand(x):
    if isinstance(x, tuple):
        return _pair_specs(D_MODEL), list(x)
    return [pl.BlockSpec((TM, D_MODEL), lambda i: (i, 0))], [x]


def _pick(i, ctx_ref, lat_ref):
    return jnp.where(i < CTX_TILES, ctx_ref[...], lat_ref[...])


def _seg_of_tile(i, tile=TM):
    ctx_tiles = N_CTX // tile
    return jnp.where(i < ctx_tiles, 0, (i - ctx_tiles) // (DEC_SEQ // tile) + 1)


MOD_TN = 1536


def _mod_kernel(ct_ref, w_ref, b_ref, o_ref):
    c = ct_ref[...]
    s = c * _sigmoid(c)
    w = w_ref[0]
    rows = [jnp.sum(s[:, v:v + 1] * w, axis=0, keepdims=True) for v in range(N_SEG)]
    rows.append(jnp.zeros((SUBLANE - N_SEG, w.shape[1]), F32))
    o_ref[0] = jnp.concatenate(rows, axis=0) + b_ref[0]


def _modulation(cvec8, w_mod, b_mod):
    n = 6 * D_MODEL
    return pl.pallas_call(
        _mod_kernel,
        grid=(DEPTH, n // MOD_TN),
        in_specs=[
            pl.BlockSpec((D_MODEL, SUBLANE), lambda l, j: (0, 0)),
            pl.BlockSpec((1, D_MODEL, MOD_TN), lambda l, j: (l, 0, j)),
            pl.BlockSpec((1, 1, MOD_TN), lambda l, j: (l, 0, j)),
        ],
        out_specs=pl.BlockSpec((1, SUBLANE, MOD_TN), lambda l, j: (l, 0, j)),
        out_shape=jax.ShapeDtypeStruct((DEPTH, SUBLANE, n), F32),
        compiler_params=_cparams(("arbitrary", "arbitrary")),
        name="modulation",
    )(cvec8, w_mod, b_mod.reshape(DEPTH, 1, n))


def _rope_tile(p, cos, sin, rope_groups):
    lane = lax.broadcasted_iota(I32, (1, LANE), 1)
    first_half = (lane % (HEAD_DIM // 2)) < HEAD_DIM // 4
    pieces = []
    for k in range(p.shape[1] // LANE):
        xg = p[:, k * LANE:(k + 1) * LANE]
        if k in rope_groups:
            nf = HEAD_DIM // 4
            partner = jnp.where(first_half, pltpu.roll(xg, LANE - nf, 1), pltpu.roll(xg, nf, 1))
            xg = xg * cos + partner * sin
        pieces.append(xg)
    return jnp.concatenate(pieces, axis=1)


def _inproj_kernel(*refs, rope_groups, cache_cols):
    i = pl.program_id(0)
    if rope_groups:
        x_ref, mod_ref, w_ref, cos_ref, sin_ref, o_ref = refs[:6]
        cache_refs, wbf_ref = refs[6:-1], refs[-1]
    else:
        xc_ref, xl_ref, mod_ref, w_ref, o_ref, gz_ref, wbf_ref = refs

    @pl.when(i == 0)
    def _():
        wbf_ref[...] = w_ref[0].astype(BF16)

    m = mod_ref[0, 0]
    x = x_ref[...] if rope_groups else _pick(i, xc_ref, xl_ref)
    u = x * (1.0 + m[1:2, :]) + m[0:1, :]
    p = _dot(u.astype(BF16), wbf_ref[...])
    if rope_groups:
        @pl.when(i < CTX_TILES)
        def _():
            o_ref[...] = p.astype(BF16)
            for ref, (start, width, transposed) in zip(cache_refs, cache_cols):
                if transposed:
                    for b in range(TM // SEQ):
                        ref[b] = p[b * SEQ:(b + 1) * SEQ, start:start + width].T
                else:
                    ref[...] = p[:, start:start + width]

        @pl.when(i >= CTX_TILES)
        def _():
            o_ref[...] = _rope_tile(p, cos_ref[...], sin_ref[...], rope_groups).astype(BF16)
    else:
        o_ref[...] = p[:, :D_RG]
        gz_ref[...] = p[:, D_RG:].astype(BF16)


def _inproj(x, mod, layer, w_in, j, rope=None, cache_cols=()):
    n = w_in.shape[-1]
    rope_groups = ()
    x_specs, x_args = _x_operand(x)
    in_specs = x_specs + [
        pl.BlockSpec((1, 1, 6, D_MODEL), lambda i: (layer, _seg_of_tile(i), 0, 0)),
        pl.BlockSpec((1, D_MODEL, n), lambda i: (j, 0, 0)),
    ]
    args = x_args + [mod, w_in]
    out_specs = pl.BlockSpec((TM, n), lambda i: (i, 0))
    if rope is None:
        out_specs = [pl.BlockSpec((TM, D_RG), lambda i: (i, 0)), pl.BlockSpec((TM, n - D_RG), lambda i: (i, 0))]
        out_shape = [jax.ShapeDtypeStruct((N_TOK, D_RG), F32), jax.ShapeDtypeStruct((N_TOK, n - D_RG), BF16)]
    else:
        cos, sin, rope_groups = rope
        tab = pl.BlockSpec((TM, LANE), lambda i: (jnp.maximum(i - CTX_TILES, 0) % LAT_TILES_PER_SEQ, 0))
        in_specs += [tab, tab]
        args += [cos, sin]
        seqs = TM // SEQ
        out_specs = [out_specs] + [
            pl.BlockSpec((seqs, width, SEQ), lambda i: (jnp.minimum(i, CTX_TILES - 1), 0, 0)) if transposed
            else _pair_specs(width)[0] for _, width, transposed in cache_cols]
        out_shape = [jax.ShapeDtypeStruct((N_TOK, n), BF16)] + [
            jax.ShapeDtypeStruct((BATCH, width, SEQ) if transposed else (N_CTX, width), F32)
            for _, width, transposed in cache_cols]
    return pl.pallas_call(
        functools.partial(_inproj_kernel, rope_groups=rope_groups, cache_cols=cache_cols),
        grid=(N_TILES,),
        in_specs=in_specs,
        out_specs=out_specs,
        out_shape=out_shape,
        scratch_shapes=[pltpu.VMEM((D_MODEL, n), BF16)],
        compiler_params=_cparams(("arbitrary",)),
        name="inproj_rope" if rope is not None else "inproj",
    )(*args)


def _rope_tables():
    s = np.arange(DEC_SEQ)
    row, col = (s // GRID_W).astype(np.float64), (s % GRID_W).astype(np.float64)
    nf = HEAD_DIM // 4
    inv = np.power(ROPE_BASE, -np.arange(nf, dtype=np.float64) / nf)
    ang = np.concatenate([row[:, None] * inv, row[:, None] * inv, col[:, None] * inv, col[:, None] * inv], axis=1)
    sign = np.concatenate([-np.ones(nf), np.ones(nf), -np.ones(nf), np.ones(nf)])
    cos = np.tile(np.cos(ang), (1, LANE // HEAD_DIM))
    sin = np.tile(np.sin(ang) * sign, (1, LANE // HEAD_DIM))
    return jnp.asarray(cos, F32), jnp.asarray(sin, F32)


RG_T = 256
RG_TILES = D_RG // MXU_DIM


def _sigmoid(x):
    return 0.5 * jnp.tanh(0.5 * x) + 0.5


def _softplus(x):
    u = jnp.exp(-jnp.abs(x))
    w = 1.0 + u
    l1p = jnp.where(w == 1.0, u, jnp.log(w) * (u / jnp.where(w == 1.0, 1.0, w - 1.0)))
    return jnp.maximum(x, 0.0) + l1p


def _rglru_kernel(xa_ref, ga_ref, h0_ref, cw_ref, cb_ref, wg_ref, br_ref, bi_ref, lam_ref,
                  y_ref, st_ref, hf_s, xc_s, a_s, b_s, hs_s, *, seq_len):
    nchunk = seq_len // RG_T
    cw = cw_ref[0]
    cb = cb_ref[0]

    def conv_chunk(c):
        base = pl.multiple_of(c * RG_T, RG_T)
        cur = xa_ref[pl.ds(base, RG_T), :]
        prev = xa_ref[pl.ds(pl.multiple_of(jnp.maximum(base - SUBLANE, 0), SUBLANE), SUBLANE), :]
        prev = jnp.where(c > 0, prev, 0.0)
        nxt = xa_ref[pl.ds(pl.multiple_of(jnp.minimum(base + RG_T, seq_len - SUBLANE), SUBLANE), SUBLANE), :]
        nxt = jnp.where(c < nchunk - 1, nxt, 0.0)
        win = jnp.concatenate([prev, cur, nxt], axis=0)
        xc = cb
        for k in range(CONV_W):
            off = SUBLANE - 1 + k
            xc = xc + win[off:off + RG_T] * cw[k:k + 1, :]
        return xc

    def gates(xc, z):
        xb = xc.astype(BF16)
        rs, gs = [], []
        for t in range(RG_TILES):
            zz = _dot(xb[:, t * MXU_DIM:(t + 1) * MXU_DIM], wg_ref[0, z, t].astype(BF16))
            rs.append(zz[:, :MXU_DIM])
            gs.append(zz[:, MXU_DIM:])
        r = _sigmoid(jnp.concatenate(rs, axis=1) + br_ref[0, z:z + 1, :])
        g = _sigmoid(jnp.concatenate(gs, axis=1) + bi_ref[0, z:z + 1, :])
        log_a = (-RG_C * _softplus(-lam_ref[0, z:z + 1, :])) * r
        a = jnp.exp(log_a)
        y = -jnp.tanh(log_a) * (a * a + 1.0)
        bt = jnp.where(y > 0.0, y * lax.rsqrt(y), 0.0) * g * xc
        return a, bt

    def scan_chunk(h, reverse):
        def step(t, h):
            tt = RG_T - 1 - t if reverse else t
            h = a_s[pl.ds(tt, 1), :] * h + b_s[pl.ds(tt, 1), :]
            hs_s[pl.ds(tt, 1), :] = h
            return h
        return lax.fori_loop(0, RG_T, step, h, unroll=8)

    def fwd_chunk(c, h):
        rows = pl.ds(pl.multiple_of(c * RG_T, RG_T), RG_T)
        xc = conv_chunk(c)
        xc_s[rows, :] = xc
        a, bt = gates(xc, 0)
        a_s[...] = a
        b_s[...] = bt
        h = scan_chunk(h, False)
        hf_s[rows, :] = hs_s[...]
        return h

    def bwd_chunk(k, h):
        c = nchunk - 1 - k
        rows = pl.ds(pl.multiple_of(c * RG_T, RG_T), RG_T)
        a, bt = gates(xc_s[rows, :], 1)
        a_s[...] = a
        b_s[...] = bt
        h = scan_chunk(h, True)
        y_ref[rows, :] = ((hf_s[rows, :] + hs_s[...]) * jax.nn.gelu(ga_ref[rows, :].astype(F32))).astype(BF16)
        return h

    hf = lax.fori_loop(0, nchunk, fwd_chunk, h0_ref[0, 0:1, :])
    hb = lax.fori_loop(0, nchunk, bwd_chunk, h0_ref[0, 1:2, :])
    st_ref[0, 0:1, :] = hf
    st_ref[0, 1:2, :] = hb


def _rglru(xa, gz, h0, j, cw, cb, wg, br, bi, lam, *, seq_len, nseq, row_block0):
    wspec3 = lambda shape: pl.BlockSpec((1,) + shape, lambda b: (j,) + (0,) * len(shape))
    return pl.pallas_call(
        functools.partial(_rglru_kernel, seq_len=seq_len),
        grid=(nseq,),
        in_specs=[
            pl.BlockSpec((seq_len, D_RG), lambda b: (row_block0 + b, 0)),
            pl.BlockSpec((seq_len, D_RG), lambda b: (row_block0 + b, 0)),
            pl.BlockSpec((1, 2, D_RG), lambda b: (b, 0, 0)),
            wspec3((CONV_W, D_RG)),
            wspec3((1, D_RG)),
            wspec3((2, RG_TILES, MXU_DIM, 2 * MXU_DIM)),
            wspec3((2, D_RG)),
            wspec3((2, D_RG)),
            wspec3((2, D_RG)),
        ],
        out_specs=[
            pl.BlockSpec((seq_len, D_RG), lambda b: (b, 0)),
            pl.BlockSpec((1, 2, D_RG), lambda b: (b, 0, 0)),
        ],
        out_shape=[
            jax.ShapeDtypeStruct((nseq * seq_len, D_RG), BF16),
            jax.ShapeDtypeStruct((nseq, 2, D_RG), F32),
        ],
        scratch_shapes=[
            pltpu.VMEM((seq_len, D_RG), F32),
            pltpu.VMEM((seq_len, D_RG), F32),
            pltpu.VMEM((RG_T, D_RG), F32),
            pltpu.VMEM((RG_T, D_RG), F32),
            pltpu.VMEM((RG_T, D_RG), F32),
        ],
        compiler_params=_cparams(("arbitrary",)),
        name=f"rglru_s{seq_len}",
    )(xa, gz, h0, cw, cb, wg, br, bi, lam)


def _gate_tiles(w_r, w_i):
    per_tile = MXU_DIM // RG_BLOCK
    eye = jnp.eye(per_tile, dtype=F32)

    def tiles(w):
        w = w.reshape(w.shape[0], 2, RG_TILES, per_tile, RG_BLOCK, RG_BLOCK)
        return jnp.einsum('nztbcd,be->nztbced', w, eye).reshape(w.shape[0], 2, RG_TILES, MXU_DIM, MXU_DIM)

    return jnp.concatenate([tiles(w_r), tiles(w_i)], axis=-1)


FNET_ROW_TILE = 512


def _fnet_kernel(z_ref, cc_ref, sc_ref, w_ref, cs_ref, ns_ref, y_ref, p_s, q_s):
    b = pl.program_id(1)

    @pl.when(pl.program_id(0) == 0)
    def _():
        w = w_ref[0]
        a = _dot3(cc_ref[...], w).astype(BF16)
        bm = _dot3(sc_ref[...], w).astype(BF16)
        zb = z_ref[...].astype(BF16)
        p_s[b] = _dot(zb, a).astype(BF16)
        q_s[b] = _dot(zb, bm).astype(BF16)

    y_ref[...] = (_dot(cs_ref[...].astype(BF16), p_s[b])
                  + _dot(ns_ref[...].astype(BF16), q_s[b])).astype(BF16)


def _fnet(gz, j, wbd, *, seq_len, nseq, row_block0):
    tr = min(seq_len, FNET_ROW_TILE)
    nrt = seq_len // tr
    cc, sc = _dft_tables(FNET_GROUP_DIM)
    ccbd = jnp.asarray(np.kron(np.eye(FNET_GROUPS), cc), F32)
    scbd = jnp.asarray(np.kron(np.eye(FNET_GROUPS), sc), F32)
    cs, ss = _dft_tables(seq_len)
    cs_b, ns_b = jnp.asarray(cs, F32), jnp.asarray(-ss, F32)
    zcol = D_RG // D_FNET
    const = lambda shape: pl.BlockSpec(shape, lambda r, b: (0,) * len(shape))
    return pl.pallas_call(
        _fnet_kernel,
        grid=(nrt, nseq),
        in_specs=[
            pl.BlockSpec((seq_len, D_FNET), lambda r, b: (row_block0 + b, zcol)),
            const((D_FNET, D_FNET)),
            const((D_FNET, D_FNET)),
            pl.BlockSpec((1, D_FNET, D_FNET), lambda r, b: (j, 0, 0)),
            pl.BlockSpec((tr, seq_len), lambda r, b: (r, 0)),
            pl.BlockSpec((tr, seq_len), lambda r, b: (r, 0)),
        ],
        out_specs=pl.BlockSpec((tr, D_FNET), lambda r, b: (b * nrt + r, 0)),
        out_shape=jax.ShapeDtypeStruct((nseq * seq_len, D_FNET), BF16),
        scratch_shapes=[pltpu.VMEM((nseq, seq_len, D_FNET), BF16), pltpu.VMEM((nseq, seq_len, D_FNET), BF16)],
        compiler_params=_cparams(("arbitrary", "arbitrary")),
        name=f"fnet_s{seq_len}",
    )(gz, ccbd, scbd, wbd, cs_b, ns_b)


def _dft_tables(n):
    k = np.arange(n)
    ang = 2.0 * np.pi * ((k[:, None] * k[None, :]) % n) / n
    return np.cos(ang) / np.sqrt(n), np.sin(ang) / np.sqrt(n)


def _fnet_blockdiag(w_f):
    eye = jnp.eye(FNET_GROUPS, dtype=F32)
    return jnp.einsum('ngcd,gh->ngchd', w_f, eye).reshape(w_f.shape[0], D_FNET, D_FNET)


def _outproj_kernel(ac_ref, al_ref, bc_ref, bl_ref, w_ref, *refs, ka):
    i = pl.program_id(0)
    x_refs, refs = refs[:-8], refs[-8:]
    mod_ref, g_ref, be_ref, wr_ref, x1_ref, u2_ref, aff_ref, wbf_ref = refs

    @pl.when(i == 0)
    def _():
        wbf_ref[...] = w_ref[0].astype(BF16)

    out = _dot(_pick(i, ac_ref, al_ref), wbf_ref[0:ka, :]) + _dot(_pick(i, bc_ref, bl_ref), wbf_ref[ka:, :])
    m = mod_ref[0, 0]
    x = _pick(i, *x_refs) if len(x_refs) == 2 else x_refs[0][...]
    x1 = _layernorm(ALPHA * x + m[2:3, :] * out, g_ref[0, 0:1, :], be_ref[0, 0:1, :])
    x1_ref[...] = x1
    u2 = x1 * (1.0 + m[4:5, :]) + m[3:4, :]
    u2_ref[...] = u2.astype(BF16)
    uh, ul = _split(u2)
    wh, wl = _split(wr_ref[0])
    lg = _dot(jnp.concatenate([uh, ul], axis=0), jnp.concatenate([wh, wl], axis=1))
    logits = (lg[:TM, :N_EXPERTS] + lg[:TM, N_EXPERTS:]) + (lg[TM:, :N_EXPERTS] + lg[TM:, N_EXPERTS:])
    mx = jnp.max(logits, axis=-1, keepdims=True)
    ex = jnp.exp(logits - mx)
    aff = ex / jnp.sum(ex, axis=-1, keepdims=True)
    aff = jnp.concatenate([aff, jnp.zeros((TM, LANE - N_EXPERTS), F32)], axis=1)
    aff_ref[...] = aff.T[:N_EXPERTS, :]


def _outproj(a_pair, b_pair, w_out, j, x, mod, layer, ln_g, ln_b, w_router):
    ka, kb = a_pair[0].shape[1], b_pair[0].shape[1]
    x_specs, x_args = _x_operand(x)
    return pl.pallas_call(
        functools.partial(_outproj_kernel, ka=ka),
        grid=(N_TILES,),
        in_specs=_pair_specs(ka) + _pair_specs(kb) + [
            pl.BlockSpec((1, D_MODEL, D_MODEL), lambda i: (j, 0, 0)),
        ] + x_specs + [
            pl.BlockSpec((1, 1, 6, D_MODEL), lambda i: (layer, _seg_of_tile(i), 0, 0)),
            pl.BlockSpec((1, 2, D_MODEL), lambda i: (layer, 0, 0)),
            pl.BlockSpec((1, 2, D_MODEL), lambda i: (layer, 0, 0)),
            pl.BlockSpec((1, D_MODEL, N_EXPERTS), lambda i: (layer, 0, 0)),
        ],
        out_specs=[
            pl.BlockSpec((TM, D_MODEL), lambda i: (i, 0)),
            pl.BlockSpec((TM, D_MODEL), lambda i: (i, 0)),
            pl.BlockSpec((N_EXPERTS, TM), lambda i: (0, i)),
        ],
        out_shape=[
            jax.ShapeDtypeStruct((N_TOK, D_MODEL), F32),
            jax.ShapeDtypeStruct((N_TOK, D_MODEL), BF16),
            jax.ShapeDtypeStruct((N_EXPERTS, N_TOK), F32),
        ],
        scratch_shapes=[pltpu.VMEM((D_MODEL, D_MODEL), BF16)],
        compiler_params=_cparams(("arbitrary",)),
        name="outproj_ln_router",
    )(*a_pair, *b_pair, w_out, *x_args, mod, ln_g, ln_b, w_router)


def _topk_kernel(aff_ref, pos_ref, posn_ref, cnt_ref):
    aff = aff_ref[...]
    cap = jnp.float32(CAP)

    def count_ge(t):
        return jnp.sum(jnp.where(aff >= t, 1.0, 0.0), axis=1, keepdims=True)

    lo_v = jnp.zeros((N_EXPERTS, 1), F32)
    hi_v = jnp.full((N_EXPERTS, 1), 2.0, F32)
    lo_t = jnp.full((N_EXPERTS, 1), TOPK_MIN_EXP, F32)
    hi_t = jnp.full((N_EXPERTS, 1), 1.0, F32)
    for _ in range(TOPK_GEO_STEPS):
        mid_t = 0.5 * (lo_t + hi_t)
        cand = jnp.exp2(mid_t)
        ok = count_ge(cand) >= cap
        lo_t = jnp.where(ok, mid_t, lo_t)
        hi_t = jnp.where(ok, hi_t, mid_t)
        lo_v = jnp.where(ok, jnp.maximum(lo_v, cand), lo_v)
        hi_v = jnp.where(ok, hi_v, jnp.minimum(hi_v, cand))
    for _ in range(TOPK_LIN_STEPS):
        cand = lo_v + 0.5 * (hi_v - lo_v)
        ok = count_ge(cand) >= cap
        lo_v = jnp.where(ok, cand, lo_v)
        hi_v = jnp.where(ok, hi_v, cand)
    gt = aff >= hi_v
    eq = (aff >= lo_v) & (aff < hi_v)
    need = cap - jnp.sum(jnp.where(gt, 1.0, 0.0), axis=1, keepdims=True)

    tri = (lax.broadcasted_iota(I32, (LANE, LANE), 0) <= lax.broadcasted_iota(I32, (LANE, LANE), 1))
    tri = jnp.where(tri, 1.0, 0.0).astype(BF16)

    def cumsum_blocks(mask_f32):
        carry = jnp.zeros((N_EXPERTS, 1), F32)
        outs = []
        for k in range(mask_f32.shape[1] // LANE):
            blk = mask_f32[:, k * LANE:(k + 1) * LANE]
            inc = _dot(blk.astype(BF16), tri) + carry
            outs.append(inc)
            carry = carry + jnp.sum(blk, axis=1, keepdims=True)
        return outs

    eq_f = jnp.where(eq, 1.0, 0.0)
    gt_f = jnp.where(gt, 1.0, 0.0)
    eq_rank = cumsum_blocks(eq_f)
    sel_blocks = []
    for k, rk in enumerate(eq_rank):
        lanes = slice(k * LANE, (k + 1) * LANE)
        sel_blocks.append(jnp.where((eq_f[:, lanes] > 0.0) & (rk <= need), 1.0, gt_f[:, lanes]))
    sel = jnp.concatenate(sel_blocks, axis=1)
    sel_rank = cumsum_blocks(sel)
    pad = jnp.zeros((LANE - N_EXPERTS, LANE), F32)
    for k, rk in enumerate(sel_rank):
        lanes = slice(k * LANE, (k + 1) * LANE)
        pos = jnp.where(sel[:, lanes] > 0.0, rk - 1.0, -1.0)
        pos_ref[:, lanes] = pos.astype(I32)
        posn_ref[lanes, :] = jnp.concatenate([pos, pad], axis=0).T[:, :N_EXPERTS].astype(I32)
    lane = lax.broadcasted_iota(I32, (N_EXPERTS, LANE), 1)
    cnt = jnp.zeros((N_EXPERTS, LANE), F32)
    for c in range(1, N_CHUNKS + 1):
        cnt = jnp.where(lane == c, sel_rank[c * (GC // LANE) - 1][:, LANE - 1:LANE], cnt)
    cnt_ref[0] = cnt.astype(I32)


def _topk(aff_t):
    return pl.pallas_call(
        _topk_kernel,
        grid=(2,),
        in_specs=[pl.BlockSpec((N_EXPERTS, N_CTX), lambda g: (0, g))],
        out_specs=[
            pl.BlockSpec((N_EXPERTS, N_CTX), lambda g: (0, g)),
            pl.BlockSpec((N_CTX, N_EXPERTS), lambda g: (g, 0)),
            pl.BlockSpec((1, N_EXPERTS, LANE), lambda g: (g, 0, 0)),
        ],
        out_shape=[
            jax.ShapeDtypeStruct((N_EXPERTS, N_TOK), I32),
            jax.ShapeDtypeStruct((N_TOK, N_EXPERTS), I32),
            jax.ShapeDtypeStruct((2, N_EXPERTS, LANE), I32),
        ],
        compiler_params=_cparams(("arbitrary",)),
        name="topk_select",
    )(aff_t)


def _chunk_windows(cnt_ref, g, e, c):
    c0 = cnt_ref[g, e, c]
    c1 = cnt_ref[g, e, c + 1]
    base = (c0 // WIN_ALIGN) * WIN_ALIGN
    k = jnp.where(c1 > c0, (c1 - base + WIN - 1) // WIN, 0)
    return base, k


def _gather_kernel(cnt_ref, pos_ref, aff_ref, u_ref, xe_ref, gs_ref):
    g = pl.program_id(0)
    c = pl.program_id(1)

    @pl.when(c == 0)
    def _():
        xe_ref[...] = jnp.zeros_like(xe_ref)
        gs_ref[...] = jnp.zeros_like(gs_ref)

    pos = pos_ref[...]
    aff = aff_ref[...]
    slot0 = lax.broadcasted_iota(I32, (WIN, GC), 0)

    def window(e, lo):
        start = pl.multiple_of(jnp.minimum(lo, CAP - WIN), WIN_ALIGN)
        pe = pos[e:e + 1, :]
        hit = (slot0 + start) == jnp.where(pe >= lo, pe, -1)
        gates = jnp.sum(jnp.where(hit, aff[e:e + 1, :], 0.0), axis=1, keepdims=True)
        return start, jnp.where(hit, 1.0, 0.0).astype(BF16), gates

    def add_rows(e, start, rows, gates):
        dst = (e, 0, pl.ds(start, WIN), slice(None))
        xe_ref[dst] = (xe_ref[dst].astype(F32) + rows).astype(BF16)
        gs_ref[dst] += gates

    windows = [_chunk_windows(cnt_ref, g, e, c) for e in range(N_EXPERTS)]
    firsts = [window(e, base) for e, (base, _) in enumerate(windows)]
    rows = _dot(jnp.concatenate([oh for _, oh, _ in firsts], axis=0), u_ref[...])
    for e, (start, _, gates) in enumerate(firsts):
        add_rows(e, start, rows[e * WIN:(e + 1) * WIN], gates)
    for e, (base, k) in enumerate(windows):
        def extra(w, carry, e=e, base=base):
            start, oh, gates = window(e, base + w * WIN)
            add_rows(e, start, _dot(oh, u_ref[...]), gates)
            return carry
        lax.fori_loop(1, k, extra, 0)


def _gather(cnt, pos_t, aff_t, u2):
    return pl.pallas_call(
        _gather_kernel,
        grid_spec=pltpu.PrefetchScalarGridSpec(
            num_scalar_prefetch=1,
            grid=(2, N_CHUNKS),
            in_specs=[
                pl.BlockSpec((N_EXPERTS, GC), lambda g, c, cnt: (0, g * N_CHUNKS + c)),
                pl.BlockSpec((N_EXPERTS, GC), lambda g, c, cnt: (0, g * N_CHUNKS + c)),
                pl.BlockSpec((GC, D_MODEL), lambda g, c, cnt: (g * N_CHUNKS + c, 0)),
            ],
            out_specs=[
                pl.BlockSpec((N_EXPERTS, 1, CAP, D_MODEL), lambda g, c, cnt: (0, g, 0, 0)),
                pl.BlockSpec((N_EXPERTS, 1, CAP, 1), lambda g, c, cnt: (0, g, 0, 0)),
            ],
        ),
        out_shape=[
            jax.ShapeDtypeStruct((N_EXPERTS, 2, CAP, D_MODEL), BF16),
            jax.ShapeDtypeStruct((N_EXPERTS, 2, CAP, 1), F32),
        ],
        compiler_params=_cparams(("arbitrary", "arbitrary")),
        name="moe_gather",
    )(cnt, pos_t, aff_t, u2)


TF = 1024


def _ffn_kernel(xe_ref, gs_ref, wg_ref, wu_ref, wd_ref, ye_ref, acc_ref):
    f = pl.program_id(1)
    x = xe_ref[0]
    hg = _dot(x, wg_ref[0, 0].astype(BF16))
    hu = _dot(x, wu_ref[0, 0].astype(BF16))
    h = (hg * _sigmoid(hg) * hu).astype(BF16)

    @pl.when(f == 0)
    def _():
        acc_ref[...] = _dot(h, wd_ref[0, 0].astype(BF16))

    @pl.when(f > 0)
    def _():
        acc_ref[...] += _dot(h, wd_ref[0, 0].astype(BF16))

    @pl.when(f == pl.num_programs(1) - 1)
    def _():
        ye_ref[0] = (acc_ref[...] * gs_ref[0]).astype(BF16)


def _ffn(xe, gs, layer, w_gate, w_up, w_down):
    rows = 2 * CAP
    return pl.pallas_call(
        _ffn_kernel,
        grid=(N_EXPERTS, D_EXPERT // TF),
        in_specs=[
            pl.BlockSpec((1, rows, D_MODEL), lambda e, f: (e, 0, 0)),
            pl.BlockSpec((1, rows, 1), lambda e, f: (e, 0, 0)),
            pl.BlockSpec((1, 1, D_MODEL, TF), lambda e, f: (layer, e, 0, f)),
            pl.BlockSpec((1, 1, D_MODEL, TF), lambda e, f: (layer, e, 0, f)),
            pl.BlockSpec((1, 1, TF, D_MODEL), lambda e, f: (layer, e, f, 0)),
        ],
        out_specs=pl.BlockSpec((1, rows, D_MODEL), lambda e, f: (e, 0, 0)),
        out_shape=jax.ShapeDtypeStruct((N_EXPERTS, rows, D_MODEL), BF16),
        scratch_shapes=[pltpu.VMEM((rows, D_MODEL), F32)],
        compiler_params=_cparams(("arbitrary", "arbitrary")),
        name="moe_ffn",
    )(xe, gs, w_gate, w_up, w_down)


def _combine_kernel(*refs, split_out):
    if split_out:
        cnt_ref, posn_ref, ye_ref, x1_ref, mod_ref, g_ref, be_ref, oc_ref, ol_ref, acc_s = refs
    else:
        cnt_ref, posn_ref, ye_ref, x1_ref, mod_ref, g_ref, be_ref, o_ref, acc_s = refs
    i = pl.program_id(0)
    g = i // N_CHUNKS
    c = i % N_CHUNKS
    posn = posn_ref[...]
    lane = lax.broadcasted_iota(I32, (1, MXU_DIM), 1)
    part = lane // WIN
    slot_in_win = lane % WIN

    def by_part(vals):
        out = vals[-1]
        for q in range(len(vals) - 2, -1, -1):
            out = jnp.where(part == q, vals[q], out)
        return out

    def start_of(lo):
        return pl.multiple_of(jnp.minimum(lo, CAP - WIN), WIN_ALIGN)

    def group_product(experts, los):
        starts = [start_of(lo) for lo in los]
        slots = by_part(starts) + slot_in_win
        cols = by_part([jnp.where(posn[:, e:e + 1] >= lo, posn[:, e:e + 1], -1) for e, lo in zip(experts, los)])
        onehot = jnp.where(slots == cols, 1.0, 0.0).astype(BF16)
        rows = jnp.concatenate([ye_ref[e, 0, pl.ds(st, WIN), :] for e, st in zip(experts, starts)], axis=0)
        return _dot(onehot, rows)

    windows = [_chunk_windows(cnt_ref, g, e, c) for e in range(N_EXPERTS)]
    acc = None
    for e0 in range(0, N_EXPERTS, WINS_PER_DOT):
        experts = list(range(e0, e0 + WINS_PER_DOT))
        d = group_product(experts, [windows[e][0] for e in experts])
        acc = d if acc is None else acc + d
    acc_s[...] = acc
    slot0 = lax.broadcasted_iota(I32, (GC, WIN), 1)
    for e, (base, k) in enumerate(windows):
        def extra(w, carry, e=e, base=base):
            lo = base + w * WIN
            start = start_of(lo)
            col = posn[:, e:e + 1]
            hit = (slot0 + start) == jnp.where(col >= lo, col, -1)
            acc_s[...] += _dot(jnp.where(hit, 1.0, 0.0).astype(BF16), ye_ref[e, 0, pl.ds(start, WIN), :])
            return carry
        lax.fori_loop(1, k, extra, 0)
    m = mod_ref[0, 0]
    y = ALPHA * x1_ref[...] + m[5:6, :] * acc_s[...]
    res = _layernorm(y, g_ref[0, 1:2, :], be_ref[0, 1:2, :])
    if split_out:
        @pl.when(i < N_CHUNKS)
        def _():
            oc_ref[...] = res

        @pl.when(i >= N_CHUNKS)
        def _():
            ol_ref[...] = res
    else:
        o_ref[...] = res


def _combine(cnt, pos_n, ye, x1, mod, layer, ln_g, ln_b, split_out):
    if split_out:
        out_specs = _pair_specs(D_MODEL, tile=GC)
        out_shape = [jax.ShapeDtypeStruct((N_CTX, D_MODEL), F32), jax.ShapeDtypeStruct((N_LAT, D_MODEL), F32)]
    else:
        out_specs = pl.BlockSpec((GC, D_MODEL), lambda i, cnt: (i, 0))
        out_shape = jax.ShapeDtypeStruct((N_TOK, D_MODEL), F32)
    return pl.pallas_call(
        functools.partial(_combine_kernel, split_out=split_out),
        grid_spec=pltpu.PrefetchScalarGridSpec(
            num_scalar_prefetch=1,
            grid=(N_TOK // GC,),
            in_specs=[
                pl.BlockSpec((GC, N_EXPERTS), lambda i, cnt: (i, 0)),
                pl.BlockSpec((N_EXPERTS, 1, CAP, D_MODEL), lambda i, cnt: (0, i // N_CHUNKS, 0, 0)),
                pl.BlockSpec((GC, D_MODEL), lambda i, cnt: (i, 0)),
                pl.BlockSpec((1, 1, 6, D_MODEL), lambda i, cnt: (layer, _seg_of_tile(i, GC), 0, 0)),
                pl.BlockSpec((1, 2, D_MODEL), lambda i, cnt: (layer, 0, 0)),
                pl.BlockSpec((1, 2, D_MODEL), lambda i, cnt: (layer, 0, 0)),
            ],
            out_specs=out_specs,
            scratch_shapes=[pltpu.VMEM((GC, D_MODEL), F32)],
        ),
        out_shape=out_shape,
        compiler_params=_cparams(("arbitrary",)),
        name="moe_combine_ln",
    )(cnt, pos_n, ye, x1, mod, ln_g, ln_b)


def _moe(x1, u2, aff_t, mod, layer, ln_g, ln_b, w_gate, w_up, w_down, split_out=False):
    pos_t, pos_n, cnt = _topk(aff_t)
    xe, gs = _gather(cnt, pos_t, aff_t, u2)
    ye = _ffn(xe.reshape(N_EXPERTS, 2 * CAP, D_MODEL), gs.reshape(N_EXPERTS, 2 * CAP, 1),
              layer, w_gate, w_up, w_down)
    return _combine(cnt, pos_n, ye.reshape(N_EXPERTS, 2, CAP, D_MODEL), x1, mod, layer, ln_g, ln_b, split_out)


def _sink_softmax_pv(s, sink_col, v):
    m = jnp.maximum(jnp.max(s, axis=-1, keepdims=True), sink_col)
    p = jnp.exp(s - m)
    den = jnp.sum(p, axis=-1, keepdims=True) + jnp.exp(sink_col - m)
    return _dot(p.astype(BF16), v) / den


def _sink_column(sink_ref, h, rows_per_group):
    rid = lax.broadcasted_iota(I32, (GQA_GROUP * rows_per_group, 1), 0) // rows_per_group
    col = jnp.zeros((GQA_GROUP * rows_per_group, 1), F32)
    for g in range(GQA_GROUP):
        col = jnp.where(rid == g, sink_ref[h * GQA_GROUP + g], col)
    return col


def _stack_groups(q, h):
    return jnp.concatenate(
        [q[:, (h * GQA_GROUP + g) * HEAD_DIM:(h * GQA_GROUP + g + 1) * HEAD_DIM] * ATTN_SCALE
         for g in range(GQA_GROUP)], axis=0).astype(BF16)


def _unstack_groups(outs, rows):
    return jnp.concatenate([o[g * rows:(g + 1) * rows] for o in outs for g in range(GQA_GROUP)], axis=1)


def _swa_ctx_kernel(sink_ref, q_ref, k_ref, v_ref, o_ref):
    q, k, v = q_ref[...], k_ref[...], v_ref[...]
    outs = []
    for h in range(SWA_KV_HEADS):
        cols = slice(h * HEAD_DIM, (h + 1) * HEAD_DIM)
        s = _dot_nt(_stack_groups(q, h), k[:, cols].astype(BF16))
        outs.append(_sink_softmax_pv(s, _sink_column(sink_ref, h, SEQ), v[:, cols].astype(BF16)))
    o_ref[...] = _unstack_groups(outs, SEQ).astype(BF16)


def _swa_ctx(proj, sink):
    return pl.pallas_call(
        _swa_ctx_kernel,
        grid=(BATCH,),
        in_specs=[
            pl.BlockSpec(memory_space=pltpu.SMEM),
            pl.BlockSpec((SEQ, SWA_Q_W), lambda b: (b, 0)),
            pl.BlockSpec((SEQ, SWA_KV_W), lambda b: (b, SWA_K_COL // SWA_KV_W)),
            pl.BlockSpec((SEQ, SWA_KV_W), lambda b: (b, SWA_V_COL // SWA_KV_W)),
        ],
        out_specs=pl.BlockSpec((SEQ, SWA_Q_W), lambda b: (b, 0)),
        out_shape=jax.ShapeDtypeStruct((N_CTX, SWA_Q_W), BF16),
        compiler_params=_cparams(("arbitrary",)),
        name="swa_context",
    )(sink, proj, proj, proj)


QB = 128


def _swa_lat_kernel(sink_ref, q_ref, kp_ref, kc_ref, kn_ref, vp_ref, vc_ref, vn_ref, ck_ref, cv_ref, o_ref):
    qb = pl.program_id(1)
    q = q_ref[...]
    kall = jnp.concatenate([ck_ref[0].astype(BF16), kp_ref[...], kc_ref[...], kn_ref[...]], axis=0)
    vall = jnp.concatenate([cv_ref[0].astype(BF16), vp_ref[...], vc_ref[...], vn_ref[...]], axis=0)
    nk = PAST_LEN + 3 * QB
    col = lax.broadcasted_iota(I32, (QB, nk), 1)
    qpos = qb * QB + lax.broadcasted_iota(I32, (QB, nk), 0)
    kpos = (qb - 1) * QB + col - PAST_LEN
    ok = (col < PAST_LEN) | ((jnp.abs(qpos - kpos) <= WINDOW) & (kpos >= 0) & (kpos < DEC_SEQ))
    bias = jnp.concatenate([jnp.where(ok, 0.0, NEG_INF)] * GQA_GROUP, axis=0)
    outs = []
    for h in range(SWA_KV_HEADS):
        cols = slice(h * HEAD_DIM, (h + 1) * HEAD_DIM)
        s = _dot_nt(_stack_groups(q, h), kall[:, cols]) + bias
        outs.append(_sink_softmax_pv(s, _sink_column(sink_ref, h, QB), vall[:, cols]))
    o_ref[...] = _unstack_groups(outs, QB).astype(BF16)


def _swa_lat(proj, cache_k, cache_v, sink):
    nqb = DEC_SEQ // QB
    row0 = N_CTX // QB

    def blk(col, shift):
        return pl.BlockSpec((QB, SWA_KV_W),
                            lambda b, i: (row0 + b * nqb + jnp.clip(i + shift, 0, nqb - 1), col // SWA_KV_W))

    return pl.pallas_call(
        _swa_lat_kernel,
        grid=(DEC_BATCH, nqb),
        in_specs=[
            pl.BlockSpec(memory_space=pltpu.SMEM),
            pl.BlockSpec((QB, SWA_Q_W), lambda b, i: (row0 + b * nqb + i, 0)),
            blk(SWA_K_COL, -1), blk(SWA_K_COL, 0), blk(SWA_K_COL, 1),
            blk(SWA_V_COL, -1), blk(SWA_V_COL, 0), blk(SWA_V_COL, 1),
            pl.BlockSpec((1, PAST_LEN, SWA_KV_W), lambda b, i: (b, 0, 0)),
            pl.BlockSpec((1, PAST_LEN, SWA_KV_W), lambda b, i: (b, 0, 0)),
        ],
        out_specs=pl.BlockSpec((QB, SWA_Q_W), lambda b, i: (b * nqb + i, 0)),
        out_shape=jax.ShapeDtypeStruct((N_LAT, SWA_Q_W), BF16),
        compiler_params=_cparams(("arbitrary", "arbitrary")),
        name="swa_latent",
    )(sink, proj, proj, proj, proj, proj, proj, proj, cache_k, cache_v)


def _diff_lambda_col(lam_ref, lam_init):
    l1 = jnp.sum(lam_ref[0, 0] * lam_ref[0, 1], axis=-1, keepdims=True)
    l2 = jnp.sum(lam_ref[0, 2] * lam_ref[0, 3], axis=-1, keepdims=True)
    return jnp.exp(l1) - jnp.exp(l2) + lam_init


def _softmax_pv(q, k, v):
    s = _dot_nt(q, k)
    p = jnp.exp(s - jnp.max(s, axis=-1, keepdims=True))
    return _dot(p.astype(BF16), v) / jnp.sum(p, axis=-1, keepdims=True)


def _diff_head(q, k, v, lam, subln, lam_init):
    o = [_softmax_pv((q[:, m * HEAD_DIM:(m + 1) * HEAD_DIM] * ATTN_SCALE).astype(BF16),
                     k[:, m * HEAD_DIM:(m + 1) * HEAD_DIM], v) for m in range(2)]
    o = o[0] - lam * o[1]
    return o * lax.rsqrt(jnp.mean(o * o, axis=-1, keepdims=True) + LN_EPS) * subln * (1.0 - lam_init)


def _diff_kernel(*refs, lam_init, has_cache, nh):
    if has_cache:
        lam_ref, sg_ref, q_ref, k_ref, v_ref, ck_ref, cv_ref, o_ref = refs
    else:
        lam_ref, sg_ref, q_ref, k_ref, v_ref, o_ref = refs
    hp = pl.program_id(1)
    lam_all = _diff_lambda_col(lam_ref, lam_init)
    hid = lax.broadcasted_iota(I32, (DIFF_HEADS, 1), 0)
    q, k, v = q_ref[...], k_ref[...], v_ref[...]
    if has_cache:
        k = jnp.concatenate([k, ck_ref[0].astype(BF16)], axis=0)
        v = jnp.concatenate([v, cv_ref[0].astype(BF16)], axis=0)
    outs = []
    for j in range(nh):
        lam = jnp.sum(jnp.where(hid == hp * nh + j, lam_all, 0.0), axis=0, keepdims=True)
        cols = slice(j * DIFF_V_DIM, (j + 1) * DIFF_V_DIM)
        outs.append(_diff_head(q[:, cols], k[:, cols], v[:, cols], lam, sg_ref[...], lam_init))
    o_ref[...] = jnp.concatenate(outs, axis=1).astype(BF16)


DIFF_STEP_HEADS = 2
DQ = 512
assert all(col % (DIFF_STEP_HEADS * DIFF_V_DIM) == 0 for col in (Q_COL, K_COL, V_COL))


def _diff_ctx(proj, lam_p, subln, j, lam_init):
    nh = DIFF_STEP_HEADS
    w = nh * DIFF_V_DIM
    return pl.pallas_call(
        functools.partial(_diff_kernel, lam_init=lam_init, has_cache=False, nh=nh),
        grid=(BATCH, DIFF_HEADS // nh),
        in_specs=[
            pl.BlockSpec((1, 4, DIFF_HEADS, HEAD_DIM), lambda b, h: (j, 0, 0, 0)),
            pl.BlockSpec((1, DIFF_V_DIM), lambda b, h: (j, 0)),
            pl.BlockSpec((SEQ, w), lambda b, h: (b, Q_COL // w + h)),
            pl.BlockSpec((SEQ, w), lambda b, h: (b, K_COL // w + h)),
            pl.BlockSpec((SEQ, w), lambda b, h: (b, V_COL // w + h)),
        ],
        out_specs=pl.BlockSpec((SEQ, w), lambda b, h: (b, h)),
        out_shape=jax.ShapeDtypeStruct((N_CTX, DIFF_W), BF16),
        compiler_params=_cparams(("arbitrary", "arbitrary")),
        name="diff_context",
    )(lam_p, subln, proj, proj, proj)


def _diff_lat(proj, cache_k, cache_v, lam_p, subln, j, lam_init):
    nh = DIFF_STEP_HEADS
    w = nh * DIFF_V_DIM
    nq = DEC_SEQ // DQ
    row0 = N_CTX // DQ
    seq0 = N_CTX // DEC_SEQ
    return pl.pallas_call(
        functools.partial(_diff_kernel, lam_init=lam_init, has_cache=True, nh=nh),
        grid=(DEC_BATCH, DIFF_HEADS // nh, nq),
        in_specs=[
            pl.BlockSpec((1, 4, DIFF_HEADS, HEAD_DIM), lambda b, h, i: (j, 0, 0, 0)),
            pl.BlockSpec((1, DIFF_V_DIM), lambda b, h, i: (j, 0)),
            pl.BlockSpec((DQ, w), lambda b, h, i: (row0 + b * nq + i, Q_COL // w + h)),
            pl.BlockSpec((DEC_SEQ, w), lambda b, h, i: (seq0 + b, K_COL // w + h)),
            pl.BlockSpec((DEC_SEQ, w), lambda b, h, i: (seq0 + b, V_COL // w + h)),
            pl.BlockSpec((1, PAST_LEN, w), lambda b, h, i: (b, 0, h)),
            pl.BlockSpec((1, PAST_LEN, w), lambda b, h, i: (b, 0, h)),
        ],
        out_specs=pl.BlockSpec((DQ, w), lambda b, h, i: (b * nq + i, h)),
        out_shape=jax.ShapeDtypeStruct((N_LAT, DIFF_W), BF16),
        compiler_params=_cparams(("arbitrary", "arbitrary", "arbitrary")),
        name="diff_latent",
    )(lam_p, subln, proj, proj, proj, cache_k, cache_v)


def kernel(x_prompt, x_sample, state_rglru, cache_swa_k, cache_swa_v, cache_diff_k, cache_diff_v, c, c_ctx,
           w_mod, b_mod, ln_g, ln_b, e_w_in, e_conv_w, e_conv_b, e_w_rgate, e_b_rgate, e_w_igate, e_b_igate,
           e_lambda, e_w_fnet, e_w_out, o_w_in, o_sink, o_lambda, o_subln_g, o_w_out,
           w_router, w_gate, w_up, w_down):
    x_pair = (x_prompt.reshape(N_CTX, D_MODEL), x_sample.reshape(N_LAT, D_MODEL))
    cvec8 = jnp.concatenate([c_ctx[None, :], c, jnp.zeros((SUBLANE - N_SEG, D_MODEL), F32)], axis=0).T
    mod = _modulation(cvec8, w_mod, b_mod).reshape(DEPTH, SUBLANE, 6, D_MODEL)

    j = 0
    xa, gz = _inproj(x_pair, mod, 0, e_w_in, j)
    wg = _gate_tiles(e_w_rgate, e_w_igate)
    cb = e_conv_b.reshape(-1, 1, D_RG)
    rg_args = (j, e_conv_w, cb, wg, e_b_rgate, e_b_igate, e_lambda)
    y_rg_c, st_c = _rglru(xa, gz, jnp.zeros((BATCH, 2, D_RG), F32), *rg_args,
                          seq_len=SEQ, nseq=BATCH, row_block0=0)
    y_rg_l, _ = _rglru(xa, gz, state_rglru[:, j], *rg_args,
                       seq_len=DEC_SEQ, nseq=DEC_BATCH, row_block0=N_CTX // DEC_SEQ)
    wbd = _fnet_blockdiag(e_w_fnet)
    y_fn_c = _fnet(gz, j, wbd, seq_len=SEQ, nseq=BATCH, row_block0=0)
    y_fn_l = _fnet(gz, j, wbd, seq_len=DEC_SEQ, nseq=DEC_BATCH, row_block0=N_CTX // DEC_SEQ)
    x1, u2, aff = _outproj((y_rg_c, y_rg_l), (y_fn_c, y_fn_l), e_w_out, j, x_pair, mod, 0, ln_g, ln_b, w_router)
    x = _moe(x1, u2, aff, mod, 0, ln_g, ln_b, w_gate, w_up, w_down)
    new_state_rglru = st_c[:, None]

    layer = 1
    lam_init = 0.8 - 0.6 * math.exp(-0.3 * layer)
    cos, sin = _rope_tables()
    rope_groups = tuple(g for g in range(ODD_IN // LANE)
                        if g * LANE < SWA_V_COL or Q_COL <= g * LANE < V_COL)
    cache_cols = ((SWA_K_COL, SWA_KV_W, True), (SWA_V_COL, SWA_KV_W, True),
                  (K_COL, DIFF_W, True), (V_COL, DIFF_W, False))
    proj, ks, vs, kd, vd = _inproj(x, mod, 1, o_w_in, j, rope=(cos, sin, rope_groups), cache_cols=cache_cols)
    sink = o_sink[j]
    ys_c = _swa_ctx(proj, sink)
    yd_c = _diff_ctx(proj, o_lambda, o_subln_g, j, lam_init)
    ck = cache_swa_k[:, j].reshape(DEC_BATCH, PAST_LEN, SWA_KV_HEADS * HEAD_DIM)
    cv = cache_swa_v[:, j].reshape(DEC_BATCH, PAST_LEN, SWA_KV_HEADS * HEAD_DIM)
    ys_l = _swa_lat(proj, ck, cv, sink)
    cdk = cache_diff_k[:, j].reshape(DEC_BATCH, PAST_LEN, DIFF_HEADS * 2 * HEAD_DIM)
    cdv = cache_diff_v[:, j].reshape(DEC_BATCH, PAST_LEN, DIFF_HEADS * DIFF_V_DIM)
    yd_l = _diff_lat(proj, cdk, cdv, o_lambda, o_subln_g, j, lam_init)
    x1, u2, aff = _outproj((ys_c, ys_l), (yd_c, yd_l), o_w_out, j, x, mod, 1, ln_g, ln_b, w_router)
    y_c, y_l = _moe(x1, u2, aff, mod, 1, ln_g, ln_b, w_gate, w_up, w_down, split_out=True)

    new_cache_swa_k = ks.reshape(BATCH, 1, SWA_KV_HEADS, HEAD_DIM, SEQ).transpose(0, 1, 4, 2, 3)
    new_cache_swa_v = vs.reshape(BATCH, 1, SWA_KV_HEADS, HEAD_DIM, SEQ).transpose(0, 1, 4, 2, 3)
    new_cache_diff_k = kd.reshape(BATCH, 1, DIFF_HEADS, 2, HEAD_DIM, SEQ).transpose(0, 1, 5, 2, 3, 4)
    new_cache_diff_v = vd.reshape(BATCH, 1, SEQ, DIFF_HEADS, DIFF_V_DIM)
    y_prompt = y_c.reshape(BATCH, SEQ, D_MODEL)
    y_sample = y_l.reshape(DEC_BATCH, DEC_SEQ, D_MODEL)
    return (y_prompt, y_sample, new_state_rglru, new_cache_swa_k, new_cache_swa_v,
            new_cache_diff_k, new_cache_diff_v)
```

```python
import functools
import math

import numpy as np
import jax
import jax.numpy as jnp
from jax import lax
from jax.experimental import pallas as pl
from jax.experimental.pallas import tpu as pltpu

F32 = jnp.float32
BF16 = jnp.bfloat16
I32 = jnp.int32

D_MODEL = 1024
BATCH, SEQ = 16, 256
DEC_BATCH, DEC_SEQ = 2, 2048
PAST_LEN = 256
DEPTH = 2
GRID_W = 64
HEAD_DIM = 64
D_RG = 768
RG_BLOCK = 64
RG_C = 8.0
CONV_W = 4
D_FNET = 256
FNET_GROUP_DIM = 64
FNET_GROUPS = 4
SWA_HEADS = 8
SWA_KV_HEADS = 2
GQA_GROUP = 4
WINDOW = 128
DIFF_HEADS = 4
DIFF_V_DIM = 128
N_EXPERTS = 16
EC_FACTOR = 2
D_EXPERT = 2048
ROPE_BASE = 10000.0
LN_EPS = 1e-5
NEG_INF = -1e30
ATTN_SCALE = HEAD_DIM ** -0.5
ALPHA = (2 * DEPTH) ** 0.25
SWA_Q_W = SWA_HEADS * HEAD_DIM
SWA_KV_W = SWA_KV_HEADS * HEAD_DIM
DIFF_W = DIFF_HEADS * DIFF_V_DIM
SWA_K_COL = SWA_Q_W
SWA_V_COL = SWA_K_COL + SWA_KV_W
Q_COL = SWA_V_COL + SWA_KV_W
K_COL = Q_COL + DIFF_W
V_COL = K_COL + DIFF_W
ODD_IN = V_COL + DIFF_W

N_CTX = BATCH * SEQ
N_LAT = DEC_BATCH * DEC_SEQ
N_TOK = N_CTX + N_LAT
N_SEG = 1 + DEC_BATCH
CAP = EC_FACTOR * N_CTX // N_EXPERTS
assert N_CTX == N_LAT

LANE = 128
SUBLANE = 8
MXU_DIM = 256
VMEM_LIMIT = 56 * 1024 * 1024
TM = 512
GC = 256
N_CHUNKS = N_CTX // GC
WIN = 64
WIN_ALIGN = 16
WINS_PER_DOT = MXU_DIM // WIN
TOPK_MIN_EXP = -150.0
TOPK_GEO_STEPS = 16
TOPK_LIN_STEPS = 24
N_TILES = N_TOK // TM
CTX_TILES = N_CTX // TM
LAT_TILES_PER_SEQ = DEC_SEQ // TM


def _cparams(sem):
    return pltpu.CompilerParams(dimension_semantics=sem, vmem_limit_bytes=VMEM_LIMIT)


def _dot(a, b):
    return jnp.dot(a, b, preferred_element_type=F32)


def _dot_nt(a, b):
    return lax.dot_general(a, b, (((1,), (1,)), ((), ())), preferred_element_type=F32)


def _split(a):
    hi = a.astype(BF16)
    lo = (a - hi.astype(F32)).astype(BF16)
    return hi, lo


def _dot3(a, b):
    ah, al = _split(a)
    bh, bl = _split(b)
    return _dot(ah, bh) + (_dot(ah, bl) + _dot(al, bh))


def _layernorm(y, g, b):
    mu = jnp.mean(y, axis=-1, keepdims=True)
    d = y - mu
    var = jnp.mean(d * d, axis=-1, keepdims=True)
    return d * lax.rsqrt(var + LN_EPS) * g + b


def _pair_specs(width, tile=TM):
    ctx_tiles = N_CTX // tile
    return [
        pl.BlockSpec((tile, width), lambda i, *_: (jnp.minimum(i, ctx_tiles - 1), 0)),
        pl.BlockSpec((tile, width), lambda i, *_: (jnp.maximum(i - ctx_tiles, 0), 0)),
    ]


def _x_operand(x):
    if isinstance(x, tuple):
        return _pair_specs(D_MODEL), list(x)
    return [pl.BlockSpec((TM, D_MODEL), lambda i: (i, 0))], [x]


def _pick(i, ctx_ref, lat_ref):
    return jnp.where(i < CTX_TILES, ctx_ref[...], lat_ref[...])


def _seg_of_tile(i, tile=TM):
    ctx_tiles = N_CTX // tile
    return jnp.where(i < ctx_tiles, 0, (i - ctx_tiles) // (DEC_SEQ // tile) + 1)


MOD_TN = 1536


def _mod_kernel(ct_ref, w_ref, b_ref, o_ref):
    c = ct_ref[...]
    s = c * _sigmoid(c)
    w = w_ref[0]
    rows = [jnp.sum(s[:, v:v + 1] * w, axis=0, keepdims=True) for v in range(N_SEG)]
    rows.append(jnp.zeros((SUBLANE - N_SEG, w.shape[1]), F32))
    o_ref[0] = jnp.concatenate(rows, axis=0) + b_ref[0]


def _modulation(cvec8, w_mod, b_mod):
    n = 6 * D_MODEL
    return pl.pallas_call(
        _mod_kernel,
        grid=(DEPTH, n // MOD_TN),
        in_specs=[
            pl.BlockSpec((D_MODEL, SUBLANE), lambda l, j: (0, 0)),
            pl.BlockSpec((1, D_MODEL, MOD_TN), lambda l, j: (l, 0, j)),
            pl.BlockSpec((1, 1, MOD_TN), lambda l, j: (l, 0, j)),
        ],
        out_specs=pl.BlockSpec((1, SUBLANE, MOD_TN), lambda l, j: (l, 0, j)),
        out_shape=jax.ShapeDtypeStruct((DEPTH, SUBLANE, n), F32),
        compiler_params=_cparams(("arbitrary", "arbitrary")),
        name="modulation",
    )(cvec8, w_mod, b_mod.reshape(DEPTH, 1, n))


def _rope_tile(p, cos, sin, rope_groups):
    lane = lax.broadcasted_iota(I32, (1, LANE), 1)
    first_half = (lane % (HEAD_DIM // 2)) < HEAD_DIM // 4
    pieces = []
    for k in range(p.shape[1] // LANE):
        xg = p[:, k * LANE:(k + 1) * LANE]
        if k in rope_groups:
            nf = HEAD_DIM // 4
            partner = jnp.where(first_half, pltpu.roll(xg, LANE - nf, 1), pltpu.roll(xg, nf, 1))
            xg = xg * cos + partner * sin
        pieces.append(xg)
    return jnp.concatenate(pieces, axis=1)


def _inproj_kernel(*refs, rope_groups, cache_cols):
    i = pl.program_id(0)
    if rope_groups:
        x_ref, mod_ref, w_ref, cos_ref, sin_ref, o_ref = refs[:6]
        cache_refs, wbf_ref = refs[6:-1], refs[-1]
    else:
        xc_ref, xl_ref, mod_ref, w_ref, o_ref, gz_ref, wbf_ref = refs

    @pl.when(i == 0)
    def _():
        wbf_ref[...] = w_ref[0].astype(BF16)

    m = mod_ref[0, 0]
    x = x_ref[...] if rope_groups else _pick(i, xc_ref, xl_ref)
    u = x * (1.0 + m[1:2, :]) + m[0:1, :]
    p = _dot(u.astype(BF16), wbf_ref[...])
    if rope_groups:
        @pl.when(i < CTX_TILES)
        def _():
            o_ref[...] = p.astype(BF16)
            for ref, (start, width, transposed) in zip(cache_refs, cache_cols):
                if transposed:
                    for b in range(TM // SEQ):
                        ref[b] = p[b * SEQ:(b + 1) * SEQ, start:start + width].T
                else:
                    ref[...] = p[:, start:start + width]

        @pl.when(i >= CTX_TILES)
        def _():
            o_ref[...] = _rope_tile(p, cos_ref[...], sin_ref[...], rope_groups).astype(BF16)
    else:
        o_ref[...] = p[:, :D_RG]
        gz_ref[...] = p[:, D_RG:].astype(BF16)


def _inproj(x, mod, layer, w_in, j, rope=None, cache_cols=()):
    n = w_in.shape[-1]
    rope_groups = ()
    x_specs, x_args = _x_operand(x)
    in_specs = x_specs + [
        pl.BlockSpec((1, 1, 6, D_MODEL), lambda i: (layer, _seg_of_tile(i), 0, 0)),
        pl.BlockSpec((1, D_MODEL, n), lambda i: (j, 0, 0)),
    ]
    args = x_args + [mod, w_in]
    out_specs = pl.BlockSpec((TM, n), lambda i: (i, 0))
    if rope is None:
        out_specs = [pl.BlockSpec((TM, D_RG), lambda i: (i, 0)), pl.BlockSpec((TM, n - D_RG), lambda i: (i, 0))]
        out_shape = [jax.ShapeDtypeStruct((N_TOK, D_RG), F32), jax.ShapeDtypeStruct((N_TOK, n - D_RG), BF16)]
    else:
        cos, sin, rope_groups = rope
        tab = pl.BlockSpec((TM, LANE), lambda i: (jnp.maximum(i - CTX_TILES, 0) % LAT_TILES_PER_SEQ, 0))
        in_specs += [tab, tab]
        args += [cos, sin]
        seqs = TM // SEQ
        out_specs = [out_specs] + [
            pl.BlockSpec((seqs, width, SEQ), lambda i: (jnp.minimum(i, CTX_TILES - 1), 0, 0)) if transposed
            else _pair_specs(width)[0] for _, width, transposed in cache_cols]
        out_shape = [jax.ShapeDtypeStruct((N_TOK, n), BF16)] + [
            jax.ShapeDtypeStruct((BATCH, width, SEQ) if transposed else (N_CTX, width), F32)
            for _, width, transposed in cache_cols]
    return pl.pallas_call(
        functools.partial(_inproj_kernel, rope_groups=rope_groups, cache_cols=cache_cols),
        grid=(N_TILES,),
        in_specs=in_specs,
        out_specs=out_specs,
        out_shape=out_shape,
        scratch_shapes=[pltpu.VMEM((D_MODEL, n), BF16)],
        compiler_params=_cparams(("arbitrary",)),
        name="inproj_rope" if rope is not None else "inproj",
    )(*args)


def _rope_tables():
    s = np.arange(DEC_SEQ)
    row, col = (s // GRID_W).astype(np.float64), (s % GRID_W).astype(np.float64)
    nf = HEAD_DIM // 4
    inv = np.power(ROPE_BASE, -np.arange(nf, dtype=np.float64) / nf)
    ang = np.concatenate([row[:, None] * inv, row[:, None] * inv, col[:, None] * inv, col[:, None] * inv], axis=1)
    sign = np.concatenate([-np.ones(nf), np.ones(nf), -np.ones(nf), np.ones(nf)])
    cos = np.tile(np.cos(ang), (1, LANE // HEAD_DIM))
    sin = np.tile(np.sin(ang) * sign, (1, LANE // HEAD_DIM))
    return jnp.asarray(cos, F32), jnp.asarray(sin, F32)


RG_T = 256
RG_TILES = D_RG // MXU_DIM


def _sigmoid(x):
    return 0.5 * jnp.tanh(0.5 * x) + 0.5


def _softplus(x):
    u = jnp.exp(-jnp.abs(x))
    w = 1.0 + u
    l1p = jnp.where(w == 1.0, u, jnp.log(w) * (u / jnp.where(w == 1.0, 1.0, w - 1.0)))
    return jnp.maximum(x, 0.0) + l1p


def _rglru_kernel(xa_ref, ga_ref, h0_ref, cw_ref, cb_ref, wg_ref, br_ref, bi_ref, lam_ref,
                  y_ref, st_ref, hf_s, xc_s, a_s, b_s, hs_s, *, seq_len):
    nchunk = seq_len // RG_T
    cw = cw_ref[0]
    cb = cb_ref[0]

    def conv_chunk(c):
        base = pl.multiple_of(c * RG_T, RG_T)
        cur = xa_ref[pl.ds(base, RG_T), :]
        prev = xa_ref[pl.ds(pl.multiple_of(jnp.maximum(base - SUBLANE, 0), SUBLANE), SUBLANE), :]
        prev = jnp.where(c > 0, prev, 0.0)
        nxt = xa_ref[pl.ds(pl.multiple_of(jnp.minimum(base + RG_T, seq_len - SUBLANE), SUBLANE), SUBLANE), :]
        nxt = jnp.where(c < nchunk - 1, nxt, 0.0)
        win = jnp.concatenate([prev, cur, nxt], axis=0)
        xc = cb
        for k in range(CONV_W):
            off = SUBLANE - 1 + k
            xc = xc + win[off:off + RG_T] * cw[k:k + 1, :]
        return xc

    def gates(xc, z):
        xb = xc.astype(BF16)
        rs, gs = [], []
        for t in range(RG_TILES):
            zz = _dot(xb[:, t * MXU_DIM:(t + 1) * MXU_DIM], wg_ref[0, z, t].astype(BF16))
            rs.append(zz[:, :MXU_DIM])
            gs.append(zz[:, MXU_DIM:])
        r = _sigmoid(jnp.concatenate(rs, axis=1) + br_ref[0, z:z + 1, :])
        g = _sigmoid(jnp.concatenate(gs, axis=1) + bi_ref[0, z:z + 1, :])
        log_a = (-RG_C * _softplus(-lam_ref[0, z:z + 1, :])) * r
        a = jnp.exp(log_a)
        y = -jnp.tanh(log_a) * (a * a + 1.0)
        bt = jnp.where(y > 0.0, y * lax.rsqrt(y), 0.0) * g * xc
        return a, bt

    def scan_chunk(h, reverse):
        def step(t, h):
            tt = RG_T - 1 - t if reverse else t
            h = a_s[pl.ds(tt, 1), :] * h + b_s[pl.ds(tt, 1), :]
            hs_s[pl.ds(tt, 1), :] = h
            return h
        return lax.fori_loop(0, RG_T, step, h, unroll=8)

    def fwd_chunk(c, h):
        rows = pl.ds(pl.multiple_of(c * RG_T, RG_T), RG_T)
        xc = conv_chunk(c)
        xc_s[rows, :] = xc
        a, bt = gates(xc, 0)
        a_s[...] = a
        b_s[...] = bt
        h = scan_chunk(h, False)
        hf_s[rows, :] = hs_s[...]
        return h

    def bwd_chunk(k, h):
        c = nchunk - 1 - k
        rows = pl.ds(pl.multiple_of(c * RG_T, RG_T), RG_T)
        a, bt = gates(xc_s[rows, :], 1)
        a_s[...] = a
        b_s[...] = bt
        h = scan_chunk(h, True)
        y_ref[rows, :] = ((hf_s[rows, :] + hs_s[...]) * jax.nn.gelu(ga_ref[rows, :].astype(F32))).astype(BF16)
        return h

    hf = lax.fori_loop(0, nchunk, fwd_chunk, h0_ref[0, 0:1, :])
    hb = lax.fori_loop(0, nchunk, bwd_chunk, h0_ref[0, 1:2, :])
    st_ref[0, 0:1, :] = hf
    st_ref[0, 1:2, :] = hb


def _rglru(xa, gz, h0, j, cw, cb, wg, br, bi, lam, *, seq_len, nseq, row_block0):
    wspec3 = lambda shape: pl.BlockSpec((1,) + shape, lambda b: (j,) + (0,) * len(shape))
    return pl.pallas_call(
        functools.partial(_rglru_kernel, seq_len=seq_len),
        grid=(nseq,),
        in_specs=[
            pl.BlockSpec((seq_len, D_RG), lambda b: (row_block0 + b, 0)),
            pl.BlockSpec((seq_len, D_RG), lambda b: (row_block0 + b, 0)),
            pl.BlockSpec((1, 2, D_RG), lambda b: (b, 0, 0)),
            wspec3((CONV_W, D_RG)),
            wspec3((1, D_RG)),
            wspec3((2, RG_TILES, MXU_DIM, 2 * MXU_DIM)),
            wspec3((2, D_RG)),
            wspec3((2, D_RG)),
            wspec3((2, D_RG)),
        ],
        out_specs=[
            pl.BlockSpec((seq_len, D_RG), lambda b: (b, 0)),
            pl.BlockSpec((1, 2, D_RG), lambda b: (b, 0, 0)),
        ],
        out_shape=[
            jax.ShapeDtypeStruct((nseq * seq_len, D_RG), BF16),
            jax.ShapeDtypeStruct((nseq, 2, D_RG), F32),
        ],
        scratch_shapes=[
            pltpu.VMEM((seq_len, D_RG), F32),
            pltpu.VMEM((seq_len, D_RG), F32),
            pltpu.VMEM((RG_T, D_RG), F32),
            pltpu.VMEM((RG_T, D_RG), F32),
            pltpu.VMEM((RG_T, D_RG), F32),
        ],
        compiler_params=_cparams(("arbitrary",)),
        name=f"rglru_s{seq_len}",
    )(xa, gz, h0, cw, cb, wg, br, bi, lam)


def _gate_tiles(w_r, w_i):
    per_tile = MXU_DIM // RG_BLOCK
    eye = jnp.eye(per_tile, dtype=F32)

    def tiles(w):
        w = w.reshape(w.shape[0], 2, RG_TILES, per_tile, RG_BLOCK, RG_BLOCK)
        return jnp.einsum('nztbcd,be->nztbced', w, eye).reshape(w.shape[0], 2, RG_TILES, MXU_DIM, MXU_DIM)

    return jnp.concatenate([tiles(w_r), tiles(w_i)], axis=-1)


FNET_ROW_TILE = 512


def _fnet_kernel(z_ref, cc_ref, sc_ref, w_ref, cs_ref, ns_ref, y_ref, p_s, q_s):
    b = pl.program_id(1)

    @pl.when(pl.program_id(0) == 0)
    def _():
        w = w_ref[0]
        a = _dot3(cc_ref[...], w).astype(BF16)
        bm = _dot3(sc_ref[...], w).astype(BF16)
        zb = z_ref[...].astype(BF16)
        p_s[b] = _dot(zb, a).astype(BF16)
        q_s[b] = _dot(zb, bm).astype(BF16)

    y_ref[...] = (_dot(cs_ref[...].astype(BF16), p_s[b])
                  + _dot(ns_ref[...].astype(BF16), q_s[b])).astype(BF16)


def _fnet(gz, j, wbd, *, seq_len, nseq, row_block0):
    tr = min(seq_len, FNET_ROW_TILE)
    nrt = seq_len // tr
    cc, sc = _dft_tables(FNET_GROUP_DIM)
    ccbd = jnp.asarray(np.kron(np.eye(FNET_GROUPS), cc), F32)
    scbd = jnp.asarray(np.kron(np.eye(FNET_GROUPS), sc), F32)
    cs, ss = _dft_tables(seq_len)
    cs_b, ns_b = jnp.asarray(cs, F32), jnp.asarray(-ss, F32)
    zcol = D_RG // D_FNET
    const = lambda shape: pl.BlockSpec(shape, lambda r, b: (0,) * len(shape))
    return pl.pallas_call(
        _fnet_kernel,
        grid=(nrt, nseq),
        in_specs=[
            pl.BlockSpec((seq_len, D_FNET), lambda r, b: (row_block0 + b, zcol)),
            const((D_FNET, D_FNET)),
            const((D_FNET, D_FNET)),
            pl.BlockSpec((1, D_FNET, D_FNET), lambda r, b: (j, 0, 0)),
            pl.BlockSpec((tr, seq_len), lambda r, b: (r, 0)),
            pl.BlockSpec((tr, seq_len), lambda r, b: (r, 0)),
        ],
        out_specs=pl.BlockSpec((tr, D_FNET), lambda r, b: (b * nrt + r, 0)),
        out_shape=jax.ShapeDtypeStruct((nseq * seq_len, D_FNET), BF16),
        scratch_shapes=[pltpu.VMEM((nseq, seq_len, D_FNET), BF16), pltpu.VMEM((nseq, seq_len, D_FNET), BF16)],
        compiler_params=_cparams(("arbitrary", "arbitrary")),
        name=f"fnet_s{seq_len}",
    )(gz, ccbd, scbd, wbd, cs_b, ns_b)


def _dft_tables(n):
    k = np.arange(n)
    ang = 2.0 * np.pi * ((k[:, None] * k[None, :]) % n) / n
    return np.cos(ang) / np.sqrt(n), np.sin(ang) / np.sqrt(n)


def _fnet_blockdiag(w_f):
    eye = jnp.eye(FNET_GROUPS, dtype=F32)
    return jnp.einsum('ngcd,gh->ngchd', w_f, eye).reshape(w_f.shape[0], D_FNET, D_FNET)


def _outproj_kernel(ac_ref, al_ref, bc_ref, bl_ref, w_ref, *refs, ka):
    i = pl.program_id(0)
    x_refs, refs = refs[:-8], refs[-8:]
    mod_ref, g_ref, be_ref, wr_ref, x1_ref, u2_ref, aff_ref, wbf_ref = refs

    @pl.when(i == 0)
    def _():
        wbf_ref[...] = w_ref[0].astype(BF16)

    out = _dot(_pick(i, ac_ref, al_ref), wbf_ref[0:ka, :]) + _dot(_pick(i, bc_ref, bl_ref), wbf_ref[ka:, :])
    m = mod_ref[0, 0]
    x = _pick(i, *x_refs) if len(x_refs) == 2 else x_refs[0][...]
    x1 = _layernorm(ALPHA * x + m[2:3, :] * out, g_ref[0, 0:1, :], be_ref[0, 0:1, :])
    x1_ref[...] = x1
    u2 = x1 * (1.0 + m[4:5, :]) + m[3:4, :]
    u2_ref[...] = u2.astype(BF16)
    uh, ul = _split(u2)
    wh, wl = _split(wr_ref[0])
    lg = _dot(jnp.concatenate([uh, ul], axis=0), jnp.concatenate([wh, wl], axis=1))
    logits = (lg[:TM, :N_EXPERTS] + lg[:TM, N_EXPERTS:]) + (lg[TM:, :N_EXPERTS] + lg[TM:, N_EXPERTS:])
    mx = jnp.max(logits, axis=-1, keepdims=True)
    ex = jnp.exp(logits - mx)
    aff = ex / jnp.sum(ex, axis=-1, keepdims=True)
    aff = jnp.concatenate([aff, jnp.zeros((TM, LANE - N_EXPERTS), F32)], axis=1)
    aff_ref[...] = aff.T[:N_EXPERTS, :]


def _outproj(a_pair, b_pair, w_out, j, x, mod, layer, ln_g, ln_b, w_router):
    ka, kb = a_pair[0].shape[1], b_pair[0].shape[1]
    x_specs, x_args = _x_operand(x)
    return pl.pallas_call(
        functools.partial(_outproj_kernel, ka=ka),
        grid=(N_TILES,),
        in_specs=_pair_specs(ka) + _pair_specs(kb) + [
            pl.BlockSpec((1, D_MODEL, D_MODEL), lambda i: (j, 0, 0)),
        ] + x_specs + [
            pl.BlockSpec((1, 1, 6, D_MODEL), lambda i: (layer, _seg_of_tile(i), 0, 0)),
            pl.BlockSpec((1, 2, D_MODEL), lambda i: (layer, 0, 0)),
            pl.BlockSpec((1, 2, D_MODEL), lambda i: (layer, 0, 0)),
            pl.BlockSpec((1, D_MODEL, N_EXPERTS), lambda i: (layer, 0, 0)),
        ],
        out_specs=[
            pl.BlockSpec((TM, D_MODEL), lambda i: (i, 0)),
            pl.BlockSpec((TM, D_MODEL), lambda i: (i, 0)),
            pl.BlockSpec((N_EXPERTS, TM), lambda i: (0, i)),
        ],
        out_shape=[
            jax.ShapeDtypeStruct((N_TOK, D_MODEL), F32),
            jax.ShapeDtypeStruct((N_TOK, D_MODEL), BF16),
            jax.ShapeDtypeStruct((N_EXPERTS, N_TOK), F32),
        ],
        scratch_shapes=[pltpu.VMEM((D_MODEL, D_MODEL), BF16)],
        compiler_params=_cparams(("arbitrary",)),
        name="outproj_ln_router",
    )(*a_pair, *b_pair, w_out, *x_args, mod, ln_g, ln_b, w_router)


def _topk_kernel(aff_ref, pos_ref, posn_ref, cnt_ref):
    aff = aff_ref[...]
    cap = jnp.float32(CAP)

    def count_ge(t):
        return jnp.sum(jnp.where(aff >= t, 1.0, 0.0), axis=1, keepdims=True)

    lo_v = jnp.zeros((N_EXPERTS, 1), F32)
    hi_v = jnp.full((N_EXPERTS, 1), 2.0, F32)
    lo_t = jnp.full((N_EXPERTS, 1), TOPK_MIN_EXP, F32)
    hi_t = jnp.full((N_EXPERTS, 1), 1.0, F32)
    for _ in range(TOPK_GEO_STEPS):
        mid_t = 0.5 * (lo_t + hi_t)
        cand = jnp.exp2(mid_t)
        ok = count_ge(cand) >= cap
        lo_t = jnp.where(ok, mid_t, lo_t)
        hi_t = jnp.where(ok, hi_t, mid_t)
        lo_v = jnp.where(ok, jnp.maximum(lo_v, cand), lo_v)
        hi_v = jnp.where(ok, hi_v, jnp.minimum(hi_v, cand))
    for _ in range(TOPK_LIN_STEPS):
        cand = lo_v + 0.5 * (hi_v - lo_v)
        ok = count_ge(cand) >= cap
        lo_v = jnp.where(ok, cand, lo_v)
        hi_v = jnp.where(ok, hi_v, cand)
    gt = aff >= hi_v
    eq = (aff >= lo_v) & (aff < hi_v)
    need = cap - jnp.sum(jnp.where(gt, 1.0, 0.0), axis=1, keepdims=True)

    tri = (lax.broadcasted_iota(I32, (LANE, LANE), 0) <= lax.broadcasted_iota(I32, (LANE, LANE), 1))
    tri = jnp.where(tri, 1.0, 0.0).astype(BF16)

    def cumsum_blocks(mask_f32):
        carry = jnp.zeros((N_EXPERTS, 1), F32)
        outs = []
        for k in range(mask_f32.shape[1] // LANE):
            blk = mask_f32[:, k * LANE:(k + 1) * LANE]
            inc = _dot(blk.astype(BF16), tri) + carry
            outs.append(inc)
            carry = carry + jnp.sum(blk, axis=1, keepdims=True)
        return outs

    eq_f = jnp.where(eq, 1.0, 0.0)
    gt_f = jnp.where(gt, 1.0, 0.0)
    eq_rank = cumsum_blocks(eq_f)
    sel_blocks = []
    for k, rk in enumerate(eq_rank):
        lanes = slice(k * LANE, (k + 1) * LANE)
        sel_blocks.append(jnp.where((eq_f[:, lanes] > 0.0) & (rk <= need), 1.0, gt_f[:, lanes]))
    sel = jnp.concatenate(sel_blocks, axis=1)
    sel_rank = cumsum_blocks(sel)
    pad = jnp.zeros((LANE - N_EXPERTS, LANE), F32)
    for k, rk in enumerate(sel_rank):
        lanes = slice(k * LANE, (k + 1) * LANE)
        pos = jnp.where(sel[:, lanes] > 0.0, rk - 1.0, -1.0)
        pos_ref[:, lanes] = pos.astype(I32)
        posn_ref[lanes, :] = jnp.concatenate([pos, pad], axis=0).T[:, :N_EXPERTS].astype(I32)
    lane = lax.broadcasted_iota(I32, (N_EXPERTS, LANE), 1)
    cnt = jnp.zeros((N_EXPERTS, LANE), F32)
    for c in range(1, N_CHUNKS + 1):
        cnt = jnp.where(lane == c, sel_rank[c * (GC // LANE) - 1][:, LANE - 1:LANE], cnt)
    cnt_ref[0] = cnt.astype(I32)


def _topk(aff_t):
    return pl.pallas_call(
        _topk_kernel,
        grid=(2,),
        in_specs=[pl.BlockSpec((N_EXPERTS, N_CTX), lambda g: (0, g))],
        out_specs=[
            pl.BlockSpec((N_EXPERTS, N_CTX), lambda g: (0, g)),
            pl.BlockSpec((N_CTX, N_EXPERTS), lambda g: (g, 0)),
            pl.BlockSpec((1, N_EXPERTS, LANE), lambda g: (g, 0, 0)),
        ],
        out_shape=[
            jax.ShapeDtypeStruct((N_EXPERTS, N_TOK), I32),
            jax.ShapeDtypeStruct((N_TOK, N_EXPERTS), I32),
            jax.ShapeDtypeStruct((2, N_EXPERTS, LANE), I32),
        ],
        compiler_params=_cparams(("arbitrary",)),
        name="topk_select",
    )(aff_t)


def _chunk_windows(cnt_ref, g, e, c):
    c0 = cnt_ref[g, e, c]
    c1 = cnt_ref[g, e, c + 1]
    base = (c0 // WIN_ALIGN) * WIN_ALIGN
    k = jnp.where(c1 > c0, (c1 - base + WIN - 1) // WIN, 0)
    return base, k


def _gather_kernel(cnt_ref, pos_ref, aff_ref, u_ref, xe_ref, gs_ref):
    g = pl.program_id(0)
    c = pl.program_id(1)

    @pl.when(c == 0)
    def _():
        xe_ref[...] = jnp.zeros_like(xe_ref)
        gs_ref[...] = jnp.zeros_like(gs_ref)

    pos = pos_ref[...]
    aff = aff_ref[...]
    slot0 = lax.broadcasted_iota(I32, (WIN, GC), 0)

    def window(e, lo):
        start = pl.multiple_of(jnp.minimum(lo, CAP - WIN), WIN_ALIGN)
        pe = pos[e:e + 1, :]
        hit = (slot0 + start) == jnp.where(pe >= lo, pe, -1)
        gates = jnp.sum(jnp.where(hit, aff[e:e + 1, :], 0.0), axis=1, keepdims=True)
        return start, jnp.where(hit, 1.0, 0.0).astype(BF16), gates

    def add_rows(e, start, rows, gates):
        dst = (e, 0, pl.ds(start, WIN), slice(None))
        xe_ref[dst] = (xe_ref[dst].astype(F32) + rows).astype(BF16)
        gs_ref[dst] += gates

    windows = [_chunk_windows(cnt_ref, g, e, c) for e in range(N_EXPERTS)]
    firsts = [window(e, base) for e, (base, _) in enumerate(windows)]
    rows = _dot(jnp.concatenate([oh for _, oh, _ in firsts], axis=0), u_ref[...])
    for e, (start, _, gates) in enumerate(firsts):
        add_rows(e, start, rows[e * WIN:(e + 1) * WIN], gates)
    for e, (base, k) in enumerate(windows):
        def extra(w, carry, e=e, base=base):
            start, oh, gates = window(e, base + w * WIN)
            add_rows(e, start, _dot(oh, u_ref[...]), gates)
            return carry
        lax.fori_loop(1, k, extra, 0)


def _gather(cnt, pos_t, aff_t, u2):
    return pl.pallas_call(
        _gather_kernel,
        grid_spec=pltpu.PrefetchScalarGridSpec(
            num_scalar_prefetch=1,
            grid=(2, N_CHUNKS),
            in_specs=[
                pl.BlockSpec((N_EXPERTS, GC), lambda g, c, cnt: (0, g * N_CHUNKS + c)),
                pl.BlockSpec((N_EXPERTS, GC), lambda g, c, cnt: (0, g * N_CHUNKS + c)),
                pl.BlockSpec((GC, D_MODEL), lambda g, c, cnt: (g * N_CHUNKS + c, 0)),
            ],
            out_specs=[
                pl.BlockSpec((N_EXPERTS, 1, CAP, D_MODEL), lambda g, c, cnt: (0, g, 0, 0)),
                pl.BlockSpec((N_EXPERTS, 1, CAP, 1), lambda g, c, cnt: (0, g, 0, 0)),
            ],
        ),
        out_shape=[
            jax.ShapeDtypeStruct((N_EXPERTS, 2, CAP, D_MODEL), BF16),
            jax.ShapeDtypeStruct((N_EXPERTS, 2, CAP, 1), F32),
        ],
        compiler_params=_cparams(("arbitrary", "arbitrary")),
        name="moe_gather",
    )(cnt, pos_t, aff_t, u2)


TF = 1024
assert D_EXPERT // TF >= 2


def _ffn_kernel(xe_ref, gs_ref, wg_ref, wu_ref, wd_ref, ye_ref, acc_ref):
    f = pl.program_id(1)
    x = xe_ref[0]
    hg = _dot(x, wg_ref[0, 0].astype(BF16))
    hu = _dot(x, wu_ref[0, 0].astype(BF16))
    h = (hg * _sigmoid(hg) * hu).astype(BF16)

    @pl.when(f == 0)
    def _():
        acc_ref[...] = _dot(h, wd_ref[0, 0].astype(BF16))

    last = pl.num_programs(1) - 1

    @pl.when((f > 0) & (f < last))
    def _():
        acc_ref[...] += _dot(h, wd_ref[0, 0].astype(BF16))

    @pl.when(f == last)
    def _():
        ye_ref[0] = ((acc_ref[...] + _dot(h, wd_ref[0, 0].astype(BF16))) * gs_ref[0]).astype(BF16)


def _ffn(xe, gs, layer, w_gate, w_up, w_down):
    rows = 2 * CAP
    return pl.pallas_call(
        _ffn_kernel,
        grid=(N_EXPERTS, D_EXPERT // TF),
        in_specs=[
            pl.BlockSpec((1, rows, D_MODEL), lambda e, f: (e, 0, 0)),
            pl.BlockSpec((1, rows, 1), lambda e, f: (e, 0, 0)),
            pl.BlockSpec((1, 1, D_MODEL, TF), lambda e, f: (layer, e, 0, f)),
            pl.BlockSpec((1, 1, D_MODEL, TF), lambda e, f: (layer, e, 0, f)),
            pl.BlockSpec((1, 1, TF, D_MODEL), lambda e, f: (layer, e, f, 0)),
        ],
        out_specs=pl.BlockSpec((1, rows, D_MODEL), lambda e, f: (e, 0, 0)),
        out_shape=jax.ShapeDtypeStruct((N_EXPERTS, rows, D_MODEL), BF16),
        scratch_shapes=[pltpu.VMEM((rows, D_MODEL), F32)],
        compiler_params=_cparams(("arbitrary", "arbitrary")),
        name="moe_ffn",
    )(xe, gs, w_gate, w_up, w_down)


def _combine_kernel(*refs, split_out):
    if split_out:
        cnt_ref, posn_ref, ye_ref, x1_ref, mod_ref, g_ref, be_ref, oc_ref, ol_ref, acc_s = refs
    else:
        cnt_ref, posn_ref, ye_ref, x1_ref, mod_ref, g_ref, be_ref, o_ref, acc_s = refs
    i = pl.program_id(0)
    g = i // N_CHUNKS
    c = i % N_CHUNKS
    posn = posn_ref[...]
    lane = lax.broadcasted_iota(I32, (1, MXU_DIM), 1)
    part = lane // WIN
    slot_in_win = lane % WIN

    def by_part(vals):
        out = vals[-1]
        for q in range(len(vals) - 2, -1, -1):
            out = jnp.where(part == q, vals[q], out)
        return out

    def start_of(lo):
        return pl.multiple_of(jnp.minimum(lo, CAP - WIN), WIN_ALIGN)

    def group_product(experts, los):
        starts = [start_of(lo) for lo in los]
        slots = by_part(starts) + slot_in_win
        cols = by_part([jnp.where(posn[:, e:e + 1] >= lo, posn[:, e:e + 1], -1) for e, lo in zip(experts, los)])
        onehot = jnp.where(slots == cols, 1.0, 0.0).astype(BF16)
        rows = jnp.concatenate([ye_ref[e, 0, pl.ds(st, WIN), :] for e, st in zip(experts, starts)], axis=0)
        return _dot(onehot, rows)

    windows = [_chunk_windows(cnt_ref, g, e, c) for e in range(N_EXPERTS)]
    acc = None
    for e0 in range(0, N_EXPERTS, WINS_PER_DOT):
        experts = list(range(e0, e0 + WINS_PER_DOT))
        d = group_product(experts, [windows[e][0] for e in experts])
        acc = d if acc is None else acc + d
    acc_s[...] = acc
    slot0 = lax.broadcasted_iota(I32, (GC, WIN), 1)
    for e, (base, k) in enumerate(windows):
        def extra(w, carry, e=e, base=base):
            lo = base + w * WIN
            start = start_of(lo)
            col = posn[:, e:e + 1]
            hit = (slot0 + start) == jnp.where(col >= lo, col, -1)
            acc_s[...] += _dot(jnp.where(hit, 1.0, 0.0).astype(BF16), ye_ref[e, 0, pl.ds(start, WIN), :])
            return carry
        lax.fori_loop(1, k, extra, 0)
    m = mod_ref[0, 0]
    y = ALPHA * x1_ref[...] + m[5:6, :] * acc_s[...]
    res = _layernorm(y, g_ref[0, 1:2, :], be_ref[0, 1:2, :])
    if split_out:
        @pl.when(i < N_CHUNKS)
        def _():
            oc_ref[...] = res

        @pl.when(i >= N_CHUNKS)
        def _():
            ol_ref[...] = res
    else:
        o_ref[...] = res


def _combine(cnt, pos_n, ye, x1, mod, layer, ln_g, ln_b, split_out):
    if split_out:
        out_specs = _pair_specs(D_MODEL, tile=GC)
        out_shape = [jax.ShapeDtypeStruct((N_CTX, D_MODEL), F32), jax.ShapeDtypeStruct((N_LAT, D_MODEL), F32)]
    else:
        out_specs = pl.BlockSpec((GC, D_MODEL), lambda i, cnt: (i, 0))
        out_shape = jax.ShapeDtypeStruct((N_TOK, D_MODEL), F32)
    return pl.pallas_call(
        functools.partial(_combine_kernel, split_out=split_out),
        grid_spec=pltpu.PrefetchScalarGridSpec(
            num_scalar_prefetch=1,
            grid=(N_TOK // GC,),
            in_specs=[
                pl.BlockSpec((GC, N_EXPERTS), lambda i, cnt: (i, 0)),
                pl.BlockSpec((N_EXPERTS, 1, CAP, D_MODEL), lambda i, cnt: (0, i // N_CHUNKS, 0, 0)),
                pl.BlockSpec((GC, D_MODEL), lambda i, cnt: (i, 0)),
                pl.BlockSpec((1, 1, 6, D_MODEL), lambda i, cnt: (layer, _seg_of_tile(i, GC), 0, 0)),
                pl.BlockSpec((1, 2, D_MODEL), lambda i, cnt: (layer, 0, 0)),
                pl.BlockSpec((1, 2, D_MODEL), lambda i, cnt: (layer, 0, 0)),
            ],
            out_specs=out_specs,
            scratch_shapes=[pltpu.VMEM((GC, D_MODEL), F32)],
        ),
        out_shape=out_shape,
        compiler_params=_cparams(("arbitrary",)),
        name="moe_combine_ln",
    )(cnt, pos_n, ye, x1, mod, ln_g, ln_b)


def _moe(x1, u2, aff_t, mod, layer, ln_g, ln_b, w_gate, w_up, w_down, split_out=False):
    pos_t, pos_n, cnt = _topk(aff_t)
    xe, gs = _gather(cnt, pos_t, aff_t, u2)
    ye = _ffn(xe.reshape(N_EXPERTS, 2 * CAP, D_MODEL), gs.reshape(N_EXPERTS, 2 * CAP, 1),
              layer, w_gate, w_up, w_down)
    return _combine(cnt, pos_n, ye.reshape(N_EXPERTS, 2, CAP, D_MODEL), x1, mod, layer, ln_g, ln_b, split_out)


def _sink_softmax_pv(s, sink_col, v):
    m = jnp.maximum(jnp.max(s, axis=-1, keepdims=True), sink_col)
    p = jnp.exp(s - m)
    den = jnp.sum(p, axis=-1, keepdims=True) + jnp.exp(sink_col - m)
    return _dot(p.astype(BF16), v) / den


def _sink_column(sink_ref, h, rows_per_group):
    rid = lax.broadcasted_iota(I32, (GQA_GROUP * rows_per_group, 1), 0) // rows_per_group
    col = jnp.zeros((GQA_GROUP * rows_per_group, 1), F32)
    for g in range(GQA_GROUP):
        col = jnp.where(rid == g, sink_ref[h * GQA_GROUP + g], col)
    return col


def _stack_groups(q, h):
    return jnp.concatenate(
        [q[:, (h * GQA_GROUP + g) * HEAD_DIM:(h * GQA_GROUP + g + 1) * HEAD_DIM] * ATTN_SCALE
         for g in range(GQA_GROUP)], axis=0).astype(BF16)


def _unstack_groups(outs, rows):
    return jnp.concatenate([o[g * rows:(g + 1) * rows] for o in outs for g in range(GQA_GROUP)], axis=1)


def _swa_ctx_kernel(sink_ref, q_ref, k_ref, v_ref, o_ref):
    q, k, v = q_ref[...], k_ref[...], v_ref[...]
    outs = []
    for h in range(SWA_KV_HEADS):
        cols = slice(h * HEAD_DIM, (h + 1) * HEAD_DIM)
        s = _dot_nt(_stack_groups(q, h), k[:, cols].astype(BF16))
        outs.append(_sink_softmax_pv(s, _sink_column(sink_ref, h, SEQ), v[:, cols].astype(BF16)))
    o_ref[...] = _unstack_groups(outs, SEQ).astype(BF16)


def _swa_ctx(proj, sink):
    return pl.pallas_call(
        _swa_ctx_kernel,
        grid=(BATCH,),
        in_specs=[
            pl.BlockSpec(memory_space=pltpu.SMEM),
            pl.BlockSpec((SEQ, SWA_Q_W), lambda b: (b, 0)),
            pl.BlockSpec((SEQ, SWA_KV_W), lambda b: (b, SWA_K_COL // SWA_KV_W)),
            pl.BlockSpec((SEQ, SWA_KV_W), lambda b: (b, SWA_V_COL // SWA_KV_W)),
        ],
        out_specs=pl.BlockSpec((SEQ, SWA_Q_W), lambda b: (b, 0)),
        out_shape=jax.ShapeDtypeStruct((N_CTX, SWA_Q_W), BF16),
        compiler_params=_cparams(("arbitrary",)),
        name="swa_context",
    )(sink, proj, proj, proj)


QB = 128


def _swa_lat_kernel(sink_ref, q_ref, kp_ref, kc_ref, kn_ref, vp_ref, vc_ref, vn_ref, ck_ref, cv_ref, o_ref):
    qb = pl.program_id(1)
    q = q_ref[...]
    kall = jnp.concatenate([ck_ref[0].astype(BF16), kp_ref[...], kc_ref[...], kn_ref[...]], axis=0)
    vall = jnp.concatenate([cv_ref[0].astype(BF16), vp_ref[...], vc_ref[...], vn_ref[...]], axis=0)
    nk = PAST_LEN + 3 * QB
    col = lax.broadcasted_iota(I32, (QB, nk), 1)
    qpos = qb * QB + lax.broadcasted_iota(I32, (QB, nk), 0)
    kpos = (qb - 1) * QB + col - PAST_LEN
    ok = (col < PAST_LEN) | ((jnp.abs(qpos - kpos) <= WINDOW) & (kpos >= 0) & (kpos < DEC_SEQ))
    bias = jnp.concatenate([jnp.where(ok, 0.0, NEG_INF)] * GQA_GROUP, axis=0)
    outs = []
    for h in range(SWA_KV_HEADS):
        cols = slice(h * HEAD_DIM, (h + 1) * HEAD_DIM)
        s = _dot_nt(_stack_groups(q, h), kall[:, cols]) + bias
        outs.append(_sink_softmax_pv(s, _sink_column(sink_ref, h, QB), vall[:, cols]))
    o_ref[...] = _unstack_groups(outs, QB).astype(BF16)


def _swa_lat(proj, cache_k, cache_v, sink):
    nqb = DEC_SEQ // QB
    row0 = N_CTX // QB

    def blk(col, shift):
        return pl.BlockSpec((QB, SWA_KV_W),
                            lambda b, i: (row0 + b * nqb + jnp.clip(i + shift, 0, nqb - 1), col // SWA_KV_W))

    return pl.pallas_call(
        _swa_lat_kernel,
        grid=(DEC_BATCH, nqb),
        in_specs=[
            pl.BlockSpec(memory_space=pltpu.SMEM),
            pl.BlockSpec((QB, SWA_Q_W), lambda b, i: (row0 + b * nqb + i, 0)),
            blk(SWA_K_COL, -1), blk(SWA_K_COL, 0), blk(SWA_K_COL, 1),
            blk(SWA_V_COL, -1), blk(SWA_V_COL, 0), blk(SWA_V_COL, 1),
            pl.BlockSpec((1, PAST_LEN, SWA_KV_W), lambda b, i: (b, 0, 0)),
            pl.BlockSpec((1, PAST_LEN, SWA_KV_W), lambda b, i: (b, 0, 0)),
        ],
        out_specs=pl.BlockSpec((QB, SWA_Q_W), lambda b, i: (b * nqb + i, 0)),
        out_shape=jax.ShapeDtypeStruct((N_LAT, SWA_Q_W), BF16),
        compiler_params=_cparams(("arbitrary", "arbitrary")),
        name="swa_latent",
    )(sink, proj, proj, proj, proj, proj, proj, proj, cache_k, cache_v)


def _diff_lambda_col(lam_ref, lam_init):
    l1 = jnp.sum(lam_ref[0, 0] * lam_ref[0, 1], axis=-1, keepdims=True)
    l2 = jnp.sum(lam_ref[0, 2] * lam_ref[0, 3], axis=-1, keepdims=True)
    return jnp.exp(l1) - jnp.exp(l2) + lam_init


def _softmax_pv(q, k, v):
    s = _dot_nt(q, k)
    p = jnp.exp(s - jnp.max(s, axis=-1, keepdims=True))
    return _dot(p.astype(BF16), v) / jnp.sum(p, axis=-1, keepdims=True)


def _diff_head(q, k, v, lam, subln, lam_init):
    o = [_softmax_pv((q[:, m * HEAD_DIM:(m + 1) * HEAD_DIM] * ATTN_SCALE).astype(BF16),
                     k[:, m * HEAD_DIM:(m + 1) * HEAD_DIM], v) for m in range(2)]
    o = o[0] - lam * o[1]
    return o * lax.rsqrt(jnp.mean(o * o, axis=-1, keepdims=True) + LN_EPS) * subln * (1.0 - lam_init)


def _diff_kernel(*refs, lam_init, has_cache, nh):
    if has_cache:
        lam_ref, sg_ref, q_ref, k_ref, v_ref, ck_ref, cv_ref, o_ref = refs
    else:
        lam_ref, sg_ref, q_ref, k_ref, v_ref, o_ref = refs
    hp = pl.program_id(1)
    lam_all = _diff_lambda_col(lam_ref, lam_init)
    hid = lax.broadcasted_iota(I32, (DIFF_HEADS, 1), 0)
    q, k, v = q_ref[...], k_ref[...], v_ref[...]
    if has_cache:
        k = jnp.concatenate([k, ck_ref[0].astype(BF16)], axis=0)
        v = jnp.concatenate([v, cv_ref[0].astype(BF16)], axis=0)
    outs = []
    for j in range(nh):
        lam = jnp.sum(jnp.where(hid == hp * nh + j, lam_all, 0.0), axis=0, keepdims=True)
        cols = slice(j * DIFF_V_DIM, (j + 1) * DIFF_V_DIM)
        outs.append(_diff_head(q[:, cols], k[:, cols], v[:, cols], lam, sg_ref[...], lam_init))
    o_ref[...] = jnp.concatenate(outs, axis=1).astype(BF16)


DIFF_STEP_HEADS = 2
DQ = 512
assert all(col % (DIFF_STEP_HEADS * DIFF_V_DIM) == 0 for col in (Q_COL, K_COL, V_COL))


def _diff_ctx(proj, lam_p, subln, j, lam_init):
    nh = DIFF_STEP_HEADS
    w = nh * DIFF_V_DIM
    return pl.pallas_call(
        functools.partial(_diff_kernel, lam_init=lam_init, has_cache=False, nh=nh),
        grid=(BATCH, DIFF_HEADS // nh),
        in_specs=[
            pl.BlockSpec((1, 4, DIFF_HEADS, HEAD_DIM), lambda b, h: (j, 0, 0, 0)),
            pl.BlockSpec((1, DIFF_V_DIM), lambda b, h: (j, 0)),
            pl.BlockSpec((SEQ, w), lambda b, h: (b, Q_COL // w + h)),
            pl.BlockSpec((SEQ, w), lambda b, h: (b, K_COL // w + h)),
            pl.BlockSpec((SEQ, w), lambda b, h: (b, V_COL // w + h)),
        ],
        out_specs=pl.BlockSpec((SEQ, w), lambda b, h: (b, h)),
        out_shape=jax.ShapeDtypeStruct((N_CTX, DIFF_W), BF16),
        compiler_params=_cparams(("arbitrary", "arbitrary")),
        name="diff_context",
    )(lam_p, subln, proj, proj, proj)


def _diff_lat(proj, cache_k, cache_v, lam_p, subln, j, lam_init):
    nh = DIFF_STEP_HEADS
    w = nh * DIFF_V_DIM
    nq = DEC_SEQ // DQ
    row0 = N_CTX // DQ
    seq0 = N_CTX // DEC_SEQ
    return pl.pallas_call(
        functools.partial(_diff_kernel, lam_init=lam_init, has_cache=True, nh=nh),
        grid=(DEC_BATCH, DIFF_HEADS // nh, nq),
        in_specs=[
            pl.BlockSpec((1, 4, DIFF_HEADS, HEAD_DIM), lambda b, h, i: (j, 0, 0, 0)),
            pl.BlockSpec((1, DIFF_V_DIM), lambda b, h, i: (j, 0)),
            pl.BlockSpec((DQ, w), lambda b, h, i: (row0 + b * nq + i, Q_COL // w + h)),
            pl.BlockSpec((DEC_SEQ, w), lambda b, h, i: (seq0 + b, K_COL // w + h)),
            pl.BlockSpec((DEC_SEQ, w), lambda b, h, i: (seq0 + b, V_COL // w + h)),
            pl.BlockSpec((1, PAST_LEN, w), lambda b, h, i: (b, 0, h)),
            pl.BlockSpec((1, PAST_LEN, w), lambda b, h, i: (b, 0, h)),
        ],
        out_specs=pl.BlockSpec((DQ, w), lambda b, h, i: (b * nq + i, h)),
        out_shape=jax.ShapeDtypeStruct((N_LAT, DIFF_W), BF16),
        compiler_params=_cparams(("arbitrary", "arbitrary", "arbitrary")),
        name="diff_latent",
    )(lam_p, subln, proj, proj, proj, cache_k, cache_v)


def kernel(x_prompt, x_sample, state_rglru, cache_swa_k, cache_swa_v, cache_diff_k, cache_diff_v, c, c_ctx,
           w_mod, b_mod, ln_g, ln_b, e_w_in, e_conv_w, e_conv_b, e_w_rgate, e_b_rgate, e_w_igate, e_b_igate,
           e_lambda, e_w_fnet, e_w_out, o_w_in, o_sink, o_lambda, o_subln_g, o_w_out,
           w_router, w_gate, w_up, w_down):
    x_pair = (x_prompt.reshape(N_CTX, D_MODEL), x_sample.reshape(N_LAT, D_MODEL))
    cvec8 = jnp.concatenate([c_ctx[None, :], c, jnp.zeros((SUBLANE - N_SEG, D_MODEL), F32)], axis=0).T
    mod = _modulation(cvec8, w_mod, b_mod).reshape(DEPTH, SUBLANE, 6, D_MODEL)

    j = 0
    xa, gz = _inproj(x_pair, mod, 0, e_w_in, j)
    wg = _gate_tiles(e_w_rgate, e_w_igate)
    cb = e_conv_b.reshape(-1, 1, D_RG)
    rg_args = (j, e_conv_w, cb, wg, e_b_rgate, e_b_igate, e_lambda)
    y_rg_c, st_c = _rglru(xa, gz, jnp.zeros((BATCH, 2, D_RG), F32), *rg_args,
                          seq_len=SEQ, nseq=BATCH, row_block0=0)
    y_rg_l, _ = _rglru(xa, gz, state_rglru[:, j], *rg_args,
                       seq_len=DEC_SEQ, nseq=DEC_BATCH, row_block0=N_CTX // DEC_SEQ)
    wbd = _fnet_blockdiag(e_w_fnet)
    y_fn_c = _fnet(gz, j, wbd, seq_len=SEQ, nseq=BATCH, row_block0=0)
    y_fn_l = _fnet(gz, j, wbd, seq_len=DEC_SEQ, nseq=DEC_BATCH, row_block0=N_CTX // DEC_SEQ)
    x1, u2, aff = _outproj((y_rg_c, y_rg_l), (y_fn_c, y_fn_l), e_w_out, j, x_pair, mod, 0, ln_g, ln_b, w_router)
    x = _moe(x1, u2, aff, mod, 0, ln_g, ln_b, w_gate, w_up, w_down)
    new_state_rglru = st_c[:, None]

    layer = 1
    lam_init = 0.8 - 0.6 * math.exp(-0.3 * layer)
    cos, sin = _rope_tables()
    rope_groups = tuple(g for g in range(ODD_IN // LANE)
                        if g * LANE < SWA_V_COL or Q_COL <= g * LANE < V_COL)
    cache_cols = ((SWA_K_COL, SWA_KV_W, True), (SWA_V_COL, SWA_KV_W, True),
                  (K_COL, DIFF_W, True), (V_COL, DIFF_W, False))
    proj, ks, vs, kd, vd = _inproj(x, mod, 1, o_w_in, j, rope=(cos, sin, rope_groups), cache_cols=cache_cols)
    sink = o_sink[j]
    ys_c = _swa_ctx(proj, sink)
    yd_c = _diff_ctx(proj, o_lambda, o_subln_g, j, lam_init)
    ck = cache_swa_k[:, j].reshape(DEC_BATCH, PAST_LEN, SWA_KV_HEADS * HEAD_DIM)
    cv = cache_swa_v[:, j].reshape(DEC_BATCH, PAST_LEN, SWA_KV_HEADS * HEAD_DIM)
    ys_l = _swa_lat(proj, ck, cv, sink)
    cdk = cache_diff_k[:, j].reshape(DEC_BATCH, PAST_LEN, DIFF_HEADS * 2 * HEAD_DIM)
    cdv = cache_diff_v[:, j].reshape(DEC_BATCH, PAST_LEN, DIFF_HEADS * DIFF_V_DIM)
    yd_l = _diff_lat(proj, cdk, cdv, o_lambda, o_subln_g, j, lam_init)
    x1, u2, aff = _outproj((ys_c, ys_l), (yd_c, yd_l), o_w_out, j, x, mod, 1, ln_g, ln_b, w_router)
    y_c, y_l = _moe(x1, u2, aff, mod, 1, ln_g, ln_b, w_gate, w_up, w_down, split_out=True)

    new_cache_swa_k = ks.reshape(BATCH, 1, SWA_KV_HEADS, HEAD_DIM, SEQ).transpose(0, 1, 4, 2, 3)
    new_cache_swa_v = vs.reshape(BATCH, 1, SWA_KV_HEADS, HEAD_DIM, SEQ).transpose(0, 1, 4, 2, 3)
    new_cache_diff_k = kd.reshape(BATCH, 1, DIFF_HEADS, 2, HEAD_DIM, SEQ).transpose(0, 1, 5, 2, 3, 4)
    new_cache_diff_v = vd.reshape(BATCH, 1, SEQ, DIFF_HEADS, DIFF_V_DIM)
    y_prompt = y_c.reshape(BATCH, SEQ, D_MODEL)
    y_sample = y_l.reshape(DEC_BATCH, DEC_SEQ, D_MODEL)
    return (y_prompt, y_sample, new_state_rglru, new_cache_swa_k, new_cache_swa_v,
            new_cache_diff_k, new_cache_diff_v)
```

```python
import functools
import math

import numpy as np
import jax
import jax.numpy as jnp
from jax import lax
from jax.experimental import pallas as pl
from jax.experimental.pallas import tpu as pltpu

F32 = jnp.float32
BF16 = jnp.bfloat16
I32 = jnp.int32

D_MODEL = 1024
BATCH, SEQ = 16, 256
DEC_BATCH, DEC_SEQ = 2, 2048
PAST_LEN = 256
DEPTH = 2
GRID_W = 64
HEAD_DIM = 64
D_RG = 768
RG_BLOCK = 64
RG_C = 8.0
CONV_W = 4
D_FNET = 256
FNET_GROUP_DIM = 64
FNET_GROUPS = 4
SWA_HEADS = 8
SWA_KV_HEADS = 2
GQA_GROUP = 4
WINDOW = 128
DIFF_HEADS = 4
DIFF_V_DIM = 128
N_EXPERTS = 16
EC_FACTOR = 2
D_EXPERT = 2048
ROPE_BASE = 10000.0
LN_EPS = 1e-5
NEG_INF = -1e30
ATTN_SCALE = HEAD_DIM ** -0.5
ALPHA = (2 * DEPTH) ** 0.25
SWA_Q_W = SWA_HEADS * HEAD_DIM
SWA_KV_W = SWA_KV_HEADS * HEAD_DIM
DIFF_W = DIFF_HEADS * DIFF_V_DIM
SWA_K_COL = SWA_Q_W
SWA_V_COL = SWA_K_COL + SWA_KV_W
Q_COL = SWA_V_COL + SWA_KV_W
K_COL = Q_COL + DIFF_W
V_COL = K_COL + DIFF_W
ODD_IN = V_COL + DIFF_W

N_CTX = BATCH * SEQ
N_LAT = DEC_BATCH * DEC_SEQ
N_TOK = N_CTX + N_LAT
N_SEG = 1 + DEC_BATCH
CAP = EC_FACTOR * N_CTX // N_EXPERTS
assert N_CTX == N_LAT

LANE = 128
SUBLANE = 8
MXU_DIM = 256
VMEM_LIMIT = 56 * 1024 * 1024
TM = 512
GC = 256
N_CHUNKS = N_CTX // GC
WIN = 64
WIN_ALIGN = 16
WINS_PER_DOT = MXU_DIM // WIN
TOPK_MIN_EXP = -150.0
TOPK_GEO_STEPS = 16
TOPK_LIN_STEPS = 24
N_TILES = N_TOK // TM
CTX_TILES = N_CTX // TM
LAT_TILES_PER_SEQ = DEC_SEQ // TM


def _cparams(sem):
    return pltpu.CompilerParams(dimension_semantics=sem, vmem_limit_bytes=VMEM_LIMIT)


def _dot(a, b):
    return jnp.dot(a, b, preferred_element_type=F32)


def _dot_nt(a, b):
    return lax.dot_general(a, b, (((1,), (1,)), ((), ())), preferred_element_type=F32)


def _split(a):
    hi = a.astype(BF16)
    lo = (a - hi.astype(F32)).astype(BF16)
    return hi, lo


def _dot3(a, b):
    ah, al = _split(a)
    bh, bl = _split(b)
    return _dot(ah, bh) + (_dot(ah, bl) + _dot(al, bh))


def _layernorm(y, g, b):
    mu = jnp.mean(y, axis=-1, keepdims=True)
    d = y - mu
    var = jnp.mean(d * d, axis=-1, keepdims=True)
    return d * lax.rsqrt(var + LN_EPS) * g + b


def _pair_specs(width, tile=TM):
    ctx_tiles = N_CTX // tile
    return [
        pl.BlockSpec((tile, width), lambda i, *_: (jnp.minimum(i, ctx_tiles - 1), 0)),
        pl.BlockSpec((tile, width), lambda i, *_: (jnp.maximum(i - ctx_tiles, 0), 0)),
    ]


def _x_operand(x):
    if isinstance(x, tuple):
        return _pair_specs(D_MODEL), list(x)
    return [pl.BlockSpec((TM, D_MODEL), lambda i: (i, 0))], [x]


def _pick(i, ctx_ref, lat_ref):
    return jnp.where(i < CTX_TILES, ctx_ref[...], lat_ref[...])


def _seg_of_tile(i, tile=TM):
    ctx_tiles = N_CTX // tile
    return jnp.where(i < ctx_tiles, 0, (i - ctx_tiles) // (DEC_SEQ // tile) + 1)


MOD_TN = 1536


MOD_STEPS = DEPTH * (6 * D_MODEL // MOD_TN)
MOD_SLOTS = 3


def _mod_weight_copy(w_hbm, wbuf, sem, step):
    tiles = 6 * D_MODEL // MOD_TN
    col = pl.multiple_of((step % tiles) * MOD_TN, LANE)
    slot = step % MOD_SLOTS
    return pltpu.make_async_copy(w_hbm.at[step // tiles, :, pl.ds(col, MOD_TN)], wbuf.at[slot], sem.at[slot])


def _mod_kernel(ct_ref, w_hbm, b_ref, o_ref, wbuf, sem):
    step = pl.program_id(0) * pl.num_programs(1) + pl.program_id(1)

    @pl.when(step == 0)
    def _():
        for first in range(MOD_SLOTS - 1):
            _mod_weight_copy(w_hbm, wbuf, sem, first).start()

    @pl.when(step + (MOD_SLOTS - 1) < MOD_STEPS)
    def _():
        _mod_weight_copy(w_hbm, wbuf, sem, step + (MOD_SLOTS - 1)).start()

    _mod_weight_copy(w_hbm, wbuf, sem, step).wait()
    c = ct_ref[...]
    s = c * _sigmoid(c)
    w = wbuf[step % MOD_SLOTS]
    rows = [jnp.sum(s[:, v:v + 1] * w, axis=0, keepdims=True) for v in range(N_SEG)]
    rows.append(jnp.zeros((SUBLANE - N_SEG, w.shape[1]), F32))
    o_ref[0] = jnp.concatenate(rows, axis=0) + b_ref[0]


def _modulation(cvec8, w_mod, b_mod):
    n = 6 * D_MODEL
    return pl.pallas_call(
        _mod_kernel,
        grid=(DEPTH, n // MOD_TN),
        in_specs=[
            pl.BlockSpec((D_MODEL, SUBLANE), lambda l, j: (0, 0)),
            pl.BlockSpec(memory_space=pl.ANY),
            pl.BlockSpec((1, 1, MOD_TN), lambda l, j: (l, 0, j)),
        ],
        out_specs=pl.BlockSpec((1, SUBLANE, MOD_TN), lambda l, j: (l, 0, j)),
        out_shape=jax.ShapeDtypeStruct((DEPTH, SUBLANE, n), F32),
        scratch_shapes=[pltpu.VMEM((MOD_SLOTS, D_MODEL, MOD_TN), F32), pltpu.SemaphoreType.DMA((MOD_SLOTS,))],
        compiler_params=_cparams(("arbitrary", "arbitrary")),
        name="modulation",
    )(cvec8, w_mod, b_mod.reshape(DEPTH, 1, n))


def _rope_tile(p, cos, sin, rope_groups):
    lane = lax.broadcasted_iota(I32, (1, LANE), 1)
    first_half = (lane % (HEAD_DIM // 2)) < HEAD_DIM // 4
    pieces = []
    for k in range(p.shape[1] // LANE):
        xg = p[:, k * LANE:(k + 1) * LANE]
        if k in rope_groups:
            nf = HEAD_DIM // 4
            partner = jnp.where(first_half, pltpu.roll(xg, LANE - nf, 1), pltpu.roll(xg, nf, 1))
            xg = xg * cos + partner * sin
        pieces.append(xg)
    return jnp.concatenate(pieces, axis=1)


def _inproj_kernel(*refs, rope_groups, cache_cols):
    i = pl.program_id(0)
    if rope_groups:
        x_ref, mod_ref, w_ref, cos_ref, sin_ref, o_ref = refs[:6]
        cache_refs, wbf_ref = refs[6:-1], refs[-1]
    else:
        xc_ref, xl_ref, mod_ref, w_ref, o_ref, gz_ref, wbf_ref = refs

    @pl.when(i == 0)
    def _():
        wbf_ref[...] = w_ref[0].astype(BF16)

    m = mod_ref[0, 0]
    x = x_ref[...] if rope_groups else _pick(i, xc_ref, xl_ref)
    u = x * (1.0 + m[1:2, :]) + m[0:1, :]
    p = _dot(u.astype(BF16), wbf_ref[...])
    if rope_groups:
        @pl.when(i < CTX_TILES)
        def _():
            o_ref[...] = p.astype(BF16)
            for ref, (start, width, transposed) in zip(cache_refs, cache_cols):
                if transposed:
                    for b in range(TM // SEQ):
                        ref[b] = p[b * SEQ:(b + 1) * SEQ, start:start + width].T
                else:
                    ref[...] = p[:, start:start + width]

        @pl.when(i >= CTX_TILES)
        def _():
            o_ref[...] = _rope_tile(p, cos_ref[...], sin_ref[...], rope_groups).astype(BF16)
    else:
        o_ref[...] = p[:, :D_RG]
        gz_ref[...] = p[:, D_RG:].astype(BF16)


def _inproj(x, mod, layer, w_in, j, rope=None, cache_cols=()):
    n = w_in.shape[-1]
    rope_groups = ()
    x_specs, x_args = _x_operand(x)
    in_specs = x_specs + [
        pl.BlockSpec((1, 1, 6, D_MODEL), lambda i: (layer, _seg_of_tile(i), 0, 0)),
        pl.BlockSpec((1, D_MODEL, n), lambda i: (j, 0, 0)),
    ]
    args = x_args + [mod, w_in]
    out_specs = pl.BlockSpec((TM, n), lambda i: (i, 0))
    if rope is None:
        out_specs = [pl.BlockSpec((TM, D_RG), lambda i: (i, 0)), pl.BlockSpec((TM, n - D_RG), lambda i: (i, 0))]
        out_shape = [jax.ShapeDtypeStruct((N_TOK, D_RG), F32), jax.ShapeDtypeStruct((N_TOK, n - D_RG), BF16)]
    else:
        cos, sin, rope_groups = rope
        tab = pl.BlockSpec((TM, LANE), lambda i: (jnp.maximum(i - CTX_TILES, 0) % LAT_TILES_PER_SEQ, 0))
        in_specs += [tab, tab]
        args += [cos, sin]
        seqs = TM // SEQ
        out_specs = [out_specs] + [
            pl.BlockSpec((seqs, width, SEQ), lambda i: (jnp.minimum(i, CTX_TILES - 1), 0, 0)) if transposed
            else _pair_specs(width)[0] for _, width, transposed in cache_cols]
        out_shape = [jax.ShapeDtypeStruct((N_TOK, n), BF16)] + [
            jax.ShapeDtypeStruct((BATCH, width, SEQ) if transposed else (N_CTX, width), F32)
            for _, width, transposed in cache_cols]
    return pl.pallas_call(
        functools.partial(_inproj_kernel, rope_groups=rope_groups, cache_cols=cache_cols),
        grid=(N_TILES,),
        in_specs=in_specs,
        out_specs=out_specs,
        out_shape=out_shape,
        scratch_shapes=[pltpu.VMEM((D_MODEL, n), BF16)],
        compiler_params=_cparams(("arbitrary",)),
        name="inproj_rope" if rope is not None else "inproj",
    )(*args)


def _rope_tables():
    s = np.arange(DEC_SEQ)
    row, col = (s // GRID_W).astype(np.float64), (s % GRID_W).astype(np.float64)
    nf = HEAD_DIM // 4
    inv = np.power(ROPE_BASE, -np.arange(nf, dtype=np.float64) / nf)
    ang = np.concatenate([row[:, None] * inv, row[:, None] * inv, col[:, None] * inv, col[:, None] * inv], axis=1)
    sign = np.concatenate([-np.ones(nf), np.ones(nf), -np.ones(nf), np.ones(nf)])
    cos = np.tile(np.cos(ang), (1, LANE // HEAD_DIM))
    sin = np.tile(np.sin(ang) * sign, (1, LANE // HEAD_DIM))
    return jnp.asarray(cos, F32), jnp.asarray(sin, F32)


RG_T = 256
RG_TILES = D_RG // MXU_DIM


def _sigmoid(x):
    return 0.5 * jnp.tanh(0.5 * x) + 0.5


def _softplus(x):
    u = jnp.exp(-jnp.abs(x))
    w = 1.0 + u
    l1p = jnp.where(w == 1.0, u, jnp.log(w) * (u / jnp.where(w == 1.0, 1.0, w - 1.0)))
    return jnp.maximum(x, 0.0) + l1p


def _rglru_kernel(xa_ref, ga_ref, h0_ref, cw_ref, cb_ref, wg_ref, br_ref, bi_ref, lam_ref,
                  y_ref, st_ref, hf_s, xc_s, a_s, b_s, hs_s, *, seq_len):
    nchunk = seq_len // RG_T
    cw = cw_ref[0]
    cb = cb_ref[0]

    def conv_chunk(c):
        base = pl.multiple_of(c * RG_T, RG_T)
        cur = xa_ref[pl.ds(base, RG_T), :]
        prev = xa_ref[pl.ds(pl.multiple_of(jnp.maximum(base - SUBLANE, 0), SUBLANE), SUBLANE), :]
        prev = jnp.where(c > 0, prev, 0.0)
        nxt = xa_ref[pl.ds(pl.multiple_of(jnp.minimum(base + RG_T, seq_len - SUBLANE), SUBLANE), SUBLANE), :]
        nxt = jnp.where(c < nchunk - 1, nxt, 0.0)
        win = jnp.concatenate([prev, cur, nxt], axis=0)
        xc = cb
        for k in range(CONV_W):
            off = SUBLANE - 1 + k
            xc = xc + win[off:off + RG_T] * cw[k:k + 1, :]
        return xc

    def gates(xc, z):
        xb = xc.astype(BF16)
        rs, gs = [], []
        for t in range(RG_TILES):
            zz = _dot(xb[:, t * MXU_DIM:(t + 1) * MXU_DIM], wg_ref[0, z, t].astype(BF16))
            rs.append(zz[:, :MXU_DIM])
            gs.append(zz[:, MXU_DIM:])
        r = _sigmoid(jnp.concatenate(rs, axis=1) + br_ref[0, z:z + 1, :])
        g = _sigmoid(jnp.concatenate(gs, axis=1) + bi_ref[0, z:z + 1, :])
        log_a = (-RG_C * _softplus(-lam_ref[0, z:z + 1, :])) * r
        a = jnp.exp(log_a)
        y = -jnp.tanh(log_a) * (a * a + 1.0)
        bt = jnp.where(y > 0.0, y * lax.rsqrt(y), 0.0) * g * xc
        return a, bt

    def scan_chunk(h, reverse):
        def step(t, h):
            tt = RG_T - 1 - t if reverse else t
            h = a_s[pl.ds(tt, 1), :] * h + b_s[pl.ds(tt, 1), :]
            hs_s[pl.ds(tt, 1), :] = h
            return h
        return lax.fori_loop(0, RG_T, step, h, unroll=8)

    def fwd_chunk(c, h):
        rows = pl.ds(pl.multiple_of(c * RG_T, RG_T), RG_T)
        xc = conv_chunk(c)
        xc_s[rows, :] = xc
        a, bt = gates(xc, 0)
        a_s[...] = a
        b_s[...] = bt
        h = scan_chunk(h, False)
        hf_s[rows, :] = hs_s[...]
        return h

    def bwd_chunk(k, h):
        c = nchunk - 1 - k
        rows = pl.ds(pl.multiple_of(c * RG_T, RG_T), RG_T)
        a, bt = gates(xc_s[rows, :], 1)
        a_s[...] = a
        b_s[...] = bt
        h = scan_chunk(h, True)
        y_ref[rows, :] = ((hf_s[rows, :] + hs_s[...]) * jax.nn.gelu(ga_ref[rows, :].astype(F32))).astype(BF16)
        return h

    hf = lax.fori_loop(0, nchunk, fwd_chunk, h0_ref[0, 0:1, :])
    hb = lax.fori_loop(0, nchunk, bwd_chunk, h0_ref[0, 1:2, :])
    st_ref[0, 0:1, :] = hf
    st_ref[0, 1:2, :] = hb


def _rglru(xa, gz, h0, j, cw, cb, wg, br, bi, lam, *, seq_len, nseq, row_block0):
    wspec3 = lambda shape: pl.BlockSpec((1,) + shape, lambda b: (j,) + (0,) * len(shape))
    return pl.pallas_call(
        functools.partial(_rglru_kernel, seq_len=seq_len),
        grid=(nseq,),
        in_specs=[
            pl.BlockSpec((seq_len, D_RG), lambda b: (row_block0 + b, 0)),
            pl.BlockSpec((seq_len, D_RG), lambda b: (row_block0 + b, 0)),
            pl.BlockSpec((1, 2, D_RG), lambda b: (b, 0, 0)),
            wspec3((CONV_W, D_RG)),
            wspec3((1, D_RG)),
            wspec3((2, RG_TILES, MXU_DIM, 2 * MXU_DIM)),
            wspec3((2, D_RG)),
            wspec3((2, D_RG)),
            wspec3((2, D_RG)),
        ],
        out_specs=[
            pl.BlockSpec((seq_len, D_RG), lambda b: (b, 0)),
            pl.BlockSpec((1, 2, D_RG), lambda b: (b, 0, 0)),
        ],
        out_shape=[
            jax.ShapeDtypeStruct((nseq * seq_len, D_RG), BF16),
            jax.ShapeDtypeStruct((nseq, 2, D_RG), F32),
        ],
        scratch_shapes=[
            pltpu.VMEM((seq_len, D_RG), F32),
            pltpu.VMEM((seq_len, D_RG), F32),
            pltpu.VMEM((RG_T, D_RG), F32),
            pltpu.VMEM((RG_T, D_RG), F32),
            pltpu.VMEM((RG_T, D_RG), F32),
        ],
        compiler_params=_cparams(("arbitrary",)),
        name=f"rglru_s{seq_len}",
    )(xa, gz, h0, cw, cb, wg, br, bi, lam)


def _gate_tiles(w_r, w_i):
    per_tile = MXU_DIM // RG_BLOCK
    eye = jnp.eye(per_tile, dtype=F32)

    def tiles(w):
        w = w.reshape(w.shape[0], 2, RG_TILES, per_tile, RG_BLOCK, RG_BLOCK)
        return jnp.einsum('nztbcd,be->nztbced', w, eye).reshape(w.shape[0], 2, RG_TILES, MXU_DIM, MXU_DIM)

    return jnp.concatenate([tiles(w_r), tiles(w_i)], axis=-1)


FNET_ROW_TILE = 512


def _fnet_kernel(z_ref, cc_ref, sc_ref, w_ref, cs_ref, ns_ref, y_ref, p_s, q_s):
    b = pl.program_id(1)

    @pl.when(pl.program_id(0) == 0)
    def _():
        w = w_ref[0]
        a = _dot3(cc_ref[...], w).astype(BF16)
        bm = _dot3(sc_ref[...], w).astype(BF16)
        zb = z_ref[...].astype(BF16)
        p_s[b] = _dot(zb, a).astype(BF16)
        q_s[b] = _dot(zb, bm).astype(BF16)

    y_ref[...] = (_dot(cs_ref[...].astype(BF16), p_s[b])
                  + _dot(ns_ref[...].astype(BF16), q_s[b])).astype(BF16)


def _fnet(gz, j, wbd, *, seq_len, nseq, row_block0):
    tr = min(seq_len, FNET_ROW_TILE)
    nrt = seq_len // tr
    cc, sc = _dft_tables(FNET_GROUP_DIM)
    ccbd = jnp.asarray(np.kron(np.eye(FNET_GROUPS), cc), F32)
    scbd = jnp.asarray(np.kron(np.eye(FNET_GROUPS), sc), F32)
    cs, ss = _dft_tables(seq_len)
    cs_b, ns_b = jnp.asarray(cs, F32), jnp.asarray(-ss, F32)
    zcol = D_RG // D_FNET
    const = lambda shape: pl.BlockSpec(shape, lambda r, b: (0,) * len(shape))
    return pl.pallas_call(
        _fnet_kernel,
        grid=(nrt, nseq),
        in_specs=[
            pl.BlockSpec((seq_len, D_FNET), lambda r, b: (row_block0 + b, zcol)),
            const((D_FNET, D_FNET)),
            const((D_FNET, D_FNET)),
            pl.BlockSpec((1, D_FNET, D_FNET), lambda r, b: (j, 0, 0)),
            pl.BlockSpec((tr, seq_len), lambda r, b: (r, 0)),
            pl.BlockSpec((tr, seq_len), lambda r, b: (r, 0)),
        ],
        out_specs=pl.BlockSpec((tr, D_FNET), lambda r, b: (b * nrt + r, 0)),
        out_shape=jax.ShapeDtypeStruct((nseq * seq_len, D_FNET), BF16),
        scratch_shapes=[pltpu.VMEM((nseq, seq_len, D_FNET), BF16), pltpu.VMEM((nseq, seq_len, D_FNET), BF16)],
        compiler_params=_cparams(("arbitrary", "arbitrary")),
        name=f"fnet_s{seq_len}",
    )(gz, ccbd, scbd, wbd, cs_b, ns_b)


def _dft_tables(n):
    k = np.arange(n)
    ang = 2.0 * np.pi * ((k[:, None] * k[None, :]) % n) / n
    return np.cos(ang) / np.sqrt(n), np.sin(ang) / np.sqrt(n)


def _fnet_blockdiag(w_f):
    eye = jnp.eye(FNET_GROUPS, dtype=F32)
    return jnp.einsum('ngcd,gh->ngchd', w_f, eye).reshape(w_f.shape[0], D_FNET, D_FNET)


def _outproj_kernel(ac_ref, al_ref, bc_ref, bl_ref, w_ref, *refs, ka):
    i = pl.program_id(0)
    x_refs, refs = refs[:-8], refs[-8:]
    mod_ref, g_ref, be_ref, wr_ref, x1_ref, u2_ref, aff_ref, wbf_ref = refs

    @pl.when(i == 0)
    def _():
        wbf_ref[...] = w_ref[0].astype(BF16)

    out = _dot(_pick(i, ac_ref, al_ref), wbf_ref[0:ka, :]) + _dot(_pick(i, bc_ref, bl_ref), wbf_ref[ka:, :])
    m = mod_ref[0, 0]
    x = _pick(i, *x_refs) if len(x_refs) == 2 else x_refs[0][...]
    x1 = _layernorm(ALPHA * x + m[2:3, :] * out, g_ref[0, 0:1, :], be_ref[0, 0:1, :])
    x1_ref[...] = x1
    u2 = x1 * (1.0 + m[4:5, :]) + m[3:4, :]
    u2_ref[...] = u2.astype(BF16)
    uh, ul = _split(u2)
    wh, wl = _split(wr_ref[0])
    lg = _dot(jnp.concatenate([uh, ul], axis=0), jnp.concatenate([wh, wl], axis=1))
    logits = (lg[:TM, :N_EXPERTS] + lg[:TM, N_EXPERTS:]) + (lg[TM:, :N_EXPERTS] + lg[TM:, N_EXPERTS:])
    mx = jnp.max(logits, axis=-1, keepdims=True)
    ex = jnp.exp(logits - mx)
    aff = ex / jnp.sum(ex, axis=-1, keepdims=True)
    aff = jnp.concatenate([aff, jnp.zeros((TM, LANE - N_EXPERTS), F32)], axis=1)
    aff_ref[...] = aff.T[:N_EXPERTS, :]


def _outproj(a_pair, b_pair, w_out, j, x, mod, layer, ln_g, ln_b, w_router):
    ka, kb = a_pair[0].shape[1], b_pair[0].shape[1]
    x_specs, x_args = _x_operand(x)
    return pl.pallas_call(
        functools.partial(_outproj_kernel, ka=ka),
        grid=(N_TILES,),
        in_specs=_pair_specs(ka) + _pair_specs(kb) + [
            pl.BlockSpec((1, D_MODEL, D_MODEL), lambda i: (j, 0, 0)),
        ] + x_specs + [
            pl.BlockSpec((1, 1, 6, D_MODEL), lambda i: (layer, _seg_of_tile(i), 0, 0)),
            pl.BlockSpec((1, 2, D_MODEL), lambda i: (layer, 0, 0)),
            pl.BlockSpec((1, 2, D_MODEL), lambda i: (layer, 0, 0)),
            pl.BlockSpec((1, D_MODEL, N_EXPERTS), lambda i: (layer, 0, 0)),
        ],
        out_specs=[
            pl.BlockSpec((TM, D_MODEL), lambda i: (i, 0)),
            pl.BlockSpec((TM, D_MODEL), lambda i: (i, 0)),
            pl.BlockSpec((N_EXPERTS, TM), lambda i: (0, i)),
        ],
        out_shape=[
            jax.ShapeDtypeStruct((N_TOK, D_MODEL), F32),
            jax.ShapeDtypeStruct((N_TOK, D_MODEL), BF16),
            jax.ShapeDtypeStruct((N_EXPERTS, N_TOK), F32),
        ],
        scratch_shapes=[pltpu.VMEM((D_MODEL, D_MODEL), BF16)],
        compiler_params=_cparams(("arbitrary",)),
        name="outproj_ln_router",
    )(*a_pair, *b_pair, w_out, *x_args, mod, ln_g, ln_b, w_router)


def _topk_kernel(aff_ref, pos_ref, posn_ref, cnt_ref):
    aff = aff_ref[...]
    cap = jnp.float32(CAP)

    def count_ge(t):
        return jnp.sum(jnp.where(aff >= t, 1.0, 0.0), axis=1, keepdims=True)

    lo_v = jnp.zeros((N_EXPERTS, 1), F32)
    hi_v = jnp.full((N_EXPERTS, 1), 2.0, F32)
    lo_t = jnp.full((N_EXPERTS, 1), TOPK_MIN_EXP, F32)
    hi_t = jnp.full((N_EXPERTS, 1), 1.0, F32)
    for _ in range(TOPK_GEO_STEPS):
        mid_t = 0.5 * (lo_t + hi_t)
        cand = jnp.exp2(mid_t)
        ok = count_ge(cand) >= cap
        lo_t = jnp.where(ok, mid_t, lo_t)
        hi_t = jnp.where(ok, hi_t, mid_t)
        lo_v = jnp.where(ok, jnp.maximum(lo_v, cand), lo_v)
        hi_v = jnp.where(ok, hi_v, jnp.minimum(hi_v, cand))
    for _ in range(TOPK_LIN_STEPS):
        cand = lo_v + 0.5 * (hi_v - lo_v)
        ok = count_ge(cand) >= cap
        lo_v = jnp.where(ok, cand, lo_v)
        hi_v = jnp.where(ok, hi_v, cand)
    gt = aff >= hi_v
    eq = (aff >= lo_v) & (aff < hi_v)
    need = cap - jnp.sum(jnp.where(gt, 1.0, 0.0), axis=1, keepdims=True)

    tri = (lax.broadcasted_iota(I32, (LANE, LANE), 0) <= lax.broadcasted_iota(I32, (LANE, LANE), 1))
    tri = jnp.where(tri, 1.0, 0.0).astype(BF16)

    def cumsum_blocks(mask_f32):
        carry = jnp.zeros((N_EXPERTS, 1), F32)
        outs = []
        for k in range(mask_f32.shape[1] // LANE):
            blk = mask_f32[:, k * LANE:(k + 1) * LANE]
            inc = _dot(blk.astype(BF16), tri) + carry
            outs.append(inc)
            carry = carry + jnp.sum(blk, axis=1, keepdims=True)
        return outs

    eq_f = jnp.where(eq, 1.0, 0.0)
    gt_f = jnp.where(gt, 1.0, 0.0)
    eq_rank = cumsum_blocks(eq_f)
    sel_blocks = []
    for k, rk in enumerate(eq_rank):
        lanes = slice(k * LANE, (k + 1) * LANE)
        sel_blocks.append(jnp.where((eq_f[:, lanes] > 0.0) & (rk <= need), 1.0, gt_f[:, lanes]))
    sel = jnp.concatenate(sel_blocks, axis=1)
    sel_rank = cumsum_blocks(sel)
    pad = jnp.zeros((LANE - N_EXPERTS, LANE), F32)
    for k, rk in enumerate(sel_rank):
        lanes = slice(k * LANE, (k + 1) * LANE)
        pos = jnp.where(sel[:, lanes] > 0.0, rk - 1.0, -1.0)
        pos_ref[:, lanes] = pos.astype(I32)
        posn_ref[lanes, :] = jnp.concatenate([pos, pad], axis=0).T[:, :N_EXPERTS].astype(I32)
    lane = lax.broadcasted_iota(I32, (N_EXPERTS, LANE), 1)
    cnt = jnp.zeros((N_EXPERTS, LANE), F32)
    for c in range(1, N_CHUNKS + 1):
        cnt = jnp.where(lane == c, sel_rank[c * (GC // LANE) - 1][:, LANE - 1:LANE], cnt)
    cnt_ref[0] = cnt.astype(I32)


def _topk(aff_t):
    return pl.pallas_call(
        _topk_kernel,
        grid=(2,),
        in_specs=[pl.BlockSpec((N_EXPERTS, N_CTX), lambda g: (0, g))],
        out_specs=[
            pl.BlockSpec((N_EXPERTS, N_CTX), lambda g: (0, g)),
            pl.BlockSpec((N_CTX, N_EXPERTS), lambda g: (g, 0)),
            pl.BlockSpec((1, N_EXPERTS, LANE), lambda g: (g, 0, 0)),
        ],
        out_shape=[
            jax.ShapeDtypeStruct((N_EXPERTS, N_TOK), I32),
            jax.ShapeDtypeStruct((N_TOK, N_EXPERTS), I32),
            jax.ShapeDtypeStruct((2, N_EXPERTS, LANE), I32),
        ],
        compiler_params=_cparams(("arbitrary",)),
        name="topk_select",
    )(aff_t)


def _chunk_windows(cnt_ref, g, e, c):
    c0 = cnt_ref[g, e, c]
    c1 = cnt_ref[g, e, c + 1]
    base = (c0 // WIN_ALIGN) * WIN_ALIGN
    k = jnp.where(c1 > c0, (c1 - base + WIN - 1) // WIN, 0)
    return base, k


def _gather_kernel(cnt_ref, pos_ref, aff_ref, u_ref, xe_ref, gs_ref):
    g = pl.program_id(0)
    c = pl.program_id(1)

    @pl.when(c == 0)
    def _():
        xe_ref[...] = jnp.zeros_like(xe_ref)
        gs_ref[...] = jnp.zeros_like(gs_ref)

    pos = pos_ref[...]
    aff = aff_ref[...]
    slot0 = lax.broadcasted_iota(I32, (WIN, GC), 0)

    def window(e, lo):
        start = pl.multiple_of(jnp.minimum(lo, CAP - WIN), WIN_ALIGN)
        pe = pos[e:e + 1, :]
        hit = (slot0 + start) == jnp.where(pe >= lo, pe, -1)
        gates = jnp.sum(jnp.where(hit, aff[e:e + 1, :], 0.0), axis=1, keepdims=True)
        return start, jnp.where(hit, 1.0, 0.0).astype(BF16), gates

    def add_rows(e, start, rows, gates):
        dst = (e, 0, pl.ds(start, WIN), slice(None))
        xe_ref[dst] = (xe_ref[dst].astype(F32) + rows).astype(BF16)
        gs_ref[dst] += gates

    windows = [_chunk_windows(cnt_ref, g, e, c) for e in range(N_EXPERTS)]
    firsts = [window(e, base) for e, (base, _) in enumerate(windows)]
    rows = _dot(jnp.concatenate([oh for _, oh, _ in firsts], axis=0), u_ref[...])
    for e, (start, _, gates) in enumerate(firsts):
        add_rows(e, start, rows[e * WIN:(e + 1) * WIN], gates)
    for e, (base, k) in enumerate(windows):
        def extra(w, carry, e=e, base=base):
            start, oh, gates = window(e, base + w * WIN)
            add_rows(e, start, _dot(oh, u_ref[...]), gates)
            return carry
        lax.fori_loop(1, k, extra, 0)


def _gather(cnt, pos_t, aff_t, u2):
    return pl.pallas_call(
        _gather_kernel,
        grid_spec=pltpu.PrefetchScalarGridSpec(
            num_scalar_prefetch=1,
            grid=(2, N_CHUNKS),
            in_specs=[
                pl.BlockSpec((N_EXPERTS, GC), lambda g, c, cnt: (0, g * N_CHUNKS + c)),
                pl.BlockSpec((N_EXPERTS, GC), lambda g, c, cnt: (0, g * N_CHUNKS + c)),
                pl.BlockSpec((GC, D_MODEL), lambda g, c, cnt: (g * N_CHUNKS + c, 0)),
            ],
            out_specs=[
                pl.BlockSpec((N_EXPERTS, 1, CAP, D_MODEL), lambda g, c, cnt: (0, g, 0, 0)),
                pl.BlockSpec((N_EXPERTS, 1, CAP, 1), lambda g, c, cnt: (0, g, 0, 0)),
            ],
        ),
        out_shape=[
            jax.ShapeDtypeStruct((N_EXPERTS, 2, CAP, D_MODEL), BF16),
            jax.ShapeDtypeStruct((N_EXPERTS, 2, CAP, 1), F32),
        ],
        compiler_params=_cparams(("arbitrary", "arbitrary")),
        name="moe_gather",
    )(cnt, pos_t, aff_t, u2)


TF = 1024
assert D_EXPERT // TF >= 2


def _ffn_kernel(xe_ref, gs_ref, wg_ref, wu_ref, wd_ref, ye_ref, acc_ref):
    f = pl.program_id(1)
    x = xe_ref[0]
    hg = _dot(x, wg_ref[0, 0].astype(BF16))
    hu = _dot(x, wu_ref[0, 0].astype(BF16))
    h = (hg * _sigmoid(hg) * hu).astype(BF16)

    @pl.when(f == 0)
    def _():
        acc_ref[...] = _dot(h, wd_ref[0, 0].astype(BF16))

    last = pl.num_programs(1) - 1

    @pl.when((f > 0) & (f < last))
    def _():
        acc_ref[...] += _dot(h, wd_ref[0, 0].astype(BF16))

    @pl.when(f == last)
    def _():
        ye_ref[0] = ((acc_ref[...] + _dot(h, wd_ref[0, 0].astype(BF16))) * gs_ref[0]).astype(BF16)


def _ffn(xe, gs, layer, w_gate, w_up, w_down):
    rows = 2 * CAP
    return pl.pallas_call(
        _ffn_kernel,
        grid=(N_EXPERTS, D_EXPERT // TF),
        in_specs=[
            pl.BlockSpec((1, rows, D_MODEL), lambda e, f: (e, 0, 0)),
            pl.BlockSpec((1, rows, 1), lambda e, f: (e, 0, 0)),
            pl.BlockSpec((1, 1, D_MODEL, TF), lambda e, f: (layer, e, 0, f)),
            pl.BlockSpec((1, 1, D_MODEL, TF), lambda e, f: (layer, e, 0, f)),
            pl.BlockSpec((1, 1, TF, D_MODEL), lambda e, f: (layer, e, f, 0)),
        ],
        out_specs=pl.BlockSpec((1, rows, D_MODEL), lambda e, f: (e, 0, 0)),
        out_shape=jax.ShapeDtypeStruct((N_EXPERTS, rows, D_MODEL), BF16),
        scratch_shapes=[pltpu.VMEM((rows, D_MODEL), F32)],
        compiler_params=_cparams(("arbitrary", "arbitrary")),
        name="moe_ffn",
    )(xe, gs, w_gate, w_up, w_down)


def _combine_kernel(*refs, split_out):
    if split_out:
        cnt_ref, posn_ref, ye_ref, x1_ref, mod_ref, g_ref, be_ref, oc_ref, ol_ref, acc_s = refs
    else:
        cnt_ref, posn_ref, ye_ref, x1_ref, mod_ref, g_ref, be_ref, o_ref, acc_s = refs
    i = pl.program_id(0)
    g = i // N_CHUNKS
    c = i % N_CHUNKS
    posn = posn_ref[...]
    lane = lax.broadcasted_iota(I32, (1, MXU_DIM), 1)
    part = lane // WIN
    slot_in_win = lane % WIN

    def by_part(vals):
        out = vals[-1]
        for q in range(len(vals) - 2, -1, -1):
            out = jnp.where(part == q, vals[q], out)
        return out

    def start_of(lo):
        return pl.multiple_of(jnp.minimum(lo, CAP - WIN), WIN_ALIGN)

    def group_product(experts, los):
        starts = [start_of(lo) for lo in los]
        slots = by_part(starts) + slot_in_win
        cols = by_part([jnp.where(posn[:, e:e + 1] >= lo, posn[:, e:e + 1], -1) for e, lo in zip(experts, los)])
        onehot = jnp.where(slots == cols, 1.0, 0.0).astype(BF16)
        rows = jnp.concatenate([ye_ref[e, 0, pl.ds(st, WIN), :] for e, st in zip(experts, starts)], axis=0)
        return _dot(onehot, rows)

    windows = [_chunk_windows(cnt_ref, g, e, c) for e in range(N_EXPERTS)]
    acc = None
    for e0 in range(0, N_EXPERTS, WINS_PER_DOT):
        experts = list(range(e0, e0 + WINS_PER_DOT))
        d = group_product(experts, [windows[e][0] for e in experts])
        acc = d if acc is None else acc + d
    acc_s[...] = acc
    slot0 = lax.broadcasted_iota(I32, (GC, WIN), 1)
    for e, (base, k) in enumerate(windows):
        def extra(w, carry, e=e, base=base):
            lo = base + w * WIN
            start = start_of(lo)
            col = posn[:, e:e + 1]
            hit = (slot0 + start) == jnp.where(col >= lo, col, -1)
            acc_s[...] += _dot(jnp.where(hit, 1.0, 0.0).astype(BF16), ye_ref[e, 0, pl.ds(start, WIN), :])
            return carry
        lax.fori_loop(1, k, extra, 0)
    m = mod_ref[0, 0]
    y = ALPHA * x1_ref[...] + m[5:6, :] * acc_s[...]
    res = _layernorm(y, g_ref[0, 1:2, :], be_ref[0, 1:2, :])
    if split_out:
        @pl.when(i < N_CHUNKS)
        def _():
            oc_ref[...] = res

        @pl.when(i >= N_CHUNKS)
        def _():
            ol_ref[...] = res
    else:
        o_ref[...] = res


def _combine(cnt, pos_n, ye, x1, mod, layer, ln_g, ln_b, split_out):
    if split_out:
        out_specs = _pair_specs(D_MODEL, tile=GC)
        out_shape = [jax.ShapeDtypeStruct((N_CTX, D_MODEL), F32), jax.ShapeDtypeStruct((N_LAT, D_MODEL), F32)]
    else:
        out_specs = pl.BlockSpec((GC, D_MODEL), lambda i, cnt: (i, 0))
        out_shape = jax.ShapeDtypeStruct((N_TOK, D_MODEL), F32)
    return pl.pallas_call(
        functools.partial(_combine_kernel, split_out=split_out),
        grid_spec=pltpu.PrefetchScalarGridSpec(
            num_scalar_prefetch=1,
            grid=(N_TOK // GC,),
            in_specs=[
                pl.BlockSpec((GC, N_EXPERTS), lambda i, cnt: (i, 0)),
                pl.BlockSpec((N_EXPERTS, 1, CAP, D_MODEL), lambda i, cnt: (0, i // N_CHUNKS, 0, 0)),
                pl.BlockSpec((GC, D_MODEL), lambda i, cnt: (i, 0)),
                pl.BlockSpec((1, 1, 6, D_MODEL), lambda i, cnt: (layer, _seg_of_tile(i, GC), 0, 0)),
                pl.BlockSpec((1, 2, D_MODEL), lambda i, cnt: (layer, 0, 0)),
                pl.BlockSpec((1, 2, D_MODEL), lambda i, cnt: (layer, 0, 0)),
            ],
            out_specs=out_specs,
            scratch_shapes=[pltpu.VMEM((GC, D_MODEL), F32)],
        ),
        out_shape=out_shape,
        compiler_params=_cparams(("arbitrary",)),
        name="moe_combine_ln",
    )(cnt, pos_n, ye, x1, mod, ln_g, ln_b)


def _moe(x1, u2, aff_t, mod, layer, ln_g, ln_b, w_gate, w_up, w_down, split_out=False):
    pos_t, pos_n, cnt = _topk(aff_t)
    xe, gs = _gather(cnt, pos_t, aff_t, u2)
    ye = _ffn(xe.reshape(N_EXPERTS, 2 * CAP, D_MODEL), gs.reshape(N_EXPERTS, 2 * CAP, 1),
              layer, w_gate, w_up, w_down)
    return _combine(cnt, pos_n, ye.reshape(N_EXPERTS, 2, CAP, D_MODEL), x1, mod, layer, ln_g, ln_b, split_out)


def _sink_softmax_pv(s, sink_col, v):
    m = jnp.maximum(jnp.max(s, axis=-1, keepdims=True), sink_col)
    p = jnp.exp(s - m)
    den = jnp.sum(p, axis=-1, keepdims=True) + jnp.exp(sink_col - m)
    return _dot(p.astype(BF16), v) / den


def _sink_column(sink_ref, h, rows_per_group):
    rid = lax.broadcasted_iota(I32, (GQA_GROUP * rows_per_group, 1), 0) // rows_per_group
    col = jnp.zeros((GQA_GROUP * rows_per_group, 1), F32)
    for g in range(GQA_GROUP):
        col = jnp.where(rid == g, sink_ref[h * GQA_GROUP + g], col)
    return col


def _stack_groups(q, h):
    return jnp.concatenate(
        [q[:, (h * GQA_GROUP + g) * HEAD_DIM:(h * GQA_GROUP + g + 1) * HEAD_DIM] * ATTN_SCALE
         for g in range(GQA_GROUP)], axis=0).astype(BF16)


def _unstack_groups(outs, rows):
    return jnp.concatenate([o[g * rows:(g + 1) * rows] for o in outs for g in range(GQA_GROUP)], axis=1)


def _swa_ctx_kernel(sink_ref, q_ref, k_ref, v_ref, o_ref):
    q, k, v = q_ref[...], k_ref[...], v_ref[...]
    outs = []
    for h in range(SWA_KV_HEADS):
        cols = slice(h * HEAD_DIM, (h + 1) * HEAD_DIM)
        s = _dot_nt(_stack_groups(q, h), k[:, cols].astype(BF16))
        outs.append(_sink_softmax_pv(s, _sink_column(sink_ref, h, SEQ), v[:, cols].astype(BF16)))
    o_ref[...] = _unstack_groups(outs, SEQ).astype(BF16)


def _swa_ctx(proj, sink):
    return pl.pallas_call(
        _swa_ctx_kernel,
        grid=(BATCH,),
        in_specs=[
            pl.BlockSpec(memory_space=pltpu.SMEM),
            pl.BlockSpec((SEQ, SWA_Q_W), lambda b: (b, 0)),
            pl.BlockSpec((SEQ, SWA_KV_W), lambda b: (b, SWA_K_COL // SWA_KV_W)),
            pl.BlockSpec((SEQ, SWA_KV_W), lambda b: (b, SWA_V_COL // SWA_KV_W)),
        ],
        out_specs=pl.BlockSpec((SEQ, SWA_Q_W), lambda b: (b, 0)),
        out_shape=jax.ShapeDtypeStruct((N_CTX, SWA_Q_W), BF16),
        compiler_params=_cparams(("arbitrary",)),
        name="swa_context",
    )(sink, proj, proj, proj)


QB = 128


def _swa_lat_kernel(sink_ref, q_ref, kp_ref, kc_ref, kn_ref, vp_ref, vc_ref, vn_ref, ck_ref, cv_ref, o_ref):
    qb = pl.program_id(1)
    q = q_ref[...]
    kall = jnp.concatenate([ck_ref[0].astype(BF16), kp_ref[...], kc_ref[...], kn_ref[...]], axis=0)
    vall = jnp.concatenate([cv_ref[0].astype(BF16), vp_ref[...], vc_ref[...], vn_ref[...]], axis=0)
    nk = PAST_LEN + 3 * QB
    col = lax.broadcasted_iota(I32, (QB, nk), 1)
    qpos = qb * QB + lax.broadcasted_iota(I32, (QB, nk), 0)
    kpos = (qb - 1) * QB + col - PAST_LEN
    ok = (col < PAST_LEN) | ((jnp.abs(qpos - kpos) <= WINDOW) & (kpos >= 0) & (kpos < DEC_SEQ))
    bias = jnp.concatenate([jnp.where(ok, 0.0, NEG_INF)] * GQA_GROUP, axis=0)
    outs = []
    for h in range(SWA_KV_HEADS):
        cols = slice(h * HEAD_DIM, (h + 1) * HEAD_DIM)
        s = _dot_nt(_stack_groups(q, h), kall[:, cols]) + bias
        outs.append(_sink_softmax_pv(s, _sink_column(sink_ref, h, QB), vall[:, cols]))
    o_ref[...] = _unstack_groups(outs, QB).astype(BF16)


def _swa_lat(proj, cache_k, cache_v, sink):
    nqb = DEC_SEQ // QB
    row0 = N_CTX // QB

    def blk(col, shift):
        return pl.BlockSpec((QB, SWA_KV_W),
                            lambda b, i: (row0 + b * nqb + jnp.clip(i + shift, 0, nqb - 1), col // SWA_KV_W))

    return pl.pallas_call(
        _swa_lat_kernel,
        grid=(DEC_BATCH, nqb),
        in_specs=[
            pl.BlockSpec(memory_space=pltpu.SMEM),
            pl.BlockSpec((QB, SWA_Q_W), lambda b, i: (row0 + b * nqb + i, 0)),
            blk(SWA_K_COL, -1), blk(SWA_K_COL, 0), blk(SWA_K_COL, 1),
            blk(SWA_V_COL, -1), blk(SWA_V_COL, 0), blk(SWA_V_COL, 1),
            pl.BlockSpec((1, PAST_LEN, SWA_KV_W), lambda b, i: (b, 0, 0)),
            pl.BlockSpec((1, PAST_LEN, SWA_KV_W), lambda b, i: (b, 0, 0)),
        ],
        out_specs=pl.BlockSpec((QB, SWA_Q_W), lambda b, i: (b * nqb + i, 0)),
        out_shape=jax.ShapeDtypeStruct((N_LAT, SWA_Q_W), BF16),
        compiler_params=_cparams(("arbitrary", "arbitrary")),
        name="swa_latent",
    )(sink, proj, proj, proj, proj, proj, proj, proj, cache_k, cache_v)


def _diff_lambda_col(lam_ref, lam_init):
    l1 = jnp.sum(lam_ref[0, 0] * lam_ref[0, 1], axis=-1, keepdims=True)
    l2 = jnp.sum(lam_ref[0, 2] * lam_ref[0, 3], axis=-1, keepdims=True)
    return jnp.exp(l1) - jnp.exp(l2) + lam_init


def _softmax_pv(q, k, v):
    s = _dot_nt(q, k)
    p = jnp.exp(s - jnp.max(s, axis=-1, keepdims=True))
    return _dot(p.astype(BF16), v) / jnp.sum(p, axis=-1, keepdims=True)


def _diff_head(q, k, v, lam, subln, lam_init):
    o = [_softmax_pv((q[:, m * HEAD_DIM:(m + 1) * HEAD_DIM] * ATTN_SCALE).astype(BF16),
                     k[:, m * HEAD_DIM:(m + 1) * HEAD_DIM], v) for m in range(2)]
    o = o[0] - lam * o[1]
    return o * lax.rsqrt(jnp.mean(o * o, axis=-1, keepdims=True) + LN_EPS) * subln * (1.0 - lam_init)


def _diff_kernel(*refs, lam_init, has_cache, nh):
    if has_cache:
        lam_ref, sg_ref, q_ref, k_ref, v_ref, ck_ref, cv_ref, o_ref = refs
    else:
        lam_ref, sg_ref, q_ref, k_ref, v_ref, o_ref = refs
    hp = pl.program_id(1)
    lam_all = _diff_lambda_col(lam_ref, lam_init)
    hid = lax.broadcasted_iota(I32, (DIFF_HEADS, 1), 0)
    q, k, v = q_ref[...], k_ref[...], v_ref[...]
    if has_cache:
        k = jnp.concatenate([k, ck_ref[0].astype(BF16)], axis=0)
        v = jnp.concatenate([v, cv_ref[0].astype(BF16)], axis=0)
    outs = []
    for j in range(nh):
        lam = jnp.sum(jnp.where(hid == hp * nh + j, lam_all, 0.0), axis=0, keepdims=True)
        cols = slice(j * DIFF_V_DIM, (j + 1) * DIFF_V_DIM)
        outs.append(_diff_head(q[:, cols], k[:, cols], v[:, cols], lam, sg_ref[...], lam_init))
    o_ref[...] = jnp.concatenate(outs, axis=1).astype(BF16)


DIFF_STEP_HEADS = 2
DQ = 512
assert all(col % (DIFF_STEP_HEADS * DIFF_V_DIM) == 0 for col in (Q_COL, K_COL, V_COL))


def _diff_ctx(proj, lam_p, subln, j, lam_init):
    nh = DIFF_STEP_HEADS
    w = nh * DIFF_V_DIM
    return pl.pallas_call(
        functools.partial(_diff_kernel, lam_init=lam_init, has_cache=False, nh=nh),
        grid=(BATCH, DIFF_HEADS // nh),
        in_specs=[
            pl.BlockSpec((1, 4, DIFF_HEADS, HEAD_DIM), lambda b, h: (j, 0, 0, 0)),
            pl.BlockSpec((1, DIFF_V_DIM), lambda b, h: (j, 0)),
            pl.BlockSpec((SEQ, w), lambda b, h: (b, Q_COL // w + h)),
            pl.BlockSpec((SEQ, w), lambda b, h: (b, K_COL // w + h)),
            pl.BlockSpec((SEQ, w), lambda b, h: (b, V_COL // w + h)),
        ],
        out_specs=pl.BlockSpec((SEQ, w), lambda b, h: (b, h)),
        out_shape=jax.ShapeDtypeStruct((N_CTX, DIFF_W), BF16),
        compiler_params=_cparams(("arbitrary", "arbitrary")),
        name="diff_context",
    )(lam_p, subln, proj, proj, proj)


def _diff_lat(proj, cache_k, cache_v, lam_p, subln, j, lam_init):
    nh = DIFF_STEP_HEADS
    w = nh * DIFF_V_DIM
    nq = DEC_SEQ // DQ
    row0 = N_CTX // DQ
    seq0 = N_CTX // DEC_SEQ
    return pl.pallas_call(
        functools.partial(_diff_kernel, lam_init=lam_init, has_cache=True, nh=nh),
        grid=(DEC_BATCH, DIFF_HEADS // nh, nq),
        in_specs=[
            pl.BlockSpec((1, 4, DIFF_HEADS, HEAD_DIM), lambda b, h, i: (j, 0, 0, 0)),
            pl.BlockSpec((1, DIFF_V_DIM), lambda b, h, i: (j, 0)),
            pl.BlockSpec((DQ, w), lambda b, h, i: (row0 + b * nq + i, Q_COL // w + h)),
            pl.BlockSpec((DEC_SEQ, w), lambda b, h, i: (seq0 + b, K_COL // w + h)),
            pl.BlockSpec((DEC_SEQ, w), lambda b, h, i: (seq0 + b, V_COL // w + h)),
            pl.BlockSpec((1, PAST_LEN, w), lambda b, h, i: (b, 0, h)),
            pl.BlockSpec((1, PAST_LEN, w), lambda b, h, i: (b, 0, h)),
        ],
        out_specs=pl.BlockSpec((DQ, w), lambda b, h, i: (b * nq + i, h)),
        out_shape=jax.ShapeDtypeStruct((N_LAT, DIFF_W), BF16),
        compiler_params=_cparams(("arbitrary", "arbitrary", "arbitrary")),
        name="diff_latent",
    )(lam_p, subln, proj, proj, proj, cache_k, cache_v)


def kernel(x_prompt, x_sample, state_rglru, cache_swa_k, cache_swa_v, cache_diff_k, cache_diff_v, c, c_ctx,
           w_mod, b_mod, ln_g, ln_b, e_w_in, e_conv_w, e_conv_b, e_w_rgate, e_b_rgate, e_w_igate, e_b_igate,
           e_lambda, e_w_fnet, e_w_out, o_w_in, o_sink, o_lambda, o_subln_g, o_w_out,
           w_router, w_gate, w_up, w_down):
    x_pair = (x_prompt.reshape(N_CTX, D_MODEL), x_sample.reshape(N_LAT, D_MODEL))
    cvec8 = jnp.concatenate([c_ctx[None, :], c, jnp.zeros((SUBLANE - N_SEG, D_MODEL), F32)], axis=0).T
    mod = _modulation(cvec8, w_mod, b_mod).reshape(DEPTH, SUBLANE, 6, D_MODEL)

    j = 0
    xa, gz = _inproj(x_pair, mod, 0, e_w_in, j)
    wg = _gate_tiles(e_w_rgate, e_w_igate)
    cb = e_conv_b.reshape(-1, 1, D_RG)
    rg_args = (j, e_conv_w, cb, wg, e_b_rgate, e_b_igate, e_lambda)
    y_rg_c, st_c = _rglru(xa, gz, jnp.zeros((BATCH, 2, D_RG), F32), *rg_args,
                          seq_len=SEQ, nseq=BATCH, row_block0=0)
    y_rg_l, _ = _rglru(xa, gz, state_rglru[:, j], *rg_args,
                       seq_len=DEC_SEQ, nseq=DEC_BATCH, row_block0=N_CTX // DEC_SEQ)
    wbd = _fnet_blockdiag(e_w_fnet)
    y_fn_c = _fnet(gz, j, wbd, seq_len=SEQ, nseq=BATCH, row_block0=0)
    y_fn_l = _fnet(gz, j, wbd, seq_len=DEC_SEQ, nseq=DEC_BATCH, row_block0=N_CTX // DEC_SEQ)
    x1, u2, aff = _outproj((y_rg_c, y_rg_l), (y_fn_c, y_fn_l), e_w_out, j, x_pair, mod, 0, ln_g, ln_b, w_router)
    x = _moe(x1, u2, aff, mod, 0, ln_g, ln_b, w_gate, w_up, w_down)
    new_state_rglru = st_c[:, None]

    layer = 1
    lam_init = 0.8 - 0.6 * math.exp(-0.3 * layer)
    cos, sin = _rope_tables()
    rope_groups = tuple(g for g in range(ODD_IN // LANE)
                        if g * LANE < SWA_V_COL or Q_COL <= g * LANE < V_COL)
    cache_cols = ((SWA_K_COL, SWA_KV_W, True), (SWA_V_COL, SWA_KV_W, True),
                  (K_COL, DIFF_W, True), (V_COL, DIFF_W, False))
    proj, ks, vs, kd, vd = _inproj(x, mod, 1, o_w_in, j, rope=(cos, sin, rope_groups), cache_cols=cache_cols)
    sink = o_sink[j]
    ys_c = _swa_ctx(proj, sink)
    yd_c = _diff_ctx(proj, o_lambda, o_subln_g, j, lam_init)
    ck = cache_swa_k[:, j].reshape(DEC_BATCH, PAST_LEN, SWA_KV_HEADS * HEAD_DIM)
    cv = cache_swa_v[:, j].reshape(DEC_BATCH, PAST_LEN, SWA_KV_HEADS * HEAD_DIM)
    ys_l = _swa_lat(proj, ck, cv, sink)
    cdk = cache_diff_k[:, j].reshape(DEC_BATCH, PAST_LEN, DIFF_HEADS * 2 * HEAD_DIM)
    cdv = cache_diff_v[:, j].reshape(DEC_BATCH, PAST_LEN, DIFF_HEADS * DIFF_V_DIM)
    yd_l = _diff_lat(proj, cdk, cdv, o_lambda, o_subln_g, j, lam_init)
    x1, u2, aff = _outproj((ys_c, ys_l), (yd_c, yd_l), o_w_out, j, x, mod, 1, ln_g, ln_b, w_router)
    y_c, y_l = _moe(x1, u2, aff, mod, 1, ln_g, ln_b, w_gate, w_up, w_down, split_out=True)

    new_cache_swa_k = ks.reshape(BATCH, 1, SWA_KV_HEADS, HEAD_DIM, SEQ).transpose(0, 1, 4, 2, 3)
    new_cache_swa_v = vs.reshape(BATCH, 1, SWA_KV_HEADS, HEAD_DIM, SEQ).transpose(0, 1, 4, 2, 3)
    new_cache_diff_k = kd.reshape(BATCH, 1, DIFF_HEADS, 2, HEAD_DIM, SEQ).transpose(0, 1, 5, 2, 3, 4)
    new_cache_diff_v = vd.reshape(BATCH, 1, SEQ, DIFF_HEADS, DIFF_V_DIM)
    y_prompt = y_c.reshape(BATCH, SEQ, D_MODEL)
    y_sample = y_l.reshape(DEC_BATCH, DEC_SEQ, D_MODEL)
    return (y_prompt, y_sample, new_state_rglru, new_cache_swa_k, new_cache_swa_v,
            new_cache_diff_k, new_cache_diff_v)
```
